```python
import math
import jax, jax.numpy as jnp
from jax import lax
import numpy as np

D_MODEL = 1024
BATCH = 4
SEQ = 8192
DEPTH = 2

MIX_WIDTH = D_MODEL
A_HEADS = 6
A_QK_DIM = 32
A_V_DIM = 2 * A_QK_DIM
A_WIDTH = A_HEADS * A_V_DIM
POOL_WINDOWS = (2, 4, 8, 16)
POOL_GROUP_DIM = 64
POOL_GROUPS = len(POOL_WINDOWS)
POOL_WIDTH = POOL_GROUPS * POOL_GROUP_DIM
C_HEADS = 6
C_KV_GROUPS = 2
C_HEADS_PER_GROUP = C_HEADS // C_KV_GROUPS
C_HEAD_DIM = 64
C_WIDTH = C_HEADS * C_HEAD_DIM
C_KV_WIDTH = C_KV_GROUPS * C_HEAD_DIM
CMP_BLOCK = 32
CMP_STRIDE = 16
SEL_BLOCK = 64
N_SEL = 16
WINDOW = 512
N_BRANCH = 3
FORCED_SCORE = 1.0e4
D_FF = 4 * D_MODEL
ROPE_THETA = 10000.0
LN_EPS = 1e-5
RMS_EPS = 1e-6
ALPHA = (2 * DEPTH) ** 0.25
BETA = (8 * DEPTH) ** -0.25
Q_BLOCK = 128
NSA_Q_BLOCK = 64

IN_SPLITS = (A_HEADS * 2 * A_QK_DIM,
             A_HEADS * 2 * A_QK_DIM,
             A_WIDTH,
             POOL_WIDTH,
             C_WIDTH,
             C_KV_WIDTH, C_KV_WIDTH,
             C_KV_WIDTH, C_KV_WIDTH,
             C_KV_WIDTH, C_KV_WIDTH,
             C_HEADS * N_BRANCH)
N_IN = sum(IN_SPLITS)

kernel_name = "hymba_diffattn_pool_nsa_deepnorm"


def layer_norm(x, g, b):
    xf = x.astype(jnp.float32)
    mu = jnp.mean(xf, -1, keepdims=True)
    var = jnp.mean(jnp.square(xf - mu), -1, keepdims=True)
    y = (xf - mu) * lax.rsqrt(var + LN_EPS) * g.astype(jnp.float32) + b.astype(jnp.float32)
    return y.astype(x.dtype)


def rope_tables(seq, dim):
    inv = 1.0 / (ROPE_THETA ** (jnp.arange(0, dim, 2, dtype=jnp.float32) / dim))
    ang = jnp.arange(seq, dtype=jnp.float32)[:, None] * inv[None, :]
    return jnp.cos(ang), jnp.sin(ang)


def apply_rope(x, cos, sin):
    xf = x.astype(jnp.float32)
    x1, x2 = jnp.split(xf, 2, axis=-1)
    return jnp.concatenate([x1 * cos - x2 * sin, x2 * cos + x1 * sin], -1).astype(x.dtype)


def masked_softmax(sc, mask):
    sc = sc.astype(jnp.float32)
    p = jax.nn.softmax(jnp.where(mask, sc, -1e30), axis=-1)
    return jnp.where(mask, p, 0.0)


def diff_attention(q1, q2, k1, k2, v, lam, lam_init, subln_g):
    B, H, T, dq = q1.shape
    scale = dq ** -0.5
    kpos = jnp.arange(T)

    def block(i):
        s = i * Q_BLOCK
        qpos = s + jnp.arange(Q_BLOCK)
        mask = kpos[None, :] <= qpos[:, None]

        def probs(q, k):
            qb = lax.dynamic_slice_in_dim(q, s, Q_BLOCK, axis=2)
            sc = jnp.einsum('bhqd,bhkd->bhqk', qb, k).astype(jnp.float32) * scale
            return jax.nn.softmax(jnp.where(mask, sc, -jnp.inf), axis=-1)

        p = probs(q1, k1) - lam * probs(q2, k2)
        return jnp.einsum('bhqk,bhkd->bhqd', p.astype(v.dtype), v)

    o = lax.map(block, jnp.arange(T // Q_BLOCK))
    o = jnp.transpose(o, (1, 0, 3, 2, 4)).reshape(B, T, H, -1)
    of = o.astype(jnp.float32)
    of = of * lax.rsqrt(jnp.mean(jnp.square(of), -1, keepdims=True) + RMS_EPS)
    of = of * subln_g.astype(jnp.float32) * (1.0 - lam_init)
    return of.reshape(B, T, -1).astype(v.dtype)


def pool_mixer(u, w_pool, pool_scale):
    B, T, _ = u.shape
    uf = u.astype(jnp.float32).reshape(B, T, POOL_GROUPS, POOL_GROUP_DIM)
    csum = jnp.pad(jnp.cumsum(uf, axis=1), ((0, 0), (1, 0), (0, 0), (0, 0)))
    t = jnp.arange(T)
    means = []
    for g, w in enumerate(POOL_WINDOWS):
        lo = jnp.maximum(t + 1 - w, 0)
        cnt = jnp.minimum(t + 1, w).astype(jnp.float32)
        means.append((csum[:, 1:, g] - csum[:, lo, g]) / cnt[None, :, None])
    y = jnp.stack(means, axis=2) - uf
    y = jnp.einsum('btgc,gcd->btgd', y, w_pool.astype(jnp.float32))
    y = y.reshape(B, T, POOL_WIDTH) * pool_scale.astype(jnp.float32)
    return y.astype(u.dtype)


def _gather_blocks(blocks, idx):
    return blocks[idx]


def nsa_attention(q, k_cmp, v_cmp, k_slc, v_slc, k_win, v_win, gates,
                  pe_k, pe_v, w_ck, w_cv):
    B, G, h, T, d = q.shape
    scale = d ** -0.5
    n_cmp = (T - CMP_BLOCK) // CMP_STRIDE + 1
    cmp_start = jnp.arange(n_cmp) * CMP_STRIDE
    cmp_idx = cmp_start[:, None] + jnp.arange(CMP_BLOCK)[None, :]
    kc = jnp.einsum('bgnld,lde->bgne', k_cmp[:, :, cmp_idx] + pe_k, w_ck)
    vc = jnp.einsum('bgnld,lde->bgne', v_cmp[:, :, cmp_idx] + pe_v, w_cv)
    cmp_end = cmp_start + CMP_BLOCK - 1
    n_blk = T // SEL_BLOCK
    n_sel = min(N_SEL, n_blk)
    blk_start = jnp.arange(n_blk) * SEL_BLOCK
    overlap = ((cmp_start[:, None] < blk_start[None, :] + SEL_BLOCK)
               & (cmp_start[:, None] + CMP_BLOCK > blk_start[None, :])).astype(jnp.float32)
    k_blocks = k_slc.reshape(B, G, n_blk, SEL_BLOCK, d)
    v_blocks = v_slc.reshape(B, G, n_blk, SEL_BLOCK, d)
    gather = jax.vmap(jax.vmap(_gather_blocks))
    pad = ((0, 0), (0, 0), (WINDOW, 0), (0, 0))
    k_win_p = jnp.pad(k_win, pad)
    v_win_p = jnp.pad(v_win, pad)
    blk_ids = jnp.arange(n_blk)[None, :]

    def block(i):
        s = i * NSA_Q_BLOCK
        t = s + jnp.arange(NSA_Q_BLOCK)
        qb = lax.dynamic_slice_in_dim(q, s, NSA_Q_BLOCK, axis=3)
        cmask = cmp_end[None, :] <= t[:, None]
        p_cmp = masked_softmax(jnp.einsum('bghqd,bgnd->bghqn', qb, kc) * scale, cmask)
        o_cmp = jnp.einsum('bghqn,bgnd->bghqd', p_cmp.astype(vc.dtype), vc)
        imp = jnp.einsum('bghqn,ns->bgqs', p_cmp, overlap)
        cur = (t // SEL_BLOCK)[:, None]
        forced = (blk_ids == 0) | (blk_ids == cur) | (blk_ids == cur - 1)
        imp = jnp.where(forced, FORCED_SCORE, imp)
        imp = jnp.where(blk_ids > cur, -1.0, imp)
        _, sel = lax.top_k(imp, n_sel)
        ks = gather(k_blocks, sel).reshape(B, G, NSA_Q_BLOCK, n_sel * SEL_BLOCK, d)
        vs = gather(v_blocks, sel).reshape(B, G, NSA_Q_BLOCK, n_sel * SEL_BLOCK, d)
        kpos = (sel[..., None] * SEL_BLOCK + jnp.arange(SEL_BLOCK)).reshape(
            B, G, NSA_Q_BLOCK, n_sel * SEL_BLOCK)
        smask = (kpos <= t[None, None, :, None])[:, :, None]
        p_slc = masked_softmax(jnp.einsum('bghqd,bgqkd->bghqk', qb, ks) * scale, smask)
        o_slc = jnp.einsum('bghqk,bgqkd->bghqd', p_slc.astype(vs.dtype), vs)
        kw = lax.dynamic_slice_in_dim(k_win_p, s, WINDOW + NSA_Q_BLOCK, axis=2)
        vw = lax.dynamic_slice_in_dim(v_win_p, s, WINDOW + NSA_Q_BLOCK, axis=2)
        wpos = s - WINDOW + jnp.arange(WINDOW + NSA_Q_BLOCK)
        wmask = ((wpos[None, :] <= t[:, None]) & (wpos[None, :] > t[:, None] - WINDOW)
                 & (wpos[None, :] >= 0))
        p_win = masked_softmax(jnp.einsum('bghqd,bgkd->bghqk', qb, kw) * scale, wmask)
        o_win = jnp.einsum('bghqk,bgkd->bghqd', p_win.astype(vw.dtype), vw)
        return jnp.stack([o_cmp, o_slc, o_win], axis=-1)

    o = lax.map(block, jnp.arange(T // NSA_Q_BLOCK))
    o = jnp.transpose(o, (1, 0, 4, 2, 3, 5, 6)).reshape(B, T, G, h, d, N_BRANCH)
    out = jnp.einsum('btghdr,btghr->btghd', o, gates)
    return out.reshape(B, T, G * h * d)


def _to_kv(z, B, T):
    return z.reshape(B, T, C_KV_GROUPS, C_HEAD_DIM).transpose(0, 2, 1, 3)


def setup_inputs(seed: int = 0) -> dict:
    key = jax.random.key(seed)
    ks = jax.random.split(key, 22)

    def nrm(k, shape, s):
        return jax.random.normal(k, shape, jnp.float32) * s

    return {
        'x': nrm(ks[0], (BATCH, SEQ, D_MODEL), 1.0),
        'w_in': nrm(ks[1], (DEPTH, D_MODEL, N_IN), D_MODEL ** -0.5),
        'lam_q1': nrm(ks[2], (DEPTH, A_QK_DIM), 0.1),
        'lam_k1': nrm(ks[3], (DEPTH, A_QK_DIM), 0.1),
        'lam_q2': nrm(ks[4], (DEPTH, A_QK_DIM), 0.1),
        'lam_k2': nrm(ks[5], (DEPTH, A_QK_DIM), 0.1),
        'subln_g': 1.0 + nrm(ks[6], (DEPTH, A_V_DIM), 0.02),
        'w_pool': nrm(ks[7], (DEPTH, POOL_GROUPS, POOL_GROUP_DIM, POOL_GROUP_DIM), POOL_GROUP_DIM ** -0.5),
        'pool_scale': 1.0 + nrm(ks[8], (DEPTH, POOL_WIDTH), 0.1),
        'cmp_pe_k': nrm(ks[9], (DEPTH, CMP_BLOCK, C_HEAD_DIM), 0.1),
        'cmp_pe_v': nrm(ks[10], (DEPTH, CMP_BLOCK, C_HEAD_DIM), 0.1),
        'w_cmp_k': nrm(ks[11], (DEPTH, CMP_BLOCK, C_HEAD_DIM, C_HEAD_DIM), (CMP_BLOCK * C_HEAD_DIM) ** -0.5),
        'w_cmp_v': nrm(ks[12], (DEPTH, CMP_BLOCK, C_HEAD_DIM, C_HEAD_DIM), (CMP_BLOCK * C_HEAD_DIM) ** -0.5),
        'w_out': nrm(ks[13], (DEPTH, MIX_WIDTH, D_MODEL), BETA * MIX_WIDTH ** -0.5),
        'ln1_g': 1.0 + nrm(ks[14], (DEPTH, D_MODEL), 0.02),
        'ln1_b': nrm(ks[15], (DEPTH, D_MODEL), 0.02),
        'w_up': nrm(ks[16], (DEPTH, D_MODEL, D_FF), D_MODEL ** -0.5),
        'w_down': nrm(ks[17], (DEPTH, D_FF, D_MODEL), BETA * D_FF ** -0.5),
        'ln2_g': 1.0 + nrm(ks[18], (DEPTH, D_MODEL), 0.02),
        'ln2_b': nrm(ks[19], (DEPTH, D_MODEL), 0.02),
    }


def reference(x, w_in, lam_q1, lam_k1, lam_q2, lam_k2, subln_g, w_pool, pool_scale,
              cmp_pe_k, cmp_pe_v, w_cmp_k, w_cmp_v, w_out, ln1_g, ln1_b,
              w_up, w_down, ln2_g, ln2_b):
    B, T, _ = x.shape
    G, h = C_KV_GROUPS, C_HEADS_PER_GROUP
    cos_a, sin_a = rope_tables(T, A_QK_DIM)
    cos_c, sin_c = rope_tables(T, C_HEAD_DIM)
    offsets = np.cumsum(IN_SPLITS)[:-1].tolist()
    for l in range(DEPTH):
        lam_init = 0.8 - 0.6 * math.exp(-0.3 * l)
        proj = jnp.einsum('btd,dn->btn', x, w_in[l])
        (qa, ka, va, u, qc, kcm, vcm, ksl, vsl, kwi, vwi, gc) = jnp.split(proj, offsets, axis=-1)
        qa = apply_rope(qa.reshape(B, T, A_HEADS, 2, A_QK_DIM).transpose(0, 2, 3, 1, 4), cos_a, sin_a)
        ka = apply_rope(ka.reshape(B, T, A_HEADS, 2, A_QK_DIM).transpose(0, 2, 3, 1, 4), cos_a, sin_a)
        va = va.reshape(B, T, A_HEADS, A_V_DIM).transpose(0, 2, 1, 3)
        lam = (jnp.exp(jnp.sum(lam_q1[l].astype(jnp.float32) * lam_k1[l].astype(jnp.float32)))
               - jnp.exp(jnp.sum(lam_q2[l].astype(jnp.float32) * lam_k2[l].astype(jnp.float32)))
               + lam_init)
        o_a = diff_attention(qa[:, :, 0], qa[:, :, 1], ka[:, :, 0], ka[:, :, 1], va,
                             lam, lam_init, subln_g[l])
        o_b = pool_mixer(u, w_pool[l], pool_scale[l])
        qc = apply_rope(qc.reshape(B, T, G, h, C_HEAD_DIM).transpose(0, 2, 3, 1, 4), cos_c, sin_c)
        kcm = apply_rope(_to_kv(kcm, B, T), cos_c, sin_c)
        ksl = apply_rope(_to_kv(ksl, B, T), cos_c, sin_c)
        kwi = apply_rope(_to_kv(kwi, B, T), cos_c, sin_c)
        gates = jax.nn.sigmoid(gc.astype(jnp.float32)).astype(x.dtype).reshape(B, T, G, h, N_BRANCH)
        o_c = nsa_attention(qc, kcm, _to_kv(vcm, B, T), ksl, _to_kv(vsl, B, T),
                            kwi, _to_kv(vwi, B, T), gates,
                            cmp_pe_k[l], cmp_pe_v[l], w_cmp_k[l], w_cmp_v[l])
        mix = jnp.einsum('btm,md->btd', jnp.concatenate([o_a, o_b, o_c], axis=-1), w_out[l])
        x = layer_norm(ALPHA * x + mix, ln1_g[l], ln1_b[l])
        hid = jnp.square(jax.nn.relu(jnp.einsum('btd,df->btf', x, w_up[l])))
        x = layer_norm(ALPHA * x + jnp.einsum('btf,fd->btd', hid, w_down[l]), ln2_g[l], ln2_b[l])
    return x
```

```python
import functools
import math

import jax
import jax.numpy as jnp
import numpy as np
from jax import lax
from jax.experimental import pallas as pl
from jax.experimental.pallas import tpu as pltpu

A_HEADS = 6
A_QK_DIM = 32
A_V_DIM = 64
A_WIDTH = A_HEADS * A_V_DIM
POOL_WINDOWS = (2, 4, 8, 16)
POOL_GROUP_DIM = 64
POOL_WIDTH = 256
C_HEADS = 6
C_KV_GROUPS = 2
C_HPG = 3
C_HEAD_DIM = 64
C_WIDTH = C_HEADS * C_HEAD_DIM
CMP_BLOCK = 32
CMP_STRIDE = 16
SEL_BLOCK = 64
N_SEL = 16
WINDOW = 512
N_BRANCH = 3
FORCED_SCORE = 1.0e4
ROPE_THETA = 10000.0
LN_EPS = 1e-5
RMS_EPS = 1e-6

LANES = 128
LOG2E = 1.4426950408889634
NEG_BIG = -1e30
VMEM_LIMIT = 56 * 1024 * 1024

BF16 = jnp.bfloat16
F32 = jnp.float32


def _dot(a, b):
    return jnp.dot(a, b, preferred_element_type=F32)


def _params(sem, vmem=VMEM_LIMIT):
    return pltpu.CompilerParams(dimension_semantics=sem, vmem_limit_bytes=vmem)


def _perm_diff():
    idx = np.zeros(A_HEADS * 2 * A_QK_DIM, np.int32)
    for p in range(A_HEADS // 2):
        for l in range(LANES):
            half, grp, j = l // 64, (l % 64) // 16, l % 16
            hl, m = grp // 2, grp % 2
            idx[p * LANES + l] = (2 * p + hl) * 64 + m * 32 + half * 16 + j
    return idx


def _perm_cq():
    idx = np.zeros(C_WIDTH, np.int32)
    for hh in range(C_HPG):
        for l in range(LANES):
            half, g, j = l // 64, (l % 64) // 32, l % 32
            idx[hh * LANES + l] = (g * C_HPG + hh) * 64 + half * 32 + j
    return idx


def _perm_ck():
    idx = np.zeros(C_KV_GROUPS * C_HEAD_DIM, np.int32)
    for l in range(LANES):
        half, g, j = l // 64, (l % 64) // 32, l % 32
        idx[l] = g * 64 + half * 32 + j
    return idx


_IN_SPLITS = (384, 384, 384, 256, 384, 128, 128, 128, 128, 128, 128, 18)
_OFFS = np.concatenate([[0], np.cumsum(_IN_SPLITS)]).astype(np.int64)
A_QSCALE = A_QK_DIM ** -0.5 * LOG2E
C_QSCALE = C_HEAD_DIM ** -0.5 * LOG2E


def _prep_w_in(w):
    o = _OFFS
    pd, pq, pk = _perm_diff(), _perm_cq(), _perm_ck()
    seg = lambda i: w[:, o[i]:o[i + 1]]
    cols = [
        seg(0)[:, pd], seg(1)[:, pd], seg(2), seg(3), seg(4)[:, pq],
        seg(5)[:, pk], seg(6), seg(7)[:, pk], seg(8), seg(9)[:, pk], seg(10),
        jnp.pad(seg(11), ((0, 0), (0, LANES - 18))),
    ]
    return jnp.concatenate(cols, axis=1).astype(BF16)


def _rope_tables(T):
    def tab(dim):
        half = dim // 2
        inv = 1.0 / (ROPE_THETA ** (jnp.arange(0, dim, 2, dtype=F32) / dim))
        ang = jnp.arange(T, dtype=F32)[:, None] * inv[None, :]
        lane = np.arange(LANES)
        j = lane % half
        sign = np.where(lane < 64, -1.0, 1.0).astype(np.float32)
        return jnp.cos(ang)[:, j], jnp.sin(ang)[:, j] * sign[None, :]
    ca, sa = tab(A_QK_DIM)
    cc, sc = tab(C_HEAD_DIM)
    return ca, sa, cc, sc


def _inproj_kernel(x_ref, w_ref, ca_ref, sa_ref, cc_ref, sc_ref,
                   qa_ref, ka_ref, va_ref, u_ref, qc_ref,
                   kcm_ref, vcm_ref, ksl_ref, vsl_ref, kwi_ref, vwi_ref, gt_ref):
    xb = x_ref[...].astype(BF16)

    def proj(c0, n):
        return _dot(xb, w_ref[:, c0:c0 + n])

    def rope(p, cos, sin):
        return p * cos + pltpu.roll(p, 64, axis=1) * sin

    ca, sa, cc, sc = ca_ref[...], sa_ref[...], cc_ref[...], sc_ref[...]
    for t in range(3):
        p = proj(t * LANES, LANES)
        qa_ref[:, t * LANES:(t + 1) * LANES] = (rope(p, ca, sa) * A_QSCALE).astype(BF16)
        p = proj(384 + t * LANES, LANES)
        ka_ref[:, t * LANES:(t + 1) * LANES] = rope(p, ca, sa).astype(BF16)
        p = proj(1408 + t * LANES, LANES)
        qc_ref[:, t * LANES:(t + 1) * LANES] = (rope(p, cc, sc) * C_QSCALE).astype(BF16)
    va_ref[...] = proj(768, 384).astype(BF16)
    u_ref[...] = proj(1152, 256)
    for i, (kr, vr) in enumerate(((kcm_ref, vcm_ref), (ksl_ref, vsl_ref), (kwi_ref, vwi_ref))):
        c0 = 1792 + i * 256
        kr[...] = rope(proj(c0, LANES), cc, sc).astype(BF16)
        vr[...] = proj(c0 + LANES, LANES).astype(BF16)
    gt_ref[...] = jax.nn.sigmoid(proj(2560, LANES))


def _inproj(x2, w_all, tabs, tm, T):
    BT, D = x2.shape
    nT = T // tm
    row = lambda n: pl.BlockSpec((tm, n), lambda i: (i, 0))
    tabspec = pl.BlockSpec((tm, LANES), lambda i: (i % nT, 0))
    outs = [(384, BF16), (384, BF16), (384, BF16), (256, F32), (384, BF16)] + [(LANES, BF16)] * 6 + [(LANES, F32)]
    return pl.pallas_call(
        _inproj_kernel,
        grid=(BT // tm,),
        in_specs=[row(D), pl.BlockSpec(w_all.shape, lambda i: (0, 0))] + [tabspec] * 4,
        out_specs=[row(n) for n, _ in outs],
        out_shape=[jax.ShapeDtypeStruct((BT, n), dt) for n, dt in outs],
        compiler_params=_params(("parallel",)),
        name="inproj_rope",
    )(x2, w_all, *tabs)


def _softmax_step(s, m, l, acc, v, mask):
    if mask is not None:
        s = jnp.where(mask, s, NEG_BIG)
    m_new = jnp.maximum(m, jnp.max(s, axis=0, keepdims=True))
    alpha = jnp.exp2(m - m_new)
    p = jnp.exp2(s - m_new)
    if mask is not None:
        p = jnp.where(mask, p, 0.0)
    l = l * alpha + jnp.sum(p, axis=0, keepdims=True)
    acc = acc * alpha + _dot(v, p.astype(BF16))
    return m_new, l, acc


def _diff_attn_kernel(lq1_ref, lk1_ref, lq2_ref, lk2_ref, g_ref, q_ref, k_ref, v_ref, o_ref,
                      *, tq, tk, lam_init):
    h = pl.program_id(1)
    qi = pl.program_id(2)
    hl = h % 2
    q = q_ref[...]
    grp = (lax.broadcasted_iota(jnp.int32, q.shape, 0) % 64) // 16
    zero = jnp.zeros_like(q)
    q1 = jnp.where(grp == 2 * hl, q, zero)
    q2 = jnp.where(grp == 2 * hl + 1, q, zero)

    def tile(j, carry, diag):
        m1, l1, a1, m2, l2, a2 = carry
        k0 = pl.multiple_of(j * tk, tk)
        k = k_ref[pl.ds(k0, tk), :]
        v = v_ref[:, pl.ds(k0, tk)]
        mask = None
        if diag:
            kpos = k0 + lax.broadcasted_iota(jnp.int32, (tk, tq), 0)
            qpos = qi * tq + lax.broadcasted_iota(jnp.int32, (tk, tq), 1)
            mask = kpos <= qpos
        m1, l1, a1 = _softmax_step(_dot(k, q1), m1, l1, a1, v, mask)
        m2, l2, a2 = _softmax_step(_dot(k, q2), m2, l2, a2, v, mask)
        return m1, l1, a1, m2, l2, a2

    st = lambda: (jnp.full((1, tq), NEG_BIG, F32), jnp.zeros((1, tq), F32), jnp.zeros((A_V_DIM, tq), F32))
    carry = st() + st()
    n_full = (qi * tq) // tk
    carry = lax.fori_loop(0, n_full, lambda j, c: tile(j, c, False), carry)
    n_diag = tq // tk
    for d in range(n_diag):
        carry = tile(n_full + d, carry, True)
    m1, l1, a1, m2, l2, a2 = carry

    dotp = lambda a, b: jnp.sum(a[...] * b[...], axis=1, keepdims=True)
    lam = jnp.exp(dotp(lq1_ref, lk1_ref)) - jnp.exp(dotp(lq2_ref, lk2_ref)) + lam_init
    o = a1 / l1 - lam * (a2 / l2)
    ms = jnp.mean(o * o, axis=0, keepdims=True)
    o = o * lax.rsqrt(ms + RMS_EPS) * g_ref[...] * (1.0 - lam_init)
    o_ref[...] = o.astype(o_ref.dtype)


def _diff_attn(qaT, ka, vaT, lq1, lk1, lq2, lk2, g_col, lam_init, tq, tk):
    B, _, T = qaT.shape
    vec = pl.BlockSpec((1, A_QK_DIM), lambda b, h, i: (0, 0))
    return pl.pallas_call(
        functools.partial(_diff_attn_kernel, tq=tq, tk=tk, lam_init=lam_init),
        grid=(B, A_HEADS, T // tq),
        in_specs=[vec, vec, vec, vec,
                  pl.BlockSpec((A_V_DIM, 1), lambda b, h, i: (0, 0)),
                  pl.BlockSpec((None, LANES, tq), lambda b, h, i: (b, h // 2, i)),
                  pl.BlockSpec((None, T, LANES), lambda b, h, i: (b, 0, h // 2)),
                  pl.BlockSpec((None, A_V_DIM, T), lambda b, h, i: (b, h, 0))],
        out_specs=pl.BlockSpec((None, A_V_DIM, tq), lambda b, h, i: (b, h, i)),
        out_shape=jax.ShapeDtypeStruct((B, A_WIDTH, T), BF16),
        compiler_params=_params(("parallel", "parallel", "arbitrary")),
        name="diff_attn",
    )(lq1, lk1, lq2, lk2, g_col, qaT, ka, vaT)


POOL_HALO = 16


def _pool_kernel(prev_ref, cur_ref, w_ref, scale_ref, o_ref, *, tm):
    i = pl.program_id(1)
    cur = cur_ref[...]
    halo = prev_ref[tm - POOL_HALO:, :]
    halo = jnp.where(i == 0, jnp.zeros_like(halo), halo)
    arr = jnp.concatenate([halo, cur], axis=0)
    s2 = arr + pltpu.roll(arr, 1, axis=0)
    s4 = s2 + pltpu.roll(s2, 2, axis=0)
    s8 = s4 + pltpu.roll(s4, 4, axis=0)
    s16 = s8 + pltpu.roll(s8, 8, axis=0)
    sums = [s[POOL_HALO:, :] for s in (s2, s4, s8, s16)]
    lane = lax.broadcasted_iota(jnp.int32, (tm, POOL_WIDTH), 1)
    tpos = i * tm + lax.broadcasted_iota(jnp.int32, (tm, POOL_WIDTH), 0)
    grp = lane // POOL_GROUP_DIM
    win = jnp.where(grp == 0, sums[0], jnp.where(grp == 1, sums[1], jnp.where(grp == 2, sums[2], sums[3])))
    width = jnp.where(grp == 0, 2, jnp.where(grp == 1, 4, jnp.where(grp == 2, 8, 16)))
    cnt = jnp.minimum(tpos + 1, width).astype(F32)
    y = win / cnt - cur
    z = _dot(y.astype(BF16), w_ref[...])
    o_ref[...] = (z * scale_ref[...]).astype(o_ref.dtype)


def _pool(u3, w_bd, scale, tm):
    B, T, W = u3.shape
    return pl.pallas_call(
        functools.partial(_pool_kernel, tm=tm),
        grid=(B, T // tm),
        in_specs=[pl.BlockSpec((None, tm, W), lambda b, i: (b, jnp.maximum(i - 1, 0), 0)),
                  pl.BlockSpec((None, tm, W), lambda b, i: (b, i, 0)),
                  pl.BlockSpec((W, W), lambda b, i: (0, 0)),
                  pl.BlockSpec((1, W), lambda b, i: (0, 0))],
        out_specs=pl.BlockSpec((None, tm, W), lambda b, i: (b, i, 0)),
        out_shape=jax.ShapeDtypeStruct((B, T, W), BF16),
        compiler_params=_params(("parallel", "arbitrary")),
        name="pool_mixer",
    )(u3, u3, w_bd, scale)


def _cmp_kernel(k_ref, v_ref, pk_ref, pv_ref, wk_ref, wv_ref, kc_ref, vc_ref):
    n = k_ref.shape[0]
    rowid = lax.broadcasted_iota(jnp.int32, (n, LANES), 0)

    def compress(x_ref, pe_ref, w_ref):
        x = x_ref[...].astype(F32)
        lo = _dot((x + pe_ref[0:1, :]).astype(BF16), w_ref[0])
        hi = _dot((x + pe_ref[1:2, :]).astype(BF16), w_ref[1])
        out = lo + pltpu.roll(hi, n - 1, axis=0)
        return jnp.where(rowid < n - 1, out, 0.0)

    kc_ref[...] = compress(k_ref, pk_ref, wk_ref).astype(kc_ref.dtype)
    vc_ref[...] = compress(v_ref, pv_ref, wv_ref).astype(vc_ref.dtype)


def _compress(kr, vr, pe_k2, pe_v2, wk2, wv2):
    B, n, W = kr.shape
    blk = pl.BlockSpec((None, n, W), lambda b: (b, 0, 0))
    pe = pl.BlockSpec((2, W), lambda b: (0, 0))
    wsp = pl.BlockSpec((2, W, LANES), lambda b: (0, 0, 0))
    osp = pl.BlockSpec((None, n, LANES), lambda b: (b, 0, 0))
    return pl.pallas_call(
        _cmp_kernel,
        grid=(B,),
        in_specs=[blk, blk, pe, pe, wsp, wsp],
        out_specs=[osp, osp],
        out_shape=[jax.ShapeDtypeStruct((B, n, LANES), BF16)] * 2,
        compiler_params=_params(("parallel",)),
        name="nsa_compress",
    )(kr, vr, pe_k2, pe_v2, wk2, wv2)


def _prep_cmp(pe_k, pe_v, w_ck, w_cv):
    pk = _perm_ck()
    lane = np.arange(LANES)
    g_of_k = (lane % 64) // 32
    d_of_k = pk % 64
    g_of_v = lane // 64
    d_of_v = lane % 64

    def build(w, pe, g_of, d_of):
        wl = w[:, d_of, :]
        onehot = jnp.asarray((g_of[:, None] == np.arange(C_KV_GROUPS)[None, :]).astype(np.float32))
        w2 = wl[:, :, None, :] * onehot[None, :, :, None]
        w2 = w2.reshape(2, 16 * LANES, C_KV_GROUPS * C_HEAD_DIM).astype(BF16)
        pe2 = pe[:, d_of].reshape(2, 16 * LANES)
        return w2, pe2

    wk2, pk2 = build(w_ck, pe_k, g_of_k, d_of_k)
    wv2, pv2 = build(w_cv, pe_v, g_of_v, d_of_v)
    return pk2, pv2, wk2[:, :, pk], wv2


def _nsa_kernel(q_ref, gt_ref, kc_ref, vc_ref, ov_ref, ks_ref, vs_ref, kw_ref, vw_ref, o_ref, sel_ref,
                *, tq, tk, T):
    g = pl.program_id(1)
    qi = pl.program_id(2)
    q0 = qi * tq
    n_cmp = kc_ref.shape[0]
    n_blk = T // SEL_BLOCK
    d = C_HEAD_DIM
    tpos = q0 + lax.broadcasted_iota(jnp.int32, (1, tq), 1)

    rowg = (lax.broadcasted_iota(jnp.int32, (LANES, tq), 0) % 64) // 32
    qs = []
    for hh in range(C_HPG):
        qt = q_ref[hh * LANES:(hh + 1) * LANES, :]
        qs.append(jnp.where(rowg == g, qt, jnp.zeros_like(qt)))

    kc = kc_ref[...]
    vc = vc_ref[...]
    cend = lax.broadcasted_iota(jnp.int32, (n_cmp, tq), 0) * CMP_STRIDE + (CMP_BLOCK - 1)
    cmask = cend <= tpos
    psum = jnp.zeros((n_cmp, tq), F32)
    o_cmp = []
    for hh in range(C_HPG):
        s = jnp.where(cmask, _dot(kc, qs[hh]), NEG_BIG)
        m = jnp.max(s, axis=0, keepdims=True)
        e = jnp.where(cmask, jnp.exp2(s - m), 0.0)
        l = jnp.sum(e, axis=0, keepdims=True)
        p = e / jnp.where(l > 0.0, l, 1.0)
        psum = psum + p
        o_cmp.append(_dot(vc, p.astype(BF16)))

    p_hi = psum.astype(BF16)
    p_lo = (psum - p_hi.astype(F32)).astype(BF16)
    imp = _dot(ov_ref[...], p_hi) + _dot(ov_ref[...], p_lo)
    jidx = lax.broadcasted_iota(jnp.int32, (n_blk, tq), 0).astype(F32)
    cur = (tpos // SEL_BLOCK).astype(F32)
    forced = (jidx == 0.0) | (jidx == cur) | (jidx == cur - 1.0)
    imp = jnp.where(forced, FORCED_SCORE, imp)
    imp = jnp.where(jidx > cur, -1.0, imp)
    sel = jnp.zeros((n_blk, tq), F32)
    for _ in range(min(N_SEL, n_blk)):
        mx = jnp.max(imp, axis=0, keepdims=True)
        first = jnp.min(jnp.where(imp == mx, jidx, float(n_blk)), axis=0, keepdims=True)
        hit = jidx == first
        sel = jnp.where(hit, 1.0, sel)
        imp = jnp.where(hit, -3.0e38, imp)
    sel_ref[...] = sel

    bpt = tk // SEL_BLOCK

    def sel_tile(j, carry, causal):
        k0 = pl.multiple_of(j * tk, tk)
        k = ks_ref[pl.ds(k0, tk), :]
        v = vs_ref[:, pl.ds(k0, tk)]
        b0 = pl.multiple_of(j * bpt, bpt)
        rows = sel_ref[pl.ds(b0, bpt), :]
        selx = jnp.concatenate(
            [jnp.broadcast_to(rows[i:i + 1, :], (SEL_BLOCK, tq)) for i in range(bpt)], axis=0)
        mask = selx > 0.5
        if causal:
            kpos = k0 + lax.broadcasted_iota(jnp.int32, (tk, tq), 0)
            mask = mask & (kpos <= tpos)
        out = []
        for hh in range(C_HPG):
            m, l, a = carry[hh]
            out.append(_softmax_step(_dot(k, qs[hh]), m, l, a, v, mask))
        return tuple(out)

    st = lambda: (jnp.full((1, tq), NEG_BIG, F32), jnp.zeros((1, tq), F32), jnp.zeros((d, tq), F32))
    carry = tuple(st() for _ in range(C_HPG))
    n_full = q0 // tk
    carry = lax.fori_loop(0, n_full, lambda j, c: sel_tile(j, c, False), carry)
    n_last = pl.cdiv(tq, tk) if tq >= tk else 1
    for t in range(n_last):
        carry = sel_tile(n_full + t, carry, True)
    o_slc = [a / l for (m, l, a) in carry]

    band = WINDOW + tq
    w0 = pl.multiple_of(jnp.maximum(q0 - WINDOW, 0), LANES)
    kw = kw_ref[pl.ds(w0, band), :]
    vw = vw_ref[:, pl.ds(w0, band)]
    wpos = w0 + lax.broadcasted_iota(jnp.int32, (band, tq), 0)
    wmask = (wpos <= tpos) & (wpos > tpos - WINDOW)
    o_win = []
    for hh in range(C_HPG):
        s = jnp.where(wmask, _dot(kw, qs[hh]), NEG_BIG)
        m = jnp.max(s, axis=0, keepdims=True)
        e = jnp.where(wmask, jnp.exp2(s - m), 0.0)
        l = jnp.sum(e, axis=0, keepdims=True)
        o_win.append(_dot(vw, e.astype(BF16)) / l)

    gts = gt_ref[...]
    for hh in range(C_HPG):
        r = hh * N_BRANCH
        out = (o_cmp[hh] * gts[r:r + 1, :] + o_slc[hh] * gts[r + 1:r + 2, :]
               + o_win[hh] * gts[r + 2:r + 3, :])
        o_ref[hh * d:(hh + 1) * d, :] = out.astype(o_ref.dtype)


def _nsa(qcT, gtT, kc2, vcT, ovT, ksl, vslT, kwi, vwiT, tq, tk):
    B, _, T = qcT.shape
    n_cmp = kc2.shape[1]
    n_blk = T // SEL_BLOCK
    kv_tok = pl.BlockSpec((None, T, LANES), lambda b, g, i: (b, 0, 0))
    kv_feat = pl.BlockSpec((None, C_HEAD_DIM, T), lambda b, g, i: (b, g, 0))
    return pl.pallas_call(
        functools.partial(_nsa_kernel, tq=tq, tk=tk, T=T),
        grid=(B, C_KV_GROUPS, T // tq),
        in_specs=[pl.BlockSpec((None, C_WIDTH, tq), lambda b, g, i: (b, 0, i)),
                  pl.BlockSpec((None, None, GATE_ROWS, tq), lambda b, g, i: (b, g, 0, i)),
                  pl.BlockSpec((None, n_cmp, LANES), lambda b, g, i: (b, 0, 0)),
                  pl.BlockSpec((None, C_HEAD_DIM, n_cmp), lambda b, g, i: (b, g, 0)),
                  pl.BlockSpec((n_blk, n_cmp), lambda b, g, i: (0, 0)),
                  kv_tok, kv_feat, kv_tok, kv_feat],
        out_specs=pl.BlockSpec((None, C_HPG * C_HEAD_DIM, tq), lambda b, g, i: (b, g, i)),
        out_shape=jax.ShapeDtypeStruct((B, C_WIDTH, T), BF16),
        scratch_shapes=[pltpu.VMEM((n_blk, tq), F32)],
        compiler_params=_params(("parallel", "parallel", "arbitrary")),
        name="nsa_attn",
    )(qcT, gtT, kc2, vcT, ovT, ksl, vslT, kwi, vwiT)


def _overlap_T(T):
    n_cmp = T // CMP_STRIDE
    n_blk = T // SEL_BLOCK
    cs = np.arange(n_cmp)[None, :] * CMP_STRIDE
    bs = np.arange(n_blk)[:, None] * SEL_BLOCK
    ov = (cs < bs + SEL_BLOCK) & (cs + CMP_BLOCK > bs) & (np.arange(n_cmp)[None, :] < n_cmp - 1)
    return jnp.asarray(ov.astype(np.float32), dtype=BF16)


def _layer_norm(y, g, b):
    mu = jnp.mean(y, axis=-1, keepdims=True)
    yc = y - mu
    var = jnp.mean(yc * yc, axis=-1, keepdims=True)
    return yc * lax.rsqrt(var + LN_EPS) * g + b


def _outproj_kernel(x_ref, oa_ref, ob_ref, oc_ref, wa_ref, wb_ref, wc_ref, g_ref, b_ref, o_ref, *, alpha):
    mix = _dot(oa_ref[...], wa_ref[...]) + _dot(ob_ref[...], wb_ref[...]) + _dot(oc_ref[...], wc_ref[...])
    o_ref[...] = _layer_norm(alpha * x_ref[...] + mix, g_ref[...], b_ref[...])


def _outproj(x2, oa, ob, oc, wa, wb, wc, g, b, alpha, tm):
    BT, D = x2.shape
    row = lambda n: pl.BlockSpec((tm, n), lambda i: (i, 0))
    full = lambda a: pl.BlockSpec(a.shape, lambda i: (0, 0))
    return pl.pallas_call(
        functools.partial(_outproj_kernel, alpha=alpha),
        grid=(BT // tm,),
        in_specs=[row(D), row(oa.shape[1]), row(ob.shape[1]), row(oc.shape[1]),
                  full(wa), full(wb), full(wc), full(g), full(b)],
        out_specs=row(D),
        out_shape=jax.ShapeDtypeStruct((BT, D), F32),
        compiler_params=_params(("parallel",)),
        name="outproj_ln",
    )(x2, oa, ob, oc, wa, wb, wc, g, b)


def _mlp_kernel(x_ref, wu_ref, wd_ref, g_ref, b_ref, o_ref, acc_ref, *, alpha):
    f = pl.program_id(1)

    @pl.when(f == 0)
    def _():
        acc_ref[...] = jnp.zeros_like(acc_ref)

    hid = _dot(x_ref[...].astype(BF16), wu_ref[...])
    hid = jnp.square(jnp.maximum(hid, 0.0))
    acc_ref[...] += _dot(hid.astype(BF16), wd_ref[...])

    @pl.when(f == pl.num_programs(1) - 1)
    def _():
        o_ref[...] = _layer_norm(alpha * x_ref[...] + acc_ref[...], g_ref[...], b_ref[...])


def _mlp(x2, wu, wd, g, b, alpha, tm, tf):
    BT, D = x2.shape
    F = wu.shape[1]
    return pl.pallas_call(
        functools.partial(_mlp_kernel, alpha=alpha),
        grid=(BT // tm, F // tf),
        in_specs=[pl.BlockSpec((tm, D), lambda i, f: (i, 0)),
                  pl.BlockSpec((D, tf), lambda i, f: (0, f)),
                  pl.BlockSpec((tf, D), lambda i, f: (f, 0)),
                  pl.BlockSpec((1, D), lambda i, f: (0, 0)),
                  pl.BlockSpec((1, D), lambda i, f: (0, 0))],
        out_specs=pl.BlockSpec((tm, D), lambda i, f: (i, 0)),
        out_shape=jax.ShapeDtypeStruct((BT, D), F32),
        scratch_shapes=[pltpu.VMEM((tm, D), F32)],
        compiler_params=_params(("parallel", "arbitrary")),
        name="mlp_ln",
    )(x2, wu, wd, g, b)


def _tiles(T):
    return dict(
        tm_proj=min(512, T), tm_pool=min(512, T), tm_out=min(512, T), tm_mlp=min(1024, T), tf_mlp=1024,
        a_tq=min(256, T), a_tk=min(256, T), c_tq=min(256, T), c_tk=min(512, T),
    )


def _to_feat(a, B, T):
    return a.reshape(B, T, a.shape[-1]).transpose(0, 2, 1)


GATE_ROWS = 16


def _gates_feat(gt, B, T):
    per = C_HPG * N_BRANCH
    g4 = gt[:, :C_KV_GROUPS * per].reshape(B, T, C_KV_GROUPS, per)
    g4 = jnp.pad(g4, ((0, 0), (0, 0), (0, 0), (0, GATE_ROWS - per)))
    return g4.transpose(0, 2, 3, 1)


def _to_tok(a):
    B, n, T = a.shape
    return a.transpose(0, 2, 1).reshape(B * T, n)


def kernel(x, w_in, lam_q1, lam_k1, lam_q2, lam_k2, subln_g, w_pool, pool_scale, cmp_pe_k, cmp_pe_v,
           w_cmp_k, w_cmp_v, w_out, ln1_g, ln1_b, w_up, w_down, ln2_g, ln2_b):
    B, T, D = x.shape
    depth = w_in.shape[0]
    alpha = (2 * depth) ** 0.25
    tl = _tiles(T)
    tabs = _rope_tables(T)
    ovT = _overlap_T(T)
    x2 = x.reshape(B * T, D)
    for l in range(depth):
        lam_init = 0.8 - 0.6 * math.exp(-0.3 * l)
        (qa, ka, va, u, qc, kcm, vcm, ksl, vsl, kwi, vwi, gt) = _inproj(
            x2, _prep_w_in(w_in[l]), tabs, tl["tm_proj"], T)

        oaT = _diff_attn(_to_feat(qa, B, T), ka.reshape(B, T, -1), _to_feat(va, B, T),
                         lam_q1[l][None, :], lam_k1[l][None, :], lam_q2[l][None, :], lam_k2[l][None, :],
                         subln_g[l][:, None], lam_init, tl["a_tq"], tl["a_tk"])

        w_bd = jax.scipy.linalg.block_diag(*[w_pool[l, g] for g in range(len(POOL_WINDOWS))]).astype(BF16)
        ob = _pool(u.reshape(B, T, POOL_WIDTH), w_bd, pool_scale[l][None, :], tl["tm_pool"])

        pk2, pv2, wk2, wv2 = _prep_cmp(cmp_pe_k[l], cmp_pe_v[l], w_cmp_k[l], w_cmp_v[l])
        n16 = T // CMP_STRIDE
        kc2, vc2 = _compress(kcm.reshape(B, n16, 16 * LANES), vcm.reshape(B, n16, 16 * LANES),
                             pk2, pv2, wk2, wv2)
        ocT = _nsa(_to_feat(qc, B, T), _gates_feat(gt, B, T), kc2, vc2.transpose(0, 2, 1), ovT,
                   ksl.reshape(B, T, LANES), _to_feat(vsl, B, T), kwi.reshape(B, T, LANES),
                   _to_feat(vwi, B, T), tl["c_tq"], tl["c_tk"])

        wo = w_out[l].astype(BF16)
        x2 = _outproj(x2, _to_tok(oaT), ob.reshape(B * T, POOL_WIDTH), _to_tok(ocT),
                      wo[:A_WIDTH], wo[A_WIDTH:A_WIDTH + POOL_WIDTH], wo[A_WIDTH + POOL_WIDTH:],
                      ln1_g[l][None, :], ln1_b[l][None, :], alpha, tl["tm_out"])
        x2 = _mlp(x2, w_up[l].astype(BF16), w_down[l].astype(BF16), ln2_g[l][None, :], ln2_b[l][None, :],
                  alpha, tl["tm_mlp"], tl["tf_mlp"])
    return x2.reshape(B, T, D)
```

```python
import functools
import math

import jax
import jax.numpy as jnp
import numpy as np
from jax import lax
from jax.experimental import pallas as pl
from jax.experimental.pallas import tpu as pltpu

A_HEADS = 6
A_QK_DIM = 32
A_V_DIM = 64
A_WIDTH = A_HEADS * A_V_DIM
POOL_WINDOWS = (2, 4, 8, 16)
POOL_GROUP_DIM = 64
POOL_WIDTH = 256
C_HEADS = 6
C_KV_GROUPS = 2
C_HPG = 3
C_HEAD_DIM = 64
C_WIDTH = C_HEADS * C_HEAD_DIM
CMP_BLOCK = 32
CMP_STRIDE = 16
SEL_BLOCK = 64
N_SEL = 16
WINDOW = 512
N_BRANCH = 3
FORCED_SCORE = 1.0e4
ROPE_THETA = 10000.0
LN_EPS = 1e-5
RMS_EPS = 1e-6

LANES = 128
LOG2E = 1.4426950408889634
NEG_BIG = -1e30
MASKED = -3e38
VMEM_LIMIT = 56 * 1024 * 1024

BF16 = jnp.bfloat16
F32 = jnp.float32


def _dot(a, b):
    return jnp.dot(a, b, preferred_element_type=F32)


def _params(sem, vmem=VMEM_LIMIT):
    return pltpu.CompilerParams(dimension_semantics=sem, vmem_limit_bytes=vmem)


def _perm_diff():
    idx = np.zeros(A_HEADS * 2 * A_QK_DIM, np.int32)
    for p in range(A_HEADS // 2):
        for l in range(LANES):
            half, grp, j = l // 64, (l % 64) // 16, l % 16
            hl, m = grp // 2, grp % 2
            idx[p * LANES + l] = (2 * p + hl) * 64 + m * 32 + half * 16 + j
    return idx


def _perm_cq():
    idx = np.zeros(C_WIDTH, np.int32)
    for hh in range(C_HPG):
        for l in range(LANES):
            half, g, j = l // 64, (l % 64) // 32, l % 32
            idx[hh * LANES + l] = (g * C_HPG + hh) * 64 + half * 32 + j
    return idx


def _perm_ck():
    idx = np.zeros(C_KV_GROUPS * C_HEAD_DIM, np.int32)
    for l in range(LANES):
        half, g, j = l // 64, (l % 64) // 32, l % 32
        idx[l] = g * 64 + half * 32 + j
    return idx


_IN_SPLITS = (384, 384, 384, 256, 384, 128, 128, 128, 128, 128, 128, 18)
_OFFS = np.concatenate([[0], np.cumsum(_IN_SPLITS)]).astype(np.int64)
A_QSCALE = A_QK_DIM ** -0.5 * LOG2E
C_QSCALE = C_HEAD_DIM ** -0.5 * LOG2E


def _prep_w_in(w):
    o = _OFFS
    pd, pq, pk = _perm_diff(), _perm_cq(), _perm_ck()
    seg = lambda i: w[:, o[i]:o[i + 1]]
    cols = [
        seg(0)[:, pd], seg(1)[:, pd], seg(2), seg(3), seg(4)[:, pq],
        seg(5)[:, pk], seg(6), seg(7)[:, pk], seg(8), seg(9)[:, pk], seg(10),
        jnp.pad(seg(11), ((0, 0), (0, LANES - 18))),
    ]
    return jnp.concatenate(cols, axis=1).astype(BF16)


def _rope_tables(T):
    def tab(dim):
        half = dim // 2
        inv = 1.0 / (ROPE_THETA ** (jnp.arange(0, dim, 2, dtype=F32) / dim))
        ang = jnp.arange(T, dtype=F32)[:, None] * inv[None, :]
        lane = np.arange(LANES)
        j = lane % half
        sign = np.where(lane < 64, -1.0, 1.0).astype(np.float32)
        return jnp.cos(ang)[:, j], jnp.sin(ang)[:, j] * sign[None, :]
    ca, sa = tab(A_QK_DIM)
    cc, sc = tab(C_HEAD_DIM)
    return ca, sa, cc, sc


def _inproj_kernel(x_ref, w_ref, ca_ref, sa_ref, cc_ref, sc_ref,
                   qa_ref, ka_ref, va_ref, u_ref, qc_ref,
                   kcm_ref, vcm_ref, ksl_ref, vsl_ref, kwi_ref, vwi_ref, gt_ref):
    xb = x_ref[...].astype(BF16)

    def proj(c0, n):
        return _dot(xb, w_ref[:, c0:c0 + n])

    def rope(p, cos, sin):
        return p * cos + pltpu.roll(p, 64, axis=1) * sin

    ca, sa, cc, sc = ca_ref[...], sa_ref[...], cc_ref[...], sc_ref[...]
    for t in range(3):
        p = proj(t * LANES, LANES)
        qa_ref[:, t * LANES:(t + 1) * LANES] = (rope(p, ca, sa) * A_QSCALE).astype(BF16)
        p = proj(384 + t * LANES, LANES)
        ka_ref[:, t * LANES:(t + 1) * LANES] = rope(p, ca, sa).astype(BF16)
        p = proj(1408 + t * LANES, LANES)
        qc_ref[:, t * LANES:(t + 1) * LANES] = (rope(p, cc, sc) * C_QSCALE).astype(BF16)
    va_ref[...] = proj(768, 384).astype(BF16)
    u_ref[...] = proj(1152, 256)
    for i, (kr, vr) in enumerate(((kcm_ref, vcm_ref), (ksl_ref, vsl_ref), (kwi_ref, vwi_ref))):
        c0 = 1792 + i * 256
        kr[...] = rope(proj(c0, LANES), cc, sc).astype(BF16)
        vr[...] = proj(c0 + LANES, LANES).astype(BF16)
    gt_ref[...] = jax.nn.sigmoid(proj(2560, LANES))


def _inproj(x2, w_all, tabs, tm, T):
    BT, D = x2.shape
    nT = T // tm
    row = lambda n: pl.BlockSpec((tm, n), lambda i: (i, 0))
    tabspec = pl.BlockSpec((tm, LANES), lambda i: (i % nT, 0))
    outs = [(384, BF16), (384, BF16), (384, BF16), (256, F32), (384, BF16)] + [(LANES, BF16)] * 6 + [(LANES, F32)]
    return pl.pallas_call(
        _inproj_kernel,
        grid=(BT // tm,),
        in_specs=[row(D), pl.BlockSpec(w_all.shape, lambda i: (0, 0))] + [tabspec] * 4,
        out_specs=[row(n) for n, _ in outs],
        out_shape=[jax.ShapeDtypeStruct((BT, n), dt) for n, dt in outs],
        compiler_params=_params(("parallel",)),
        name="inproj_rope",
    )(x2, w_all, *tabs)


def _softmax_step(s, m, l, acc, v, mask):
    if mask is not None:
        s = jnp.where(mask, s, MASKED)
    m_new = jnp.maximum(m, jnp.max(s, axis=0, keepdims=True))
    alpha = jnp.exp2(m - m_new)
    p = jnp.exp2(s - m_new)
    l = l * alpha + jnp.sum(p, axis=0, keepdims=True)
    acc = acc * alpha + _dot(v, p.astype(BF16))
    return m_new, l, acc


def _diff_attn_kernel(lq1_ref, lk1_ref, lq2_ref, lk2_ref, g_ref, q_ref, k_ref, v_ref, o_ref,
                      *, tq, tk, lam_init):
    h = pl.program_id(1)
    qi = pl.program_id(2)
    hl = h % 2
    q = q_ref[...]
    grp = (lax.broadcasted_iota(jnp.int32, q.shape, 0) % 64) // 16
    zero = jnp.zeros_like(q)
    q12 = jnp.concatenate([jnp.where(grp == 2 * hl, q, zero), jnp.where(grp == 2 * hl + 1, q, zero)], axis=1)
    nq = 2 * tq
    qpos = qi * tq + lax.broadcasted_iota(jnp.int32, (1, nq), 1) % tq

    def tile(j, carry, diag):
        k0 = pl.multiple_of(j * tk, tk)
        k = k_ref[pl.ds(k0, tk), :]
        v = v_ref[:, pl.ds(k0, tk)]
        mask = None
        if diag:
            mask = k0 + lax.broadcasted_iota(jnp.int32, (tk, nq), 0) <= qpos
        return _softmax_step(_dot(k, q12), *carry, v, mask)

    carry = (jnp.full((1, nq), NEG_BIG, F32), jnp.zeros((1, nq), F32), jnp.zeros((A_V_DIM, nq), F32))
    n_full = (qi * tq) // tk
    carry = lax.fori_loop(0, n_full, lambda j, c: tile(j, c, False), carry)
    for d in range(tq // tk):
        carry = tile(n_full + d, carry, True)
    m, l, acc = carry
    on = acc / l

    dotp = lambda a, b: jnp.sum(a[...] * b[...], axis=1, keepdims=True)
    lam = jnp.exp(dotp(lq1_ref, lk1_ref)) - jnp.exp(dotp(lq2_ref, lk2_ref)) + lam_init
    o = on[:, :tq] - lam * on[:, tq:]
    ms = jnp.mean(o * o, axis=0, keepdims=True)
    o = o * lax.rsqrt(ms + RMS_EPS) * g_ref[...] * (1.0 - lam_init)
    o_ref[...] = o.astype(o_ref.dtype)


def _diff_attn(qaT, ka, vaT, lq1, lk1, lq2, lk2, g_col, lam_init, tq, tk):
    B, _, T = qaT.shape
    vec = pl.BlockSpec((1, A_QK_DIM), lambda b, h, i: (0, 0))
    return pl.pallas_call(
        functools.partial(_diff_attn_kernel, tq=tq, tk=tk, lam_init=lam_init),
        grid=(B, A_HEADS, T // tq),
        in_specs=[vec, vec, vec, vec,
                  pl.BlockSpec((A_V_DIM, 1), lambda b, h, i: (0, 0)),
                  pl.BlockSpec((None, LANES, tq), lambda b, h, i: (b, h // 2, i)),
                  pl.BlockSpec((None, T, LANES), lambda b, h, i: (b, 0, h // 2)),
                  pl.BlockSpec((None, A_V_DIM, T), lambda b, h, i: (b, h, 0))],
        out_specs=pl.BlockSpec((None, A_V_DIM, tq), lambda b, h, i: (b, h, i)),
        out_shape=jax.ShapeDtypeStruct((B, A_WIDTH, T), BF16),
        compiler_params=_params(("parallel", "parallel", "arbitrary")),
        name="diff_attn",
    )(lq1, lk1, lq2, lk2, g_col, qaT, ka, vaT)


POOL_HALO = 16


def _pool_kernel(prev_ref, cur_ref, w_ref, scale_ref, o_ref, *, tm):
    i = pl.program_id(1)
    cur = cur_ref[...]
    halo = prev_ref[tm - POOL_HALO:, :]
    halo = jnp.where(i == 0, jnp.zeros_like(halo), halo)
    arr = jnp.concatenate([halo, cur], axis=0)
    s2 = arr + pltpu.roll(arr, 1, axis=0)
    s4 = s2 + pltpu.roll(s2, 2, axis=0)
    s8 = s4 + pltpu.roll(s4, 4, axis=0)
    s16 = s8 + pltpu.roll(s8, 8, axis=0)
    sums = [s[POOL_HALO:, :] for s in (s2, s4, s8, s16)]
    lane = lax.broadcasted_iota(jnp.int32, (tm, POOL_WIDTH), 1)
    tpos = i * tm + lax.broadcasted_iota(jnp.int32, (tm, POOL_WIDTH), 0)
    grp = lane // POOL_GROUP_DIM
    win = jnp.where(grp == 0, sums[0], jnp.where(grp == 1, sums[1], jnp.where(grp == 2, sums[2], sums[3])))
    width = jnp.where(grp == 0, 2, jnp.where(grp == 1, 4, jnp.where(grp == 2, 8, 16)))
    cnt = jnp.minimum(tpos + 1, width).astype(F32)
    y = win / cnt - cur
    z = _dot(y.astype(BF16), w_ref[...])
    o_ref[...] = (z * scale_ref[...]).astype(o_ref.dtype)


def _pool(u3, w_bd, scale, tm):
    B, T, W = u3.shape
    return pl.pallas_call(
        functools.partial(_pool_kernel, tm=tm),
        grid=(B, T // tm),
        in_specs=[pl.BlockSpec((None, tm, W), lambda b, i: (b, jnp.maximum(i - 1, 0), 0)),
                  pl.BlockSpec((None, tm, W), lambda b, i: (b, i, 0)),
                  pl.BlockSpec((W, W), lambda b, i: (0, 0)),
                  pl.BlockSpec((1, W), lambda b, i: (0, 0))],
        out_specs=pl.BlockSpec((None, tm, W), lambda b, i: (b, i, 0)),
        out_shape=jax.ShapeDtypeStruct((B, T, W), BF16),
        compiler_params=_params(("parallel", "arbitrary")),
        name="pool_mixer",
    )(u3, u3, w_bd, scale)


def _cmp_kernel(k_ref, v_ref, pk_ref, pv_ref, wk_ref, wv_ref, kc_ref, vc_ref):
    n = k_ref.shape[0]
    rowid = lax.broadcasted_iota(jnp.int32, (n, LANES), 0)

    def compress(x_ref, pe_ref, w_ref):
        x = x_ref[...].astype(F32)
        lo = _dot((x + pe_ref[0:1, :]).astype(BF16), w_ref[0])
        hi = _dot((x + pe_ref[1:2, :]).astype(BF16), w_ref[1])
        out = lo + pltpu.roll(hi, n - 1, axis=0)
        return jnp.where(rowid < n - 1, out, 0.0)

    kc_ref[...] = compress(k_ref, pk_ref, wk_ref).astype(kc_ref.dtype)
    vc_ref[...] = compress(v_ref, pv_ref, wv_ref).astype(vc_ref.dtype)


def _compress(kr, vr, pe_k2, pe_v2, wk2, wv2):
    B, n, W = kr.shape
    blk = pl.BlockSpec((None, n, W), lambda b: (b, 0, 0))
    pe = pl.BlockSpec((2, W), lambda b: (0, 0))
    wsp = pl.BlockSpec((2, W, LANES), lambda b: (0, 0, 0))
    osp = pl.BlockSpec((None, n, LANES), lambda b: (b, 0, 0))
    return pl.pallas_call(
        _cmp_kernel,
        grid=(B,),
        in_specs=[blk, blk, pe, pe, wsp, wsp],
        out_specs=[osp, osp],
        out_shape=[jax.ShapeDtypeStruct((B, n, LANES), BF16)] * 2,
        compiler_params=_params(("parallel",)),
        name="nsa_compress",
    )(kr, vr, pe_k2, pe_v2, wk2, wv2)


def _prep_cmp(pe_k, pe_v, w_ck, w_cv):
    pk = _perm_ck()
    lane = np.arange(LANES)
    g_of_k = (lane % 64) // 32
    d_of_k = pk % 64
    g_of_v = lane // 64
    d_of_v = lane % 64

    def build(w, pe, g_of, d_of):
        wl = w[:, d_of, :]
        onehot = jnp.asarray((g_of[:, None] == np.arange(C_KV_GROUPS)[None, :]).astype(np.float32))
        w2 = wl[:, :, None, :] * onehot[None, :, :, None]
        w2 = w2.reshape(2, 16 * LANES, C_KV_GROUPS * C_HEAD_DIM).astype(BF16)
        pe2 = pe[:, d_of].reshape(2, 16 * LANES)
        return w2, pe2

    wk2, pk2 = build(w_ck, pe_k, g_of_k, d_of_k)
    wv2, pv2 = build(w_cv, pe_v, g_of_v, d_of_v)
    return pk2, pv2, wk2[:, :, pk], wv2


def _nsa_kernel(q_ref, gt_ref, kc_ref, vc_ref, ov_ref, ks_ref, vs_ref, kw_ref, vw_ref, o_ref, sel_ref,
                *, tq, tk, T):
    g = pl.program_id(1)
    qi = pl.program_id(2)
    q0 = qi * tq
    n_cmp = kc_ref.shape[0]
    n_blk = T // SEL_BLOCK
    d = C_HEAD_DIM
    tpos = q0 + lax.broadcasted_iota(jnp.int32, (1, tq), 1)

    nq = C_HPG * tq
    tpos3 = q0 + lax.broadcasted_iota(jnp.int32, (1, nq), 1) % tq
    rowg = (lax.broadcasted_iota(jnp.int32, (LANES, tq), 0) % 64) // 32
    qs = []
    for hh in range(C_HPG):
        qt = q_ref[hh * LANES:(hh + 1) * LANES, :]
        qs.append(jnp.where(rowg == g, qt, jnp.zeros_like(qt)))
    q3 = jnp.concatenate(qs, axis=1)

    cend = lax.broadcasted_iota(jnp.int32, (n_cmp, nq), 0) * CMP_STRIDE + (CMP_BLOCK - 1)
    s = jnp.where(cend <= tpos3, _dot(kc_ref[...], q3), MASKED)
    m = jnp.maximum(jnp.max(s, axis=0, keepdims=True), NEG_BIG)
    e = jnp.exp2(s - m)
    l = jnp.sum(e, axis=0, keepdims=True)
    p = e / jnp.where(l > 0.0, l, 1.0)
    o_cmp = _dot(vc_ref[...], p.astype(BF16))
    psum = p[:, :tq] + p[:, tq:2 * tq] + p[:, 2 * tq:]

    p_hi = psum.astype(BF16)
    p_lo = (psum - p_hi.astype(F32)).astype(BF16)
    imp = _dot(ov_ref[...], p_hi) + _dot(ov_ref[...], p_lo)
    jidx = lax.broadcasted_iota(jnp.int32, (n_blk, tq), 0).astype(F32)
    cur = (tpos // SEL_BLOCK).astype(F32)
    forced = (jidx == 0.0) | (jidx == cur) | (jidx == cur - 1.0)
    imp = jnp.where(forced, FORCED_SCORE, imp)
    imp = jnp.where(jidx > cur, -1.0, imp)
    bias = jnp.full((n_blk, tq), MASKED, F32)
    for _ in range(min(N_SEL, n_blk)):
        mx = jnp.max(imp, axis=0, keepdims=True)
        first = jnp.min(jnp.where(imp == mx, jidx, float(n_blk)), axis=0, keepdims=True)
        hit = jidx == first
        bias = jnp.where(hit, 0.0, bias)
        imp = jnp.where(hit, -3.0e38, imp)
    sel_ref[...] = jnp.concatenate([bias] * C_HPG, axis=1)

    bpt = tk // SEL_BLOCK

    def sel_tile(j, carry, causal):
        k0 = pl.multiple_of(j * tk, tk)
        k = ks_ref[pl.ds(k0, tk), :]
        v = vs_ref[:, pl.ds(k0, tk)]
        b0 = pl.multiple_of(j * bpt, bpt)
        rows = sel_ref[pl.ds(b0, bpt), :]
        s = _dot(k, q3) + jnp.concatenate(
            [jnp.broadcast_to(rows[i:i + 1, :], (SEL_BLOCK, nq)) for i in range(bpt)], axis=0)
        if causal:
            s = jnp.where(k0 + lax.broadcasted_iota(jnp.int32, (tk, nq), 0) <= tpos3, s, MASKED)
        return _softmax_step(s, *carry, v, None)

    carry = (jnp.full((1, nq), NEG_BIG, F32), jnp.zeros((1, nq), F32), jnp.zeros((d, nq), F32))
    n_full = q0 // tk
    carry = lax.fori_loop(0, n_full, lambda j, c: sel_tile(j, c, False), carry)
    n_last = pl.cdiv(tq, tk) if tq >= tk else 1
    for t in range(n_last):
        carry = sel_tile(n_full + t, carry, True)
    o_slc = carry[2] / carry[1]

    band = WINDOW + tq
    w0 = pl.multiple_of(jnp.maximum(q0 - WINDOW, 0), LANES)
    wpos = w0 + lax.broadcasted_iota(jnp.int32, (band, nq), 0)
    wmask = (wpos <= tpos3) & (wpos > tpos3 - WINDOW)
    s = jnp.where(wmask, _dot(kw_ref[pl.ds(w0, band), :], q3), MASKED)
    e = jnp.exp2(s - jnp.max(s, axis=0, keepdims=True))
    o_win = _dot(vw_ref[:, pl.ds(w0, band)], e.astype(BF16)) / jnp.sum(e, axis=0, keepdims=True)

    gts = gt_ref[...]
    for hh in range(C_HPG):
        r = hh * N_BRANCH
        c = slice(hh * tq, (hh + 1) * tq)
        out = (o_cmp[:, c] * gts[r:r + 1, :] + o_slc[:, c] * gts[r + 1:r + 2, :]
               + o_win[:, c] * gts[r + 2:r + 3, :])
        o_ref[hh * d:(hh + 1) * d, :] = out.astype(o_ref.dtype)


def _nsa(qcT, gtT, kc2, vcT, ovT, ksl, vslT, kwi, vwiT, tq, tk):
    B, _, T = qcT.shape
    n_cmp = kc2.shape[1]
    n_blk = T // SEL_BLOCK
    kv_tok = pl.BlockSpec((None, T, LANES), lambda b, g, i: (b, 0, 0))
    kv_feat = pl.BlockSpec((None, C_HEAD_DIM, T), lambda b, g, i: (b, g, 0))
    return pl.pallas_call(
        functools.partial(_nsa_kernel, tq=tq, tk=tk, T=T),
        grid=(B, C_KV_GROUPS, T // tq),
        in_specs=[pl.BlockSpec((None, C_WIDTH, tq), lambda b, g, i: (b, 0, i)),
                  pl.BlockSpec((None, None, GATE_ROWS, tq), lambda b, g, i: (b, g, 0, i)),
                  pl.BlockSpec((None, n_cmp, LANES), lambda b, g, i: (b, 0, 0)),
                  pl.BlockSpec((None, C_HEAD_DIM, n_cmp), lambda b, g, i: (b, g, 0)),
                  pl.BlockSpec((n_blk, n_cmp), lambda b, g, i: (0, 0)),
                  kv_tok, kv_feat, kv_tok, kv_feat],
        out_specs=pl.BlockSpec((None, C_HPG * C_HEAD_DIM, tq), lambda b, g, i: (b, g, i)),
        out_shape=jax.ShapeDtypeStruct((B, C_WIDTH, T), BF16),
        scratch_shapes=[pltpu.VMEM((n_blk, C_HPG * tq), F32)],
        compiler_params=_params(("parallel", "parallel", "arbitrary")),
        name="nsa_attn",
    )(qcT, gtT, kc2, vcT, ovT, ksl, vslT, kwi, vwiT)


def _overlap_T(T):
    n_cmp = T // CMP_STRIDE
    n_blk = T // SEL_BLOCK
    cs = np.arange(n_cmp)[None, :] * CMP_STRIDE
    bs = np.arange(n_blk)[:, None] * SEL_BLOCK
    ov = (cs < bs + SEL_BLOCK) & (cs + CMP_BLOCK > bs) & (np.arange(n_cmp)[None, :] < n_cmp - 1)
    return jnp.asarray(ov.astype(np.float32), dtype=BF16)


def _layer_norm(y, g, b):
    mu = jnp.mean(y, axis=-1, keepdims=True)
    yc = y - mu
    var = jnp.mean(yc * yc, axis=-1, keepdims=True)
    return yc * lax.rsqrt(var + LN_EPS) * g + b


def _outproj_kernel(x_ref, oa_ref, ob_ref, oc_ref, wa_ref, wb_ref, wc_ref, g_ref, b_ref, o_ref, *, alpha):
    mix = _dot(oa_ref[...], wa_ref[...]) + _dot(ob_ref[...], wb_ref[...]) + _dot(oc_ref[...], wc_ref[...])
    o_ref[...] = _layer_norm(alpha * x_ref[...] + mix, g_ref[...], b_ref[...])


def _outproj(x2, oa, ob, oc, wa, wb, wc, g, b, alpha, tm):
    BT, D = x2.shape
    row = lambda n: pl.BlockSpec((tm, n), lambda i: (i, 0))
    full = lambda a: pl.BlockSpec(a.shape, lambda i: (0, 0))
    return pl.pallas_call(
        functools.partial(_outproj_kernel, alpha=alpha),
        grid=(BT // tm,),
        in_specs=[row(D), row(oa.shape[1]), row(ob.shape[1]), row(oc.shape[1]),
                  full(wa), full(wb), full(wc), full(g), full(b)],
        out_specs=row(D),
        out_shape=jax.ShapeDtypeStruct((BT, D), F32),
        compiler_params=_params(("parallel",)),
        name="outproj_ln",
    )(x2, oa, ob, oc, wa, wb, wc, g, b)


def _mlp_kernel(x_ref, wu_ref, wd_ref, g_ref, b_ref, o_ref, acc_ref, *, alpha):
    f = pl.program_id(1)

    @pl.when(f == 0)
    def _():
        acc_ref[...] = jnp.zeros_like(acc_ref)

    hid = _dot(x_ref[...].astype(BF16), wu_ref[...])
    hid = jnp.square(jnp.maximum(hid, 0.0))
    acc_ref[...] += _dot(hid.astype(BF16), wd_ref[...])

    @pl.when(f == pl.num_programs(1) - 1)
    def _():
        o_ref[...] = _layer_norm(alpha * x_ref[...] + acc_ref[...], g_ref[...], b_ref[...])


def _mlp(x2, wu, wd, g, b, alpha, tm, tf):
    BT, D = x2.shape
    F = wu.shape[1]
    return pl.pallas_call(
        functools.partial(_mlp_kernel, alpha=alpha),
        grid=(BT // tm, F // tf),
        in_specs=[pl.BlockSpec((tm, D), lambda i, f: (i, 0)),
                  pl.BlockSpec((D, tf), lambda i, f: (0, f)),
                  pl.BlockSpec((tf, D), lambda i, f: (f, 0)),
                  pl.BlockSpec((1, D), lambda i, f: (0, 0)),
                  pl.BlockSpec((1, D), lambda i, f: (0, 0))],
        out_specs=pl.BlockSpec((tm, D), lambda i, f: (i, 0)),
        out_shape=jax.ShapeDtypeStruct((BT, D), F32),
        scratch_shapes=[pltpu.VMEM((tm, D), F32)],
        compiler_params=_params(("parallel", "arbitrary")),
        name="mlp_ln",
    )(x2, wu, wd, g, b)


def _tiles(T):
    return dict(
        tm_proj=min(512, T), tm_pool=min(512, T), tm_out=min(512, T), tm_mlp=min(1024, T), tf_mlp=1024,
        a_tq=min(512, T), a_tk=min(512, T), c_tq=min(256, T), c_tk=min(512, T),
    )


def _to_feat(a, B, T):
    return a.reshape(B, T, a.shape[-1]).transpose(0, 2, 1)


GATE_ROWS = 16


def _gates_feat(gt, B, T):
    per = C_HPG * N_BRANCH
    g4 = gt[:, :C_KV_GROUPS * per].reshape(B, T, C_KV_GROUPS, per)
    g4 = jnp.pad(g4, ((0, 0), (0, 0), (0, 0), (0, GATE_ROWS - per)))
    return g4.transpose(0, 2, 3, 1)


def _to_tok(a):
    B, n, T = a.shape
    return a.transpose(0, 2, 1).reshape(B * T, n)


def kernel(x, w_in, lam_q1, lam_k1, lam_q2, lam_k2, subln_g, w_pool, pool_scale, cmp_pe_k, cmp_pe_v,
           w_cmp_k, w_cmp_v, w_out, ln1_g, ln1_b, w_up, w_down, ln2_g, ln2_b):
    B, T, D = x.shape
    depth = w_in.shape[0]
    alpha = (2 * depth) ** 0.25
    tl = _tiles(T)
    tabs = _rope_tables(T)
    ovT = _overlap_T(T)
    x2 = x.reshape(B * T, D)
    for l in range(depth):
        lam_init = 0.8 - 0.6 * math.exp(-0.3 * l)
        (qa, ka, va, u, qc, kcm, vcm, ksl, vsl, kwi, vwi, gt) = _inproj(
            x2, _prep_w_in(w_in[l]), tabs, tl["tm_proj"], T)

        oaT = _diff_attn(_to_feat(qa, B, T), ka.reshape(B, T, -1), _to_feat(va, B, T),
                         lam_q1[l][None, :], lam_k1[l][None, :], lam_q2[l][None, :], lam_k2[l][None, :],
                         subln_g[l][:, None], lam_init, tl["a_tq"], tl["a_tk"])

        w_bd = jax.scipy.linalg.block_diag(*[w_pool[l, g] for g in range(len(POOL_WINDOWS))]).astype(BF16)
        ob = _pool(u.reshape(B, T, POOL_WIDTH), w_bd, pool_scale[l][None, :], tl["tm_pool"])

        pk2, pv2, wk2, wv2 = _prep_cmp(cmp_pe_k[l], cmp_pe_v[l], w_cmp_k[l], w_cmp_v[l])
        n16 = T // CMP_STRIDE
        kc2, vc2 = _compress(kcm.reshape(B, n16, 16 * LANES), vcm.reshape(B, n16, 16 * LANES),
                             pk2, pv2, wk2, wv2)
        ocT = _nsa(_to_feat(qc, B, T), _gates_feat(gt, B, T), kc2, vc2.transpose(0, 2, 1), ovT,
                   ksl.reshape(B, T, LANES), _to_feat(vsl, B, T), kwi.reshape(B, T, LANES),
                   _to_feat(vwi, B, T), tl["c_tq"], tl["c_tk"])

        wo = w_out[l].astype(BF16)
        x2 = _outproj(x2, _to_tok(oaT), ob.reshape(B * T, POOL_WIDTH), _to_tok(ocT),
                      wo[:A_WIDTH], wo[A_WIDTH:A_WIDTH + POOL_WIDTH], wo[A_WIDTH + POOL_WIDTH:],
                      ln1_g[l][None, :], ln1_b[l][None, :], alpha, tl["tm_out"])
        x2 = _mlp(x2, w_up[l].astype(BF16), w_down[l].astype(BF16), ln2_g[l][None, :], ln2_b[l][None, :],
                  alpha, tl["tm_mlp"], tl["tf_mlp"])
    return x2.reshape(B, T, D)
```

```python
import functools
import math

import jax
import jax.numpy as jnp
import numpy as np
from jax import lax
from jax.experimental import pallas as pl
from jax.experimental.pallas import tpu as pltpu

A_HEADS = 6
A_QK_DIM = 32
A_V_DIM = 64
A_WIDTH = A_HEADS * A_V_DIM
POOL_WINDOWS = (2, 4, 8, 16)
POOL_GROUP_DIM = 64
POOL_WIDTH = 256
C_HEADS = 6
C_KV_GROUPS = 2
C_HPG = 3
C_HEAD_DIM = 64
C_WIDTH = C_HEADS * C_HEAD_DIM
CMP_BLOCK = 32
CMP_STRIDE = 16
SEL_BLOCK = 64
N_SEL = 16
WINDOW = 512
N_BRANCH = 3
FORCED_SCORE = 1.0e4
ROPE_THETA = 10000.0
LN_EPS = 1e-5
RMS_EPS = 1e-6

LANES = 128
LOG2E = 1.4426950408889634
NEG_BIG = -1e30
MASKED = -3e38
VMEM_LIMIT = 56 * 1024 * 1024

BF16 = jnp.bfloat16
F32 = jnp.float32


def _dot(a, b):
    return jnp.dot(a, b, preferred_element_type=F32)


def _params(sem, vmem=VMEM_LIMIT):
    return pltpu.CompilerParams(dimension_semantics=sem, vmem_limit_bytes=vmem)


def _perm_diff():
    idx = np.zeros(A_HEADS * 2 * A_QK_DIM, np.int32)
    for p in range(A_HEADS // 2):
        for l in range(LANES):
            half, grp, j = l // 64, (l % 64) // 16, l % 16
            hl, m = grp // 2, grp % 2
            idx[p * LANES + l] = (2 * p + hl) * 64 + m * 32 + half * 16 + j
    return idx


def _perm_cq():
    idx = np.zeros(C_WIDTH, np.int32)
    for hh in range(C_HPG):
        for l in range(LANES):
            half, g, j = l // 64, (l % 64) // 32, l % 32
            idx[hh * LANES + l] = (g * C_HPG + hh) * 64 + half * 32 + j
    return idx


def _perm_ck():
    idx = np.zeros(C_KV_GROUPS * C_HEAD_DIM, np.int32)
    for l in range(LANES):
        half, g, j = l // 64, (l % 64) // 32, l % 32
        idx[l] = g * 64 + half * 32 + j
    return idx


_IN_SPLITS = (384, 384, 384, 256, 384, 128, 128, 128, 128, 128, 128, 18)
_OFFS = np.concatenate([[0], np.cumsum(_IN_SPLITS)]).astype(np.int64)
A_QSCALE = A_QK_DIM ** -0.5 * LOG2E
C_QSCALE = C_HEAD_DIM ** -0.5 * LOG2E


def _prep_w_in(w):
    o = _OFFS
    pd, pq, pk = _perm_diff(), _perm_cq(), _perm_ck()
    seg = lambda i: w[:, o[i]:o[i + 1]]
    cols = [
        seg(0)[:, pd], seg(1)[:, pd], seg(2), seg(3), seg(4)[:, pq],
        seg(5)[:, pk], seg(6), seg(7)[:, pk], seg(8), seg(9)[:, pk], seg(10),
        jnp.pad(seg(11), ((0, 0), (0, LANES - 18))),
    ]
    return jnp.concatenate(cols, axis=1).astype(BF16)


def _rope_tables(T):
    def tab(dim):
        half = dim // 2
        inv = 1.0 / (ROPE_THETA ** (jnp.arange(0, dim, 2, dtype=F32) / dim))
        ang = jnp.arange(T, dtype=F32)[:, None] * inv[None, :]
        lane = np.arange(LANES)
        j = lane % half
        sign = np.where(lane < 64, -1.0, 1.0).astype(np.float32)
        return jnp.cos(ang)[:, j], jnp.sin(ang)[:, j] * sign[None, :]
    ca, sa = tab(A_QK_DIM)
    cc, sc = tab(C_HEAD_DIM)
    return ca, sa, cc, sc


def _inproj_kernel(x_ref, w_ref, ca_ref, sa_ref, cc_ref, sc_ref,
                   qa_ref, ka_ref, va_ref, u_ref, qc_ref,
                   kcm_ref, vcm_ref, ksl_ref, vsl_ref, kwi_ref, vwi_ref, gt_ref):
    xb = x_ref[...].astype(BF16)

    def proj(c0, n):
        return _dot(xb, w_ref[:, c0:c0 + n])

    def rope(p, cos, sin):
        return p * cos + pltpu.roll(p, 64, axis=1) * sin

    ca, sa, cc, sc = ca_ref[...], sa_ref[...], cc_ref[...], sc_ref[...]
    for t in range(3):
        p = proj(t * LANES, LANES)
        qa_ref[:, t * LANES:(t + 1) * LANES] = (rope(p, ca, sa) * A_QSCALE).astype(BF16)
        p = proj(384 + t * LANES, LANES)
        ka_ref[:, t * LANES:(t + 1) * LANES] = rope(p, ca, sa).astype(BF16)
        p = proj(1408 + t * LANES, LANES)
        qc_ref[:, t * LANES:(t + 1) * LANES] = (rope(p, cc, sc) * C_QSCALE).astype(BF16)
    va_ref[...] = proj(768, 384).astype(BF16)
    u_ref[...] = proj(1152, 256)
    for i, (kr, vr) in enumerate(((kcm_ref, vcm_ref), (ksl_ref, vsl_ref), (kwi_ref, vwi_ref))):
        c0 = 1792 + i * 256
        kr[...] = rope(proj(c0, LANES), cc, sc).astype(BF16)
        vr[...] = proj(c0 + LANES, LANES).astype(BF16)
    gt_ref[...] = jax.nn.sigmoid(proj(2560, LANES))


def _inproj(x2, w_all, tabs, tm, T):
    BT, D = x2.shape
    nT = T // tm
    row = lambda n: pl.BlockSpec((tm, n), lambda i: (i, 0))
    tabspec = pl.BlockSpec((tm, LANES), lambda i: (i % nT, 0))
    outs = [(384, BF16), (384, BF16), (384, BF16), (256, F32), (384, BF16)] + [(LANES, BF16)] * 6 + [(LANES, F32)]
    return pl.pallas_call(
        _inproj_kernel,
        grid=(BT // tm,),
        in_specs=[row(D), pl.BlockSpec(w_all.shape, lambda i: (0, 0))] + [tabspec] * 4,
        out_specs=[row(n) for n, _ in outs],
        out_shape=[jax.ShapeDtypeStruct((BT, n), dt) for n, dt in outs],
        compiler_params=_params(("parallel",)),
        name="inproj_rope",
    )(x2, w_all, *tabs)


def _softmax_update(s, v, m_ref, l_ref, acc_ref):
    m = m_ref[...]
    m_new = jnp.maximum(m, jnp.max(s, axis=0, keepdims=True))
    alpha = jnp.exp2(m - m_new)
    p = jnp.exp2(s - m_new)
    l_ref[...] = l_ref[...] * alpha + jnp.sum(p, axis=0, keepdims=True)
    acc_ref[...] = acc_ref[...] * alpha + _dot(v, p.astype(BF16))
    m_ref[...] = m_new


def _causal_sweep(n_full, score, causal, value, s0_ref, s1_ref, m_ref, l_ref, acc_ref):
    m_ref[...] = jnp.full(m_ref.shape, NEG_BIG, F32)
    l_ref[...] = jnp.zeros(l_ref.shape, F32)
    acc_ref[...] = jnp.zeros(acc_ref.shape, F32)
    s0_ref[...] = score(0)

    def pair(i, c):
        j = 2 * i
        s1_ref[...] = score(j + 1)
        _softmax_update(s0_ref[...], value(j), m_ref, l_ref, acc_ref)
        s0_ref[...] = score(j + 2)
        _softmax_update(s1_ref[...], value(j + 1), m_ref, l_ref, acc_ref)
        return c

    lax.fori_loop(0, n_full // 2, pair, 0)

    @pl.when(n_full % 2 == 0)
    def _():
        _softmax_update(causal(s0_ref[...], n_full), value(n_full), m_ref, l_ref, acc_ref)

    @pl.when(n_full % 2 == 1)
    def _():
        s1_ref[...] = score(n_full)
        _softmax_update(s0_ref[...], value(n_full - 1), m_ref, l_ref, acc_ref)
        _softmax_update(causal(s1_ref[...], n_full), value(n_full), m_ref, l_ref, acc_ref)


def _diff_attn_kernel(lq1_ref, lk1_ref, lq2_ref, lk2_ref, g_ref, q_ref, k_ref, v_ref, o_ref,
                      s0_ref, s1_ref, m_ref, l_ref, acc_ref, *, tq, tk, lam_init):
    h = pl.program_id(1)
    qi = pl.program_id(2)
    hl = h % 2
    q = q_ref[...]
    grp = (lax.broadcasted_iota(jnp.int32, q.shape, 0) % 64) // 16
    zero = jnp.zeros_like(q)
    q12 = jnp.concatenate([jnp.where(grp == 2 * hl, q, zero), jnp.where(grp == 2 * hl + 1, q, zero)], axis=1)
    nq = 2 * tq
    qpos = qi * tq + lax.broadcasted_iota(jnp.int32, (1, nq), 1) % tq

    def score(j):
        return _dot(k_ref[pl.ds(pl.multiple_of(j * tk, tk), tk), :], q12)

    def causal(s, j):
        return jnp.where(j * tk + lax.broadcasted_iota(jnp.int32, (tk, nq), 0) <= qpos, s, MASKED)

    def value(j):
        return v_ref[:, pl.ds(pl.multiple_of(j * tk, tk), tk)]

    _causal_sweep(qi, score, causal, value, s0_ref, s1_ref, m_ref, l_ref, acc_ref)
    on = acc_ref[...] / l_ref[...]

    dotp = lambda a, b: jnp.sum(a[...] * b[...], axis=1, keepdims=True)
    lam = jnp.exp(dotp(lq1_ref, lk1_ref)) - jnp.exp(dotp(lq2_ref, lk2_ref)) + lam_init
    o = on[:, :tq] - lam * on[:, tq:]
    ms = jnp.mean(o * o, axis=0, keepdims=True)
    o = o * lax.rsqrt(ms + RMS_EPS) * g_ref[...] * (1.0 - lam_init)
    o_ref[...] = o.astype(o_ref.dtype)


def _diff_attn(qaT, ka, vaT, lq1, lk1, lq2, lk2, g_col, lam_init, tq, tk):
    B, _, T = qaT.shape
    assert tq == tk, "the causal sweep takes query block i to end at key tile i"
    nq = 2 * tq
    vec = pl.BlockSpec((1, A_QK_DIM), lambda b, h, i: (0, 0))
    return pl.pallas_call(
        functools.partial(_diff_attn_kernel, tq=tq, tk=tk, lam_init=lam_init),
        grid=(B, A_HEADS, T // tq),
        in_specs=[vec, vec, vec, vec,
                  pl.BlockSpec((A_V_DIM, 1), lambda b, h, i: (0, 0)),
                  pl.BlockSpec((None, LANES, tq), lambda b, h, i: (b, h // 2, i)),
                  pl.BlockSpec((None, T, LANES), lambda b, h, i: (b, 0, h // 2)),
                  pl.BlockSpec((None, A_V_DIM, T), lambda b, h, i: (b, h, 0))],
        out_specs=pl.BlockSpec((None, A_V_DIM, tq), lambda b, h, i: (b, h, i)),
        out_shape=jax.ShapeDtypeStruct((B, A_WIDTH, T), BF16),
        scratch_shapes=[pltpu.VMEM((tk, nq), F32), pltpu.VMEM((tk, nq), F32),
                        pltpu.VMEM((1, nq), F32), pltpu.VMEM((1, nq), F32), pltpu.VMEM((A_V_DIM, nq), F32)],
        compiler_params=_params(("parallel", "parallel", "arbitrary")),
        name="diff_attn",
    )(lq1, lk1, lq2, lk2, g_col, qaT, ka, vaT)


POOL_HALO = 16


def _pool_kernel(prev_ref, cur_ref, w_ref, scale_ref, o_ref, *, tm):
    i = pl.program_id(1)
    cur = cur_ref[...]
    halo = prev_ref[tm - POOL_HALO:, :]
    halo = jnp.where(i == 0, jnp.zeros_like(halo), halo)
    arr = jnp.concatenate([halo, cur], axis=0)
    s2 = arr + pltpu.roll(arr, 1, axis=0)
    s4 = s2 + pltpu.roll(s2, 2, axis=0)
    s8 = s4 + pltpu.roll(s4, 4, axis=0)
    s16 = s8 + pltpu.roll(s8, 8, axis=0)
    sums = [s[POOL_HALO:, :] for s in (s2, s4, s8, s16)]
    lane = lax.broadcasted_iota(jnp.int32, (tm, POOL_WIDTH), 1)
    tpos = i * tm + lax.broadcasted_iota(jnp.int32, (tm, POOL_WIDTH), 0)
    grp = lane // POOL_GROUP_DIM
    win = jnp.where(grp == 0, sums[0], jnp.where(grp == 1, sums[1], jnp.where(grp == 2, sums[2], sums[3])))
    width = jnp.where(grp == 0, 2, jnp.where(grp == 1, 4, jnp.where(grp == 2, 8, 16)))
    cnt = jnp.minimum(tpos + 1, width).astype(F32)
    y = win / cnt - cur
    z = _dot(y.astype(BF16), w_ref[...])
    o_ref[...] = (z * scale_ref[...]).astype(o_ref.dtype)


def _pool(u3, w_bd, scale, tm):
    B, T, W = u3.shape
    return pl.pallas_call(
        functools.partial(_pool_kernel, tm=tm),
        grid=(B, T // tm),
        in_specs=[pl.BlockSpec((None, tm, W), lambda b, i: (b, jnp.maximum(i - 1, 0), 0)),
                  pl.BlockSpec((None, tm, W), lambda b, i: (b, i, 0)),
                  pl.BlockSpec((W, W), lambda b, i: (0, 0)),
                  pl.BlockSpec((1, W), lambda b, i: (0, 0))],
        out_specs=pl.BlockSpec((None, tm, W), lambda b, i: (b, i, 0)),
        out_shape=jax.ShapeDtypeStruct((B, T, W), BF16),
        compiler_params=_params(("parallel", "arbitrary")),
        name="pool_mixer",
    )(u3, u3, w_bd, scale)


def _cmp_kernel(k_ref, v_ref, pk_ref, pv_ref, wk_ref, wv_ref, kc_ref, vc_ref):
    n = k_ref.shape[0]
    rowid = lax.broadcasted_iota(jnp.int32, (n, LANES), 0)

    def compress(x_ref, pe_ref, w_ref):
        x = x_ref[...].astype(F32)
        lo = _dot((x + pe_ref[0:1, :]).astype(BF16), w_ref[0])
        hi = _dot((x + pe_ref[1:2, :]).astype(BF16), w_ref[1])
        out = lo + pltpu.roll(hi, n - 1, axis=0)
        return jnp.where(rowid < n - 1, out, 0.0)

    kc_ref[...] = compress(k_ref, pk_ref, wk_ref).astype(kc_ref.dtype)
    vc_ref[...] = compress(v_ref, pv_ref, wv_ref).astype(vc_ref.dtype)


def _compress(kr, vr, pe_k2, pe_v2, wk2, wv2):
    B, n, W = kr.shape
    blk = pl.BlockSpec((None, n, W), lambda b: (b, 0, 0))
    pe = pl.BlockSpec((2, W), lambda b: (0, 0))
    wsp = pl.BlockSpec((2, W, LANES), lambda b: (0, 0, 0))
    osp = pl.BlockSpec((None, n, LANES), lambda b: (b, 0, 0))
    return pl.pallas_call(
        _cmp_kernel,
        grid=(B,),
        in_specs=[blk, blk, pe, pe, wsp, wsp],
        out_specs=[osp, osp],
        out_shape=[jax.ShapeDtypeStruct((B, n, LANES), BF16)] * 2,
        compiler_params=_params(("parallel",)),
        name="nsa_compress",
    )(kr, vr, pe_k2, pe_v2, wk2, wv2)


def _prep_cmp(pe_k, pe_v, w_ck, w_cv):
    pk = _perm_ck()
    lane = np.arange(LANES)
    g_of_k = (lane % 64) // 32
    d_of_k = pk % 64
    g_of_v = lane // 64
    d_of_v = lane % 64

    def build(w, pe, g_of, d_of):
        wl = w[:, d_of, :]
        onehot = jnp.asarray((g_of[:, None] == np.arange(C_KV_GROUPS)[None, :]).astype(np.float32))
        w2 = wl[:, :, None, :] * onehot[None, :, :, None]
        w2 = w2.reshape(2, 16 * LANES, C_KV_GROUPS * C_HEAD_DIM).astype(BF16)
        pe2 = pe[:, d_of].reshape(2, 16 * LANES)
        return w2, pe2

    wk2, pk2 = build(w_ck, pe_k, g_of_k, d_of_k)
    wv2, pv2 = build(w_cv, pe_v, g_of_v, d_of_v)
    return pk2, pv2, wk2[:, :, pk], wv2


def _nsa_kernel(q_ref, gt_ref, kc_ref, vc_ref, ov_ref, ks_ref, vs_ref, kw_ref, vw_ref, o_ref, sel_ref,
                s0_ref, s1_ref, m_ref, l_ref, acc_ref, *, tq, tk, T):
    g = pl.program_id(1)
    qi = pl.program_id(2)
    q0 = qi * tq
    n_cmp = kc_ref.shape[0]
    n_blk = T // SEL_BLOCK
    d = C_HEAD_DIM
    tpos = q0 + lax.broadcasted_iota(jnp.int32, (1, tq), 1)

    nq = C_HPG * tq
    tpos3 = q0 + lax.broadcasted_iota(jnp.int32, (1, nq), 1) % tq
    rowg = (lax.broadcasted_iota(jnp.int32, (LANES, tq), 0) % 64) // 32
    qs = []
    for hh in range(C_HPG):
        qt = q_ref[hh * LANES:(hh + 1) * LANES, :]
        qs.append(jnp.where(rowg == g, qt, jnp.zeros_like(qt)))
    q3 = jnp.concatenate(qs, axis=1)

    cend = lax.broadcasted_iota(jnp.int32, (n_cmp, nq), 0) * CMP_STRIDE + (CMP_BLOCK - 1)
    s = jnp.where(cend <= tpos3, _dot(kc_ref[...], q3), MASKED)
    m = jnp.maximum(jnp.max(s, axis=0, keepdims=True), NEG_BIG)
    e = jnp.exp2(s - m)
    l = jnp.sum(e, axis=0, keepdims=True)
    p = e / jnp.where(l > 0.0, l, 1.0)
    o_cmp = _dot(vc_ref[...], p.astype(BF16))
    psum = p[:, :tq] + p[:, tq:2 * tq] + p[:, 2 * tq:]

    p_hi = psum.astype(BF16)
    p_lo = (psum - p_hi.astype(F32)).astype(BF16)
    imp = _dot(ov_ref[...], p_hi) + _dot(ov_ref[...], p_lo)
    jidx = lax.broadcasted_iota(jnp.int32, (n_blk, tq), 0).astype(F32)
    cur = (tpos // SEL_BLOCK).astype(F32)
    forced = (jidx == 0.0) | (jidx == cur) | (jidx == cur - 1.0)
    imp = jnp.where(forced, FORCED_SCORE, imp)
    imp = jnp.where(jidx > cur, -1.0, imp)
    bias = jnp.full((n_blk, tq), MASKED, F32)
    for _ in range(min(N_SEL, n_blk)):
        mx = jnp.max(imp, axis=0, keepdims=True)
        first = jnp.min(jnp.where(imp == mx, jidx, float(n_blk)), axis=0, keepdims=True)
        hit = jidx == first
        bias = jnp.where(hit, 0.0, bias)
        imp = jnp.where(hit, -3.0e38, imp)
    sel_ref[...] = jnp.concatenate([bias] * C_HPG, axis=1)

    bpt = tk // SEL_BLOCK

    def score(j):
        rows = sel_ref[pl.ds(pl.multiple_of(j * bpt, bpt), bpt), :]
        return _dot(ks_ref[pl.ds(pl.multiple_of(j * tk, tk), tk), :], q3) + jnp.concatenate(
            [jnp.broadcast_to(rows[i:i + 1, :], (SEL_BLOCK, nq)) for i in range(bpt)], axis=0)

    def causal(s, j):
        return jnp.where(j * tk + lax.broadcasted_iota(jnp.int32, (tk, nq), 0) <= tpos3, s, MASKED)

    def value(j):
        return vs_ref[:, pl.ds(pl.multiple_of(j * tk, tk), tk)]

    _causal_sweep(q0 // tk, score, causal, value, s0_ref, s1_ref, m_ref, l_ref, acc_ref)
    o_slc = acc_ref[...] / l_ref[...]

    band = WINDOW + tq
    w0 = pl.multiple_of(jnp.maximum(q0 - WINDOW, 0), LANES)
    wpos = w0 + lax.broadcasted_iota(jnp.int32, (band, nq), 0)
    wmask = (wpos <= tpos3) & (wpos > tpos3 - WINDOW)
    s = jnp.where(wmask, _dot(kw_ref[pl.ds(w0, band), :], q3), MASKED)
    e = jnp.exp2(s - jnp.max(s, axis=0, keepdims=True))
    o_win = _dot(vw_ref[:, pl.ds(w0, band)], e.astype(BF16)) / jnp.sum(e, axis=0, keepdims=True)

    gts = gt_ref[...]
    for hh in range(C_HPG):
        r = hh * N_BRANCH
        c = slice(hh * tq, (hh + 1) * tq)
        out = (o_cmp[:, c] * gts[r:r + 1, :] + o_slc[:, c] * gts[r + 1:r + 2, :]
               + o_win[:, c] * gts[r + 2:r + 3, :])
        o_ref[hh * d:(hh + 1) * d, :] = out.astype(o_ref.dtype)


def _nsa(qcT, gtT, kc2, vcT, ovT, ksl, vslT, kwi, vwiT, tq, tk):
    B, _, T = qcT.shape
    n_cmp = kc2.shape[1]
    n_blk = T // SEL_BLOCK
    nq = C_HPG * tq
    assert tk % tq == 0, "the causal key tile must contain the whole query block"
    kv_tok =pl.BlockSpec((None, T, LANES), lambda b, g, i: (b, 0, 0))
    kv_feat = pl.BlockSpec((None, C_HEAD_DIM, T), lambda b, g, i: (b, g, 0))
    return pl.pallas_call(
        functools.partial(_nsa_kernel, tq=tq, tk=tk, T=T),
        grid=(B, C_KV_GROUPS, T // tq),
        in_specs=[pl.BlockSpec((None, C_WIDTH, tq), lambda b, g, i: (b, 0, i)),
                  pl.BlockSpec((None, None, GATE_ROWS, tq), lambda b, g, i: (b, g, 0, i)),
                  pl.BlockSpec((None, n_cmp, LANES), lambda b, g, i: (b, 0, 0)),
                  pl.BlockSpec((None, C_HEAD_DIM, n_cmp), lambda b, g, i: (b, g, 0)),
                  pl.BlockSpec((n_blk, n_cmp), lambda b, g, i: (0, 0)),
                  kv_tok, kv_feat, kv_tok, kv_feat],
        out_specs=pl.BlockSpec((None, C_HPG * C_HEAD_DIM, tq), lambda b, g, i: (b, g, i)),
        out_shape=jax.ShapeDtypeStruct((B, C_WIDTH, T), BF16),
        scratch_shapes=[pltpu.VMEM((n_blk, nq), F32), pltpu.VMEM((tk, nq), F32), pltpu.VMEM((tk, nq), F32),
                        pltpu.VMEM((1, nq), F32), pltpu.VMEM((1, nq), F32), pltpu.VMEM((C_HEAD_DIM, nq), F32)],
        compiler_params=_params(("parallel", "parallel", "arbitrary")),
        name="nsa_attn",
    )(qcT, gtT, kc2, vcT, ovT, ksl, vslT, kwi, vwiT)


def _overlap_T(T):
    n_cmp = T // CMP_STRIDE
    n_blk = T // SEL_BLOCK
    cs = np.arange(n_cmp)[None, :] * CMP_STRIDE
    bs = np.arange(n_blk)[:, None] * SEL_BLOCK
    ov = (cs < bs + SEL_BLOCK) & (cs + CMP_BLOCK > bs) & (np.arange(n_cmp)[None, :] < n_cmp - 1)
    return jnp.asarray(ov.astype(np.float32), dtype=BF16)


def _layer_norm(y, g, b):
    mu = jnp.mean(y, axis=-1, keepdims=True)
    yc = y - mu
    var = jnp.mean(yc * yc, axis=-1, keepdims=True)
    return yc * lax.rsqrt(var + LN_EPS) * g + b


def _outproj_kernel(x_ref, oa_ref, ob_ref, oc_ref, wa_ref, wb_ref, wc_ref, g_ref, b_ref, o_ref, *, alpha):
    mix = _dot(oa_ref[...], wa_ref[...]) + _dot(ob_ref[...], wb_ref[...]) + _dot(oc_ref[...], wc_ref[...])
    o_ref[...] = _layer_norm(alpha * x_ref[...] + mix, g_ref[...], b_ref[...])


def _outproj(x2, oa, ob, oc, wa, wb, wc, g, b, alpha, tm):
    BT, D = x2.shape
    row = lambda n: pl.BlockSpec((tm, n), lambda i: (i, 0))
    full = lambda a: pl.BlockSpec(a.shape, lambda i: (0, 0))
    return pl.pallas_call(
        functools.partial(_outproj_kernel, alpha=alpha),
        grid=(BT // tm,),
        in_specs=[row(D), row(oa.shape[1]), row(ob.shape[1]), row(oc.shape[1]),
                  full(wa), full(wb), full(wc), full(g), full(b)],
        out_specs=row(D),
        out_shape=jax.ShapeDtypeStruct((BT, D), F32),
        compiler_params=_params(("parallel",)),
        name="outproj_ln",
    )(x2, oa, ob, oc, wa, wb, wc, g, b)


def _mlp_kernel(x_ref, wu_ref, wd_ref, g_ref, b_ref, o_ref, acc_ref, *, alpha):
    f = pl.program_id(1)

    @pl.when(f == 0)
    def _():
        acc_ref[...] = jnp.zeros_like(acc_ref)

    hid = _dot(x_ref[...].astype(BF16), wu_ref[...])
    hid = jnp.square(jnp.maximum(hid, 0.0))
    acc_ref[...] += _dot(hid.astype(BF16), wd_ref[...])

    @pl.when(f == pl.num_programs(1) - 1)
    def _():
        o_ref[...] = _layer_norm(alpha * x_ref[...] + acc_ref[...], g_ref[...], b_ref[...])


def _mlp(x2, wu, wd, g, b, alpha, tm, tf):
    BT, D = x2.shape
    F = wu.shape[1]
    return pl.pallas_call(
        functools.partial(_mlp_kernel, alpha=alpha),
        grid=(BT // tm, F // tf),
        in_specs=[pl.BlockSpec((tm, D), lambda i, f: (i, 0)),
                  pl.BlockSpec((D, tf), lambda i, f: (0, f)),
                  pl.BlockSpec((tf, D), lambda i, f: (f, 0)),
                  pl.BlockSpec((1, D), lambda i, f: (0, 0)),
                  pl.BlockSpec((1, D), lambda i, f: (0, 0))],
        out_specs=pl.BlockSpec((tm, D), lambda i, f: (i, 0)),
        out_shape=jax.ShapeDtypeStruct((BT, D), F32),
        scratch_shapes=[pltpu.VMEM((tm, D), F32)],
        compiler_params=_params(("parallel", "arbitrary")),
        name="mlp_ln",
    )(x2, wu, wd, g, b)


def _tiles(T):
    return dict(
        tm_proj=min(512, T), tm_pool=min(512, T), tm_out=min(512, T), tm_mlp=min(1024, T), tf_mlp=1024,
        a_tq=min(512, T), a_tk=min(512, T), c_tq=min(256, T), c_tk=min(512, T),
    )


def _to_feat(a, B, T):
    return a.reshape(B, T, a.shape[-1]).transpose(0, 2, 1)


GATE_ROWS = 16


def _gates_feat(gt, B, T):
    per = C_HPG * N_BRANCH
    g4 = gt[:, :C_KV_GROUPS * per].reshape(B, T, C_KV_GROUPS, per)
    g4 = jnp.pad(g4, ((0, 0), (0, 0), (0, 0), (0, GATE_ROWS - per)))
    return g4.transpose(0, 2, 3, 1)


def _to_tok(a):
    B, n, T = a.shape
    return a.transpose(0, 2, 1).reshape(B * T, n)


def kernel(x, w_in, lam_q1, lam_k1, lam_q2, lam_k2, subln_g, w_pool, pool_scale, cmp_pe_k, cmp_pe_v,
           w_cmp_k, w_cmp_v, w_out, ln1_g, ln1_b, w_up, w_down, ln2_g, ln2_b):
    B, T, D = x.shape
    depth = w_in.shape[0]
    alpha = (2 * depth) ** 0.25
    tl = _tiles(T)
    tabs = _rope_tables(T)
    ovT = _overlap_T(T)
    x2 = x.reshape(B * T, D)
    for l in range(depth):
        lam_init = 0.8 - 0.6 * math.exp(-0.3 * l)
        (qa, ka, va, u, qc, kcm, vcm, ksl, vsl, kwi, vwi, gt) = _inproj(
            x2, _prep_w_in(w_in[l]), tabs, tl["tm_proj"], T)

        oaT = _diff_attn(_to_feat(qa, B, T), ka.reshape(B, T, -1), _to_feat(va, B, T),
                         lam_q1[l][None, :], lam_k1[l][None, :], lam_q2[l][None, :], lam_k2[l][None, :],
                         subln_g[l][:, None], lam_init, tl["a_tq"], tl["a_tk"])

        w_bd = jax.scipy.linalg.block_diag(*[w_pool[l, g] for g in range(len(POOL_WINDOWS))]).astype(BF16)
        ob = _pool(u.reshape(B, T, POOL_WIDTH), w_bd, pool_scale[l][None, :], tl["tm_pool"])

        pk2, pv2, wk2, wv2 = _prep_cmp(cmp_pe_k[l], cmp_pe_v[l], w_cmp_k[l], w_cmp_v[l])
        n16 = T // CMP_STRIDE
        kc2, vc2 = _compress(kcm.reshape(B, n16, 16 * LANES), vcm.reshape(B, n16, 16 * LANES),
                             pk2, pv2, wk2, wv2)
        ocT = _nsa(_to_feat(qc, B, T), _gates_feat(gt, B, T), kc2, vc2.transpose(0, 2, 1), ovT,
                   ksl.reshape(B, T, LANES), _to_feat(vsl, B, T), kwi.reshape(B, T, LANES),
                   _to_feat(vwi, B, T), tl["c_tq"], tl["c_tk"])

        wo = w_out[l].astype(BF16)
        x2 = _outproj(x2, _to_tok(oaT), ob.reshape(B * T, POOL_WIDTH), _to_tok(ocT),
                      wo[:A_WIDTH], wo[A_WIDTH:A_WIDTH + POOL_WIDTH], wo[A_WIDTH + POOL_WIDTH:],
                      ln1_g[l][None, :], ln1_b[l][None, :], alpha, tl["tm_out"])
        x2 = _mlp(x2, w_up[l].astype(BF16), w_down[l].astype(BF16), ln2_g[l][None, :], ln2_b[l][None, :],
                  alpha, tl["tm_mlp"], tl["tf_mlp"])
    return x2.reshape(B, T, D)
```

```python
import functools
import math

import jax
import jax.numpy as jnp
import numpy as np
from jax import lax
from jax.experimental import pallas as pl
from jax.experimental.pallas import tpu as pltpu

A_HEADS = 6
A_QK_DIM = 32
A_V_DIM = 64
A_WIDTH = A_HEADS * A_V_DIM
POOL_WINDOWS = (2, 4, 8, 16)
POOL_GROUP_DIM = 64
POOL_WIDTH = 256
C_HEADS = 6
C_KV_GROUPS = 2
C_HPG = 3
C_HEAD_DIM = 64
C_WIDTH = C_HEADS * C_HEAD_DIM
CMP_BLOCK = 32
CMP_STRIDE = 16
SEL_BLOCK = 64
N_SEL = 16
WINDOW = 512
N_BRANCH = 3
FORCED_SCORE = 1.0e4
ROPE_THETA = 10000.0
LN_EPS = 1e-5
RMS_EPS = 1e-6

LANES = 128
LOG2E = 1.4426950408889634
NEG_BIG = -1e30
SWEEP_UNROLL = 4
TAKEN = -3.0e38
MASKED = -3e38
VMEM_LIMIT = 56 * 1024 * 1024

BF16 = jnp.bfloat16
F32 = jnp.float32


def _dot(a, b):
    return jnp.dot(a, b, preferred_element_type=F32)


def _params(sem, vmem=VMEM_LIMIT):
    return pltpu.CompilerParams(dimension_semantics=sem, vmem_limit_bytes=vmem)


def _perm_diff():
    idx = np.zeros(A_HEADS * 2 * A_QK_DIM, np.int32)
    for p in range(A_HEADS // 2):
        for l in range(LANES):
            half, grp, j = l // 64, (l % 64) // 16, l % 16
            hl, m = grp // 2, grp % 2
            idx[p * LANES + l] = (2 * p + hl) * 64 + m * 32 + half * 16 + j
    return idx


def _perm_cq():
    idx = np.zeros(C_WIDTH, np.int32)
    for hh in range(C_HPG):
        for l in range(LANES):
            half, g, j = l // 64, (l % 64) // 32, l % 32
            idx[hh * LANES + l] = (g * C_HPG + hh) * 64 + half * 32 + j
    return idx


def _perm_ck():
    idx = np.zeros(C_KV_GROUPS * C_HEAD_DIM, np.int32)
    for l in range(LANES):
        half, g, j = l // 64, (l % 64) // 32, l % 32
        idx[l] = g * 64 + half * 32 + j
    return idx


_IN_SPLITS = (384, 384, 384, 256, 384, 128, 128, 128, 128, 128, 128, 18)
_OFFS = np.concatenate([[0], np.cumsum(_IN_SPLITS)]).astype(np.int64)
A_QSCALE = A_QK_DIM ** -0.5 * LOG2E
C_QSCALE = C_HEAD_DIM ** -0.5 * LOG2E


def _prep_w_in(w):
    o = _OFFS
    pd, pq, pk = _perm_diff(), _perm_cq(), _perm_ck()
    seg = lambda i: w[:, o[i]:o[i + 1]]
    cols = [
        seg(0)[:, pd], seg(1)[:, pd], seg(2), seg(3), seg(4)[:, pq],
        seg(5)[:, pk], seg(6), seg(7)[:, pk], seg(8), seg(9)[:, pk], seg(10),
        jnp.pad(seg(11), ((0, 0), (0, LANES - 18))),
    ]
    return jnp.concatenate(cols, axis=1).astype(BF16)


def _rope_tables(T):
    def tab(dim):
        half = dim // 2
        inv = 1.0 / (ROPE_THETA ** (jnp.arange(0, dim, 2, dtype=F32) / dim))
        ang = jnp.arange(T, dtype=F32)[:, None] * inv[None, :]
        lane = np.arange(LANES)
        j = lane % half
        sign = np.where(lane < 64, -1.0, 1.0).astype(np.float32)
        return jnp.cos(ang)[:, j], jnp.sin(ang)[:, j] * sign[None, :]
    ca, sa = tab(A_QK_DIM)
    cc, sc = tab(C_HEAD_DIM)
    return ca, sa, cc, sc


def _inproj_kernel(x_ref, w_ref, ca_ref, sa_ref, cc_ref, sc_ref,
                   qa_ref, ka_ref, va_ref, u_ref, qc_ref,
                   kcm_ref, vcm_ref, ksl_ref, vsl_ref, kwi_ref, vwi_ref, gt_ref):
    xb = x_ref[...].astype(BF16)

    def proj(c0, n):
        return _dot(xb, w_ref[:, c0:c0 + n])

    def rope(p, cos, sin):
        return p * cos + pltpu.roll(p, 64, axis=1) * sin

    ca, sa, cc, sc = ca_ref[...], sa_ref[...], cc_ref[...], sc_ref[...]
    for t in range(3):
        p = proj(t * LANES, LANES)
        qa_ref[:, t * LANES:(t + 1) * LANES] = (rope(p, ca, sa) * A_QSCALE).astype(BF16)
        p = proj(384 + t * LANES, LANES)
        ka_ref[:, t * LANES:(t + 1) * LANES] = rope(p, ca, sa).astype(BF16)
        p = proj(1408 + t * LANES, LANES)
        qc_ref[:, t * LANES:(t + 1) * LANES] = (rope(p, cc, sc) * C_QSCALE).astype(BF16)
    va_ref[...] = proj(768, 384).astype(BF16)
    u_ref[...] = proj(1152, 256)
    for i, (kr, vr) in enumerate(((kcm_ref, vcm_ref), (ksl_ref, vsl_ref), (kwi_ref, vwi_ref))):
        c0 = 1792 + i * 256
        kr[...] = rope(proj(c0, LANES), cc, sc).astype(BF16)
        vr[...] = proj(c0 + LANES, LANES).astype(BF16)
    gt_ref[...] = jax.nn.sigmoid(proj(2560, LANES))


def _inproj(x2, w_all, tabs, tm, T):
    BT, D = x2.shape
    nT = T // tm
    row = lambda n: pl.BlockSpec((tm, n), lambda i: (i, 0))
    tabspec = pl.BlockSpec((tm, LANES), lambda i: (i % nT, 0))
    outs = [(384, BF16), (384, BF16), (384, BF16), (256, F32), (384, BF16)] + [(LANES, BF16)] * 6 + [(LANES, F32)]
    return pl.pallas_call(
        _inproj_kernel,
        grid=(BT // tm,),
        in_specs=[row(D), pl.BlockSpec(w_all.shape, lambda i: (0, 0))] + [tabspec] * 4,
        out_specs=[row(n) for n, _ in outs],
        out_shape=[jax.ShapeDtypeStruct((BT, n), dt) for n, dt in outs],
        compiler_params=_params(("parallel",)),
        name="inproj_rope",
    )(x2, w_all, *tabs)


def _sweep_scratch(tk, nq, dv):
    return [pltpu.VMEM((tk, nq), F32), pltpu.VMEM((tk, nq), F32),
            pltpu.VMEM((1, nq), F32), pltpu.VMEM((1, nq), F32),
            pltpu.VMEM((1, nq), F32), pltpu.VMEM((1, nq), F32), pltpu.VMEM((dv, nq), F32)]


def _causal_sweep(n_full, score, qpos, value, scratch):
    s_refs, cm_refs = scratch[0:2], scratch[2:4]
    m_ref, l_ref, acc_ref = scratch[4:7]

    tk, nq = s_refs[0].shape

    def scores(j, slot):
        s = score(j)
        s_refs[slot][...] = s
        cm_refs[slot][...] = jnp.max(s, axis=0, keepdims=True)

    def softmax(slot, v, causal_tile=None):
        s = s_refs[slot][...]
        if causal_tile is None:
            cm = cm_refs[slot][...]
        else:
            kpos = causal_tile * tk + lax.broadcasted_iota(jnp.int32, (tk, nq), 0)
            s = jnp.where(kpos <= qpos, s, MASKED)
            cm = jnp.max(s, axis=0, keepdims=True)
        m = m_ref[...]
        m_new = jnp.maximum(m, cm)
        alpha = jnp.exp2(m - m_new)
        p = jnp.exp2(s - m_new)
        l_ref[...] = l_ref[...] * alpha + jnp.sum(p, axis=0, keepdims=True)
        m_ref[...] = m_new
        acc_ref[...] = acc_ref[...] * alpha + _dot(v, p.astype(BF16))

    m_ref[...] = jnp.full(m_ref.shape, NEG_BIG, F32)
    l_ref[...] = jnp.zeros(l_ref.shape, F32)
    acc_ref[...] = jnp.zeros(acc_ref.shape, F32)
    scores(0, 0)

    def group(base, tiles):
        for u in range(tiles):
            scores(base + u + 1, (u + 1) % 2)
            softmax(u % 2, value(base + u))

    def quad(i, c):
        group(SWEEP_UNROLL * i, SWEEP_UNROLL)
        return c

    def pair(i, c):
        group(done + 2 * i, 2)
        return c

    lax.fori_loop(0, n_full // SWEEP_UNROLL, quad, 0)
    done = (n_full // SWEEP_UNROLL) * SWEEP_UNROLL
    lax.fori_loop(0, (n_full - done) // 2, pair, 0)

    @pl.when(n_full % 2 == 0)
    def _():
        softmax(0, value(n_full), causal_tile=n_full)

    @pl.when(n_full % 2 == 1)
    def _():
        scores(n_full, 1)
        softmax(0, value(n_full - 1))
        softmax(1, value(n_full), causal_tile=n_full)

    return acc_ref, l_ref


def _diff_attn_kernel(lq1_ref, lk1_ref, lq2_ref, lk2_ref, g_ref, q_ref, k_ref, v_ref, o_ref,
                      *scratch, tq, tk, lam_init):
    h = pl.program_id(1)
    qi = pl.program_id(2)
    hl = h % 2
    q = q_ref[...]
    grp = (lax.broadcasted_iota(jnp.int32, q.shape, 0) % 64) // 16
    zero = jnp.zeros_like(q)
    q12 = jnp.concatenate([jnp.where(grp == 2 * hl, q, zero), jnp.where(grp == 2 * hl + 1, q, zero)], axis=1)
    nq = 2 * tq
    qpos = qi * tq + lax.broadcasted_iota(jnp.int32, (1, nq), 1) % tq

    def score(j):
        return _dot(k_ref[pl.ds(pl.multiple_of(j * tk, tk), tk), :], q12)

    def value(j):
        return v_ref[:, pl.ds(pl.multiple_of(j * tk, tk), tk)]

    acc_ref, l_ref = _causal_sweep(qi, score, qpos, value, scratch)
    on = acc_ref[...] / l_ref[...]

    dotp = lambda a, b: jnp.sum(a[...] * b[...], axis=1, keepdims=True)
    lam = jnp.exp(dotp(lq1_ref, lk1_ref)) - jnp.exp(dotp(lq2_ref, lk2_ref)) + lam_init
    o = on[:, :tq] - lam * on[:, tq:]
    ms = jnp.mean(o * o, axis=0, keepdims=True)
    o = o * lax.rsqrt(ms + RMS_EPS) * g_ref[...] * (1.0 - lam_init)
    o_ref[...] = o.astype(o_ref.dtype)


def _diff_attn(qaT, ka, vaT, lq1, lk1, lq2, lk2, g_col, lam_init, tq, tk):
    B, _, T = qaT.shape
    assert tq == tk, "the causal sweep takes query block i to end at key tile i"
    nq = 2 * tq
    vec = pl.BlockSpec((1, A_QK_DIM), lambda b, h, i: (0, 0))
    return pl.pallas_call(
        functools.partial(_diff_attn_kernel, tq=tq, tk=tk, lam_init=lam_init),
        grid=(B, A_HEADS, T // tq),
        in_specs=[vec, vec, vec, vec,
                  pl.BlockSpec((A_V_DIM, 1), lambda b, h, i: (0, 0)),
                  pl.BlockSpec((None, LANES, tq), lambda b, h, i: (b, h // 2, i)),
                  pl.BlockSpec((None, T, LANES), lambda b, h, i: (b, 0, h // 2)),
                  pl.BlockSpec((None, A_V_DIM, T), lambda b, h, i: (b, h, 0))],
        out_specs=pl.BlockSpec((None, A_V_DIM, tq), lambda b, h, i: (b, h, i)),
        out_shape=jax.ShapeDtypeStruct((B, A_WIDTH, T), BF16),
        scratch_shapes=_sweep_scratch(tk, nq, A_V_DIM),
        compiler_params=_params(("parallel", "parallel", "arbitrary")),
        name="diff_attn",
    )(lq1, lk1, lq2, lk2, g_col, qaT, ka, vaT)


POOL_HALO = 16


def _pool_kernel(prev_ref, cur_ref, w_ref, scale_ref, o_ref, *, tm):
    i = pl.program_id(1)
    cur = cur_ref[...]
    halo = prev_ref[tm - POOL_HALO:, :]
    halo = jnp.where(i == 0, jnp.zeros_like(halo), halo)
    arr = jnp.concatenate([halo, cur], axis=0)
    s2 = arr + pltpu.roll(arr, 1, axis=0)
    s4 = s2 + pltpu.roll(s2, 2, axis=0)
    s8 = s4 + pltpu.roll(s4, 4, axis=0)
    s16 = s8 + pltpu.roll(s8, 8, axis=0)
    sums = [s[POOL_HALO:, :] for s in (s2, s4, s8, s16)]
    lane = lax.broadcasted_iota(jnp.int32, (tm, POOL_WIDTH), 1)
    tpos = i * tm + lax.broadcasted_iota(jnp.int32, (tm, POOL_WIDTH), 0)
    grp = lane // POOL_GROUP_DIM
    win = jnp.where(grp == 0, sums[0], jnp.where(grp == 1, sums[1], jnp.where(grp == 2, sums[2], sums[3])))
    width = jnp.where(grp == 0, 2, jnp.where(grp == 1, 4, jnp.where(grp == 2, 8, 16)))
    cnt = jnp.minimum(tpos + 1, width).astype(F32)
    y = win / cnt - cur
    z = _dot(y.astype(BF16), w_ref[...])
    o_ref[...] = (z * scale_ref[...]).astype(o_ref.dtype)


def _pool(u3, w_bd, scale, tm):
    B, T, W = u3.shape
    return pl.pallas_call(
        functools.partial(_pool_kernel, tm=tm),
        grid=(B, T // tm),
        in_specs=[pl.BlockSpec((None, tm, W), lambda b, i: (b, jnp.maximum(i - 1, 0), 0)),
                  pl.BlockSpec((None, tm, W), lambda b, i: (b, i, 0)),
                  pl.BlockSpec((W, W), lambda b, i: (0, 0)),
                  pl.BlockSpec((1, W), lambda b, i: (0, 0))],
        out_specs=pl.BlockSpec((None, tm, W), lambda b, i: (b, i, 0)),
        out_shape=jax.ShapeDtypeStruct((B, T, W), BF16),
        compiler_params=_params(("parallel", "arbitrary")),
        name="pool_mixer",
    )(u3, u3, w_bd, scale)


def _cmp_kernel(k_ref, v_ref, pk_ref, pv_ref, wk_ref, wv_ref, kc_ref, vc_ref):
    n = k_ref.shape[0]
    rowid = lax.broadcasted_iota(jnp.int32, (n, LANES), 0)

    def compress(x_ref, pe_ref, w_ref):
        x = x_ref[...].astype(F32)
        lo = _dot((x + pe_ref[0:1, :]).astype(BF16), w_ref[0])
        hi = _dot((x + pe_ref[1:2, :]).astype(BF16), w_ref[1])
        out = lo + pltpu.roll(hi, n - 1, axis=0)
        return jnp.where(rowid < n - 1, out, 0.0)

    kc_ref[...] = compress(k_ref, pk_ref, wk_ref).astype(kc_ref.dtype)
    vc_ref[...] = compress(v_ref, pv_ref, wv_ref).astype(vc_ref.dtype)


def _compress(kr, vr, pe_k2, pe_v2, wk2, wv2):
    B, n, W = kr.shape
    blk = pl.BlockSpec((None, n, W), lambda b: (b, 0, 0))
    pe = pl.BlockSpec((2, W), lambda b: (0, 0))
    wsp = pl.BlockSpec((2, W, LANES), lambda b: (0, 0, 0))
    osp = pl.BlockSpec((None, n, LANES), lambda b: (b, 0, 0))
    return pl.pallas_call(
        _cmp_kernel,
        grid=(B,),
        in_specs=[blk, blk, pe, pe, wsp, wsp],
        out_specs=[osp, osp],
        out_shape=[jax.ShapeDtypeStruct((B, n, LANES), BF16)] * 2,
        compiler_params=_params(("parallel",)),
        name="nsa_compress",
    )(kr, vr, pe_k2, pe_v2, wk2, wv2)


def _prep_cmp(pe_k, pe_v, w_ck, w_cv):
    pk = _perm_ck()
    lane = np.arange(LANES)
    g_of_k = (lane % 64) // 32
    d_of_k = pk % 64
    g_of_v = lane // 64
    d_of_v = lane % 64

    def build(w, pe, g_of, d_of):
        wl = w[:, d_of, :]
        onehot = jnp.asarray((g_of[:, None] == np.arange(C_KV_GROUPS)[None, :]).astype(np.float32))
        w2 = wl[:, :, None, :] * onehot[None, :, :, None]
        w2 = w2.reshape(2, 16 * LANES, C_KV_GROUPS * C_HEAD_DIM).astype(BF16)
        pe2 = pe[:, d_of].reshape(2, 16 * LANES)
        return w2, pe2

    wk2, pk2 = build(w_ck, pe_k, g_of_k, d_of_k)
    wv2, pv2 = build(w_cv, pe_v, g_of_v, d_of_v)
    return pk2, pv2, wk2[:, :, pk], wv2


def _nsa_kernel(q_ref, gt_ref, kc_ref, vc_ref, ov_ref, ks_ref, vs_ref, kw_ref, vw_ref, o_ref, sel_ref,
                *scratch, tq, tk, T):
    g = pl.program_id(1)
    qi = pl.program_id(2)
    q0 = qi * tq
    n_cmp = kc_ref.shape[0]
    n_blk = T // SEL_BLOCK
    d = C_HEAD_DIM
    tpos = q0 + lax.broadcasted_iota(jnp.int32, (1, tq), 1)

    nq = C_HPG * tq
    tpos3 = q0 + lax.broadcasted_iota(jnp.int32, (1, nq), 1) % tq
    rowg = (lax.broadcasted_iota(jnp.int32, (LANES, tq), 0) % 64) // 32
    qs = []
    for hh in range(C_HPG):
        qt = q_ref[hh * LANES:(hh + 1) * LANES, :]
        qs.append(jnp.where(rowg == g, qt, jnp.zeros_like(qt)))
    q3 = jnp.concatenate(qs, axis=1)

    cend = lax.broadcasted_iota(jnp.int32, (n_cmp, nq), 0) * CMP_STRIDE + (CMP_BLOCK - 1)
    s = jnp.where(cend <= tpos3, _dot(kc_ref[...], q3), MASKED)
    m = jnp.maximum(jnp.max(s, axis=0, keepdims=True), NEG_BIG)
    e = jnp.exp2(s - m)
    l = jnp.sum(e, axis=0, keepdims=True)
    p = e / jnp.where(l > 0.0, l, 1.0)
    o_cmp = _dot(vc_ref[...], p.astype(BF16))
    psum = p[:, :tq] + p[:, tq:2 * tq] + p[:, 2 * tq:]

    p_hi = psum.astype(BF16)
    p_lo = (psum - p_hi.astype(F32)).astype(BF16)
    imp = _dot(ov_ref[...], p_hi) + _dot(ov_ref[...], p_lo)
    jidx = lax.broadcasted_iota(jnp.int32, (n_blk, tq), 0).astype(F32)
    cur = (tpos // SEL_BLOCK).astype(F32)
    forced = (jidx == 0.0) | (jidx == cur) | (jidx == cur - 1.0)
    imp = jnp.where(forced, FORCED_SCORE, imp)
    imp = jnp.where(jidx > cur, -1.0, imp)
    for _ in range(min(N_SEL, n_blk)):
        mx = jnp.max(imp, axis=0, keepdims=True)
        first = jnp.min(jnp.where(imp == mx, jidx, float(n_blk)), axis=0, keepdims=True)
        imp = jnp.where(jidx == first, TAKEN, imp)
    bias = jnp.where(imp == TAKEN, 0.0, MASKED)
    sel_ref[...] = jnp.concatenate([bias] * C_HPG, axis=1)

    bpt = tk // SEL_BLOCK

    def score(j):
        rows = sel_ref[pl.ds(pl.multiple_of(j * bpt, bpt), bpt), :]
        return _dot(ks_ref[pl.ds(pl.multiple_of(j * tk, tk), tk), :], q3) + jnp.concatenate(
            [jnp.broadcast_to(rows[i:i + 1, :], (SEL_BLOCK, nq)) for i in range(bpt)], axis=0)

    def value(j):
        return vs_ref[:, pl.ds(pl.multiple_of(j * tk, tk), tk)]

    acc_ref, l_ref = _causal_sweep(q0 // tk, score, tpos3, value, scratch)
    o_slc = acc_ref[...] / l_ref[...]

    band = WINDOW + tq
    w0 = pl.multiple_of(jnp.maximum(q0 - WINDOW, 0), LANES)
    ow_ref = scratch[-1]

    def window(interior):
        s = _dot(kw_ref[pl.ds(w0, band), :], q3)
        if interior:
            r = lax.broadcasted_iota(jnp.int32, (tq, nq), 0)
            cq = tpos3 - q0
            s = jnp.concatenate([jnp.where(r > cq, s[:tq], MASKED), s[tq:band - tq],
                                 jnp.where(r <= cq, s[band - tq:], MASKED)], axis=0)
        else:
            wpos = w0 + lax.broadcasted_iota(jnp.int32, (band, nq), 0)
            s = jnp.where((wpos <= tpos3) & (wpos > tpos3 - WINDOW), s, MASKED)
        e = jnp.exp2(s - jnp.max(s, axis=0, keepdims=True))
        ow_ref[...] = _dot(vw_ref[:, pl.ds(w0, band)], e.astype(BF16)) / jnp.sum(e, axis=0, keepdims=True)

    pl.when(q0 >= WINDOW)(lambda: window(True))
    pl.when(q0 < WINDOW)(lambda: window(False))
    o_win = ow_ref[...]

    gts = gt_ref[...]
    for hh in range(C_HPG):
        r = hh * N_BRANCH
        c = slice(hh * tq, (hh + 1) * tq)
        out = (o_cmp[:, c] * gts[r:r + 1, :] + o_slc[:, c] * gts[r + 1:r + 2, :]
               + o_win[:, c] * gts[r + 2:r + 3, :])
        o_ref[hh * d:(hh + 1) * d, :] = out.astype(o_ref.dtype)


def _nsa(qcT, gtT, kc2, vcT, ovT, ksl, vslT, kwi, vwiT, tq, tk):
    B, _, T = qcT.shape
    n_cmp = kc2.shape[1]
    n_blk = T // SEL_BLOCK
    nq = C_HPG * tq
    assert tk % tq == 0, "the causal key tile must contain the whole query block"
    kv_tok =pl.BlockSpec((None, T, LANES), lambda b, g, i: (b, 0, 0))
    kv_feat = pl.BlockSpec((None, C_HEAD_DIM, T), lambda b, g, i: (b, g, 0))
    return pl.pallas_call(
        functools.partial(_nsa_kernel, tq=tq, tk=tk, T=T),
        grid=(B, C_KV_GROUPS, T // tq),
        in_specs=[pl.BlockSpec((None, C_WIDTH, tq), lambda b, g, i: (b, 0, i)),
                  pl.BlockSpec((None, None, GATE_ROWS, tq), lambda b, g, i: (b, g, 0, i)),
                  pl.BlockSpec((None, n_cmp, LANES), lambda b, g, i: (b, 0, 0)),
                  pl.BlockSpec((None, C_HEAD_DIM, n_cmp), lambda b, g, i: (b, g, 0)),
                  pl.BlockSpec((n_blk, n_cmp), lambda b, g, i: (0, 0)),
                  kv_tok, kv_feat, kv_tok, kv_feat],
        out_specs=pl.BlockSpec((None, C_HPG * C_HEAD_DIM, tq), lambda b, g, i: (b, g, i)),
        out_shape=jax.ShapeDtypeStruct((B, C_WIDTH, T), BF16),
        scratch_shapes=([pltpu.VMEM((n_blk, nq), F32)] + _sweep_scratch(tk, nq, C_HEAD_DIM)
                        + [pltpu.VMEM((C_HEAD_DIM, nq), F32)]),
        compiler_params=_params(("parallel", "parallel", "arbitrary")),
        name="nsa_attn",
    )(qcT, gtT, kc2, vcT, ovT, ksl, vslT, kwi, vwiT)


def _overlap_T(T):
    n_cmp = T // CMP_STRIDE
    n_blk = T // SEL_BLOCK
    cs = np.arange(n_cmp)[None, :] * CMP_STRIDE
    bs = np.arange(n_blk)[:, None] * SEL_BLOCK
    ov = (cs < bs + SEL_BLOCK) & (cs + CMP_BLOCK > bs) & (np.arange(n_cmp)[None, :] < n_cmp - 1)
    return jnp.asarray(ov.astype(np.float32), dtype=BF16)


def _layer_norm(y, g, b):
    mu = jnp.mean(y, axis=-1, keepdims=True)
    yc = y - mu
    var = jnp.mean(yc * yc, axis=-1, keepdims=True)
    return yc * lax.rsqrt(var + LN_EPS) * g + b


def _outproj_kernel(x_ref, oa_ref, ob_ref, oc_ref, wa_ref, wb_ref, wc_ref, g_ref, b_ref, o_ref, *, alpha):
    mix = _dot(oa_ref[...], wa_ref[...]) + _dot(ob_ref[...], wb_ref[...]) + _dot(oc_ref[...], wc_ref[...])
    o_ref[...] = _layer_norm(alpha * x_ref[...] + mix, g_ref[...], b_ref[...])


def _outproj(x2, oa, ob, oc, wa, wb, wc, g, b, alpha, tm):
    BT, D = x2.shape
    row = lambda n: pl.BlockSpec((tm, n), lambda i: (i, 0))
    full = lambda a: pl.BlockSpec(a.shape, lambda i: (0, 0))
    return pl.pallas_call(
        functools.partial(_outproj_kernel, alpha=alpha),
        grid=(BT // tm,),
        in_specs=[row(D), row(oa.shape[1]), row(ob.shape[1]), row(oc.shape[1]),
                  full(wa), full(wb), full(wc), full(g), full(b)],
        out_specs=row(D),
        out_shape=jax.ShapeDtypeStruct((BT, D), F32),
        compiler_params=_params(("parallel",)),
        name="outproj_ln",
    )(x2, oa, ob, oc, wa, wb, wc, g, b)


def _mlp_kernel(x_ref, wu_ref, wd_ref, g_ref, b_ref, o_ref, acc_ref, *, alpha):
    f = pl.program_id(1)

    @pl.when(f == 0)
    def _():
        acc_ref[...] = jnp.zeros_like(acc_ref)

    hid = _dot(x_ref[...].astype(BF16), wu_ref[...])
    hid = jnp.square(jnp.maximum(hid, 0.0))
    acc_ref[...] += _dot(hid.astype(BF16), wd_ref[...])

    @pl.when(f == pl.num_programs(1) - 1)
    def _():
        o_ref[...] = _layer_norm(alpha * x_ref[...] + acc_ref[...], g_ref[...], b_ref[...])


def _mlp(x2, wu, wd, g, b, alpha, tm, tf):
    BT, D = x2.shape
    F = wu.shape[1]
    return pl.pallas_call(
        functools.partial(_mlp_kernel, alpha=alpha),
        grid=(BT // tm, F // tf),
        in_specs=[pl.BlockSpec((tm, D), lambda i, f: (i, 0)),
                  pl.BlockSpec((D, tf), lambda i, f: (0, f)),
                  pl.BlockSpec((tf, D), lambda i, f: (f, 0)),
                  pl.BlockSpec((1, D), lambda i, f: (0, 0)),
                  pl.BlockSpec((1, D), lambda i, f: (0, 0))],
        out_specs=pl.BlockSpec((tm, D), lambda i, f: (i, 0)),
        out_shape=jax.ShapeDtypeStruct((BT, D), F32),
        scratch_shapes=[pltpu.VMEM((tm, D), F32)],
        compiler_params=_params(("parallel", "arbitrary")),
        name="mlp_ln",
    )(x2, wu, wd, g, b)


def _tiles(T):
    return dict(
        tm_proj=min(512, T), tm_pool=min(512, T), tm_out=min(512, T), tm_mlp=min(1024, T), tf_mlp=1024,
        a_tq=min(512, T), a_tk=min(512, T), c_tq=min(256, T), c_tk=min(512, T),
    )


def _to_feat(a, B, T):
    return a.reshape(B, T, a.shape[-1]).transpose(0, 2, 1)


GATE_ROWS = 16


def _gates_feat(gt, B, T):
    per = C_HPG * N_BRANCH
    g4 = gt[:, :C_KV_GROUPS * per].reshape(B, T, C_KV_GROUPS, per)
    g4 = jnp.pad(g4, ((0, 0), (0, 0), (0, 0), (0, GATE_ROWS - per)))
    return g4.transpose(0, 2, 3, 1)


def _to_tok(a):
    B, n, T = a.shape
    return a.transpose(0, 2, 1).reshape(B * T, n)


def kernel(x, w_in, lam_q1, lam_k1, lam_q2, lam_k2, subln_g, w_pool, pool_scale, cmp_pe_k, cmp_pe_v,
           w_cmp_k, w_cmp_v, w_out, ln1_g, ln1_b, w_up, w_down, ln2_g, ln2_b):
    B, T, D = x.shape
    depth = w_in.shape[0]
    alpha = (2 * depth) ** 0.25
    tl = _tiles(T)
    tabs = _rope_tables(T)
    ovT = _overlap_T(T)
    x2 = x.reshape(B * T, D)
    for l in range(depth):
        lam_init = 0.8 - 0.6 * math.exp(-0.3 * l)
        (qa, ka, va, u, qc, kcm, vcm, ksl, vsl, kwi, vwi, gt) = _inproj(
            x2, _prep_w_in(w_in[l]), tabs, tl["tm_proj"], T)

        oaT = _diff_attn(_to_feat(qa, B, T), ka.reshape(B, T, -1), _to_feat(va, B, T),
                         lam_q1[l][None, :], lam_k1[l][None, :], lam_q2[l][None, :], lam_k2[l][None, :],
                         subln_g[l][:, None], lam_init, tl["a_tq"], tl["a_tk"])

        w_bd = jax.scipy.linalg.block_diag(*[w_pool[l, g] for g in range(len(POOL_WINDOWS))]).astype(BF16)
        ob = _pool(u.reshape(B, T, POOL_WIDTH), w_bd, pool_scale[l][None, :], tl["tm_pool"])

        pk2, pv2, wk2, wv2 = _prep_cmp(cmp_pe_k[l], cmp_pe_v[l], w_cmp_k[l], w_cmp_v[l])
        n16 = T // CMP_STRIDE
        kc2, vc2 = _compress(kcm.reshape(B, n16, 16 * LANES), vcm.reshape(B, n16, 16 * LANES),
                             pk2, pv2, wk2, wv2)
        ocT = _nsa(_to_feat(qc, B, T), _gates_feat(gt, B, T), kc2, vc2.transpose(0, 2, 1), ovT,
                   ksl.reshape(B, T, LANES), _to_feat(vsl, B, T), kwi.reshape(B, T, LANES),
                   _to_feat(vwi, B, T), tl["c_tq"], tl["c_tk"])

        wo = w_out[l].astype(BF16)
        x2 = _outproj(x2, _to_tok(oaT), ob.reshape(B * T, POOL_WIDTH), _to_tok(ocT),
                      wo[:A_WIDTH], wo[A_WIDTH:A_WIDTH + POOL_WIDTH], wo[A_WIDTH + POOL_WIDTH:],
                      ln1_g[l][None, :], ln1_b[l][None, :], alpha, tl["tm_out"])
        x2 = _mlp(x2, w_up[l].astype(BF16), w_down[l].astype(BF16), ln2_g[l][None, :], ln2_b[l][None, :],
                  alpha, tl["tm_mlp"], tl["tf_mlp"])
    return x2.reshape(B, T, D)
```

```python
import functools
import math

import jax
import jax.numpy as jnp
import numpy as np
from jax import lax
from jax.experimental import pallas as pl
from jax.experimental.pallas import tpu as pltpu

A_HEADS = 6
A_QK_DIM = 32
A_V_DIM = 64
A_WIDTH = A_HEADS * A_V_DIM
POOL_WINDOWS = (2, 4, 8, 16)
POOL_GROUP_DIM = 64
POOL_WIDTH = 256
C_HEADS = 6
C_KV_GROUPS = 2
C_HPG = 3
C_HEAD_DIM = 64
C_WIDTH = C_HEADS * C_HEAD_DIM
CMP_BLOCK = 32
CMP_STRIDE = 16
SEL_BLOCK = 64
N_SEL = 16
WINDOW = 512
N_BRANCH = 3
FORCED_SCORE = 1.0e4
ROPE_THETA = 10000.0
LN_EPS = 1e-5
RMS_EPS = 1e-6

LANES = 128
LOG2E = 1.4426950408889634
NEG_BIG = -1e30
SWEEP_UNROLL = 4
TAKEN = -3.0e38
MASKED = -3e38
VMEM_LIMIT = 56 * 1024 * 1024

BF16 = jnp.bfloat16
F32 = jnp.float32


def _dot(a, b):
    return jnp.dot(a, b, preferred_element_type=F32)


def _params(sem, vmem=VMEM_LIMIT):
    return pltpu.CompilerParams(dimension_semantics=sem, vmem_limit_bytes=vmem)


def _perm_diff():
    idx = np.zeros(A_HEADS * 2 * A_QK_DIM, np.int32)
    for p in range(A_HEADS // 2):
        for l in range(LANES):
            half, grp, j = l // 64, (l % 64) // 16, l % 16
            hl, m = grp // 2, grp % 2
            idx[p * LANES + l] = (2 * p + hl) * 64 + m * 32 + half * 16 + j
    return idx


def _perm_cq():
    idx = np.zeros(C_WIDTH, np.int32)
    for hh in range(C_HPG):
        for l in range(LANES):
            half, g, j = l // 64, (l % 64) // 32, l % 32
            idx[hh * LANES + l] = (g * C_HPG + hh) * 64 + half * 32 + j
    return idx


def _perm_ck():
    idx = np.zeros(C_KV_GROUPS * C_HEAD_DIM, np.int32)
    for l in range(LANES):
        half, g, j = l // 64, (l % 64) // 32, l % 32
        idx[l] = g * 64 + half * 32 + j
    return idx


_IN_SPLITS = (384, 384, 384, 256, 384, 128, 128, 128, 128, 128, 128, 18)
_OFFS = np.concatenate([[0], np.cumsum(_IN_SPLITS)]).astype(np.int64)
A_QSCALE = A_QK_DIM ** -0.5 * LOG2E
C_QSCALE = C_HEAD_DIM ** -0.5 * LOG2E
GATE_ROWS = 16
N_TOK = 384 + 256 + 4 * LANES
N_FEAT = 3 * 384 + 2 * LANES + C_KV_GROUPS * GATE_ROWS


def _prep_w_in(w):
    o = _OFFS
    pd, pq, pk = _perm_diff(), _perm_cq(), _perm_ck()
    seg = lambda i: w[:, o[i]:o[i + 1]]
    w_tok = jnp.concatenate([seg(1)[:, pd], seg(3), seg(5)[:, pk], seg(6), seg(7)[:, pk], seg(9)[:, pk]], axis=1)
    per = C_HPG * N_BRANCH
    gates = jnp.pad(seg(11).reshape(-1, C_KV_GROUPS, per), ((0, 0), (0, 0), (0, GATE_ROWS - per)))
    w_feat = jnp.concatenate([seg(0)[:, pd], seg(2), seg(4)[:, pq], seg(8), seg(10),
                              gates.reshape(-1, C_KV_GROUPS * GATE_ROWS)], axis=1)
    return w_tok.astype(BF16), w_feat.T.astype(BF16)


def _rope_tables(T):
    def tab(dim):
        half = dim // 2
        inv = 1.0 / (ROPE_THETA ** (jnp.arange(0, dim, 2, dtype=F32) / dim))
        ang = jnp.arange(T, dtype=F32)[:, None] * inv[None, :]
        lane = np.arange(LANES)
        j = lane % half
        sign = np.where(lane < 64, -1.0, 1.0).astype(np.float32)
        cos, sin = jnp.cos(ang), jnp.sin(ang)
        return cos[:, j], sin[:, j] * sign[None, :], cos[:, j[:64]].T, sin[:, j[:64]].T
    return tab(A_QK_DIM) + tab(C_HEAD_DIM)


def _inproj_kernel(x_ref, wt_ref, wf_ref, ca_ref, sa_ref, caT_ref, saT_ref, cc_ref, sc_ref, ccT_ref, scT_ref,
                   ka_ref, u_ref, kcm_ref, vcm_ref, ksl_ref, kwi_ref,
                   qa_ref, va_ref, qc_ref, vsl_ref, vwi_ref, gt_ref):
    xb = x_ref[...].astype(BF16)
    tok = _dot(xb, wt_ref[...])
    feat = lax.dot_general(wf_ref[...], xb, (((1,), (1,)), ((), ())), preferred_element_type=F32)

    def rope_tok(p, cos, sin):
        return p * cos + pltpu.roll(p, 64, axis=1) * sin

    def rope_feat(p, cos, sin):
        a, b = p[:64, :], p[64:, :]
        return jnp.concatenate([a * cos - b * sin, b * cos + a * sin], axis=0)

    ca, sa, cc, sc = ca_ref[...], sa_ref[...], cc_ref[...], sc_ref[...]
    caT, saT, ccT, scT = caT_ref[...], saT_ref[...], ccT_ref[...], scT_ref[...]
    for t in range(3):
        r = slice(t * LANES, (t + 1) * LANES)
        ka_ref[:, r] = rope_tok(tok[:, r], ca, sa).astype(BF16)
        qa_ref[r, :] = (rope_feat(feat[r, :], caT, saT) * A_QSCALE).astype(BF16)
        rq = slice(768 + t * LANES, 768 + (t + 1) * LANES)
        qc_ref[r, :] = (rope_feat(feat[rq, :], ccT, scT) * C_QSCALE).astype(BF16)
    u_ref[...] = tok[:, 384:640]
    kcm_ref[...] = rope_tok(tok[:, 640:768], cc, sc).astype(BF16)
    vcm_ref[...] = tok[:, 768:896].astype(BF16)
    ksl_ref[...] = rope_tok(tok[:, 896:1024], cc, sc).astype(BF16)
    kwi_ref[...] = rope_tok(tok[:, 1024:1152], cc, sc).astype(BF16)
    va_ref[...] = feat[384:768, :].astype(BF16)
    vsl_ref[...] = feat[1152:1280, :].astype(BF16)
    vwi_ref[...] = feat[1280:1408, :].astype(BF16)
    gt_ref[...] = jax.nn.sigmoid(feat[1408:, :])


def _inproj(x3, w_tok, w_feat, tabs, tm):
    B, T, D = x3.shape
    tok = lambda n: pl.BlockSpec((None, tm, n), lambda b, i: (b, i, 0))
    feat = lambda n: pl.BlockSpec((None, n, tm), lambda b, i: (b, 0, i))
    full = lambda a: pl.BlockSpec(a.shape, lambda b, i: (0, 0))
    tab_tok = pl.BlockSpec((tm, LANES), lambda b, i: (i, 0))
    tab_feat = pl.BlockSpec((64, tm), lambda b, i: (0, i))
    tok_outs = [(384, BF16), (256, F32)] + [(LANES, BF16)] * 4
    feat_outs = [(384, BF16)] * 3 + [(LANES, BF16)] * 2 + [(C_KV_GROUPS * GATE_ROWS, F32)]
    return pl.pallas_call(
        _inproj_kernel,
        grid=(B, T // tm),
        in_specs=[tok(D), full(w_tok), full(w_feat)] + [tab_tok, tab_tok, tab_feat, tab_feat] * 2,
        out_specs=[tok(n) for n, _ in tok_outs] + [feat(n) for n, _ in feat_outs],
        out_shape=([jax.ShapeDtypeStruct((B, T, n), dt) for n, dt in tok_outs]
                   + [jax.ShapeDtypeStruct((B, n, T), dt) for n, dt in feat_outs]),
        compiler_params=_params(("parallel", "parallel")),
        name="inproj_rope",
    )(x3, w_tok, w_feat, *tabs)


def _sweep_scratch(tk, nq, dv):
    return [pltpu.VMEM((tk, nq), F32), pltpu.VMEM((tk, nq), F32),
            pltpu.VMEM((1, nq), F32), pltpu.VMEM((1, nq), F32),
            pltpu.VMEM((1, nq), F32), pltpu.VMEM((1, nq), F32), pltpu.VMEM((dv, nq), F32)]


def _causal_sweep(n_full, score, qpos, value, scratch):
    s_refs, cm_refs = scratch[0:2], scratch[2:4]
    m_ref, l_ref, acc_ref = scratch[4:7]

    tk, nq = s_refs[0].shape

    def scores(j, slot):
        s = score(j)
        s_refs[slot][...] = s
        cm_refs[slot][...] = jnp.max(s, axis=0, keepdims=True)

    def softmax(slot, v, causal_tile=None):
        s = s_refs[slot][...]
        if causal_tile is None:
            cm = cm_refs[slot][...]
        else:
            kpos = causal_tile * tk + lax.broadcasted_iota(jnp.int32, (tk, nq), 0)
            s = jnp.where(kpos <= qpos, s, MASKED)
            cm = jnp.max(s, axis=0, keepdims=True)
        m = m_ref[...]
        m_new = jnp.maximum(m, cm)
        alpha = jnp.exp2(m - m_new)
        p = jnp.exp2(s - m_new)
        l_ref[...] = l_ref[...] * alpha + jnp.sum(p, axis=0, keepdims=True)
        m_ref[...] = m_new
        acc_ref[...] = acc_ref[...] * alpha + _dot(v, p.astype(BF16))

    m_ref[...] = jnp.full(m_ref.shape, NEG_BIG, F32)
    l_ref[...] = jnp.zeros(l_ref.shape, F32)
    acc_ref[...] = jnp.zeros(acc_ref.shape, F32)
    scores(0, 0)

    def group(base, tiles):
        for u in range(tiles):
            scores(base + u + 1, (u + 1) % 2)
            softmax(u % 2, value(base + u))

    def quad(i, c):
        group(SWEEP_UNROLL * i, SWEEP_UNROLL)
        return c

    def pair(i, c):
        group(done + 2 * i, 2)
        return c

    lax.fori_loop(0, n_full // SWEEP_UNROLL, quad, 0)
    done = (n_full // SWEEP_UNROLL) * SWEEP_UNROLL
    lax.fori_loop(0, (n_full - done) // 2, pair, 0)

    @pl.when(n_full % 2 == 0)
    def _():
        softmax(0, value(n_full), causal_tile=n_full)

    @pl.when(n_full % 2 == 1)
    def _():
        scores(n_full, 1)
        softmax(0, value(n_full - 1))
        softmax(1, value(n_full), causal_tile=n_full)

    return acc_ref, l_ref


def _diff_attn_kernel(lq1_ref, lk1_ref, lq2_ref, lk2_ref, g_ref, q_ref, k_ref, v_ref, o_ref,
                      *scratch, tq, tk, lam_init):
    h = pl.program_id(1)
    qi = pl.program_id(2)
    hl = h % 2
    q = q_ref[...]
    grp = (lax.broadcasted_iota(jnp.int32, q.shape, 0) % 64) // 16
    zero = jnp.zeros_like(q)
    q12 = jnp.concatenate([jnp.where(grp == 2 * hl, q, zero), jnp.where(grp == 2 * hl + 1, q, zero)], axis=1)
    nq = 2 * tq
    qpos = qi * tq + lax.broadcasted_iota(jnp.int32, (1, nq), 1) % tq

    def score(j):
        return _dot(k_ref[pl.ds(pl.multiple_of(j * tk, tk), tk), :], q12)

    def value(j):
        return v_ref[:, pl.ds(pl.multiple_of(j * tk, tk), tk)]

    acc_ref, l_ref = _causal_sweep(qi, score, qpos, value, scratch)
    on = acc_ref[...] / l_ref[...]

    dotp = lambda a, b: jnp.sum(a[...] * b[...], axis=1, keepdims=True)
    lam = jnp.exp(dotp(lq1_ref, lk1_ref)) - jnp.exp(dotp(lq2_ref, lk2_ref)) + lam_init
    o = on[:, :tq] - lam * on[:, tq:]
    ms = jnp.mean(o * o, axis=0, keepdims=True)
    o = o * lax.rsqrt(ms + RMS_EPS) * g_ref[...] * (1.0 - lam_init)
    o_ref[...] = o.astype(o_ref.dtype)


def _diff_attn(qaT, ka, vaT, lq1, lk1, lq2, lk2, g_col, lam_init, tq, tk):
    B, _, T = qaT.shape
    assert tq == tk, "the causal sweep takes query block i to end at key tile i"
    nq = 2 * tq
    vec = pl.BlockSpec((1, A_QK_DIM), lambda b, h, i: (0, 0))
    return pl.pallas_call(
        functools.partial(_diff_attn_kernel, tq=tq, tk=tk, lam_init=lam_init),
        grid=(B, A_HEADS, T // tq),
        in_specs=[vec, vec, vec, vec,
                  pl.BlockSpec((A_V_DIM, 1), lambda b, h, i: (0, 0)),
                  pl.BlockSpec((None, LANES, tq), lambda b, h, i: (b, h // 2, i)),
                  pl.BlockSpec((None, T, LANES), lambda b, h, i: (b, 0, h // 2)),
                  pl.BlockSpec((None, A_V_DIM, T), lambda b, h, i: (b, h, 0))],
        out_specs=pl.BlockSpec((None, A_V_DIM, tq), lambda b, h, i: (b, h, i)),
        out_shape=jax.ShapeDtypeStruct((B, A_WIDTH, T), BF16),
        scratch_shapes=_sweep_scratch(tk, nq, A_V_DIM),
        compiler_params=_params(("parallel", "parallel", "arbitrary")),
        name="diff_attn",
    )(lq1, lk1, lq2, lk2, g_col, qaT, ka, vaT)


POOL_HALO = 16


def _pool_kernel(prev_ref, cur_ref, w_ref, scale_ref, o_ref, *, tm):
    i = pl.program_id(1)
    cur = cur_ref[...]
    halo = prev_ref[tm - POOL_HALO:, :]
    halo = jnp.where(i == 0, jnp.zeros_like(halo), halo)
    arr = jnp.concatenate([halo, cur], axis=0)
    s2 = arr + pltpu.roll(arr, 1, axis=0)
    s4 = s2 + pltpu.roll(s2, 2, axis=0)
    s8 = s4 + pltpu.roll(s4, 4, axis=0)
    s16 = s8 + pltpu.roll(s8, 8, axis=0)
    sums = [s[POOL_HALO:, :] for s in (s2, s4, s8, s16)]
    lane = lax.broadcasted_iota(jnp.int32, (tm, POOL_WIDTH), 1)
    tpos = i * tm + lax.broadcasted_iota(jnp.int32, (tm, POOL_WIDTH), 0)
    grp = lane // POOL_GROUP_DIM
    win = jnp.where(grp == 0, sums[0], jnp.where(grp == 1, sums[1], jnp.where(grp == 2, sums[2], sums[3])))
    width = jnp.where(grp == 0, 2, jnp.where(grp == 1, 4, jnp.where(grp == 2, 8, 16)))
    cnt = jnp.minimum(tpos + 1, width).astype(F32)
    y = win / cnt - cur
    z = _dot(y.astype(BF16), w_ref[...])
    o_ref[...] = (z * scale_ref[...]).astype(o_ref.dtype)


def _pool(u3, w_bd, scale, tm):
    B, T, W = u3.shape
    return pl.pallas_call(
        functools.partial(_pool_kernel, tm=tm),
        grid=(B, T // tm),
        in_specs=[pl.BlockSpec((None, tm, W), lambda b, i: (b, jnp.maximum(i - 1, 0), 0)),
                  pl.BlockSpec((None, tm, W), lambda b, i: (b, i, 0)),
                  pl.BlockSpec((W, W), lambda b, i: (0, 0)),
                  pl.BlockSpec((1, W), lambda b, i: (0, 0))],
        out_specs=pl.BlockSpec((None, tm, W), lambda b, i: (b, i, 0)),
        out_shape=jax.ShapeDtypeStruct((B, T, W), BF16),
        compiler_params=_params(("parallel", "arbitrary")),
        name="pool_mixer",
    )(u3, u3, w_bd, scale)


def _cmp_kernel(k_ref, v_ref, pk_ref, pv_ref, wk_ref, wv_ref, kc_ref, vc_ref):
    n = k_ref.shape[0]
    rowid = lax.broadcasted_iota(jnp.int32, (n, LANES), 0)

    def compress(x_ref, pe_ref, w_ref):
        x = x_ref[...].astype(F32)
        lo = _dot((x + pe_ref[0:1, :]).astype(BF16), w_ref[0])
        hi = _dot((x + pe_ref[1:2, :]).astype(BF16), w_ref[1])
        out = lo + pltpu.roll(hi, n - 1, axis=0)
        return jnp.where(rowid < n - 1, out, 0.0)

    kc_ref[...] = compress(k_ref, pk_ref, wk_ref).astype(kc_ref.dtype)
    vc_ref[...] = compress(v_ref, pv_ref, wv_ref).astype(vc_ref.dtype)


def _compress(kr, vr, pe_k2, pe_v2, wk2, wv2):
    B, n, W = kr.shape
    blk = pl.BlockSpec((None, n, W), lambda b: (b, 0, 0))
    pe = pl.BlockSpec((2, W), lambda b: (0, 0))
    wsp = pl.BlockSpec((2, W, LANES), lambda b: (0, 0, 0))
    osp = pl.BlockSpec((None, n, LANES), lambda b: (b, 0, 0))
    return pl.pallas_call(
        _cmp_kernel,
        grid=(B,),
        in_specs=[blk, blk, pe, pe, wsp, wsp],
        out_specs=[osp, osp],
        out_shape=[jax.ShapeDtypeStruct((B, n, LANES), BF16)] * 2,
        compiler_params=_params(("parallel",)),
        name="nsa_compress",
    )(kr, vr, pe_k2, pe_v2, wk2, wv2)


def _prep_cmp(pe_k, pe_v, w_ck, w_cv):
    pk = _perm_ck()
    lane = np.arange(LANES)
    g_of_k = (lane % 64) // 32
    d_of_k = pk % 64
    g_of_v = lane // 64
    d_of_v = lane % 64

    def build(w, pe, g_of, d_of):
        wl = w[:, d_of, :]
        onehot = jnp.asarray((g_of[:, None] == np.arange(C_KV_GROUPS)[None, :]).astype(np.float32))
        w2 = wl[:, :, None, :] * onehot[None, :, :, None]
        w2 = w2.reshape(2, 16 * LANES, C_KV_GROUPS * C_HEAD_DIM).astype(BF16)
        pe2 = pe[:, d_of].reshape(2, 16 * LANES)
        return w2, pe2

    wk2, pk2 = build(w_ck, pe_k, g_of_k, d_of_k)
    wv2, pv2 = build(w_cv, pe_v, g_of_v, d_of_v)
    return pk2, pv2, wk2[:, :, pk], wv2


def _nsa_kernel(q_ref, gt_ref, kc_ref, vc_ref, ov_ref, ks_ref, vs_ref, kw_ref, vw_ref, o_ref, sel_ref,
                *scratch, tq, tk, T):
    g = pl.program_id(1)
    qi = pl.program_id(2)
    q0 = qi * tq
    n_cmp = kc_ref.shape[0]
    n_blk = T // SEL_BLOCK
    d = C_HEAD_DIM
    tpos = q0 + lax.broadcasted_iota(jnp.int32, (1, tq), 1)

    nq = C_HPG * tq
    tpos3 = q0 + lax.broadcasted_iota(jnp.int32, (1, nq), 1) % tq
    rowg = (lax.broadcasted_iota(jnp.int32, (LANES, tq), 0) % 64) // 32
    qs = []
    for hh in range(C_HPG):
        qt = q_ref[hh * LANES:(hh + 1) * LANES, :]
        qs.append(jnp.where(rowg == g, qt, jnp.zeros_like(qt)))
    q3 = jnp.concatenate(qs, axis=1)

    cend = lax.broadcasted_iota(jnp.int32, (n_cmp, nq), 0) * CMP_STRIDE + (CMP_BLOCK - 1)
    s = jnp.where(cend <= tpos3, _dot(kc_ref[...], q3), MASKED)
    m = jnp.maximum(jnp.max(s, axis=0, keepdims=True), NEG_BIG)
    e = jnp.exp2(s - m)
    l = jnp.sum(e, axis=0, keepdims=True)
    p = e / jnp.where(l > 0.0, l, 1.0)
    o_cmp = _dot(vc_ref[...], p.astype(BF16))
    psum = p[:, :tq] + p[:, tq:2 * tq] + p[:, 2 * tq:]

    p_hi = psum.astype(BF16)
    p_lo = (psum - p_hi.astype(F32)).astype(BF16)
    imp = _dot(ov_ref[...], p_hi) + _dot(ov_ref[...], p_lo)
    jidx = lax.broadcasted_iota(jnp.int32, (n_blk, tq), 0).astype(F32)
    cur = (tpos // SEL_BLOCK).astype(F32)
    forced = (jidx == 0.0) | (jidx == cur) | (jidx == cur - 1.0)
    imp = jnp.where(forced, FORCED_SCORE, imp)
    imp = jnp.where(jidx > cur, -1.0, imp)
    for _ in range(min(N_SEL, n_blk)):
        mx = jnp.max(imp, axis=0, keepdims=True)
        first = jnp.min(jnp.where(imp == mx, jidx, float(n_blk)), axis=0, keepdims=True)
        imp = jnp.where(jidx == first, TAKEN, imp)
    bias = jnp.where(imp == TAKEN, 0.0, MASKED)
    sel_ref[...] = jnp.concatenate([bias] * C_HPG, axis=1)

    bpt = tk // SEL_BLOCK

    def score(j):
        rows = sel_ref[pl.ds(pl.multiple_of(j * bpt, bpt), bpt), :]
        return _dot(ks_ref[pl.ds(pl.multiple_of(j * tk, tk), tk), :], q3) + jnp.concatenate(
            [jnp.broadcast_to(rows[i:i + 1, :], (SEL_BLOCK, nq)) for i in range(bpt)], axis=0)

    def value(j):
        return vs_ref[:, pl.ds(pl.multiple_of(j * tk, tk), tk)]

    acc_ref, l_ref = _causal_sweep(q0 // tk, score, tpos3, value, scratch)
    o_slc = acc_ref[...] / l_ref[...]

    band = WINDOW + tq
    w0 = pl.multiple_of(jnp.maximum(q0 - WINDOW, 0), LANES)
    ow_ref = scratch[-1]

    def window(interior):
        s = _dot(kw_ref[pl.ds(w0, band), :], q3)
        if interior:
            r = lax.broadcasted_iota(jnp.int32, (tq, nq), 0)
            cq = tpos3 - q0
            s = jnp.concatenate([jnp.where(r > cq, s[:tq], MASKED), s[tq:band - tq],
                                 jnp.where(r <= cq, s[band - tq:], MASKED)], axis=0)
        else:
            wpos = w0 + lax.broadcasted_iota(jnp.int32, (band, nq), 0)
            s = jnp.where((wpos <= tpos3) & (wpos > tpos3 - WINDOW), s, MASKED)
        e = jnp.exp2(s - jnp.max(s, axis=0, keepdims=True))
        ow_ref[...] = _dot(vw_ref[:, pl.ds(w0, band)], e.astype(BF16)) / jnp.sum(e, axis=0, keepdims=True)

    pl.when(q0 >= WINDOW)(lambda: window(True))
    pl.when(q0 < WINDOW)(lambda: window(False))
    o_win = ow_ref[...]

    gts = gt_ref[...]
    for hh in range(C_HPG):
        r = hh * N_BRANCH
        c = slice(hh * tq, (hh + 1) * tq)
        out = (o_cmp[:, c] * gts[r:r + 1, :] + o_slc[:, c] * gts[r + 1:r + 2, :]
               + o_win[:, c] * gts[r + 2:r + 3, :])
        o_ref[hh * d:(hh + 1) * d, :] = out.astype(o_ref.dtype)


def _nsa(qcT, gtT, kc2, vcT, ovT, ksl, vslT, kwi, vwiT, tq, tk):
    B, _, T = qcT.shape
    n_cmp = kc2.shape[1]
    n_blk = T // SEL_BLOCK
    nq = C_HPG * tq
    assert tk % tq == 0, "the causal key tile must contain the whole query block"
    kv_tok = pl.BlockSpec((None, T, LANES), lambda b, g, i: (b, 0, 0))
    kv_feat = pl.BlockSpec((None, C_HEAD_DIM, T), lambda b, g, i: (b, g, 0))
    return pl.pallas_call(
        functools.partial(_nsa_kernel, tq=tq, tk=tk, T=T),
        grid=(B, C_KV_GROUPS, T // tq),
        in_specs=[pl.BlockSpec((None, C_WIDTH, tq), lambda b, g, i: (b, 0, i)),
                  pl.BlockSpec((None, GATE_ROWS, tq), lambda b, g, i: (b, g, i)),
                  pl.BlockSpec((None, n_cmp, LANES), lambda b, g, i: (b, 0, 0)),
                  pl.BlockSpec((None, C_HEAD_DIM, n_cmp), lambda b, g, i: (b, g, 0)),
                  pl.BlockSpec((n_blk, n_cmp), lambda b, g, i: (0, 0)),
                  kv_tok, kv_feat, kv_tok, kv_feat],
        out_specs=pl.BlockSpec((None, C_HPG * C_HEAD_DIM, tq), lambda b, g, i: (b, g, i)),
        out_shape=jax.ShapeDtypeStruct((B, C_WIDTH, T), BF16),
        scratch_shapes=([pltpu.VMEM((n_blk, nq), F32)] + _sweep_scratch(tk, nq, C_HEAD_DIM)
                        + [pltpu.VMEM((C_HEAD_DIM, nq), F32)]),
        compiler_params=_params(("parallel", "parallel", "arbitrary")),
        name="nsa_attn",
    )(qcT, gtT, kc2, vcT, ovT, ksl, vslT, kwi, vwiT)


def _overlap_T(T):
    n_cmp = T // CMP_STRIDE
    n_blk = T // SEL_BLOCK
    cs = np.arange(n_cmp)[None, :] * CMP_STRIDE
    bs = np.arange(n_blk)[:, None] * SEL_BLOCK
    ov = (cs < bs + SEL_BLOCK) & (cs + CMP_BLOCK > bs) & (np.arange(n_cmp)[None, :] < n_cmp - 1)
    return jnp.asarray(ov.astype(np.float32), dtype=BF16)


def _layer_norm(y, g, b):
    mu = jnp.mean(y, axis=-1, keepdims=True)
    yc = y - mu
    var = jnp.mean(yc * yc, axis=-1, keepdims=True)
    return yc * lax.rsqrt(var + LN_EPS) * g + b


def _outproj_kernel(x_ref, oa_ref, ob_ref, oc_ref, wa_ref, wb_ref, wc_ref, g_ref, b_ref, o_ref, *, alpha):
    tn = lambda a, w: lax.dot_general(a, w, (((0,), (0,)), ((), ())), preferred_element_type=F32)
    mix = tn(oa_ref[...], wa_ref[...]) + _dot(ob_ref[...], wb_ref[...]) + tn(oc_ref[...], wc_ref[...])
    o_ref[...] = _layer_norm(alpha * x_ref[...] + mix, g_ref[...], b_ref[...])


def _outproj(x3, oaT, ob, ocT, wa, wb, wc, g, b, alpha, tm):
    B, T, D = x3.shape
    tok = lambda n: pl.BlockSpec((None, tm, n), lambda bb, i: (bb, i, 0))
    feat = lambda n: pl.BlockSpec((None, n, tm), lambda bb, i: (bb, 0, i))
    full = lambda a: pl.BlockSpec(a.shape, lambda bb, i: (0, 0))
    return pl.pallas_call(
        functools.partial(_outproj_kernel, alpha=alpha),
        grid=(B, T // tm),
        in_specs=[tok(D), feat(oaT.shape[1]), tok(ob.shape[2]), feat(ocT.shape[1]),
                  full(wa), full(wb), full(wc), full(g), full(b)],
        out_specs=tok(D),
        out_shape=jax.ShapeDtypeStruct((B, T, D), F32),
        compiler_params=_params(("parallel", "parallel")),
        name="outproj_ln",
    )(x3, oaT, ob, ocT, wa, wb, wc, g, b)


def _mlp_kernel(x_ref, wu_ref, wd_ref, g_ref, b_ref, o_ref, acc_ref, *, alpha):
    f = pl.program_id(1)

    @pl.when(f == 0)
    def _():
        acc_ref[...] = jnp.zeros_like(acc_ref)

    hid = _dot(x_ref[...].astype(BF16), wu_ref[...])
    hid = jnp.square(jnp.maximum(hid, 0.0))
    acc_ref[...] += _dot(hid.astype(BF16), wd_ref[...])

    @pl.when(f == pl.num_programs(1) - 1)
    def _():
        o_ref[...] = _layer_norm(alpha * x_ref[...] + acc_ref[...], g_ref[...], b_ref[...])


def _mlp(x2, wu, wd, g, b, alpha, tm, tf):
    BT, D = x2.shape
    F = wu.shape[1]
    return pl.pallas_call(
        functools.partial(_mlp_kernel, alpha=alpha),
        grid=(BT // tm, F // tf),
        in_specs=[pl.BlockSpec((tm, D), lambda i, f: (i, 0)),
                  pl.BlockSpec((D, tf), lambda i, f: (0, f)),
                  pl.BlockSpec((tf, D), lambda i, f: (f, 0)),
                  pl.BlockSpec((1, D), lambda i, f: (0, 0)),
                  pl.BlockSpec((1, D), lambda i, f: (0, 0))],
        out_specs=pl.BlockSpec((tm, D), lambda i, f: (i, 0)),
        out_shape=jax.ShapeDtypeStruct((BT, D), F32),
        scratch_shapes=[pltpu.VMEM((tm, D), F32)],
        compiler_params=_params(("parallel", "arbitrary")),
        name="mlp_ln",
    )(x2, wu, wd, g, b)


def _tiles(T):
    return dict(
        tm_proj=min(512, T), tm_pool=min(512, T), tm_out=min(512, T), tm_mlp=min(1024, T), tf_mlp=1024,
        a_tq=min(512, T), a_tk=min(512, T), c_tq=min(256, T), c_tk=min(512, T),
    )


def kernel(x, w_in, lam_q1, lam_k1, lam_q2, lam_k2, subln_g, w_pool, pool_scale, cmp_pe_k, cmp_pe_v,
           w_cmp_k, w_cmp_v, w_out, ln1_g, ln1_b, w_up, w_down, ln2_g, ln2_b):
    B, T, D = x.shape
    depth = w_in.shape[0]
    alpha = (2 * depth) ** 0.25
    tl = _tiles(T)
    tabs = _rope_tables(T)
    ovT = _overlap_T(T)
    x3 = x
    for l in range(depth):
        lam_init = 0.8 - 0.6 * math.exp(-0.3 * l)
        w_tok, w_feat = _prep_w_in(w_in[l])
        (ka, u, kcm, vcm, ksl, kwi, qaT, vaT, qcT, vslT, vwiT, gtT) = _inproj(x3, w_tok, w_feat, tabs, tl["tm_proj"])

        oaT = _diff_attn(qaT, ka, vaT, lam_q1[l][None, :], lam_k1[l][None, :], lam_q2[l][None, :],
                         lam_k2[l][None, :], subln_g[l][:, None], lam_init, tl["a_tq"], tl["a_tk"])

        w_bd = jax.scipy.linalg.block_diag(*[w_pool[l, g] for g in range(len(POOL_WINDOWS))]).astype(BF16)
        ob = _pool(u, w_bd, pool_scale[l][None, :], tl["tm_pool"])

        pk2, pv2, wk2, wv2 = _prep_cmp(cmp_pe_k[l], cmp_pe_v[l], w_cmp_k[l], w_cmp_v[l])
        n16 = T // CMP_STRIDE
        kc2, vc2 = _compress(kcm.reshape(B, n16, 16 * LANES), vcm.reshape(B, n16, 16 * LANES),
                             pk2, pv2, wk2, wv2)
        ocT = _nsa(qcT, gtT, kc2, vc2.transpose(0, 2, 1), ovT, ksl, vslT, kwi, vwiT, tl["c_tq"], tl["c_tk"])

        wo = w_out[l].astype(BF16)
        x3 = _outproj(x3, oaT, ob, ocT, wo[:A_WIDTH], wo[A_WIDTH:A_WIDTH + POOL_WIDTH], wo[A_WIDTH + POOL_WIDTH:],
                      ln1_g[l][None, :], ln1_b[l][None, :], alpha, tl["tm_out"])
        x3 = _mlp(x3.reshape(B * T, D), w_up[l].astype(BF16), w_down[l].astype(BF16), ln2_g[l][None, :],
                  ln2_b[l][None, :], alpha, tl["tm_mlp"], tl["tf_mlp"]).reshape(B, T, D)
    return x3
```

```python
import functools
import math

import jax
import jax.numpy as jnp
import numpy as np
from jax import lax
from jax.experimental import pallas as pl
from jax.experimental.pallas import tpu as pltpu

A_HEADS = 6
A_QK_DIM = 32
A_V_DIM = 64
A_WIDTH = A_HEADS * A_V_DIM
POOL_WINDOWS = (2, 4, 8, 16)
POOL_GROUP_DIM = 64
POOL_WIDTH = 256
C_HEADS = 6
C_KV_GROUPS = 2
C_HPG = 3
C_HEAD_DIM = 64
C_WIDTH = C_HEADS * C_HEAD_DIM
CMP_BLOCK = 32
CMP_STRIDE = 16
SEL_BLOCK = 64
N_SEL = 16
WINDOW = 512
N_BRANCH = 3
FORCED_SCORE = 1.0e4
N_FORCED = 3
ROPE_THETA = 10000.0
LN_EPS = 1e-5
RMS_EPS = 1e-6

LANES = 128
LOG2E = 1.4426950408889634
NEG_BIG = -1e30
SWEEP_UNROLL = 4
TAKEN = -3.0e38
MASKED = -3e38
VMEM_LIMIT = 56 * 1024 * 1024

BF16 = jnp.bfloat16
F32 = jnp.float32


def _dot(a, b):
    return jnp.dot(a, b, preferred_element_type=F32)


def _params(sem, vmem=VMEM_LIMIT):
    return pltpu.CompilerParams(dimension_semantics=sem, vmem_limit_bytes=vmem)


def _perm_diff():
    idx = np.zeros(A_HEADS * 2 * A_QK_DIM, np.int32)
    for p in range(A_HEADS // 2):
        for l in range(LANES):
            half, grp, j = l // 64, (l % 64) // 16, l % 16
            hl, m = grp // 2, grp % 2
            idx[p * LANES + l] = (2 * p + hl) * 64 + m * 32 + half * 16 + j
    return idx


def _perm_cq():
    idx = np.zeros(C_WIDTH, np.int32)
    for hh in range(C_HPG):
        for l in range(LANES):
            half, g, j = l // 64, (l % 64) // 32, l % 32
            idx[hh * LANES + l] = (g * C_HPG + hh) * 64 + half * 32 + j
    return idx


def _perm_ck():
    idx = np.zeros(C_KV_GROUPS * C_HEAD_DIM, np.int32)
    for l in range(LANES):
        half, g, j = l // 64, (l % 64) // 32, l % 32
        idx[l] = g * 64 + half * 32 + j
    return idx


_IN_SPLITS = (384, 384, 384, 256, 384, 128, 128, 128, 128, 128, 128, 18)
_OFFS = np.concatenate([[0], np.cumsum(_IN_SPLITS)]).astype(np.int64)
A_QSCALE = A_QK_DIM ** -0.5 * LOG2E
C_QSCALE = C_HEAD_DIM ** -0.5 * LOG2E
GATE_ROWS = 16
N_TOK = 384 + 256 + 4 * LANES
N_FEAT = 3 * 384 + 2 * LANES + C_KV_GROUPS * GATE_ROWS


def _prep_w_in(w):
    o = _OFFS
    pd, pq, pk = _perm_diff(), _perm_cq(), _perm_ck()
    seg = lambda i: w[:, o[i]:o[i + 1]]
    w_tok = jnp.concatenate([seg(1)[:, pd], seg(3), seg(5)[:, pk], seg(6), seg(7)[:, pk], seg(9)[:, pk]], axis=1)
    per = C_HPG * N_BRANCH
    gates = jnp.pad(seg(11).reshape(-1, C_KV_GROUPS, per), ((0, 0), (0, 0), (0, GATE_ROWS - per)))
    w_feat = jnp.concatenate([seg(0)[:, pd], seg(2), seg(4)[:, pq], seg(8), seg(10),
                              gates.reshape(-1, C_KV_GROUPS * GATE_ROWS)], axis=1)
    return w_tok.astype(BF16), w_feat.T.astype(BF16)


def _rope_tables(T):
    def tab(dim):
        half = dim // 2
        inv = 1.0 / (ROPE_THETA ** (jnp.arange(0, dim, 2, dtype=F32) / dim))
        ang = jnp.arange(T, dtype=F32)[:, None] * inv[None, :]
        lane = np.arange(LANES)
        j = lane % half
        sign = np.where(lane < 64, -1.0, 1.0).astype(np.float32)
        cos, sin = jnp.cos(ang), jnp.sin(ang)
        return cos[:, j], sin[:, j] * sign[None, :], cos[:, j[:64]].T, sin[:, j[:64]].T
    return tab(A_QK_DIM) + tab(C_HEAD_DIM)


def _inproj_kernel(x_ref, wt_ref, wf_ref, ca_ref, sa_ref, caT_ref, saT_ref, cc_ref, sc_ref, ccT_ref, scT_ref,
                   ka_ref, u_ref, kcm_ref, vcm_ref, ksl_ref, kwi_ref,
                   qa_ref, va_ref, qc_ref, vsl_ref, vwi_ref, gt_ref):
    xb = x_ref[...].astype(BF16)
    tok = _dot(xb, wt_ref[...])
    feat = lax.dot_general(wf_ref[...], xb, (((1,), (1,)), ((), ())), preferred_element_type=F32)

    def rope_tok(p, cos, sin):
        return p * cos + pltpu.roll(p, 64, axis=1) * sin

    def rope_feat(p, cos, sin):
        a, b = p[:64, :], p[64:, :]
        return jnp.concatenate([a * cos - b * sin, b * cos + a * sin], axis=0)

    ca, sa, cc, sc = ca_ref[...], sa_ref[...], cc_ref[...], sc_ref[...]
    caT, saT, ccT, scT = caT_ref[...], saT_ref[...], ccT_ref[...], scT_ref[...]
    for t in range(3):
        r = slice(t * LANES, (t + 1) * LANES)
        ka_ref[:, r] = rope_tok(tok[:, r], ca, sa).astype(BF16)
        qa_ref[r, :] = (rope_feat(feat[r, :], caT, saT) * A_QSCALE).astype(BF16)
        rq = slice(768 + t * LANES, 768 + (t + 1) * LANES)
        qc_ref[r, :] = (rope_feat(feat[rq, :], ccT, scT) * C_QSCALE).astype(BF16)
    u_ref[...] = tok[:, 384:640]
    kcm_ref[...] = rope_tok(tok[:, 640:768], cc, sc).astype(BF16)
    vcm_ref[...] = tok[:, 768:896].astype(BF16)
    ksl_ref[...] = rope_tok(tok[:, 896:1024], cc, sc).astype(BF16)
    kwi_ref[...] = rope_tok(tok[:, 1024:1152], cc, sc).astype(BF16)
    va_ref[...] = feat[384:768, :].astype(BF16)
    vsl_ref[...] = feat[1152:1280, :].astype(BF16)
    vwi_ref[...] = feat[1280:1408, :].astype(BF16)
    gt_ref[...] = jax.nn.sigmoid(feat[1408:, :])


def _inproj(x3, w_tok, w_feat, tabs, tm):
    B, T, D = x3.shape
    tok = lambda n: pl.BlockSpec((None, tm, n), lambda b, i: (b, i, 0))
    feat = lambda n: pl.BlockSpec((None, n, tm), lambda b, i: (b, 0, i))
    full = lambda a: pl.BlockSpec(a.shape, lambda b, i: (0, 0))
    tab_tok = pl.BlockSpec((tm, LANES), lambda b, i: (i, 0))
    tab_feat = pl.BlockSpec((64, tm), lambda b, i: (0, i))
    tok_outs = [(384, BF16), (256, F32)] + [(LANES, BF16)] * 4
    feat_outs = [(384, BF16)] * 3 + [(LANES, BF16)] * 2 + [(C_KV_GROUPS * GATE_ROWS, F32)]
    return pl.pallas_call(
        _inproj_kernel,
        grid=(B, T // tm),
        in_specs=[tok(D), full(w_tok), full(w_feat)] + [tab_tok, tab_tok, tab_feat, tab_feat] * 2,
        out_specs=[tok(n) for n, _ in tok_outs] + [feat(n) for n, _ in feat_outs],
        out_shape=([jax.ShapeDtypeStruct((B, T, n), dt) for n, dt in tok_outs]
                   + [jax.ShapeDtypeStruct((B, n, T), dt) for n, dt in feat_outs]),
        compiler_params=_params(("parallel", "parallel")),
        name="inproj_rope",
    )(x3, w_tok, w_feat, *tabs)


def _sweep_scratch(tk, nq, dv):
    return [pltpu.VMEM((tk, nq), F32), pltpu.VMEM((tk, nq), F32),
            pltpu.VMEM((1, nq), F32), pltpu.VMEM((1, nq), F32),
            pltpu.VMEM((1, nq), F32), pltpu.VMEM((1, nq), F32), pltpu.VMEM((dv, nq), F32)]


def _causal_sweep(n_full, score, qpos, value, scratch, first_scores=None):
    s_refs, cm_refs = scratch[0:2], scratch[2:4]
    m_ref, l_ref, acc_ref = scratch[4:7]

    tk, nq = s_refs[0].shape

    def scores(j, slot, fn=score):
        s = fn(j)
        s_refs[slot][...] = s
        cm_refs[slot][...] = jnp.max(s, axis=0, keepdims=True)

    def softmax(slot, v, causal_tile=None):
        s = s_refs[slot][...]
        if causal_tile is None:
            cm = cm_refs[slot][...]
        else:
            kpos = causal_tile * tk + lax.broadcasted_iota(jnp.int32, (tk, nq), 0)
            s = jnp.where(kpos <= qpos, s, MASKED)
            cm = jnp.max(s, axis=0, keepdims=True)
        m = m_ref[...]
        m_new = jnp.maximum(m, cm)
        alpha = jnp.exp2(m - m_new)
        p = jnp.exp2(s - m_new)
        l_ref[...] = l_ref[...] * alpha + jnp.sum(p, axis=0, keepdims=True)
        m_ref[...] = m_new
        acc_ref[...] = acc_ref[...] * alpha + _dot(v, p.astype(BF16))

    m_ref[...] = jnp.full(m_ref.shape, NEG_BIG, F32)
    l_ref[...] = jnp.zeros(l_ref.shape, F32)
    acc_ref[...] = jnp.zeros(acc_ref.shape, F32)
    scores(0, 0, score if first_scores is None else lambda j: first_scores())

    def group(base, tiles):
        for u in range(tiles):
            scores(base + u + 1, (u + 1) % 2)
            softmax(u % 2, value(base + u))

    def quad(i, c):
        group(SWEEP_UNROLL * i, SWEEP_UNROLL)
        return c

    def pair(i, c):
        group(done + 2 * i, 2)
        return c

    lax.fori_loop(0, n_full // SWEEP_UNROLL, quad, 0)
    done = (n_full // SWEEP_UNROLL) * SWEEP_UNROLL
    lax.fori_loop(0, (n_full - done) // 2, pair, 0)

    @pl.when(n_full % 2 == 0)
    def _():
        softmax(0, value(n_full), causal_tile=n_full)

    @pl.when(n_full % 2 == 1)
    def _():
        scores(n_full, 1)
        softmax(0, value(n_full - 1))
        softmax(1, value(n_full), causal_tile=n_full)

    return acc_ref, l_ref


def _diff_attn_kernel(lq1_ref, lk1_ref, lq2_ref, lk2_ref, g_ref, q_ref, k_ref, v_ref, o_ref,
                      *scratch, tq, tk, lam_init):
    h = pl.program_id(1)
    qi = pl.program_id(2)
    hl = h % 2
    q = q_ref[...]
    grp = (lax.broadcasted_iota(jnp.int32, q.shape, 0) % 64) // 16
    zero = jnp.zeros_like(q)
    q12 = jnp.concatenate([jnp.where(grp == 2 * hl, q, zero), jnp.where(grp == 2 * hl + 1, q, zero)], axis=1)
    nq = 2 * tq
    qpos = qi * tq + lax.broadcasted_iota(jnp.int32, (1, nq), 1) % tq

    def score(j):
        return _dot(k_ref[pl.ds(pl.multiple_of(j * tk, tk), tk), :], q12)

    def value(j):
        return v_ref[:, pl.ds(pl.multiple_of(j * tk, tk), tk)]

    acc_ref, l_ref = _causal_sweep(qi, score, qpos, value, scratch)
    on = acc_ref[...] / l_ref[...]

    dotp = lambda a, b: jnp.sum(a[...] * b[...], axis=1, keepdims=True)
    lam = jnp.exp(dotp(lq1_ref, lk1_ref)) - jnp.exp(dotp(lq2_ref, lk2_ref)) + lam_init
    o = on[:, :tq] - lam * on[:, tq:]
    ms = jnp.mean(o * o, axis=0, keepdims=True)
    o = o * lax.rsqrt(ms + RMS_EPS) * g_ref[...] * (1.0 - lam_init)
    o_ref[...] = o.astype(o_ref.dtype)


def _diff_attn(qaT, ka, vaT, lq1, lk1, lq2, lk2, g_col, lam_init, tq, tk):
    B, _, T = qaT.shape
    assert tq == tk, "the causal sweep takes query block i to end at key tile i"
    nq = 2 * tq
    vec = pl.BlockSpec((1, A_QK_DIM), lambda b, h, i: (0, 0))
    return pl.pallas_call(
        functools.partial(_diff_attn_kernel, tq=tq, tk=tk, lam_init=lam_init),
        grid=(B, A_HEADS, T // tq),
        in_specs=[vec, vec, vec, vec,
                  pl.BlockSpec((A_V_DIM, 1), lambda b, h, i: (0, 0)),
                  pl.BlockSpec((None, LANES, tq), lambda b, h, i: (b, h // 2, i)),
                  pl.BlockSpec((None, T, LANES), lambda b, h, i: (b, 0, h // 2)),
                  pl.BlockSpec((None, A_V_DIM, T), lambda b, h, i: (b, h, 0))],
        out_specs=pl.BlockSpec((None, A_V_DIM, tq), lambda b, h, i: (b, h, i)),
        out_shape=jax.ShapeDtypeStruct((B, A_WIDTH, T), BF16),
        scratch_shapes=_sweep_scratch(tk, nq, A_V_DIM),
        compiler_params=_params(("parallel", "parallel", "arbitrary")),
        name="diff_attn",
    )(lq1, lk1, lq2, lk2, g_col, qaT, ka, vaT)


POOL_HALO = 16


def _pool_kernel(prev_ref, cur_ref, w_ref, scale_ref, o_ref, *, tm):
    i = pl.program_id(1)
    cur = cur_ref[...]
    halo = prev_ref[tm - POOL_HALO:, :]
    halo = jnp.where(i == 0, jnp.zeros_like(halo), halo)
    arr = jnp.concatenate([halo, cur], axis=0)
    s2 = arr + pltpu.roll(arr, 1, axis=0)
    s4 = s2 + pltpu.roll(s2, 2, axis=0)
    s8 = s4 + pltpu.roll(s4, 4, axis=0)
    s16 = s8 + pltpu.roll(s8, 8, axis=0)
    sums = [s[POOL_HALO:, :] for s in (s2, s4, s8, s16)]
    lane = lax.broadcasted_iota(jnp.int32, (tm, POOL_WIDTH), 1)
    tpos = i * tm + lax.broadcasted_iota(jnp.int32, (tm, POOL_WIDTH), 0)
    grp = lane // POOL_GROUP_DIM
    win = jnp.where(grp == 0, sums[0], jnp.where(grp == 1, sums[1], jnp.where(grp == 2, sums[2], sums[3])))
    width = jnp.where(grp == 0, 2, jnp.where(grp == 1, 4, jnp.where(grp == 2, 8, 16)))
    cnt = jnp.minimum(tpos + 1, width).astype(F32)
    y = win / cnt - cur
    z = _dot(y.astype(BF16), w_ref[...])
    o_ref[...] = (z * scale_ref[...]).astype(o_ref.dtype)


def _pool(u3, w_bd, scale, tm):
    B, T, W = u3.shape
    return pl.pallas_call(
        functools.partial(_pool_kernel, tm=tm),
        grid=(B, T // tm),
        in_specs=[pl.BlockSpec((None, tm, W), lambda b, i: (b, jnp.maximum(i - 1, 0), 0)),
                  pl.BlockSpec((None, tm, W), lambda b, i: (b, i, 0)),
                  pl.BlockSpec((W, W), lambda b, i: (0, 0)),
                  pl.BlockSpec((1, W), lambda b, i: (0, 0))],
        out_specs=pl.BlockSpec((None, tm, W), lambda b, i: (b, i, 0)),
        out_shape=jax.ShapeDtypeStruct((B, T, W), BF16),
        compiler_params=_params(("parallel", "arbitrary")),
        name="pool_mixer",
    )(u3, u3, w_bd, scale)


def _cmp_kernel(k_ref, v_ref, pk_ref, pv_ref, wk_ref, wv_ref, kc_ref, vc_ref):
    n = k_ref.shape[0]
    rowid = lax.broadcasted_iota(jnp.int32, (n, LANES), 0)

    def compress(x_ref, pe_ref, w_ref):
        x = x_ref[...].astype(F32)
        lo = _dot((x + pe_ref[0:1, :]).astype(BF16), w_ref[0])
        hi = _dot((x + pe_ref[1:2, :]).astype(BF16), w_ref[1])
        out = lo + pltpu.roll(hi, n - 1, axis=0)
        return jnp.where(rowid < n - 1, out, 0.0)

    kc_ref[...] = compress(k_ref, pk_ref, wk_ref).astype(kc_ref.dtype)
    vc_ref[...] = compress(v_ref, pv_ref, wv_ref).astype(vc_ref.dtype)


def _compress(kr, vr, pe_k2, pe_v2, wk2, wv2):
    B, n, W = kr.shape
    blk = pl.BlockSpec((None, n, W), lambda b: (b, 0, 0))
    pe = pl.BlockSpec((2, W), lambda b: (0, 0))
    wsp = pl.BlockSpec((2, W, LANES), lambda b: (0, 0, 0))
    osp = pl.BlockSpec((None, n, LANES), lambda b: (b, 0, 0))
    return pl.pallas_call(
        _cmp_kernel,
        grid=(B,),
        in_specs=[blk, blk, pe, pe, wsp, wsp],
        out_specs=[osp, osp],
        out_shape=[jax.ShapeDtypeStruct((B, n, LANES), BF16)] * 2,
        compiler_params=_params(("parallel",)),
        name="nsa_compress",
    )(kr, vr, pe_k2, pe_v2, wk2, wv2)


def _prep_cmp(pe_k, pe_v, w_ck, w_cv):
    pk = _perm_ck()
    lane = np.arange(LANES)
    g_of_k = (lane % 64) // 32
    d_of_k = pk % 64
    g_of_v = lane // 64
    d_of_v = lane % 64

    def build(w, pe, g_of, d_of):
        wl = w[:, d_of, :]
        onehot = jnp.asarray((g_of[:, None] == np.arange(C_KV_GROUPS)[None, :]).astype(np.float32))
        w2 = wl[:, :, None, :] * onehot[None, :, :, None]
        w2 = w2.reshape(2, 16 * LANES, C_KV_GROUPS * C_HEAD_DIM).astype(BF16)
        pe2 = pe[:, d_of].reshape(2, 16 * LANES)
        return w2, pe2

    wk2, pk2 = build(w_ck, pe_k, g_of_k, d_of_k)
    wv2, pv2 = build(w_cv, pe_v, g_of_v, d_of_v)
    return pk2, pv2, wk2[:, :, pk], wv2


def _nsa_kernel(q_ref, gt_ref, kc_ref, vc_ref, ov_ref, wb_ref, ks_ref, vs_ref, kw_ref, vw_ref, o_ref, sel_ref,
                *scratch, tq, tk, T):
    g = pl.program_id(1)
    qi = pl.program_id(2)
    q0 = qi * tq
    n_cmp = kc_ref.shape[0]
    n_blk = T // SEL_BLOCK
    d = C_HEAD_DIM
    tpos = q0 + lax.broadcasted_iota(jnp.int32, (1, tq), 1)

    nq = C_HPG * tq
    tpos3 = q0 + lax.broadcasted_iota(jnp.int32, (1, nq), 1) % tq
    rowg = (lax.broadcasted_iota(jnp.int32, (LANES, tq), 0) % 64) // 32
    qs = []
    for hh in range(C_HPG):
        qt = q_ref[hh * LANES:(hh + 1) * LANES, :]
        qs.append(jnp.where(rowg == g, qt, jnp.zeros_like(qt)))
    q3 = jnp.concatenate(qs, axis=1)

    scratch[0][...] = _dot(ks_ref[0:tk, :], q3)

    band = WINDOW + tq
    w0 = pl.multiple_of(jnp.maximum(q0 - WINDOW, 0), LANES)
    s = _dot(kw_ref[pl.ds(w0, band), :], q3) + jnp.concatenate([wb_ref[...]] * C_HPG, axis=1)
    e = jnp.exp2(s - jnp.max(s, axis=0, keepdims=True))
    o_win = _dot(vw_ref[:, pl.ds(w0, band)], e.astype(BF16)) / jnp.sum(e, axis=0, keepdims=True)

    cend = lax.broadcasted_iota(jnp.int32, (n_cmp, nq), 0) * CMP_STRIDE + (CMP_BLOCK - 1)
    s = jnp.where(cend <= tpos3, _dot(kc_ref[...], q3), MASKED)
    m = jnp.maximum(jnp.max(s, axis=0, keepdims=True), NEG_BIG)
    e = jnp.exp2(s - m)
    l = jnp.sum(e, axis=0, keepdims=True)
    p = e / jnp.where(l > 0.0, l, 1.0)
    o_cmp = _dot(vc_ref[...], p.astype(BF16))
    psum = p[:, :tq] + p[:, tq:2 * tq] + p[:, 2 * tq:]

    p_hi = psum.astype(BF16)
    p_lo = (psum - p_hi.astype(F32)).astype(BF16)
    imp = _dot(ov_ref[...], p_hi) + _dot(ov_ref[...], p_lo)
    jidx = lax.broadcasted_iota(jnp.int32, (n_blk, tq), 0).astype(F32)
    cur = (tpos // SEL_BLOCK).astype(F32)
    forced = (jidx == 0.0) | (jidx == cur) | (jidx == cur - 1.0)
    imp = jnp.where(jidx > cur, -1.0, imp)
    imp = jnp.where(forced, TAKEN, imp)
    rounds = min(N_SEL, n_blk) - N_FORCED

    def store_selection(taken_imp):
        bias = jnp.where(taken_imp == TAKEN, 0.0, MASKED)
        sel_ref[...] = jnp.concatenate([bias] * C_HPG, axis=1)

    quick = imp
    for _ in range(rounds):
        quick = jnp.where(quick == jnp.max(quick, axis=0, keepdims=True), TAKEN, quick)
    store_selection(quick)
    free = (jidx >= 1.0) & (jidx <= cur - 2.0)
    n_taken = jnp.sum(jnp.where(free & (quick == TAKEN), 1.0, 0.0), axis=0, keepdims=True)
    n_expected = jnp.minimum(jnp.maximum(cur - 2.0, 0.0), float(rounds))
    tie_in_top = jnp.max(jnp.abs(n_taken - n_expected)) > 0.0

    @pl.when(tie_in_top)
    def _():
        exact = imp
        for _ in range(rounds):
            mx = jnp.max(exact, axis=0, keepdims=True)
            first = jnp.min(jnp.where(exact == mx, jidx, float(n_blk)), axis=0, keepdims=True)
            exact = jnp.where(jidx == first, TAKEN, exact)
        store_selection(exact)

    bpt = tk // SEL_BLOCK

    def block_bias(j):
        rows = sel_ref[pl.ds(pl.multiple_of(j * bpt, bpt), bpt), :]
        return jnp.concatenate([jnp.broadcast_to(rows[i:i + 1, :], (SEL_BLOCK, nq)) for i in range(bpt)], axis=0)

    def score(j):
        return _dot(ks_ref[pl.ds(pl.multiple_of(j * tk, tk), tk), :], q3) + block_bias(j)

    def first_scores():
        return scratch[0][...] + block_bias(0)

    def value(j):
        return vs_ref[:, pl.ds(pl.multiple_of(j * tk, tk), tk)]

    acc_ref, l_ref = _causal_sweep(q0 // tk, score, tpos3, value, scratch, first_scores)
    o_slc = acc_ref[...] / l_ref[...]

    gts = gt_ref[...]
    for hh in range(C_HPG):
        r = hh * N_BRANCH
        c = slice(hh * tq, (hh + 1) * tq)
        out = (o_cmp[:, c] * gts[r:r + 1, :] + o_slc[:, c] * gts[r + 1:r + 2, :]
               + o_win[:, c] * gts[r + 2:r + 3, :])
        o_ref[hh * d:(hh + 1) * d, :] = out.astype(o_ref.dtype)


def _window_bias(tq):
    band, n_edge = WINDOW + tq, WINDOW // tq
    q0 = np.arange(n_edge + 1)[:, None, None] * tq
    wpos = np.maximum(q0 - WINDOW, 0) + np.arange(band)[None, :, None]
    tpos = q0 + np.arange(tq)[None, None, :]
    ok = (wpos <= tpos) & (wpos > tpos - WINDOW)
    return jnp.asarray(np.where(ok, 0.0, MASKED).astype(np.float32))


def _nsa(qcT, gtT, kc2, vcT, ovT, ksl, vslT, kwi, vwiT, tq, tk):
    B, _, T = qcT.shape
    n_cmp = kc2.shape[1]
    n_blk = T // SEL_BLOCK
    nq = C_HPG * tq
    assert tk % tq == 0, "the causal key tile must contain the whole query block"
    assert WINDOW % tq == 0 and T >= WINDOW + tq
    wb = _window_bias(tq)
    n_edge = WINDOW // tq
    kv_tok = pl.BlockSpec((None, T, LANES), lambda b, g, i: (b, 0, 0))
    kv_feat = pl.BlockSpec((None, C_HEAD_DIM, T), lambda b, g, i: (b, g, 0))
    return pl.pallas_call(
        functools.partial(_nsa_kernel, tq=tq, tk=tk, T=T),
        grid=(B, C_KV_GROUPS, T // tq),
        in_specs=[pl.BlockSpec((None, C_WIDTH, tq), lambda b, g, i: (b, 0, i)),
                  pl.BlockSpec((None, GATE_ROWS, tq), lambda b, g, i: (b, g, i)),
                  pl.BlockSpec((None, n_cmp, LANES), lambda b, g, i: (b, 0, 0)),
                  pl.BlockSpec((None, C_HEAD_DIM, n_cmp), lambda b, g, i: (b, g, 0)),
                  pl.BlockSpec((n_blk, n_cmp), lambda b, g, i: (0, 0)),
                  pl.BlockSpec((None, WINDOW + tq, tq), lambda b, g, i: (jnp.minimum(i, n_edge), 0, 0)),
                  kv_tok, kv_feat, kv_tok, kv_feat],
        out_specs=pl.BlockSpec((None, C_HPG * C_HEAD_DIM, tq), lambda b, g, i: (b, g, i)),
        out_shape=jax.ShapeDtypeStruct((B, C_WIDTH, T), BF16),
        scratch_shapes=[pltpu.VMEM((n_blk, nq), F32)] + _sweep_scratch(tk, nq, C_HEAD_DIM),
        compiler_params=_params(("parallel", "parallel", "arbitrary")),
        name="nsa_attn",
    )(qcT, gtT, kc2, vcT, ovT, wb, ksl, vslT, kwi, vwiT)


def _overlap_T(T):
    n_cmp = T // CMP_STRIDE
    n_blk = T // SEL_BLOCK
    cs = np.arange(n_cmp)[None, :] * CMP_STRIDE
    bs = np.arange(n_blk)[:, None] * SEL_BLOCK
    ov = (cs < bs + SEL_BLOCK) & (cs + CMP_BLOCK > bs) & (np.arange(n_cmp)[None, :] < n_cmp - 1)
    return jnp.asarray(ov.astype(np.float32), dtype=BF16)


def _layer_norm(y, g, b):
    mu = jnp.mean(y, axis=-1, keepdims=True)
    yc = y - mu
    var = jnp.mean(yc * yc, axis=-1, keepdims=True)
    return yc * lax.rsqrt(var + LN_EPS) * g + b


def _outproj_kernel(x_ref, oa_ref, ob_ref, oc_ref, wa_ref, wb_ref, wc_ref, g_ref, b_ref, o_ref, *, alpha):
    tn = lambda a, w: lax.dot_general(a, w, (((0,), (0,)), ((), ())), preferred_element_type=F32)
    mix = tn(oa_ref[...], wa_ref[...]) + _dot(ob_ref[...], wb_ref[...]) + tn(oc_ref[...], wc_ref[...])
    o_ref[...] = _layer_norm(alpha * x_ref[...] + mix, g_ref[...], b_ref[...])


def _outproj(x3, oaT, ob, ocT, wa, wb, wc, g, b, alpha, tm):
    B, T, D = x3.shape
    tok = lambda n: pl.BlockSpec((None, tm, n), lambda bb, i: (bb, i, 0))
    feat = lambda n: pl.BlockSpec((None, n, tm), lambda bb, i: (bb, 0, i))
    full = lambda a: pl.BlockSpec(a.shape, lambda bb, i: (0, 0))
    return pl.pallas_call(
        functools.partial(_outproj_kernel, alpha=alpha),
        grid=(B, T // tm),
        in_specs=[tok(D), feat(oaT.shape[1]), tok(ob.shape[2]), feat(ocT.shape[1]),
                  full(wa), full(wb), full(wc), full(g), full(b)],
        out_specs=tok(D),
        out_shape=jax.ShapeDtypeStruct((B, T, D), F32),
        compiler_params=_params(("parallel", "parallel")),
        name="outproj_ln",
    )(x3, oaT, ob, ocT, wa, wb, wc, g, b)


def _mlp_kernel(x_ref, wu_ref, wd_ref, g_ref, b_ref, o_ref, acc_ref, *, alpha):
    f = pl.program_id(1)

    @pl.when(f == 0)
    def _():
        acc_ref[...] = jnp.zeros_like(acc_ref)

    hid = _dot(x_ref[...].astype(BF16), wu_ref[...])
    hid = jnp.square(jnp.maximum(hid, 0.0))
    acc_ref[...] += _dot(hid.astype(BF16), wd_ref[...])

    @pl.when(f == pl.num_programs(1) - 1)
    def _():
        o_ref[...] = _layer_norm(alpha * x_ref[...] + acc_ref[...], g_ref[...], b_ref[...])


def _mlp(x2, wu, wd, g, b, alpha, tm, tf):
    BT, D = x2.shape
    F = wu.shape[1]
    return pl.pallas_call(
        functools.partial(_mlp_kernel, alpha=alpha),
        grid=(BT // tm, F // tf),
        in_specs=[pl.BlockSpec((tm, D), lambda i, f: (i, 0)),
                  pl.BlockSpec((D, tf), lambda i, f: (0, f)),
                  pl.BlockSpec((tf, D), lambda i, f: (f, 0)),
                  pl.BlockSpec((1, D), lambda i, f: (0, 0)),
                  pl.BlockSpec((1, D), lambda i, f: (0, 0))],
        out_specs=pl.BlockSpec((tm, D), lambda i, f: (i, 0)),
        out_shape=jax.ShapeDtypeStruct((BT, D), F32),
        scratch_shapes=[pltpu.VMEM((tm, D), F32)],
        compiler_params=_params(("parallel", "arbitrary")),
        name="mlp_ln",
    )(x2, wu, wd, g, b)


def _tiles(T):
    return dict(
        tm_proj=min(512, T), tm_pool=min(512, T), tm_out=min(512, T), tm_mlp=min(1024, T), tf_mlp=1024,
        a_tq=min(512, T), a_tk=min(512, T), c_tq=min(256, T), c_tk=min(512, T),
    )


def kernel(x, w_in, lam_q1, lam_k1, lam_q2, lam_k2, subln_g, w_pool, pool_scale, cmp_pe_k, cmp_pe_v,
           w_cmp_k, w_cmp_v, w_out, ln1_g, ln1_b, w_up, w_down, ln2_g, ln2_b):
    B, T, D = x.shape
    depth = w_in.shape[0]
    alpha = (2 * depth) ** 0.25
    tl = _tiles(T)
    tabs = _rope_tables(T)
    ovT = _overlap_T(T)
    x3 = x
    for l in range(depth):
        lam_init = 0.8 - 0.6 * math.exp(-0.3 * l)
        w_tok, w_feat = _prep_w_in(w_in[l])
        (ka, u, kcm, vcm, ksl, kwi, qaT, vaT, qcT, vslT, vwiT, gtT) = _inproj(x3, w_tok, w_feat, tabs, tl["tm_proj"])

        oaT = _diff_attn(qaT, ka, vaT, lam_q1[l][None, :], lam_k1[l][None, :], lam_q2[l][None, :],
                         lam_k2[l][None, :], subln_g[l][:, None], lam_init, tl["a_tq"], tl["a_tk"])

        w_bd = jax.scipy.linalg.block_diag(*[w_pool[l, g] for g in range(len(POOL_WINDOWS))]).astype(BF16)
        ob = _pool(u, w_bd, pool_scale[l][None, :], tl["tm_pool"])

        pk2, pv2, wk2, wv2 = _prep_cmp(cmp_pe_k[l], cmp_pe_v[l], w_cmp_k[l], w_cmp_v[l])
        n16 = T // CMP_STRIDE
        kc2, vc2 = _compress(kcm.reshape(B, n16, 16 * LANES), vcm.reshape(B, n16, 16 * LANES),
                             pk2, pv2, wk2, wv2)
        ocT = _nsa(qcT, gtT, kc2, vc2.transpose(0, 2, 1), ovT, ksl, vslT, kwi, vwiT, tl["c_tq"], tl["c_tk"])

        wo = w_out[l].astype(BF16)
        x3 = _outproj(x3, oaT, ob, ocT, wo[:A_WIDTH], wo[A_WIDTH:A_WIDTH + POOL_WIDTH], wo[A_WIDTH + POOL_WIDTH:],
                      ln1_g[l][None, :], ln1_b[l][None, :], alpha, tl["tm_out"])
        x3 = _mlp(x3.reshape(B * T, D), w_up[l].astype(BF16), w_down[l].astype(BF16), ln2_g[l][None, :],
                  ln2_b[l][None, :], alpha, tl["tm_mlp"], tl["tf_mlp"]).reshape(B, T, D)
    return x3
```

```python
import functools
import math

import jax
import jax.numpy as jnp
import numpy as np
from jax import lax
from jax.experimental import pallas as pl
from jax.experimental.pallas import tpu as pltpu

A_HEADS = 6
A_QK_DIM = 32
A_V_DIM = 64
A_WIDTH = A_HEADS * A_V_DIM
POOL_WINDOWS = (2, 4, 8, 16)
POOL_GROUP_DIM = 64
POOL_WIDTH = 256
C_HEADS = 6
C_KV_GROUPS = 2
C_HPG = 3
C_HEAD_DIM = 64
C_WIDTH = C_HEADS * C_HEAD_DIM
CMP_BLOCK = 32
CMP_STRIDE = 16
SEL_BLOCK = 64
N_SEL = 16
WINDOW = 512
N_BRANCH = 3
FORCED_SCORE = 1.0e4
N_FORCED = 3
ROPE_THETA = 10000.0
LN_EPS = 1e-5
RMS_EPS = 1e-6

LANES = 128
LOG2E = 1.4426950408889634
NEG_BIG = -1e30
SWEEP_UNROLL = 4
TAKEN = -3.0e38
MASKED = -3e38
VMEM_LIMIT = 56 * 1024 * 1024

BF16 = jnp.bfloat16
F32 = jnp.float32


def _dot(a, b):
    return jnp.dot(a, b, preferred_element_type=F32)


def _params(sem, vmem=VMEM_LIMIT):
    return pltpu.CompilerParams(dimension_semantics=sem, vmem_limit_bytes=vmem)


def _perm_diff():
    idx = np.zeros(A_HEADS * 2 * A_QK_DIM, np.int32)
    for p in range(A_HEADS // 2):
        for l in range(LANES):
            half, grp, j = l // 64, (l % 64) // 16, l % 16
            hl, m = grp // 2, grp % 2
            idx[p * LANES + l] = (2 * p + hl) * 64 + m * 32 + half * 16 + j
    return idx


def _perm_cq():
    idx = np.zeros(C_WIDTH, np.int32)
    for hh in range(C_HPG):
        for l in range(LANES):
            half, g, j = l // 64, (l % 64) // 32, l % 32
            idx[hh * LANES + l] = (g * C_HPG + hh) * 64 + half * 32 + j
    return idx


def _perm_ck():
    idx = np.zeros(C_KV_GROUPS * C_HEAD_DIM, np.int32)
    for l in range(LANES):
        half, g, j = l // 64, (l % 64) // 32, l % 32
        idx[l] = g * 64 + half * 32 + j
    return idx


_IN_SPLITS = (384, 384, 384, 256, 384, 128, 128, 128, 128, 128, 128, 18)
_OFFS = np.concatenate([[0], np.cumsum(_IN_SPLITS)]).astype(np.int64)
A_QSCALE = A_QK_DIM ** -0.5 * LOG2E
C_QSCALE = C_HEAD_DIM ** -0.5 * LOG2E
GATE_ROWS = 16
N_TOK = 384 + 256 + 4 * LANES
N_FEAT = 3 * 384 + 2 * LANES + C_KV_GROUPS * GATE_ROWS


def _prep_w_in(w):
    o = _OFFS
    pd, pq, pk = _perm_diff(), _perm_cq(), _perm_ck()
    seg = lambda i: w[:, o[i]:o[i + 1]]
    w_tok = jnp.concatenate([seg(1)[:, pd], seg(3), seg(5)[:, pk], seg(6), seg(7)[:, pk], seg(9)[:, pk]], axis=1)
    per = C_HPG * N_BRANCH
    gates = jnp.pad(seg(11).reshape(-1, C_KV_GROUPS, per), ((0, 0), (0, 0), (0, GATE_ROWS - per)))
    w_feat = jnp.concatenate([seg(0)[:, pd], seg(2), seg(4)[:, pq], seg(8), seg(10),
                              gates.reshape(-1, C_KV_GROUPS * GATE_ROWS)], axis=1)
    return w_tok.astype(BF16), w_feat.T.astype(BF16)


def _rope_tables(T):
    def tab(dim):
        half = dim // 2
        inv = 1.0 / (ROPE_THETA ** (jnp.arange(0, dim, 2, dtype=F32) / dim))
        ang = jnp.arange(T, dtype=F32)[:, None] * inv[None, :]
        lane = np.arange(LANES)
        j = lane % half
        sign = np.where(lane < 64, -1.0, 1.0).astype(np.float32)
        cos, sin = jnp.cos(ang), jnp.sin(ang)
        return cos[:, j], sin[:, j] * sign[None, :], cos[:, j[:64]].T, sin[:, j[:64]].T
    return tab(A_QK_DIM) + tab(C_HEAD_DIM)


def _inproj_kernel(x_ref, wt_ref, wf_ref, ca_ref, sa_ref, caT_ref, saT_ref, cc_ref, sc_ref, ccT_ref, scT_ref,
                   ka_ref, u_ref, kcm_ref, vcm_ref, ksl_ref, kwi_ref,
                   qa_ref, va_ref, qc_ref, vsl_ref, vwi_ref, gt_ref):
    xb = x_ref[...].astype(BF16)
    tok = _dot(xb, wt_ref[...])
    feat = lax.dot_general(wf_ref[...], xb, (((1,), (1,)), ((), ())), preferred_element_type=F32)

    def rope_tok(p, cos, sin):
        return p * cos + pltpu.roll(p, 64, axis=1) * sin

    def rope_feat(p, cos, sin):
        a, b = p[:64, :], p[64:, :]
        return jnp.concatenate([a * cos - b * sin, b * cos + a * sin], axis=0)

    ca, sa, cc, sc = ca_ref[...], sa_ref[...], cc_ref[...], sc_ref[...]
    caT, saT, ccT, scT = caT_ref[...], saT_ref[...], ccT_ref[...], scT_ref[...]
    for t in range(3):
        r = slice(t * LANES, (t + 1) * LANES)
        ka_ref[:, r] = rope_tok(tok[:, r], ca, sa).astype(BF16)
        qa_ref[r, :] = (rope_feat(feat[r, :], caT, saT) * A_QSCALE).astype(BF16)
        rq = slice(768 + t * LANES, 768 + (t + 1) * LANES)
        qc_ref[r, :] = (rope_feat(feat[rq, :], ccT, scT) * C_QSCALE).astype(BF16)
    u_ref[...] = tok[:, 384:640]
    kcm_ref[...] = rope_tok(tok[:, 640:768], cc, sc).astype(BF16)
    vcm_ref[...] = tok[:, 768:896].astype(BF16)
    ksl_ref[...] = rope_tok(tok[:, 896:1024], cc, sc).astype(BF16)
    kwi_ref[...] = rope_tok(tok[:, 1024:1152], cc, sc).astype(BF16)
    va_ref[...] = feat[384:768, :].astype(BF16)
    vsl_ref[...] = feat[1152:1280, :].astype(BF16)
    vwi_ref[...] = feat[1280:1408, :].astype(BF16)
    gt_ref[...] = jax.nn.sigmoid(feat[1408:, :])


def _inproj(x3, w_tok, w_feat, tabs, tm):
    B, T, D = x3.shape
    tok = lambda n: pl.BlockSpec((None, tm, n), lambda b, i: (b, i, 0))
    feat = lambda n: pl.BlockSpec((None, n, tm), lambda b, i: (b, 0, i))
    full = lambda a: pl.BlockSpec(a.shape, lambda b, i: (0, 0))
    tab_tok = pl.BlockSpec((tm, LANES), lambda b, i: (i, 0))
    tab_feat = pl.BlockSpec((64, tm), lambda b, i: (0, i))
    tok_outs = [(384, BF16), (256, F32)] + [(LANES, BF16)] * 4
    feat_outs = [(384, BF16)] * 3 + [(LANES, BF16)] * 2 + [(C_KV_GROUPS * GATE_ROWS, F32)]
    return pl.pallas_call(
        _inproj_kernel,
        grid=(B, T // tm),
        in_specs=[tok(D), full(w_tok), full(w_feat)] + [tab_tok, tab_tok, tab_feat, tab_feat] * 2,
        out_specs=[tok(n) for n, _ in tok_outs] + [feat(n) for n, _ in feat_outs],
        out_shape=([jax.ShapeDtypeStruct((B, T, n), dt) for n, dt in tok_outs]
                   + [jax.ShapeDtypeStruct((B, n, T), dt) for n, dt in feat_outs]),
        compiler_params=_params(("parallel", "parallel")),
        name="inproj_rope",
    )(x3, w_tok, w_feat, *tabs)


def _sweep_scratch(tk, nq, dv, n_stats=1):
    stats = [pltpu.VMEM((1, nq), F32), pltpu.VMEM((1, nq), F32), pltpu.VMEM((dv, nq), F32)]
    return [pltpu.VMEM((tk, nq), F32), pltpu.VMEM((tk, nq), F32),
            pltpu.VMEM((1, nq), F32), pltpu.VMEM((1, nq), F32)] + stats * n_stats


class _Sweep:
    def __init__(self, score, qpos, value, bufs, stats):
        self.score, self.qpos, self.value = score, qpos, value
        self.s_refs, self.cm_refs = bufs[0:2], bufs[2:4]
        self.m_ref, self.l_ref, self.acc_ref = stats
        self.tk, self.nq = self.s_refs[0].shape

    def init(self):
        self.m_ref[...] = jnp.full(self.m_ref.shape, NEG_BIG, F32)
        self.l_ref[...] = jnp.zeros(self.l_ref.shape, F32)
        self.acc_ref[...] = jnp.zeros(self.acc_ref.shape, F32)

    def scores(self, j, slot, fn=None):
        s = self.score(j) if fn is None else fn()
        self.s_refs[slot][...] = s
        self.cm_refs[slot][...] = jnp.max(s, axis=0, keepdims=True)

    def softmax(self, slot, j, causal=False):
        s = self.s_refs[slot][...]
        if causal:
            kpos = j * self.tk + lax.broadcasted_iota(jnp.int32, (self.tk, self.nq), 0)
            s = jnp.where(kpos <= self.qpos, s, MASKED)
            cm = jnp.max(s, axis=0, keepdims=True)
        else:
            cm = self.cm_refs[slot][...]
        m = self.m_ref[...]
        m_new = jnp.maximum(m, cm)
        alpha = jnp.exp2(m - m_new)
        p = jnp.exp2(s - m_new)
        self.l_ref[...] = self.l_ref[...] * alpha + jnp.sum(p, axis=0, keepdims=True)
        self.m_ref[...] = m_new
        self.acc_ref[...] = self.acc_ref[...] * alpha + _dot(self.value(j), p.astype(BF16))

    def full_tiles(self, count, par):
        def group(base, tiles):
            for u in range(tiles):
                self.scores(base + u + 1, (par + u + 1) % 2)
                self.softmax((par + u) % 2, base + u)

        def quad(i, c):
            group(SWEEP_UNROLL * i, SWEEP_UNROLL)
            return c

        def pair(i, c):
            group(done + 2 * i, 2)
            return c

        lax.fori_loop(0, count // SWEEP_UNROLL, quad, 0)
        done = (count // SWEEP_UNROLL) * SWEEP_UNROLL
        lax.fori_loop(0, (count - done) // 2, pair, 0)

    def result(self):
        return self.acc_ref[...] / self.l_ref[...]


def _causal_sweep(n_full, score, qpos, value, scratch, first_scores=None):
    sw = _Sweep(score, qpos, value, scratch[0:4], scratch[4:7])
    sw.init()
    sw.scores(0, 0, first_scores)
    sw.full_tiles(n_full - n_full % 2, 0)

    @pl.when(n_full % 2 == 0)
    def _():
        sw.softmax(0, n_full, causal=True)

    @pl.when(n_full % 2 == 1)
    def _():
        sw.scores(n_full, 1)
        sw.softmax(0, n_full - 1)
        sw.softmax(1, n_full, causal=True)

    return sw.result()


def _diff_attn_kernel(lq1_ref, lk1_ref, lq2_ref, lk2_ref, g_ref, q_ref, k_ref, v_ref, o_ref,
                      *scratch, tq, tk, lam_init):
    h = pl.program_id(1)
    i2 = pl.program_id(2)
    hl = h % 2
    nq = 2 * tq
    grp = (lax.broadcasted_iota(jnp.int32, (LANES, tq), 0) % 64) // 16
    lane_q = lax.broadcasted_iota(jnp.int32, (1, nq), 1) % tq

    def value(j):
        return v_ref[:, pl.ds(pl.multiple_of(j * tk, tk), tk)]

    def sweep(blk, stats):
        q = q_ref[:, blk * tq:(blk + 1) * tq]
        zero = jnp.zeros_like(q)
        q12 = jnp.concatenate([jnp.where(grp == 2 * hl, q, zero), jnp.where(grp == 2 * hl + 1, q, zero)], axis=1)
        qpos = (2 * i2 + blk) * tq + lane_q
        score = lambda j: _dot(k_ref[pl.ds(pl.multiple_of(j * tk, tk), tk), :], q12)
        return _Sweep(score, qpos, value, scratch[0:4], stats)

    dotp = lambda a, b: jnp.sum(a[...] * b[...], axis=1, keepdims=True)
    lam = jnp.exp(dotp(lq1_ref, lk1_ref)) - jnp.exp(dotp(lq2_ref, lk2_ref)) + lam_init

    def finish(sw, blk):
        on = sw.result()
        o = on[:, :tq] - lam * on[:, tq:]
        ms = jnp.mean(o * o, axis=0, keepdims=True)
        o = o * lax.rsqrt(ms + RMS_EPS) * g_ref[...] * (1.0 - lam_init)
        o_ref[:, blk * tq:(blk + 1) * tq] = o.astype(o_ref.dtype)

    a, b = sweep(0, scratch[4:7]), sweep(1, scratch[7:10])
    n_a = 2 * i2
    a.init()
    a.scores(0, 0)
    a.full_tiles(n_a, 0)
    b.init()
    b.scores(0, 1)
    a.softmax(0, n_a, causal=True)
    finish(a, 0)
    b.full_tiles(n_a, 1)
    b.scores(n_a + 1, 0)
    b.softmax(1, n_a)
    b.softmax(0, n_a + 1, causal=True)
    finish(b, 1)


def _diff_attn(qaT, ka, vaT, lq1, lk1, lq2, lk2, g_col, lam_init, tq, tk):
    B, _, T = qaT.shape
    assert tq == tk and T % (2 * tq) == 0, "query block i ends at key tile i; blocks are taken in pairs"
    nq = 2 * tq
    vec = pl.BlockSpec((1, A_QK_DIM), lambda b, h, i: (0, 0))
    return pl.pallas_call(
        functools.partial(_diff_attn_kernel, tq=tq, tk=tk, lam_init=lam_init),
        grid=(B, A_HEADS, T // (2 * tq)),
        in_specs=[vec, vec, vec, vec,
                  pl.BlockSpec((A_V_DIM, 1), lambda b, h, i: (0, 0)),
                  pl.BlockSpec((None, LANES, 2 * tq), lambda b, h, i: (b, h // 2, i)),
                  pl.BlockSpec((None, T, LANES), lambda b, h, i: (b, 0, h // 2)),
                  pl.BlockSpec((None, A_V_DIM, T), lambda b, h, i: (b, h, 0))],
        out_specs=pl.BlockSpec((None, A_V_DIM, 2 * tq), lambda b, h, i: (b, h, i)),
        out_shape=jax.ShapeDtypeStruct((B, A_WIDTH, T), BF16),
        scratch_shapes=_sweep_scratch(tk, nq, A_V_DIM, n_stats=2),
        compiler_params=_params(("parallel", "parallel", "arbitrary")),
        name="diff_attn",
    )(lq1, lk1, lq2, lk2, g_col, qaT, ka, vaT)


POOL_HALO = 16


def _pool_kernel(prev_ref, cur_ref, w_ref, scale_ref, o_ref, *, tm):
    i = pl.program_id(1)
    cur = cur_ref[...]
    halo = prev_ref[tm - POOL_HALO:, :]
    halo = jnp.where(i == 0, jnp.zeros_like(halo), halo)
    arr = jnp.concatenate([halo, cur], axis=0)
    s2 = arr + pltpu.roll(arr, 1, axis=0)
    s4 = s2 + pltpu.roll(s2, 2, axis=0)
    s8 = s4 + pltpu.roll(s4, 4, axis=0)
    s16 = s8 + pltpu.roll(s8, 8, axis=0)
    sums = [s[POOL_HALO:, :] for s in (s2, s4, s8, s16)]
    lane = lax.broadcasted_iota(jnp.int32, (tm, POOL_WIDTH), 1)
    tpos = i * tm + lax.broadcasted_iota(jnp.int32, (tm, POOL_WIDTH), 0)
    grp = lane // POOL_GROUP_DIM
    win = jnp.where(grp == 0, sums[0], jnp.where(grp == 1, sums[1], jnp.where(grp == 2, sums[2], sums[3])))
    width = jnp.where(grp == 0, 2, jnp.where(grp == 1, 4, jnp.where(grp == 2, 8, 16)))
    cnt = jnp.minimum(tpos + 1, width).astype(F32)
    y = win / cnt - cur
    z = _dot(y.astype(BF16), w_ref[...])
    o_ref[...] = (z * scale_ref[...]).astype(o_ref.dtype)


def _pool(u3, w_bd, scale, tm):
    B, T, W = u3.shape
    return pl.pallas_call(
        functools.partial(_pool_kernel, tm=tm),
        grid=(B, T // tm),
        in_specs=[pl.BlockSpec((None, tm, W), lambda b, i: (b, jnp.maximum(i - 1, 0), 0)),
                  pl.BlockSpec((None, tm, W), lambda b, i: (b, i, 0)),
                  pl.BlockSpec((W, W), lambda b, i: (0, 0)),
                  pl.BlockSpec((1, W), lambda b, i: (0, 0))],
        out_specs=pl.BlockSpec((None, tm, W), lambda b, i: (b, i, 0)),
        out_shape=jax.ShapeDtypeStruct((B, T, W), BF16),
        compiler_params=_params(("parallel", "arbitrary")),
        name="pool_mixer",
    )(u3, u3, w_bd, scale)


def _cmp_kernel(k_ref, v_ref, pk_ref, pv_ref, wk_ref, wv_ref, kc_ref, vc_ref):
    n = k_ref.shape[0]
    rowid = lax.broadcasted_iota(jnp.int32, (n, LANES), 0)

    def compress(x_ref, pe_ref, w_ref):
        x = x_ref[...].astype(F32)
        lo = _dot((x + pe_ref[0:1, :]).astype(BF16), w_ref[0])
        hi = _dot((x + pe_ref[1:2, :]).astype(BF16), w_ref[1])
        out = lo + pltpu.roll(hi, n - 1, axis=0)
        return jnp.where(rowid < n - 1, out, 0.0)

    kc_ref[...] = compress(k_ref, pk_ref, wk_ref).astype(kc_ref.dtype)
    vc_ref[...] = compress(v_ref, pv_ref, wv_ref).astype(vc_ref.dtype)


def _compress(kr, vr, pe_k2, pe_v2, wk2, wv2):
    B, n, W = kr.shape
    blk = pl.BlockSpec((None, n, W), lambda b: (b, 0, 0))
    pe = pl.BlockSpec((2, W), lambda b: (0, 0))
    wsp = pl.BlockSpec((2, W, LANES), lambda b: (0, 0, 0))
    osp = pl.BlockSpec((None, n, LANES), lambda b: (b, 0, 0))
    return pl.pallas_call(
        _cmp_kernel,
        grid=(B,),
        in_specs=[blk, blk, pe, pe, wsp, wsp],
        out_specs=[osp, osp],
        out_shape=[jax.ShapeDtypeStruct((B, n, LANES), BF16)] * 2,
        compiler_params=_params(("parallel",)),
        name="nsa_compress",
    )(kr, vr, pe_k2, pe_v2, wk2, wv2)


def _prep_cmp(pe_k, pe_v, w_ck, w_cv):
    pk = _perm_ck()
    lane = np.arange(LANES)
    g_of_k = (lane % 64) // 32
    d_of_k = pk % 64
    g_of_v = lane // 64
    d_of_v = lane % 64

    def build(w, pe, g_of, d_of):
        wl = w[:, d_of, :]
        onehot = jnp.asarray((g_of[:, None] == np.arange(C_KV_GROUPS)[None, :]).astype(np.float32))
        w2 = wl[:, :, None, :] * onehot[None, :, :, None]
        w2 = w2.reshape(2, 16 * LANES, C_KV_GROUPS * C_HEAD_DIM).astype(BF16)
        pe2 = pe[:, d_of].reshape(2, 16 * LANES)
        return w2, pe2

    wk2, pk2 = build(w_ck, pe_k, g_of_k, d_of_k)
    wv2, pv2 = build(w_cv, pe_v, g_of_v, d_of_v)
    return pk2, pv2, wk2[:, :, pk], wv2


def _nsa_kernel(q_ref, gt_ref, kc_ref, vc_ref, ov_ref, wb_ref, ks_ref, vs_ref, kw_ref, vw_ref, o_ref, sel_ref,
                *scratch, tq, tk, T):
    g = pl.program_id(1)
    qi = pl.program_id(2)
    q0 = qi * tq
    n_cmp = kc_ref.shape[0]
    n_blk = T // SEL_BLOCK
    d = C_HEAD_DIM
    tpos = q0 + lax.broadcasted_iota(jnp.int32, (1, tq), 1)

    nq = C_HPG * tq
    tpos3 = q0 + lax.broadcasted_iota(jnp.int32, (1, nq), 1) % tq
    rowg = (lax.broadcasted_iota(jnp.int32, (LANES, tq), 0) % 64) // 32
    qs = []
    for hh in range(C_HPG):
        qt = q_ref[hh * LANES:(hh + 1) * LANES, :]
        qs.append(jnp.where(rowg == g, qt, jnp.zeros_like(qt)))
    q3 = jnp.concatenate(qs, axis=1)

    scratch[0][...] = _dot(ks_ref[0:tk, :], q3)

    band = WINDOW + tq
    w0 = pl.multiple_of(jnp.maximum(q0 - WINDOW, 0), LANES)
    s = _dot(kw_ref[pl.ds(w0, band), :], q3) + jnp.concatenate([wb_ref[...]] * C_HPG, axis=1)
    e = jnp.exp2(s - jnp.max(s, axis=0, keepdims=True))
    o_win = _dot(vw_ref[:, pl.ds(w0, band)], e.astype(BF16)) / jnp.sum(e, axis=0, keepdims=True)

    cend = lax.broadcasted_iota(jnp.int32, (n_cmp, nq), 0) * CMP_STRIDE + (CMP_BLOCK - 1)
    s = jnp.where(cend <= tpos3, _dot(kc_ref[...], q3), MASKED)
    m = jnp.maximum(jnp.max(s, axis=0, keepdims=True), NEG_BIG)
    e = jnp.exp2(s - m)
    l = jnp.sum(e, axis=0, keepdims=True)
    p = e / jnp.where(l > 0.0, l, 1.0)
    o_cmp = _dot(vc_ref[...], p.astype(BF16))
    psum = p[:, :tq] + p[:, tq:2 * tq] + p[:, 2 * tq:]

    p_hi = psum.astype(BF16)
    p_lo = (psum - p_hi.astype(F32)).astype(BF16)
    imp = _dot(ov_ref[...], p_hi) + _dot(ov_ref[...], p_lo)
    jidx = lax.broadcasted_iota(jnp.int32, (n_blk, tq), 0).astype(F32)
    cur = (tpos // SEL_BLOCK).astype(F32)
    forced = (jidx == 0.0) | (jidx == cur) | (jidx == cur - 1.0)
    imp = jnp.where(jidx > cur, -1.0, imp)
    imp = jnp.where(forced, TAKEN, imp)
    rounds = min(N_SEL, n_blk) - N_FORCED

    def store_selection(taken_imp):
        bias = jnp.where(taken_imp == TAKEN, 0.0, MASKED)
        sel_ref[...] = jnp.concatenate([bias] * C_HPG, axis=1)

    quick = imp
    for _ in range(rounds):
        quick = jnp.where(quick == jnp.max(quick, axis=0, keepdims=True), TAKEN, quick)
    store_selection(quick)
    free = (jidx >= 1.0) & (jidx <= cur - 2.0)
    n_taken = jnp.sum(jnp.where(free & (quick == TAKEN), 1.0, 0.0), axis=0, keepdims=True)
    n_expected = jnp.minimum(jnp.maximum(cur - 2.0, 0.0), float(rounds))
    tie_in_top = jnp.max(jnp.abs(n_taken - n_expected)) > 0.0

    @pl.when(tie_in_top)
    def _():
        exact = imp
        for _ in range(rounds):
            mx = jnp.max(exact, axis=0, keepdims=True)
            first = jnp.min(jnp.where(exact == mx, jidx, float(n_blk)), axis=0, keepdims=True)
            exact = jnp.where(jidx == first, TAKEN, exact)
        store_selection(exact)

    bpt = tk // SEL_BLOCK

    def block_bias(j):
        rows = sel_ref[pl.ds(pl.multiple_of(j * bpt, bpt), bpt), :]
        return jnp.concatenate([jnp.broadcast_to(rows[i:i + 1, :], (SEL_BLOCK, nq)) for i in range(bpt)], axis=0)

    def score(j):
        return _dot(ks_ref[pl.ds(pl.multiple_of(j * tk, tk), tk), :], q3) + block_bias(j)

    def first_scores():
        return scratch[0][...] + block_bias(0)

    def value(j):
        return vs_ref[:, pl.ds(pl.multiple_of(j * tk, tk), tk)]

    o_slc = _causal_sweep(q0 // tk, score, tpos3, value, scratch, first_scores)

    gts = gt_ref[...]
    for hh in range(C_HPG):
        r = hh * N_BRANCH
        c = slice(hh * tq, (hh + 1) * tq)
        out = (o_cmp[:, c] * gts[r:r + 1, :] + o_slc[:, c] * gts[r + 1:r + 2, :]
               + o_win[:, c] * gts[r + 2:r + 3, :])
        o_ref[hh * d:(hh + 1) * d, :] = out.astype(o_ref.dtype)


def _window_bias(tq):
    band, n_edge = WINDOW + tq, WINDOW // tq
    q0 = np.arange(n_edge + 1)[:, None, None] * tq
    wpos = np.maximum(q0 - WINDOW, 0) + np.arange(band)[None, :, None]
    tpos = q0 + np.arange(tq)[None, None, :]
    ok = (wpos <= tpos) & (wpos > tpos - WINDOW)
    return jnp.asarray(np.where(ok, 0.0, MASKED).astype(np.float32))


def _nsa(qcT, gtT, kc2, vcT, ovT, ksl, vslT, kwi, vwiT, tq, tk):
    B, _, T = qcT.shape
    n_cmp = kc2.shape[1]
    n_blk = T // SEL_BLOCK
    nq = C_HPG * tq
    assert tk % tq == 0, "the causal key tile must contain the whole query block"
    assert WINDOW % tq == 0 and T >= WINDOW + tq
    wb = _window_bias(tq)
    n_edge = WINDOW // tq
    kv_tok = pl.BlockSpec((None, T, LANES), lambda b, g, i: (b, 0, 0))
    kv_feat = pl.BlockSpec((None, C_HEAD_DIM, T), lambda b, g, i: (b, g, 0))
    return pl.pallas_call(
        functools.partial(_nsa_kernel, tq=tq, tk=tk, T=T),
        grid=(B, C_KV_GROUPS, T // tq),
        in_specs=[pl.BlockSpec((None, C_WIDTH, tq), lambda b, g, i: (b, 0, i)),
                  pl.BlockSpec((None, GATE_ROWS, tq), lambda b, g, i: (b, g, i)),
                  pl.BlockSpec((None, n_cmp, LANES), lambda b, g, i: (b, 0, 0)),
                  pl.BlockSpec((None, C_HEAD_DIM, n_cmp), lambda b, g, i: (b, g, 0)),
                  pl.BlockSpec((n_blk, n_cmp), lambda b, g, i: (0, 0)),
                  pl.BlockSpec((None, WINDOW + tq, tq), lambda b, g, i: (jnp.minimum(i, n_edge), 0, 0)),
                  kv_tok, kv_feat, kv_tok, kv_feat],
        out_specs=pl.BlockSpec((None, C_HPG * C_HEAD_DIM, tq), lambda b, g, i: (b, g, i)),
        out_shape=jax.ShapeDtypeStruct((B, C_WIDTH, T), BF16),
        scratch_shapes=[pltpu.VMEM((n_blk, nq), F32)] + _sweep_scratch(tk, nq, C_HEAD_DIM),
        compiler_params=_params(("parallel", "parallel", "arbitrary")),
        name="nsa_attn",
    )(qcT, gtT, kc2, vcT, ovT, wb, ksl, vslT, kwi, vwiT)


def _overlap_T(T):
    n_cmp = T // CMP_STRIDE
    n_blk = T // SEL_BLOCK
    cs = np.arange(n_cmp)[None, :] * CMP_STRIDE
    bs = np.arange(n_blk)[:, None] * SEL_BLOCK
    ov = (cs < bs + SEL_BLOCK) & (cs + CMP_BLOCK > bs) & (np.arange(n_cmp)[None, :] < n_cmp - 1)
    return jnp.asarray(ov.astype(np.float32), dtype=BF16)


def _layer_norm(y, g, b):
    mu = jnp.mean(y, axis=-1, keepdims=True)
    yc = y - mu
    var = jnp.mean(yc * yc, axis=-1, keepdims=True)
    return yc * lax.rsqrt(var + LN_EPS) * g + b


def _outproj_kernel(x_ref, oa_ref, ob_ref, oc_ref, wa_ref, wb_ref, wc_ref, g_ref, b_ref, o_ref, *, alpha):
    tn = lambda a, w: lax.dot_general(a, w, (((0,), (0,)), ((), ())), preferred_element_type=F32)
    mix = tn(oa_ref[...], wa_ref[...]) + _dot(ob_ref[...], wb_ref[...]) + tn(oc_ref[...], wc_ref[...])
    o_ref[...] = _layer_norm(alpha * x_ref[...] + mix, g_ref[...], b_ref[...])


def _outproj(x3, oaT, ob, ocT, wa, wb, wc, g, b, alpha, tm):
    B, T, D = x3.shape
    tok = lambda n: pl.BlockSpec((None, tm, n), lambda bb, i: (bb, i, 0))
    feat = lambda n: pl.BlockSpec((None, n, tm), lambda bb, i: (bb, 0, i))
    full = lambda a: pl.BlockSpec(a.shape, lambda bb, i: (0, 0))
    return pl.pallas_call(
        functools.partial(_outproj_kernel, alpha=alpha),
        grid=(B, T // tm),
        in_specs=[tok(D), feat(oaT.shape[1]), tok(ob.shape[2]), feat(ocT.shape[1]),
                  full(wa), full(wb), full(wc), full(g), full(b)],
        out_specs=tok(D),
        out_shape=jax.ShapeDtypeStruct((B, T, D), F32),
        compiler_params=_params(("parallel", "parallel")),
        name="outproj_ln",
    )(x3, oaT, ob, ocT, wa, wb, wc, g, b)


def _mlp_kernel(x_ref, wu_ref, wd_ref, g_ref, b_ref, o_ref, acc_ref, *, alpha):
    f = pl.program_id(1)

    @pl.when(f == 0)
    def _():
        acc_ref[...] = jnp.zeros_like(acc_ref)

    hid = _dot(x_ref[...].astype(BF16), wu_ref[...])
    hid = jnp.square(jnp.maximum(hid, 0.0))
    acc_ref[...] += _dot(hid.astype(BF16), wd_ref[...])

    @pl.when(f == pl.num_programs(1) - 1)
    def _():
        o_ref[...] = _layer_norm(alpha * x_ref[...] + acc_ref[...], g_ref[...], b_ref[...])


def _mlp(x2, wu, wd, g, b, alpha, tm, tf):
    BT, D = x2.shape
    F = wu.shape[1]
    return pl.pallas_call(
        functools.partial(_mlp_kernel, alpha=alpha),
        grid=(BT // tm, F // tf),
        in_specs=[pl.BlockSpec((tm, D), lambda i, f: (i, 0)),
                  pl.BlockSpec((D, tf), lambda i, f: (0, f)),
                  pl.BlockSpec((tf, D), lambda i, f: (f, 0)),
                  pl.BlockSpec((1, D), lambda i, f: (0, 0)),
                  pl.BlockSpec((1, D), lambda i, f: (0, 0))],
        out_specs=pl.BlockSpec((tm, D), lambda i, f: (i, 0)),
        out_shape=jax.ShapeDtypeStruct((BT, D), F32),
        scratch_shapes=[pltpu.VMEM((tm, D), F32)],
        compiler_params=_params(("parallel", "arbitrary")),
        name="mlp_ln",
    )(x2, wu, wd, g, b)


def _tiles(T):
    return dict(
        tm_proj=min(512, T), tm_pool=min(512, T), tm_out=min(512, T), tm_mlp=min(1024, T), tf_mlp=1024,
        a_tq=min(512, T), a_tk=min(512, T), c_tq=min(256, T), c_tk=min(512, T),
    )


def kernel(x, w_in, lam_q1, lam_k1, lam_q2, lam_k2, subln_g, w_pool, pool_scale, cmp_pe_k, cmp_pe_v,
           w_cmp_k, w_cmp_v, w_out, ln1_g, ln1_b, w_up, w_down, ln2_g, ln2_b):
    B, T, D = x.shape
    depth = w_in.shape[0]
    alpha = (2 * depth) ** 0.25
    tl = _tiles(T)
    tabs = _rope_tables(T)
    ovT = _overlap_T(T)
    x3 = x
    for l in range(depth):
        lam_init = 0.8 - 0.6 * math.exp(-0.3 * l)
        w_tok, w_feat = _prep_w_in(w_in[l])
        (ka, u, kcm, vcm, ksl, kwi, qaT, vaT, qcT, vslT, vwiT, gtT) = _inproj(x3, w_tok, w_feat, tabs, tl["tm_proj"])

        oaT = _diff_attn(qaT, ka, vaT, lam_q1[l][None, :], lam_k1[l][None, :], lam_q2[l][None, :],
                         lam_k2[l][None, :], subln_g[l][:, None], lam_init, tl["a_tq"], tl["a_tk"])

        w_bd = jax.scipy.linalg.block_diag(*[w_pool[l, g] for g in range(len(POOL_WINDOWS))]).astype(BF16)
        ob = _pool(u, w_bd, pool_scale[l][None, :], tl["tm_pool"])

        pk2, pv2, wk2, wv2 = _prep_cmp(cmp_pe_k[l], cmp_pe_v[l], w_cmp_k[l], w_cmp_v[l])
        n16 = T // CMP_STRIDE
        kc2, vc2 = _compress(kcm.reshape(B, n16, 16 * LANES), vcm.reshape(B, n16, 16 * LANES),
                             pk2, pv2, wk2, wv2)
        ocT = _nsa(qcT, gtT, kc2, vc2.transpose(0, 2, 1), ovT, ksl, vslT, kwi, vwiT, tl["c_tq"], tl["c_tk"])

        wo = w_out[l].astype(BF16)
        x3 = _outproj(x3, oaT, ob, ocT, wo[:A_WIDTH], wo[A_WIDTH:A_WIDTH + POOL_WIDTH], wo[A_WIDTH + POOL_WIDTH:],
                      ln1_g[l][None, :], ln1_b[l][None, :], alpha, tl["tm_out"])
        x3 = _mlp(x3.reshape(B * T, D), w_up[l].astype(BF16), w_down[l].astype(BF16), ln2_g[l][None, :],
                  ln2_b[l][None, :], alpha, tl["tm_mlp"], tl["tf_mlp"]).reshape(B, T, D)
    return x3
```

```python
import functools
import math

import jax
import jax.numpy as jnp
import numpy as np
from jax import lax
from jax.experimental import pallas as pl
from jax.experimental.pallas import tpu as pltpu

A_HEADS = 6
A_QK_DIM = 32
A_V_DIM = 64
A_WIDTH = A_HEADS * A_V_DIM
POOL_WINDOWS = (2, 4, 8, 16)
POOL_GROUP_DIM = 64
POOL_WIDTH = 256
C_HEADS = 6
C_KV_GROUPS = 2
C_HPG = 3
C_HEAD_DIM = 64
C_WIDTH = C_HEADS * C_HEAD_DIM
CMP_BLOCK = 32
CMP_STRIDE = 16
SEL_BLOCK = 64
N_SEL = 16
WINDOW = 512
N_BRANCH = 3
FORCED_SCORE = 1.0e4
N_FORCED = 3
ROPE_THETA = 10000.0
LN_EPS = 1e-5
RMS_EPS = 1e-6

LANES = 128
LOG2E = 1.4426950408889634
NEG_BIG = -1e30
SWEEP_UNROLL = 4
TAKEN = -3.0e38
MASKED = -3e38
VMEM_LIMIT = 56 * 1024 * 1024

BF16 = jnp.bfloat16
F32 = jnp.float32


def _dot(a, b):
    return jnp.dot(a, b, preferred_element_type=F32)


def _params(sem, vmem=VMEM_LIMIT):
    return pltpu.CompilerParams(dimension_semantics=sem, vmem_limit_bytes=vmem)


def _perm_diff():
    idx = np.zeros(A_HEADS * 2 * A_QK_DIM, np.int32)
    for p in range(A_HEADS // 2):
        for l in range(LANES):
            half, grp, j = l // 64, (l % 64) // 16, l % 16
            hl, m = grp // 2, grp % 2
            idx[p * LANES + l] = (2 * p + hl) * 64 + m * 32 + half * 16 + j
    return idx


def _perm_cq():
    idx = np.zeros(C_WIDTH, np.int32)
    for hh in range(C_HPG):
        for l in range(LANES):
            half, g, j = l // 64, (l % 64) // 32, l % 32
            idx[hh * LANES + l] = (g * C_HPG + hh) * 64 + half * 32 + j
    return idx


def _perm_ck():
    idx = np.zeros(C_KV_GROUPS * C_HEAD_DIM, np.int32)
    for l in range(LANES):
        half, g, j = l // 64, (l % 64) // 32, l % 32
        idx[l] = g * 64 + half * 32 + j
    return idx


_IN_SPLITS = (384, 384, 384, 256, 384, 128, 128, 128, 128, 128, 128, 18)
_OFFS = np.concatenate([[0], np.cumsum(_IN_SPLITS)]).astype(np.int64)
A_QSCALE = A_QK_DIM ** -0.5 * LOG2E
C_QSCALE = C_HEAD_DIM ** -0.5 * LOG2E
GATE_ROWS = 16
N_TOK = 384 + 256 + 4 * LANES
N_FEAT = 3 * 384 + 2 * LANES + C_KV_GROUPS * GATE_ROWS


def _prep_w_in(w):
    o, n_in = _OFFS, int(_OFFS[-1])
    pd, pq, pk = _perm_diff(), _perm_cq(), _perm_ck()
    seg = lambda i, perm=None: o[i] + (np.arange(o[i + 1] - o[i]) if perm is None else perm)
    idx_tok = np.concatenate([seg(1, pd), seg(3), seg(5, pk), seg(6), seg(7, pk), seg(9, pk)])
    per = C_HPG * N_BRANCH
    gates = np.full((C_KV_GROUPS, GATE_ROWS), n_in)
    gates[:, :per] = o[11] + np.arange(C_KV_GROUPS * per).reshape(C_KV_GROUPS, per)
    idx_feat = np.concatenate([seg(0, pd), seg(2), seg(4, pq), seg(8), seg(10), gates.reshape(-1)])
    wp = jnp.pad(w, ((0, 0), (0, 0), (0, 1)))
    return wp[:, :, idx_tok].astype(BF16), jnp.swapaxes(wp[:, :, idx_feat], 1, 2).astype(BF16)


def _rope_tables(T):
    def tab(dim):
        half = dim // 2
        inv = 1.0 / (ROPE_THETA ** (jnp.arange(0, dim, 2, dtype=F32) / dim))
        ang = jnp.arange(T, dtype=F32)[:, None] * inv[None, :]
        lane = np.arange(LANES)
        j = lane % half
        sign = np.where(lane < 64, -1.0, 1.0).astype(np.float32)
        cos, sin = jnp.cos(ang), jnp.sin(ang)
        return cos[:, j], sin[:, j] * sign[None, :], cos[:, j[:64]].T, sin[:, j[:64]].T
    return tab(A_QK_DIM) + tab(C_HEAD_DIM)


def _inproj_kernel(x_ref, wt_ref, wf_ref, ca_ref, sa_ref, caT_ref, saT_ref, cc_ref, sc_ref, ccT_ref, scT_ref,
                   ka_ref, u_ref, kcm_ref, vcm_ref, ksl_ref, kwi_ref,
                   qa_ref, va_ref, qc_ref, vsl_ref, vwi_ref, gt_ref):
    xb = x_ref[...].astype(BF16)
    tok = _dot(xb, wt_ref[...])
    feat = lax.dot_general(wf_ref[...], xb, (((1,), (1,)), ((), ())), preferred_element_type=F32)

    def rope_tok(p, cos, sin):
        return p * cos + pltpu.roll(p, 64, axis=1) * sin

    def rope_feat(p, cos, sin):
        a, b = p[:64, :], p[64:, :]
        return jnp.concatenate([a * cos - b * sin, b * cos + a * sin], axis=0)

    ca, sa, cc, sc = ca_ref[...], sa_ref[...], cc_ref[...], sc_ref[...]
    caT, saT, ccT, scT = caT_ref[...], saT_ref[...], ccT_ref[...], scT_ref[...]
    for t in range(3):
        r = slice(t * LANES, (t + 1) * LANES)
        ka_ref[:, r] = rope_tok(tok[:, r], ca, sa).astype(BF16)
        qa_ref[r, :] = (rope_feat(feat[r, :], caT, saT) * A_QSCALE).astype(BF16)
        rq = slice(768 + t * LANES, 768 + (t + 1) * LANES)
        qc_ref[r, :] = (rope_feat(feat[rq, :], ccT, scT) * C_QSCALE).astype(BF16)
    u_ref[...] = tok[:, 384:640]
    kcm_ref[...] = rope_tok(tok[:, 640:768], cc, sc).astype(BF16)
    vcm_ref[...] = tok[:, 768:896].astype(BF16)
    ksl_ref[...] = rope_tok(tok[:, 896:1024], cc, sc).astype(BF16)
    kwi_ref[...] = rope_tok(tok[:, 1024:1152], cc, sc).astype(BF16)
    va_ref[...] = feat[384:768, :].astype(BF16)
    vsl_ref[...] = feat[1152:1280, :].astype(BF16)
    vwi_ref[...] = feat[1280:1408, :].astype(BF16)
    gt_ref[...] = jax.nn.sigmoid(feat[1408:, :])


def _layer_spec(a, l):
    return pl.BlockSpec((None,) + a.shape[1:], lambda *_: (l,) + (0,) * (a.ndim - 1))


def _inproj(x3, w_tok, w_feat, tabs, tm, l):
    B, T, D = x3.shape
    tok = lambda n: pl.BlockSpec((None, tm, n), lambda b, i: (b, i, 0))
    feat = lambda n: pl.BlockSpec((None, n, tm), lambda b, i: (b, 0, i))
    full = lambda a: _layer_spec(a, l)
    tab_tok = pl.BlockSpec((tm, LANES), lambda b, i: (i, 0))
    tab_feat = pl.BlockSpec((64, tm), lambda b, i: (0, i))
    tok_outs = [(384, BF16), (256, F32)] + [(LANES, BF16)] * 4
    feat_outs = [(384, BF16)] * 3 + [(LANES, BF16)] * 2 + [(C_KV_GROUPS * GATE_ROWS, F32)]
    return pl.pallas_call(
        _inproj_kernel,
        grid=(B, T // tm),
        in_specs=[tok(D), full(w_tok), full(w_feat)] + [tab_tok, tab_tok, tab_feat, tab_feat] * 2,
        out_specs=[tok(n) for n, _ in tok_outs] + [feat(n) for n, _ in feat_outs],
        out_shape=([jax.ShapeDtypeStruct((B, T, n), dt) for n, dt in tok_outs]
                   + [jax.ShapeDtypeStruct((B, n, T), dt) for n, dt in feat_outs]),
        compiler_params=_params(("parallel", "parallel")),
        name="inproj_rope",
    )(x3, w_tok, w_feat, *tabs)


def _sweep_scratch(tk, nq, dv, n_stats=1):
    stats = [pltpu.VMEM((1, nq), F32), pltpu.VMEM((1, nq), F32), pltpu.VMEM((dv, nq), F32)]
    return [pltpu.VMEM((tk, nq), F32), pltpu.VMEM((tk, nq), F32),
            pltpu.VMEM((1, nq), F32), pltpu.VMEM((1, nq), F32)] + stats * n_stats


class _Sweep:
    def __init__(self, score, qpos, value, bufs, stats):
        self.score, self.qpos, self.value = score, qpos, value
        self.s_refs, self.cm_refs = bufs[0:2], bufs[2:4]
        self.m_ref, self.l_ref, self.acc_ref = stats
        self.tk, self.nq = self.s_refs[0].shape

    def init(self):
        self.m_ref[...] = jnp.full(self.m_ref.shape, NEG_BIG, F32)
        self.l_ref[...] = jnp.zeros(self.l_ref.shape, F32)
        self.acc_ref[...] = jnp.zeros(self.acc_ref.shape, F32)

    def scores(self, j, slot, fn=None):
        s = self.score(j) if fn is None else fn()
        self.s_refs[slot][...] = s
        self.cm_refs[slot][...] = jnp.max(s, axis=0, keepdims=True)

    def softmax(self, slot, j, causal=False):
        s = self.s_refs[slot][...]
        if causal:
            kpos = j * self.tk + lax.broadcasted_iota(jnp.int32, (self.tk, self.nq), 0)
            s = jnp.where(kpos <= self.qpos, s, MASKED)
            cm = jnp.max(s, axis=0, keepdims=True)
        else:
            cm = self.cm_refs[slot][...]
        m = self.m_ref[...]
        m_new = jnp.maximum(m, cm)
        alpha = jnp.exp2(m - m_new)
        p = jnp.exp2(s - m_new)
        self.l_ref[...] = self.l_ref[...] * alpha + jnp.sum(p, axis=0, keepdims=True)
        self.m_ref[...] = m_new
        self.acc_ref[...] = self.acc_ref[...] * alpha + _dot(self.value(j), p.astype(BF16))

    def full_tiles(self, count, par):
        def group(base, tiles):
            for u in range(tiles):
                self.scores(base + u + 1, (par + u + 1) % 2)
                self.softmax((par + u) % 2, base + u)

        def quad(i, c):
            group(SWEEP_UNROLL * i, SWEEP_UNROLL)
            return c

        def pair(i, c):
            group(done + 2 * i, 2)
            return c

        lax.fori_loop(0, count // SWEEP_UNROLL, quad, 0)
        done = (count // SWEEP_UNROLL) * SWEEP_UNROLL
        lax.fori_loop(0, (count - done) // 2, pair, 0)

    def result(self):
        return self.acc_ref[...] / self.l_ref[...]


def _causal_sweep(n_full, score, qpos, value, scratch, first_scores=None):
    sw = _Sweep(score, qpos, value, scratch[0:4], scratch[4:7])
    sw.init()
    sw.scores(0, 0, first_scores)
    sw.full_tiles(n_full - n_full % 2, 0)

    @pl.when(n_full % 2 == 0)
    def _():
        sw.softmax(0, n_full, causal=True)

    @pl.when(n_full % 2 == 1)
    def _():
        sw.scores(n_full, 1)
        sw.softmax(0, n_full - 1)
        sw.softmax(1, n_full, causal=True)

    return sw.result()


def _diff_attn_kernel(lq1_ref, lk1_ref, lq2_ref, lk2_ref, g_ref, q_ref, k_ref, v_ref, o_ref,
                      *scratch, tq, tk, lam_init):
    h = pl.program_id(1)
    i2 = pl.program_id(2)
    hl = h % 2
    nq = 2 * tq
    grp = (lax.broadcasted_iota(jnp.int32, (LANES, tq), 0) % 64) // 16
    lane_q = lax.broadcasted_iota(jnp.int32, (1, nq), 1) % tq

    def value(j):
        return v_ref[:, pl.ds(pl.multiple_of(j * tk, tk), tk)]

    def sweep(blk, stats):
        q = q_ref[:, blk * tq:(blk + 1) * tq]
        zero = jnp.zeros_like(q)
        q12 = jnp.concatenate([jnp.where(grp == 2 * hl, q, zero), jnp.where(grp == 2 * hl + 1, q, zero)], axis=1)
        qpos = (2 * i2 + blk) * tq + lane_q
        score = lambda j: _dot(k_ref[pl.ds(pl.multiple_of(j * tk, tk), tk), :], q12)
        return _Sweep(score, qpos, value, scratch[0:4], stats)

    dotp = lambda a, b: jnp.sum(a[...] * b[...], axis=1, keepdims=True)
    lam = jnp.exp(dotp(lq1_ref, lk1_ref)) - jnp.exp(dotp(lq2_ref, lk2_ref)) + lam_init

    def finish(sw, blk):
        on = sw.result()
        o = on[:, :tq] - lam * on[:, tq:]
        ms = jnp.mean(o * o, axis=0, keepdims=True)
        o = o * lax.rsqrt(ms + RMS_EPS) * g_ref[...] * (1.0 - lam_init)
        o_ref[:, blk * tq:(blk + 1) * tq] = o.astype(o_ref.dtype)

    a, b = sweep(0, scratch[4:7]), sweep(1, scratch[7:10])
    n_a = 2 * i2
    a.init()
    a.scores(0, 0)
    a.full_tiles(n_a, 0)
    b.init()
    b.scores(0, 1)
    a.softmax(0, n_a, causal=True)
    finish(a, 0)
    b.full_tiles(n_a, 1)
    b.scores(n_a + 1, 0)
    b.softmax(1, n_a)
    b.softmax(0, n_a + 1, causal=True)
    finish(b, 1)


def _diff_attn(qaT, ka, vaT, lq1, lk1, lq2, lk2, g_col, lam_init, tq, tk, l):
    B, _, T = qaT.shape
    assert tq == tk and T % (2 * tq) == 0, "query block i ends at key tile i; blocks are taken in pairs"
    nq = 2 * tq
    vec = _layer_spec(lq1, l)
    return pl.pallas_call(
        functools.partial(_diff_attn_kernel, tq=tq, tk=tk, lam_init=lam_init),
        grid=(B, A_HEADS, T // (2 * tq)),
        in_specs=[vec, vec, vec, vec, _layer_spec(g_col, l),
                  pl.BlockSpec((None, LANES, 2 * tq), lambda b, h, i: (b, h // 2, i)),
                  pl.BlockSpec((None, T, LANES), lambda b, h, i: (b, 0, h // 2)),
                  pl.BlockSpec((None, A_V_DIM, T), lambda b, h, i: (b, h, 0))],
        out_specs=pl.BlockSpec((None, A_V_DIM, 2 * tq), lambda b, h, i: (b, h, i)),
        out_shape=jax.ShapeDtypeStruct((B, A_WIDTH, T), BF16),
        scratch_shapes=_sweep_scratch(tk, nq, A_V_DIM, n_stats=2),
        compiler_params=_params(("parallel", "parallel", "arbitrary")),
        name="diff_attn",
    )(lq1, lk1, lq2, lk2, g_col, qaT, ka, vaT)


POOL_HALO = 16


def _pool_kernel(prev_ref, cur_ref, w_ref, scale_ref, o_ref, *, tm):
    i = pl.program_id(1)
    cur = cur_ref[...]
    halo = prev_ref[tm - POOL_HALO:, :]
    halo = jnp.where(i == 0, jnp.zeros_like(halo), halo)
    arr = jnp.concatenate([halo, cur], axis=0)
    s2 = arr + pltpu.roll(arr, 1, axis=0)
    s4 = s2 + pltpu.roll(s2, 2, axis=0)
    s8 = s4 + pltpu.roll(s4, 4, axis=0)
    s16 = s8 + pltpu.roll(s8, 8, axis=0)
    sums = [s[POOL_HALO:, :] for s in (s2, s4, s8, s16)]
    lane = lax.broadcasted_iota(jnp.int32, (tm, POOL_WIDTH), 1)
    tpos = i * tm + lax.broadcasted_iota(jnp.int32, (tm, POOL_WIDTH), 0)
    grp = lane // POOL_GROUP_DIM
    win = jnp.where(grp == 0, sums[0], jnp.where(grp == 1, sums[1], jnp.where(grp == 2, sums[2], sums[3])))
    width = jnp.where(grp == 0, 2, jnp.where(grp == 1, 4, jnp.where(grp == 2, 8, 16)))
    cnt = jnp.minimum(tpos + 1, width).astype(F32)
    y = win / cnt - cur
    z = _dot(y.astype(BF16), w_ref[...])
    o_ref[...] = (z * scale_ref[...]).astype(o_ref.dtype)


def _prep_pool(w_pool, pool_scale):
    depth, G, C, _ = w_pool.shape
    eye = jnp.asarray(np.eye(G, dtype=np.float32))
    w_bd = (w_pool[:, :, :, None, :] * eye[None, :, None, :, None]).reshape(depth, G * C, G * C)
    return w_bd.astype(BF16), pool_scale[:, None, :]


def _pool(u3, w_bd, scale, tm, l):
    B, T, W = u3.shape
    return pl.pallas_call(
        functools.partial(_pool_kernel, tm=tm),
        grid=(B, T // tm),
        in_specs=[pl.BlockSpec((None, tm, W), lambda b, i: (b, jnp.maximum(i - 1, 0), 0)),
                  pl.BlockSpec((None, tm, W), lambda b, i: (b, i, 0)),
                  _layer_spec(w_bd, l), _layer_spec(scale, l)],
        out_specs=pl.BlockSpec((None, tm, W), lambda b, i: (b, i, 0)),
        out_shape=jax.ShapeDtypeStruct((B, T, W), BF16),
        compiler_params=_params(("parallel", "arbitrary")),
        name="pool_mixer",
    )(u3, u3, w_bd, scale)


def _cmp_kernel(k_ref, v_ref, pk_ref, pv_ref, wk_ref, wv_ref, kc_ref, vc_ref):
    n = k_ref.shape[0]
    rowid = lax.broadcasted_iota(jnp.int32, (n, LANES), 0)

    def compress(x_ref, pe_ref, w_ref):
        x = x_ref[...].astype(F32)
        lo = _dot((x + pe_ref[0:1, :]).astype(BF16), w_ref[0])
        hi = _dot((x + pe_ref[1:2, :]).astype(BF16), w_ref[1])
        out = lo + pltpu.roll(hi, n - 1, axis=0)
        return jnp.where(rowid < n - 1, out, 0.0)

    kc_ref[...] = compress(k_ref, pk_ref, wk_ref).astype(kc_ref.dtype)
    vc_ref[...] = compress(v_ref, pv_ref, wv_ref).astype(vc_ref.dtype)


def _compress(kr, vr, pe_k2, pe_v2, wk2, wv2, l):
    B, n, W = kr.shape
    blk = pl.BlockSpec((None, n, W), lambda b: (b, 0, 0))
    pe = _layer_spec(pe_k2, l)
    wsp = _layer_spec(wk2, l)
    osp = pl.BlockSpec((None, n, LANES), lambda b: (b, 0, 0))
    return pl.pallas_call(
        _cmp_kernel,
        grid=(B,),
        in_specs=[blk, blk, pe, pe, wsp, wsp],
        out_specs=[osp, osp],
        out_shape=[jax.ShapeDtypeStruct((B, n, LANES), BF16)] * 2,
        compiler_params=_params(("parallel",)),
        name="nsa_compress",
    )(kr, vr, pe_k2, pe_v2, wk2, wv2)


def _prep_cmp(pe_k, pe_v, w_ck, w_cv):
    pk = _perm_ck()
    lane = np.arange(LANES)
    g_of_k = (lane % 64) // 32
    d_of_k = pk % 64
    g_of_v = lane // 64
    d_of_v = lane % 64
    depth = w_ck.shape[0]

    def build(w, pe, g_of, d_of, out_perm):
        g_out, e_out = out_perm // C_HEAD_DIM, out_perm % C_HEAD_DIM
        onehot = jnp.asarray((g_of[:, None] == g_out[None, :]).astype(np.float32))
        w2 = w[:, :, d_of, :][:, :, :, e_out] * onehot[None, None, :, :]
        w2 = w2.reshape(depth, 2, 16 * LANES, C_KV_GROUPS * C_HEAD_DIM).astype(BF16)
        return w2, pe[:, :, d_of].reshape(depth, 2, 16 * LANES)

    wk2, pk2 = build(w_ck, pe_k, g_of_k, d_of_k, pk)
    wv2, pv2 = build(w_cv, pe_v, g_of_v, d_of_v, np.arange(C_KV_GROUPS * C_HEAD_DIM))
    return pk2, pv2, wk2, wv2


def _nsa_kernel(q_ref, gt_ref, kc_ref, vc_ref, ov_ref, wb_ref, ks_ref, vs_ref, kw_ref, vw_ref, o_ref, sel_ref,
                *scratch, tq, tk, T):
    g = pl.program_id(1)
    qi = pl.program_id(2)
    q0 = qi * tq
    n_cmp = kc_ref.shape[0]
    n_blk = T // SEL_BLOCK
    d = C_HEAD_DIM
    tpos = q0 + lax.broadcasted_iota(jnp.int32, (1, tq), 1)

    nq = C_HPG * tq
    tpos3 = q0 + lax.broadcasted_iota(jnp.int32, (1, nq), 1) % tq
    rowg = (lax.broadcasted_iota(jnp.int32, (LANES, tq), 0) % 64) // 32
    qs = []
    for hh in range(C_HPG):
        qt = q_ref[hh * LANES:(hh + 1) * LANES, :]
        qs.append(jnp.where(rowg == g, qt, jnp.zeros_like(qt)))
    q3 = jnp.concatenate(qs, axis=1)

    scratch[0][...] = _dot(ks_ref[0:tk, :], q3)

    band = WINDOW + tq
    w0 = pl.multiple_of(jnp.maximum(q0 - WINDOW, 0), LANES)
    s = _dot(kw_ref[pl.ds(w0, band), :], q3) + jnp.concatenate([wb_ref[...]] * C_HPG, axis=1)
    e = jnp.exp2(s - jnp.max(s, axis=0, keepdims=True))
    o_win = _dot(vw_ref[:, pl.ds(w0, band)], e.astype(BF16)) / jnp.sum(e, axis=0, keepdims=True)

    cend = lax.broadcasted_iota(jnp.int32, (n_cmp, nq), 0) * CMP_STRIDE + (CMP_BLOCK - 1)
    s = jnp.where(cend <= tpos3, _dot(kc_ref[...], q3), MASKED)
    m = jnp.maximum(jnp.max(s, axis=0, keepdims=True), NEG_BIG)
    e = jnp.exp2(s - m)
    l = jnp.sum(e, axis=0, keepdims=True)
    p = e / jnp.where(l > 0.0, l, 1.0)
    o_cmp = _dot(vc_ref[...], p.astype(BF16))
    psum = p[:, :tq] + p[:, tq:2 * tq] + p[:, 2 * tq:]

    p_hi = psum.astype(BF16)
    p_lo = (psum - p_hi.astype(F32)).astype(BF16)
    imp = _dot(ov_ref[...], p_hi) + _dot(ov_ref[...], p_lo)
    jidx = lax.broadcasted_iota(jnp.int32, (n_blk, tq), 0).astype(F32)
    cur = (tpos // SEL_BLOCK).astype(F32)
    forced = (jidx == 0.0) | (jidx == cur) | (jidx == cur - 1.0)
    imp = jnp.where(jidx > cur, -1.0, imp)
    imp = jnp.where(forced, TAKEN, imp)
    rounds = min(N_SEL, n_blk) - N_FORCED

    def store_selection(taken_imp):
        bias = jnp.where(taken_imp == TAKEN, 0.0, MASKED)
        sel_ref[...] = jnp.concatenate([bias] * C_HPG, axis=1)

    quick = imp
    for _ in range(rounds):
        quick = jnp.where(quick == jnp.max(quick, axis=0, keepdims=True), TAKEN, quick)
    store_selection(quick)
    free = (jidx >= 1.0) & (jidx <= cur - 2.0)
    n_taken = jnp.sum(jnp.where(free & (quick == TAKEN), 1.0, 0.0), axis=0, keepdims=True)
    n_expected = jnp.minimum(jnp.maximum(cur - 2.0, 0.0), float(rounds))
    tie_in_top = jnp.max(jnp.abs(n_taken - n_expected)) > 0.0

    @pl.when(tie_in_top)
    def _():
        exact = imp
        for _ in range(rounds):
            mx = jnp.max(exact, axis=0, keepdims=True)
            first = jnp.min(jnp.where(exact == mx, jidx, float(n_blk)), axis=0, keepdims=True)
            exact = jnp.where(jidx == first, TAKEN, exact)
        store_selection(exact)

    bpt = tk // SEL_BLOCK

    def block_bias(j):
        rows = sel_ref[pl.ds(pl.multiple_of(j * bpt, bpt), bpt), :]
        return jnp.concatenate([jnp.broadcast_to(rows[i:i + 1, :], (SEL_BLOCK, nq)) for i in range(bpt)], axis=0)

    def score(j):
        return _dot(ks_ref[pl.ds(pl.multiple_of(j * tk, tk), tk), :], q3) + block_bias(j)

    def first_scores():
        return scratch[0][...] + block_bias(0)

    def value(j):
        return vs_ref[:, pl.ds(pl.multiple_of(j * tk, tk), tk)]

    o_slc = _causal_sweep(q0 // tk, score, tpos3, value, scratch, first_scores)

    gts = gt_ref[...]
    for hh in range(C_HPG):
        r = hh * N_BRANCH
        c = slice(hh * tq, (hh + 1) * tq)
        out = (o_cmp[:, c] * gts[r:r + 1, :] + o_slc[:, c] * gts[r + 1:r + 2, :]
               + o_win[:, c] * gts[r + 2:r + 3, :])
        o_ref[hh * d:(hh + 1) * d, :] = out.astype(o_ref.dtype)


def _window_bias(tq):
    band, n_edge = WINDOW + tq, WINDOW // tq
    q0 = np.arange(n_edge + 1)[:, None, None] * tq
    wpos = np.maximum(q0 - WINDOW, 0) + np.arange(band)[None, :, None]
    tpos = q0 + np.arange(tq)[None, None, :]
    ok = (wpos <= tpos) & (wpos > tpos - WINDOW)
    return jnp.asarray(np.where(ok, 0.0, MASKED).astype(np.float32))


def _nsa(qcT, gtT, kc2, vcT, ovT, ksl, vslT, kwi, vwiT, tq, tk):
    B, _, T = qcT.shape
    n_cmp = kc2.shape[1]
    n_blk = T // SEL_BLOCK
    nq = C_HPG * tq
    assert tk % tq == 0, "the causal key tile must contain the whole query block"
    assert WINDOW % tq == 0 and T >= WINDOW + tq
    wb = _window_bias(tq)
    n_edge = WINDOW // tq
    kv_tok = pl.BlockSpec((None, T, LANES), lambda b, g, i: (b, 0, 0))
    kv_feat = pl.BlockSpec((None, C_HEAD_DIM, T), lambda b, g, i: (b, g, 0))
    return pl.pallas_call(
        functools.partial(_nsa_kernel, tq=tq, tk=tk, T=T),
        grid=(B, C_KV_GROUPS, T // tq),
        in_specs=[pl.BlockSpec((None, C_WIDTH, tq), lambda b, g, i: (b, 0, i)),
                  pl.BlockSpec((None, GATE_ROWS, tq), lambda b, g, i: (b, g, i)),
                  pl.BlockSpec((None, n_cmp, LANES), lambda b, g, i: (b, 0, 0)),
                  pl.BlockSpec((None, C_HEAD_DIM, n_cmp), lambda b, g, i: (b, g, 0)),
                  pl.BlockSpec((n_blk, n_cmp), lambda b, g, i: (0, 0)),
                  pl.BlockSpec((None, WINDOW + tq, tq), lambda b, g, i: (jnp.minimum(i, n_edge), 0, 0)),
                  kv_tok, kv_feat, kv_tok, kv_feat],
        out_specs=pl.BlockSpec((None, C_HPG * C_HEAD_DIM, tq), lambda b, g, i: (b, g, i)),
        out_shape=jax.ShapeDtypeStruct((B, C_WIDTH, T), BF16),
        scratch_shapes=[pltpu.VMEM((n_blk, nq), F32)] + _sweep_scratch(tk, nq, C_HEAD_DIM),
        compiler_params=_params(("parallel", "parallel", "arbitrary")),
        name="nsa_attn",
    )(qcT, gtT, kc2, vcT, ovT, wb, ksl, vslT, kwi, vwiT)


def _overlap_T(T):
    n_cmp = T // CMP_STRIDE
    n_blk = T // SEL_BLOCK
    cs = np.arange(n_cmp)[None, :] * CMP_STRIDE
    bs = np.arange(n_blk)[:, None] * SEL_BLOCK
    ov = (cs < bs + SEL_BLOCK) & (cs + CMP_BLOCK > bs) & (np.arange(n_cmp)[None, :] < n_cmp - 1)
    return jnp.asarray(ov.astype(np.float32), dtype=BF16)


def _layer_norm(y, g, b):
    mu = jnp.mean(y, axis=-1, keepdims=True)
    yc = y - mu
    var = jnp.mean(yc * yc, axis=-1, keepdims=True)
    return yc * lax.rsqrt(var + LN_EPS) * g + b


def _outproj_kernel(x_ref, oa_ref, ob_ref, oc_ref, w_ref, g_ref, b_ref, o_ref, *, alpha):
    tn = lambda a, w: lax.dot_general(a, w, (((0,), (0,)), ((), ())), preferred_element_type=F32)
    r1, r2 = A_WIDTH, A_WIDTH + POOL_WIDTH
    mix = tn(oa_ref[...], w_ref[:r1, :]) + _dot(ob_ref[...], w_ref[r1:r2, :]) + tn(oc_ref[...], w_ref[r2:, :])
    o_ref[...] = _layer_norm(alpha * x_ref[...] + mix, g_ref[...], b_ref[...])


def _outproj(x3, oaT, ob, ocT, w, g, b, alpha, tm, l):
    B, T, D = x3.shape
    tok = lambda n: pl.BlockSpec((None, tm, n), lambda bb, i: (bb, i, 0))
    feat = lambda n: pl.BlockSpec((None, n, tm), lambda bb, i: (bb, 0, i))
    return pl.pallas_call(
        functools.partial(_outproj_kernel, alpha=alpha),
        grid=(B, T // tm),
        in_specs=[tok(D), feat(oaT.shape[1]), tok(ob.shape[2]), feat(ocT.shape[1]),
                  _layer_spec(w, l), _layer_spec(g, l), _layer_spec(b, l)],
        out_specs=tok(D),
        out_shape=jax.ShapeDtypeStruct((B, T, D), F32),
        compiler_params=_params(("parallel", "parallel")),
        name="outproj_ln",
    )(x3, oaT, ob, ocT, w, g, b)


def _mlp_kernel(x_ref, wu_ref, wd_ref, g_ref, b_ref, o_ref, acc_ref, *, alpha):
    f = pl.program_id(1)

    @pl.when(f == 0)
    def _():
        acc_ref[...] = jnp.zeros_like(acc_ref)

    hid = _dot(x_ref[...].astype(BF16), wu_ref[...])
    hid = jnp.square(jnp.maximum(hid, 0.0))
    acc_ref[...] += _dot(hid.astype(BF16), wd_ref[...])

    @pl.when(f == pl.num_programs(1) - 1)
    def _():
        o_ref[...] = _layer_norm(alpha * x_ref[...] + acc_ref[...], g_ref[...], b_ref[...])


def _mlp(x2, wu, wd, g, b, alpha, tm, tf, l):
    BT, D = x2.shape
    F = wu.shape[2]
    return pl.pallas_call(
        functools.partial(_mlp_kernel, alpha=alpha),
        grid=(BT // tm, F // tf),
        in_specs=[pl.BlockSpec((tm, D), lambda i, f: (i, 0)),
                  pl.BlockSpec((None, D, tf), lambda i, f: (l, 0, f)),
                  pl.BlockSpec((None, tf, D), lambda i, f: (l, f, 0)),
                  _layer_spec(g, l), _layer_spec(b, l)],
        out_specs=pl.BlockSpec((tm, D), lambda i, f: (i, 0)),
        out_shape=jax.ShapeDtypeStruct((BT, D), F32),
        scratch_shapes=[pltpu.VMEM((tm, D), F32)],
        compiler_params=_params(("parallel", "arbitrary")),
        name="mlp_ln",
    )(x2, wu, wd, g, b)


def _tiles(T):
    return dict(
        tm_proj=min(512, T), tm_pool=min(512, T), tm_out=min(512, T), tm_mlp=min(1024, T), tf_mlp=1024,
        a_tq=min(512, T), a_tk=min(512, T), c_tq=min(256, T), c_tk=min(512, T),
    )


def kernel(x, w_in, lam_q1, lam_k1, lam_q2, lam_k2, subln_g, w_pool, pool_scale, cmp_pe_k, cmp_pe_v,
           w_cmp_k, w_cmp_v, w_out, ln1_g, ln1_b, w_up, w_down, ln2_g, ln2_b):
    B, T, D = x.shape
    depth = w_in.shape[0]
    alpha = (2 * depth) ** 0.25
    tl = _tiles(T)
    tabs = _rope_tables(T)
    ovT = _overlap_T(T)
    w_tok, w_feat = _prep_w_in(w_in)
    w_bd, p_scale = _prep_pool(w_pool, pool_scale)
    pk2, pv2, wk2, wv2 = _prep_cmp(cmp_pe_k, cmp_pe_v, w_cmp_k, w_cmp_v)
    wo, wu, wd = w_out.astype(BF16), w_up.astype(BF16), w_down.astype(BF16)
    row = lambda a: a[:, None, :]
    lams = [row(a) for a in (lam_q1, lam_k1, lam_q2, lam_k2)]
    n16 = T // CMP_STRIDE
    x3 = x
    for l in range(depth):
        lam_init = 0.8 - 0.6 * math.exp(-0.3 * l)
        (ka, u, kcm, vcm, ksl, kwi, qaT, vaT, qcT, vslT, vwiT, gtT) = _inproj(x3, w_tok, w_feat, tabs, tl["tm_proj"], l)

        oaT = _diff_attn(qaT, ka, vaT, *lams, subln_g[:, :, None], lam_init, tl["a_tq"], tl["a_tk"], l)

        ob = _pool(u, w_bd, p_scale, tl["tm_pool"], l)

        kc2, vc2 = _compress(kcm.reshape(B, n16, 16 * LANES), vcm.reshape(B, n16, 16 * LANES),
                             pk2, pv2, wk2, wv2, l)
        ocT = _nsa(qcT, gtT, kc2, vc2.transpose(0, 2, 1), ovT, ksl, vslT, kwi, vwiT, tl["c_tq"], tl["c_tk"])

        x3 = _outproj(x3, oaT, ob, ocT, wo, row(ln1_g), row(ln1_b), alpha, tl["tm_out"], l)
        x3 = _mlp(x3.reshape(B * T, D), wu, wd, row(ln2_g), row(ln2_b), alpha, tl["tm_mlp"], tl["tf_mlp"],
                  l).reshape(B, T, D)
    return x3
```

```python
import functools
import math

import jax
import jax.numpy as jnp
import numpy as np
from jax import lax
from jax.experimental import pallas as pl
from jax.experimental.pallas import tpu as pltpu

A_HEADS = 6
A_QK_DIM = 32
A_V_DIM = 64
A_WIDTH = A_HEADS * A_V_DIM
POOL_WINDOWS = (2, 4, 8, 16)
POOL_GROUP_DIM = 64
POOL_WIDTH = 256
C_HEADS = 6
C_KV_GROUPS = 2
C_HPG = 3
C_HEAD_DIM = 64
C_WIDTH = C_HEADS * C_HEAD_DIM
CMP_BLOCK = 32
CMP_STRIDE = 16
SEL_BLOCK = 64
N_SEL = 16
WINDOW = 512
N_BRANCH = 3
FORCED_SCORE = 1.0e4
N_FORCED = 3
ROPE_THETA = 10000.0
LN_EPS = 1e-5
RMS_EPS = 1e-6

LANES = 128
LOG2E = 1.4426950408889634
NEG_BIG = -1e30
SWEEP_UNROLL = 4
TAKEN = -3.0e38
MASKED = -3e38
VMEM_LIMIT = 56 * 1024 * 1024

BF16 = jnp.bfloat16
F32 = jnp.float32


def _dot(a, b):
    return jnp.dot(a, b, preferred_element_type=F32)


def _params(sem, vmem=VMEM_LIMIT):
    return pltpu.CompilerParams(dimension_semantics=sem, vmem_limit_bytes=vmem)


_IN_SPLITS = (384, 384, 384, 256, 384, 128, 128, 128, 128, 128, 128, 18)
_OFFS = np.concatenate([[0], np.cumsum(_IN_SPLITS)]).astype(np.int64)
A_QSCALE = A_QK_DIM ** -0.5 * LOG2E
C_QSCALE = C_HEAD_DIM ** -0.5 * LOG2E
GATE_ROWS = 16
N_TOK = 384 + 256 + 4 * LANES
N_FEAT = 3 * 384 + 2 * LANES + C_KV_GROUPS * GATE_ROWS


def _prep_w_in(w):
    o = _OFFS
    L, D = w.shape[0], w.shape[1]
    seg = lambda i: w[:, :, o[i]:o[i + 1]]
    diff = lambda a: a.reshape(L, D, A_HEADS // 2, 2, 2, 2, A_QK_DIM // 2).transpose(0, 1, 2, 5, 3, 4, 6).reshape(L, D, -1)
    cq = lambda a: a.reshape(L, D, C_KV_GROUPS, C_HPG, 2, C_HEAD_DIM // 2).transpose(0, 1, 3, 4, 2, 5).reshape(L, D, -1)
    ck = lambda a: a.reshape(L, D, C_KV_GROUPS, 2, C_HEAD_DIM // 2).transpose(0, 1, 3, 2, 4).reshape(L, D, -1)
    w_tok = jnp.concatenate([diff(seg(1)), seg(3), ck(seg(5)), seg(6), ck(seg(7)), ck(seg(9))], axis=2)
    per = C_HPG * N_BRANCH
    gates = jnp.pad(seg(11).reshape(L, D, C_KV_GROUPS, per), ((0, 0), (0, 0), (0, 0), (0, GATE_ROWS - per)))
    w_feat = jnp.concatenate([diff(seg(0)), seg(2), cq(seg(4)), seg(8), seg(10),
                              gates.reshape(L, D, C_KV_GROUPS * GATE_ROWS)], axis=2)
    return w_tok.astype(BF16), jnp.swapaxes(w_feat, 1, 2).astype(BF16)


def _rope_tables(T):
    def tab(dim):
        half = dim // 2
        inv = 1.0 / (ROPE_THETA ** (jnp.arange(0, dim, 2, dtype=F32) / dim))
        ang = jnp.arange(T, dtype=F32)[:, None] * inv[None, :]
        lane = np.arange(LANES)
        j = lane % half
        sign = np.where(lane < 64, -1.0, 1.0).astype(np.float32)
        cos, sin = jnp.cos(ang), jnp.sin(ang)
        return cos[:, j], sin[:, j] * sign[None, :], cos[:, j[:64]].T, sin[:, j[:64]].T
    return tab(A_QK_DIM) + tab(C_HEAD_DIM)


def _inproj_kernel(x_ref, wt_ref, wf_ref, ca_ref, sa_ref, caT_ref, saT_ref, cc_ref, sc_ref, ccT_ref, scT_ref,
                   ka_ref, u_ref, kcm_ref, vcm_ref, ksl_ref, kwi_ref,
                   qa_ref, va_ref, qc_ref, vsl_ref, vwi_ref, gt_ref):
    xb = x_ref[...].astype(BF16)
    tok = _dot(xb, wt_ref[...])
    feat = lax.dot_general(wf_ref[...], xb, (((1,), (1,)), ((), ())), preferred_element_type=F32)

    def rope_tok(p, cos, sin):
        return p * cos + pltpu.roll(p, 64, axis=1) * sin

    def rope_feat(p, cos, sin):
        a, b = p[:64, :], p[64:, :]
        return jnp.concatenate([a * cos - b * sin, b * cos + a * sin], axis=0)

    ca, sa, cc, sc = ca_ref[...], sa_ref[...], cc_ref[...], sc_ref[...]
    caT, saT, ccT, scT = caT_ref[...], saT_ref[...], ccT_ref[...], scT_ref[...]
    for t in range(3):
        r = slice(t * LANES, (t + 1) * LANES)
        ka_ref[:, r] = rope_tok(tok[:, r], ca, sa).astype(BF16)
        qa_ref[r, :] = (rope_feat(feat[r, :], caT, saT) * A_QSCALE).astype(BF16)
        rq = slice(768 + t * LANES, 768 + (t + 1) * LANES)
        qc_ref[r, :] = (rope_feat(feat[rq, :], ccT, scT) * C_QSCALE).astype(BF16)
    u_ref[...] = tok[:, 384:640]
    kcm_ref[...] = rope_tok(tok[:, 640:768], cc, sc).astype(BF16)
    vcm_ref[...] = tok[:, 768:896].astype(BF16)
    ksl_ref[...] = rope_tok(tok[:, 896:1024], cc, sc).astype(BF16)
    kwi_ref[...] = rope_tok(tok[:, 1024:1152], cc, sc).astype(BF16)
    va_ref[...] = feat[384:768, :].astype(BF16)
    vsl_ref[...] = feat[1152:1280, :].astype(BF16)
    vwi_ref[...] = feat[1280:1408, :].astype(BF16)
    gt_ref[...] = jax.nn.sigmoid(feat[1408:, :])


def _layer_spec(a, l):
    return pl.BlockSpec((None,) + a.shape[1:], lambda *_: (l,) + (0,) * (a.ndim - 1))


def _inproj(x3, w_tok, w_feat, tabs, tm, l):
    B, T, D = x3.shape
    tok = lambda n: pl.BlockSpec((None, tm, n), lambda b, i: (b, i, 0))
    feat = lambda n: pl.BlockSpec((None, n, tm), lambda b, i: (b, 0, i))
    full = lambda a: _layer_spec(a, l)
    tab_tok = pl.BlockSpec((tm, LANES), lambda b, i: (i, 0))
    tab_feat = pl.BlockSpec((64, tm), lambda b, i: (0, i))
    tok_outs = [(384, BF16), (256, F32)] + [(LANES, BF16)] * 4
    feat_outs = [(384, BF16)] * 3 + [(LANES, BF16)] * 2 + [(C_KV_GROUPS * GATE_ROWS, F32)]
    return pl.pallas_call(
        _inproj_kernel,
        grid=(B, T // tm),
        in_specs=[tok(D), full(w_tok), full(w_feat)] + [tab_tok, tab_tok, tab_feat, tab_feat] * 2,
        out_specs=[tok(n) for n, _ in tok_outs] + [feat(n) for n, _ in feat_outs],
        out_shape=([jax.ShapeDtypeStruct((B, T, n), dt) for n, dt in tok_outs]
                   + [jax.ShapeDtypeStruct((B, n, T), dt) for n, dt in feat_outs]),
        compiler_params=_params(("parallel", "parallel")),
        name="inproj_rope",
    )(x3, w_tok, w_feat, *tabs)


def _sweep_scratch(tk, nq, dv, n_stats=1):
    stats = [pltpu.VMEM((1, nq), F32), pltpu.VMEM((1, nq), F32), pltpu.VMEM((dv, nq), F32)]
    return [pltpu.VMEM((tk, nq), F32), pltpu.VMEM((tk, nq), F32),
            pltpu.VMEM((1, nq), F32), pltpu.VMEM((1, nq), F32)] + stats * n_stats


class _Sweep:
    def __init__(self, score, qpos, value, bufs, stats):
        self.score, self.qpos, self.value = score, qpos, value
        self.s_refs, self.cm_refs = bufs[0:2], bufs[2:4]
        self.m_ref, self.l_ref, self.acc_ref = stats
        self.tk, self.nq = self.s_refs[0].shape

    def init(self):
        self.m_ref[...] = jnp.full(self.m_ref.shape, NEG_BIG, F32)
        self.l_ref[...] = jnp.zeros(self.l_ref.shape, F32)
        self.acc_ref[...] = jnp.zeros(self.acc_ref.shape, F32)

    def scores(self, j, slot, fn=None):
        s = self.score(j) if fn is None else fn()
        self.s_refs[slot][...] = s
        self.cm_refs[slot][...] = jnp.max(s, axis=0, keepdims=True)

    def softmax(self, slot, j, causal=False):
        s = self.s_refs[slot][...]
        if causal:
            kpos = j * self.tk + lax.broadcasted_iota(jnp.int32, (self.tk, self.nq), 0)
            s = jnp.where(kpos <= self.qpos, s, MASKED)
            cm = jnp.max(s, axis=0, keepdims=True)
        else:
            cm = self.cm_refs[slot][...]
        m = self.m_ref[...]
        m_new = jnp.maximum(m, cm)
        alpha = jnp.exp2(m - m_new)
        p = jnp.exp2(s - m_new)
        self.l_ref[...] = self.l_ref[...] * alpha + jnp.sum(p, axis=0, keepdims=True)
        self.m_ref[...] = m_new
        self.acc_ref[...] = self.acc_ref[...] * alpha + _dot(self.value(j), p.astype(BF16))

    def full_tiles(self, count, par):
        def group(base, tiles):
            for u in range(tiles):
                self.scores(base + u + 1, (par + u + 1) % 2)
                self.softmax((par + u) % 2, base + u)

        def quad(i, c):
            group(SWEEP_UNROLL * i, SWEEP_UNROLL)
            return c

        def pair(i, c):
            group(done + 2 * i, 2)
            return c

        lax.fori_loop(0, count // SWEEP_UNROLL, quad, 0)
        done = (count // SWEEP_UNROLL) * SWEEP_UNROLL
        lax.fori_loop(0, (count - done) // 2, pair, 0)

    def result(self):
        return self.acc_ref[...] / self.l_ref[...]


def _causal_sweep(n_full, score, qpos, value, scratch, first_scores=None):
    sw = _Sweep(score, qpos, value, scratch[0:4], scratch[4:7])
    sw.init()
    sw.scores(0, 0, first_scores)
    sw.full_tiles(n_full - n_full % 2, 0)

    @pl.when(n_full % 2 == 0)
    def _():
        sw.softmax(0, n_full, causal=True)

    @pl.when(n_full % 2 == 1)
    def _():
        sw.scores(n_full, 1)
        sw.softmax(0, n_full - 1)
        sw.softmax(1, n_full, causal=True)

    return sw.result()


def _diff_attn_kernel(lq1_ref, lk1_ref, lq2_ref, lk2_ref, g_ref, q_ref, k_ref, v_ref, o_ref,
                      *scratch, tq, tk, lam_init):
    h = pl.program_id(1)
    i2 = pl.program_id(2)
    hl = h % 2
    nq = 2 * tq
    grp = (lax.broadcasted_iota(jnp.int32, (LANES, tq), 0) % 64) // 16
    lane_q = lax.broadcasted_iota(jnp.int32, (1, nq), 1) % tq

    def value(j):
        return v_ref[:, pl.ds(pl.multiple_of(j * tk, tk), tk)]

    def sweep(blk, stats):
        q = q_ref[:, blk * tq:(blk + 1) * tq]
        zero = jnp.zeros_like(q)
        q12 = jnp.concatenate([jnp.where(grp == 2 * hl, q, zero), jnp.where(grp == 2 * hl + 1, q, zero)], axis=1)
        qpos = (2 * i2 + blk) * tq + lane_q
        score = lambda j: _dot(k_ref[pl.ds(pl.multiple_of(j * tk, tk), tk), :], q12)
        return _Sweep(score, qpos, value, scratch[0:4], stats)

    dotp = lambda a, b: jnp.sum(a[...] * b[...], axis=1, keepdims=True)
    lam = jnp.exp(dotp(lq1_ref, lk1_ref)) - jnp.exp(dotp(lq2_ref, lk2_ref)) + lam_init

    def finish(sw, blk):
        on = sw.result()
        o = on[:, :tq] - lam * on[:, tq:]
        ms = jnp.mean(o * o, axis=0, keepdims=True)
        o = o * lax.rsqrt(ms + RMS_EPS) * g_ref[...] * (1.0 - lam_init)
        o_ref[:, blk * tq:(blk + 1) * tq] = o.astype(o_ref.dtype)

    a, b = sweep(0, scratch[4:7]), sweep(1, scratch[7:10])
    n_a = 2 * i2
    a.init()
    a.scores(0, 0)
    a.full_tiles(n_a, 0)
    b.init()
    b.scores(0, 1)
    a.softmax(0, n_a, causal=True)
    finish(a, 0)
    b.full_tiles(n_a, 1)
    b.scores(n_a + 1, 0)
    b.softmax(1, n_a)
    b.softmax(0, n_a + 1, causal=True)
    finish(b, 1)


def _diff_attn(qaT, ka, vaT, lq1, lk1, lq2, lk2, g_col, lam_init, tq, tk, l):
    B, _, T = qaT.shape
    assert tq == tk and T % (2 * tq) == 0, "query block i ends at key tile i; blocks are taken in pairs"
    nq = 2 * tq
    vec = _layer_spec(lq1, l)
    return pl.pallas_call(
        functools.partial(_diff_attn_kernel, tq=tq, tk=tk, lam_init=lam_init),
        grid=(B, A_HEADS, T // (2 * tq)),
        in_specs=[vec, vec, vec, vec, _layer_spec(g_col, l),
                  pl.BlockSpec((None, LANES, 2 * tq), lambda b, h, i: (b, h // 2, i)),
                  pl.BlockSpec((None, T, LANES), lambda b, h, i: (b, 0, h // 2)),
                  pl.BlockSpec((None, A_V_DIM, T), lambda b, h, i: (b, h, 0))],
        out_specs=pl.BlockSpec((None, A_V_DIM, 2 * tq), lambda b, h, i: (b, h, i)),
        out_shape=jax.ShapeDtypeStruct((B, A_WIDTH, T), BF16),
        scratch_shapes=_sweep_scratch(tk, nq, A_V_DIM, n_stats=2),
        compiler_params=_params(("parallel", "parallel", "arbitrary")),
        name="diff_attn",
    )(lq1, lk1, lq2, lk2, g_col, qaT, ka, vaT)


POOL_HALO = 16


def _pool_mix(halo_ref, cur_ref, w_ref, scale_ref, i, tm):
    cur = cur_ref[...]
    halo = halo_ref[...]
    halo = jnp.where(i == 0, jnp.zeros_like(halo), halo)
    arr = jnp.concatenate([halo, cur], axis=0)
    s2 = arr + pltpu.roll(arr, 1, axis=0)
    s4 = s2 + pltpu.roll(s2, 2, axis=0)
    s8 = s4 + pltpu.roll(s4, 4, axis=0)
    s16 = s8 + pltpu.roll(s8, 8, axis=0)
    sums = [s[POOL_HALO:, :] for s in (s2, s4, s8, s16)]
    lane = lax.broadcasted_iota(jnp.int32, (tm, POOL_WIDTH), 1)
    tpos = i * tm + lax.broadcasted_iota(jnp.int32, (tm, POOL_WIDTH), 0)
    grp = lane // POOL_GROUP_DIM
    win = jnp.where(grp == 0, sums[0], jnp.where(grp == 1, sums[1], jnp.where(grp == 2, sums[2], sums[3])))
    width = jnp.where(grp == 0, 2, jnp.where(grp == 1, 4, jnp.where(grp == 2, 8, 16)))
    cnt = jnp.minimum(tpos + 1, width).astype(F32)
    y = win / cnt - cur
    z = _dot(y.astype(BF16), w_ref[...])
    return (z * scale_ref[...]).astype(BF16)


def _prep_pool(w_pool, pool_scale):
    depth, G, C, _ = w_pool.shape
    eye = jnp.asarray(np.eye(G, dtype=np.float32))
    w_bd = (w_pool[:, :, :, None, :] * eye[None, :, None, :, None]).reshape(depth, G * C, G * C)
    return w_bd.astype(BF16), pool_scale[:, None, :]


def _cmp_kernel(k_ref, v_ref, pk_ref, pv_ref, wk_ref, wv_ref, kc_ref, vc_ref):
    n = k_ref.shape[0]
    rowid = lax.broadcasted_iota(jnp.int32, (n, LANES), 0)

    def compress(x_ref, pe_ref, w_ref):
        x = x_ref[...].astype(F32)
        lo = _dot((x + pe_ref[0:1, :]).astype(BF16), w_ref[0])
        hi = _dot((x + pe_ref[1:2, :]).astype(BF16), w_ref[1])
        out = lo + pltpu.roll(hi, n - 1, axis=0)
        return jnp.where(rowid < n - 1, out, 0.0)

    kc_ref[...] = compress(k_ref, pk_ref, wk_ref).astype(kc_ref.dtype)
    vc_ref[...] = compress(v_ref, pv_ref, wv_ref).astype(vc_ref.dtype)


def _compress(kr, vr, pe_k2, pe_v2, wk2, wv2, l):
    B, n, W = kr.shape
    blk = pl.BlockSpec((None, n, W), lambda b: (b, 0, 0))
    pe = _layer_spec(pe_k2, l)
    wsp = _layer_spec(wk2, l)
    osp = pl.BlockSpec((None, n, LANES), lambda b: (b, 0, 0))
    return pl.pallas_call(
        _cmp_kernel,
        grid=(B,),
        in_specs=[blk, blk, pe, pe, wsp, wsp],
        out_specs=[osp, osp],
        out_shape=[jax.ShapeDtypeStruct((B, n, LANES), BF16)] * 2,
        compiler_params=_params(("parallel",)),
        name="nsa_compress",
    )(kr, vr, pe_k2, pe_v2, wk2, wv2)


def _prep_cmp(pe_k, pe_v, w_ck, w_cv):
    L, G, d, h = w_ck.shape[0], C_KV_GROUPS, C_HEAD_DIM, C_HEAD_DIM // 2
    same = jnp.asarray(np.eye(G, dtype=np.float32))
    shape2 = (L, 2, 16 * LANES, G * d)
    wk = w_ck.reshape(L, CMP_BLOCK, 2, 1, h, 2, 1, h) * same[None, None, None, :, None, None, :, None]
    pk2 = jnp.broadcast_to(pe_k.reshape(L, CMP_BLOCK, 2, 1, h), (L, CMP_BLOCK, 2, G, h))
    wv = w_cv.reshape(L, CMP_BLOCK, 1, d, 1, d) * same[None, None, :, None, :, None]
    pv2 = jnp.broadcast_to(pe_v.reshape(L, CMP_BLOCK, 1, d), (L, CMP_BLOCK, G, d))
    flat = lambda pe: pe.reshape(L, 2, 16 * LANES)
    return flat(pk2), flat(pv2), wk.reshape(shape2).astype(BF16), wv.reshape(shape2).astype(BF16)


def _nsa_kernel(q_ref, gt_ref, kc_ref, vc_ref, ov_ref, wb_ref, ks_ref, vs_ref, kw_ref, vw_ref, o_ref, sel_ref,
                *scratch, tq, tk, T):
    g = pl.program_id(1)
    qi = pl.program_id(2)
    q0 = qi * tq
    n_cmp = kc_ref.shape[0]
    n_blk = T // SEL_BLOCK
    d = C_HEAD_DIM
    tpos = q0 + lax.broadcasted_iota(jnp.int32, (1, tq), 1)

    nq = C_HPG * tq
    tpos3 = q0 + lax.broadcasted_iota(jnp.int32, (1, nq), 1) % tq
    rowg = (lax.broadcasted_iota(jnp.int32, (LANES, tq), 0) % 64) // 32
    qs = []
    for hh in range(C_HPG):
        qt = q_ref[hh * LANES:(hh + 1) * LANES, :]
        qs.append(jnp.where(rowg == g, qt, jnp.zeros_like(qt)))
    q3 = jnp.concatenate(qs, axis=1)

    scratch[0][...] = _dot(ks_ref[0:tk, :], q3)

    band = WINDOW + tq
    w0 = pl.multiple_of(jnp.maximum(q0 - WINDOW, 0), LANES)
    s = _dot(kw_ref[pl.ds(w0, band), :], q3) + jnp.concatenate([wb_ref[...]] * C_HPG, axis=1)
    e = jnp.exp2(s - jnp.max(s, axis=0, keepdims=True))
    o_win = _dot(vw_ref[:, pl.ds(w0, band)], e.astype(BF16)) / jnp.sum(e, axis=0, keepdims=True)

    cend = lax.broadcasted_iota(jnp.int32, (n_cmp, nq), 0) * CMP_STRIDE + (CMP_BLOCK - 1)
    s = jnp.where(cend <= tpos3, _dot(kc_ref[...], q3), MASKED)
    m = jnp.maximum(jnp.max(s, axis=0, keepdims=True), NEG_BIG)
    e = jnp.exp2(s - m)
    l = jnp.sum(e, axis=0, keepdims=True)
    p = e / jnp.where(l > 0.0, l, 1.0)
    o_cmp = _dot(vc_ref[...], p.astype(BF16))
    psum = p[:, :tq] + p[:, tq:2 * tq] + p[:, 2 * tq:]

    p_hi = psum.astype(BF16)
    p_lo = (psum - p_hi.astype(F32)).astype(BF16)
    imp = _dot(ov_ref[...], p_hi) + _dot(ov_ref[...], p_lo)
    jidx = lax.broadcasted_iota(jnp.int32, (n_blk, tq), 0).astype(F32)
    cur = (tpos // SEL_BLOCK).astype(F32)
    forced = (jidx == 0.0) | (jidx == cur) | (jidx == cur - 1.0)
    imp = jnp.where(jidx > cur, -1.0, imp)
    imp = jnp.where(forced, TAKEN, imp)
    rounds = min(N_SEL, n_blk) - N_FORCED

    def store_selection(taken_imp):
        bias = jnp.where(taken_imp == TAKEN, 0.0, MASKED)
        sel_ref[...] = jnp.concatenate([bias] * C_HPG, axis=1)

    quick = imp
    for _ in range(rounds):
        quick = jnp.where(quick == jnp.max(quick, axis=0, keepdims=True), TAKEN, quick)
    store_selection(quick)
    free = (jidx >= 1.0) & (jidx <= cur - 2.0)
    n_taken = jnp.sum(jnp.where(free & (quick == TAKEN), 1.0, 0.0), axis=0, keepdims=True)
    n_expected = jnp.minimum(jnp.maximum(cur - 2.0, 0.0), float(rounds))
    tie_in_top = jnp.max(jnp.abs(n_taken - n_expected)) > 0.0

    @pl.when(tie_in_top)
    def _():
        exact = imp
        for _ in range(rounds):
            mx = jnp.max(exact, axis=0, keepdims=True)
            first = jnp.min(jnp.where(exact == mx, jidx, float(n_blk)), axis=0, keepdims=True)
            exact = jnp.where(jidx == first, TAKEN, exact)
        store_selection(exact)

    bpt = tk // SEL_BLOCK

    def block_bias(j):
        rows = sel_ref[pl.ds(pl.multiple_of(j * bpt, bpt), bpt), :]
        return jnp.concatenate([jnp.broadcast_to(rows[i:i + 1, :], (SEL_BLOCK, nq)) for i in range(bpt)], axis=0)

    def score(j):
        return _dot(ks_ref[pl.ds(pl.multiple_of(j * tk, tk), tk), :], q3) + block_bias(j)

    def first_scores():
        return scratch[0][...] + block_bias(0)

    def value(j):
        return vs_ref[:, pl.ds(pl.multiple_of(j * tk, tk), tk)]

    o_slc = _causal_sweep(q0 // tk, score, tpos3, value, scratch, first_scores)

    gts = gt_ref[...]
    for hh in range(C_HPG):
        r = hh * N_BRANCH
        c = slice(hh * tq, (hh + 1) * tq)
        out = (o_cmp[:, c] * gts[r:r + 1, :] + o_slc[:, c] * gts[r + 1:r + 2, :]
               + o_win[:, c] * gts[r + 2:r + 3, :])
        o_ref[hh * d:(hh + 1) * d, :] = out.astype(o_ref.dtype)


def _window_bias(tq):
    band, n_edge = WINDOW + tq, WINDOW // tq
    q0 = np.arange(n_edge + 1)[:, None, None] * tq
    wpos = np.maximum(q0 - WINDOW, 0) + np.arange(band)[None, :, None]
    tpos = q0 + np.arange(tq)[None, None, :]
    ok = (wpos <= tpos) & (wpos > tpos - WINDOW)
    return jnp.asarray(np.where(ok, 0.0, MASKED).astype(np.float32))


def _nsa(qcT, gtT, kc2, vcT, ovT, ksl, vslT, kwi, vwiT, tq, tk):
    B, _, T = qcT.shape
    n_cmp = kc2.shape[1]
    n_blk = T // SEL_BLOCK
    nq = C_HPG * tq
    assert tk % tq == 0, "the causal key tile must contain the whole query block"
    assert WINDOW % tq == 0 and T >= WINDOW + tq
    wb = _window_bias(tq)
    n_edge = WINDOW // tq
    kv_tok = pl.BlockSpec((None, T, LANES), lambda b, g, i: (b, 0, 0))
    kv_feat = pl.BlockSpec((None, C_HEAD_DIM, T), lambda b, g, i: (b, g, 0))
    return pl.pallas_call(
        functools.partial(_nsa_kernel, tq=tq, tk=tk, T=T),
        grid=(B, C_KV_GROUPS, T // tq),
        in_specs=[pl.BlockSpec((None, C_WIDTH, tq), lambda b, g, i: (b, 0, i)),
                  pl.BlockSpec((None, GATE_ROWS, tq), lambda b, g, i: (b, g, i)),
                  pl.BlockSpec((None, n_cmp, LANES), lambda b, g, i: (b, 0, 0)),
                  pl.BlockSpec((None, C_HEAD_DIM, n_cmp), lambda b, g, i: (b, g, 0)),
                  pl.BlockSpec((n_blk, n_cmp), lambda b, g, i: (0, 0)),
                  pl.BlockSpec((None, WINDOW + tq, tq), lambda b, g, i: (jnp.minimum(i, n_edge), 0, 0)),
                  kv_tok, kv_feat, kv_tok, kv_feat],
        out_specs=pl.BlockSpec((None, C_HPG * C_HEAD_DIM, tq), lambda b, g, i: (b, g, i)),
        out_shape=jax.ShapeDtypeStruct((B, C_WIDTH, T), BF16),
        scratch_shapes=[pltpu.VMEM((n_blk, nq), F32)] + _sweep_scratch(tk, nq, C_HEAD_DIM),
        compiler_params=_params(("parallel", "parallel", "arbitrary")),
        name="nsa_attn",
    )(qcT, gtT, kc2, vcT, ovT, wb, ksl, vslT, kwi, vwiT)


def _overlap_T(T):
    n_cmp = T // CMP_STRIDE
    n_blk = T // SEL_BLOCK
    cs = np.arange(n_cmp)[None, :] * CMP_STRIDE
    bs = np.arange(n_blk)[:, None] * SEL_BLOCK
    ov = (cs < bs + SEL_BLOCK) & (cs + CMP_BLOCK > bs) & (np.arange(n_cmp)[None, :] < n_cmp - 1)
    return jnp.asarray(ov.astype(np.float32), dtype=BF16)


def _layer_norm(y, g, b):
    mu = jnp.mean(y, axis=-1, keepdims=True)
    yc = y - mu
    var = jnp.mean(yc * yc, axis=-1, keepdims=True)
    return yc * lax.rsqrt(var + LN_EPS) * g + b


def _outproj_kernel(x_ref, oa_ref, uh_ref, u_ref, oc_ref, w_ref, wp_ref, ps_ref, g_ref, b_ref, o_ref, *, alpha, tm):
    tn = lambda a, w: lax.dot_general(a, w, (((0,), (0,)), ((), ())), preferred_element_type=F32)
    r1, r2 = A_WIDTH, A_WIDTH + POOL_WIDTH
    ob = _pool_mix(uh_ref, u_ref, wp_ref, ps_ref, pl.program_id(1), tm)
    mix = tn(oa_ref[...], w_ref[:r1, :]) + _dot(ob, w_ref[r1:r2, :]) + tn(oc_ref[...], w_ref[r2:, :])
    o_ref[...] = _layer_norm(alpha * x_ref[...] + mix, g_ref[...], b_ref[...])


def _outproj(x3, oaT, u, ocT, w, w_bd, p_scale, g, b, alpha, tm, l):
    B, T, D = x3.shape
    tok = lambda n: pl.BlockSpec((None, tm, n), lambda bb, i: (bb, i, 0))
    feat = lambda n: pl.BlockSpec((None, n, tm), lambda bb, i: (bb, 0, i))
    halo = pl.BlockSpec((None, POOL_HALO, POOL_WIDTH),
                        lambda bb, i: (bb, jnp.maximum(i * (tm // POOL_HALO) - 1, 0), 0))
    return pl.pallas_call(
        functools.partial(_outproj_kernel, alpha=alpha, tm=tm),
        grid=(B, T // tm),
        in_specs=[tok(D), feat(oaT.shape[1]), halo, tok(POOL_WIDTH), feat(ocT.shape[1]),
                  _layer_spec(w, l), _layer_spec(w_bd, l), _layer_spec(p_scale, l),
                  _layer_spec(g, l), _layer_spec(b, l)],
        out_specs=tok(D),
        out_shape=jax.ShapeDtypeStruct((B, T, D), F32),
        compiler_params=_params(("parallel", "parallel")),
        name="outproj_ln",
    )(x3, oaT, u, u, ocT, w, w_bd, p_scale, g, b)


def _mlp_kernel(x_ref, wu_ref, wd_ref, g_ref, b_ref, o_ref, acc_ref, *, alpha):
    f = pl.program_id(1)

    @pl.when(f == 0)
    def _():
        acc_ref[...] = jnp.zeros_like(acc_ref)

    hid = _dot(x_ref[...].astype(BF16), wu_ref[...])
    hid = jnp.square(jnp.maximum(hid, 0.0))
    acc_ref[...] += _dot(hid.astype(BF16), wd_ref[...])

    @pl.when(f == pl.num_programs(1) - 1)
    def _():
        o_ref[...] = _layer_norm(alpha * x_ref[...] + acc_ref[...], g_ref[...], b_ref[...])


def _mlp(x2, wu, wd, g, b, alpha, tm, tf, l):
    BT, D = x2.shape
    F = wu.shape[2]
    return pl.pallas_call(
        functools.partial(_mlp_kernel, alpha=alpha),
        grid=(BT // tm, F // tf),
        in_specs=[pl.BlockSpec((tm, D), lambda i, f: (i, 0)),
                  pl.BlockSpec((None, D, tf), lambda i, f: (l, 0, f)),
                  pl.BlockSpec((None, tf, D), lambda i, f: (l, f, 0)),
                  _layer_spec(g, l), _layer_spec(b, l)],
        out_specs=pl.BlockSpec((tm, D), lambda i, f: (i, 0)),
        out_shape=jax.ShapeDtypeStruct((BT, D), F32),
        scratch_shapes=[pltpu.VMEM((tm, D), F32)],
        compiler_params=_params(("parallel", "arbitrary")),
        name="mlp_ln",
    )(x2, wu, wd, g, b)


def _tiles(T):
    return dict(
        tm_proj=min(512, T), tm_out=min(512, T), tm_mlp=min(1024, T), tf_mlp=1024,
        a_tq=min(512, T), a_tk=min(512, T), c_tq=min(256, T), c_tk=min(512, T),
    )


def kernel(x, w_in, lam_q1, lam_k1, lam_q2, lam_k2, subln_g, w_pool, pool_scale, cmp_pe_k, cmp_pe_v,
           w_cmp_k, w_cmp_v, w_out, ln1_g, ln1_b, w_up, w_down, ln2_g, ln2_b):
    B, T, D = x.shape
    depth = w_in.shape[0]
    alpha = (2 * depth) ** 0.25
    tl = _tiles(T)
    tabs = _rope_tables(T)
    ovT = _overlap_T(T)
    w_tok, w_feat = _prep_w_in(w_in)
    w_bd, p_scale = _prep_pool(w_pool, pool_scale)
    pk2, pv2, wk2, wv2 = _prep_cmp(cmp_pe_k, cmp_pe_v, w_cmp_k, w_cmp_v)
    wo, wu, wd = w_out.astype(BF16), w_up.astype(BF16), w_down.astype(BF16)
    row = lambda a: a[:, None, :]
    lams = [row(a) for a in (lam_q1, lam_k1, lam_q2, lam_k2)]
    n16 = T // CMP_STRIDE
    x3 = x
    for l in range(depth):
        lam_init = 0.8 - 0.6 * math.exp(-0.3 * l)
        (ka, u, kcm, vcm, ksl, kwi, qaT, vaT, qcT, vslT, vwiT, gtT) = _inproj(x3, w_tok, w_feat, tabs, tl["tm_proj"], l)

        oaT = _diff_attn(qaT, ka, vaT, *lams, subln_g[:, :, None], lam_init, tl["a_tq"], tl["a_tk"], l)

        kc2, vc2 = _compress(kcm.reshape(B, n16, 16 * LANES), vcm.reshape(B, n16, 16 * LANES),
                             pk2, pv2, wk2, wv2, l)
        ocT = _nsa(qcT, gtT, kc2, vc2.transpose(0, 2, 1), ovT, ksl, vslT, kwi, vwiT, tl["c_tq"], tl["c_tk"])

        x3 = _outproj(x3, oaT, u, ocT, wo, w_bd, p_scale, row(ln1_g), row(ln1_b), alpha, tl["tm_out"], l)
        x3 = _mlp(x3.reshape(B * T, D), wu, wd, row(ln2_g), row(ln2_b), alpha, tl["tm_mlp"], tl["tf_mlp"],
                  l).reshape(B, T, D)
    return x3
```

```python
import functools
import math

import jax
import jax.numpy as jnp
import numpy as np
from jax import lax
from jax.experimental import pallas as pl
from jax.experimental.pallas import tpu as pltpu

A_HEADS = 6
A_QK_DIM = 32
A_V_DIM = 64
A_WIDTH = A_HEADS * A_V_DIM
POOL_WINDOWS = (2, 4, 8, 16)
POOL_GROUP_DIM = 64
POOL_WIDTH = 256
C_HEADS = 6
C_KV_GROUPS = 2
C_HPG = 3
C_HEAD_DIM = 64
C_WIDTH = C_HEADS * C_HEAD_DIM
CMP_BLOCK = 32
CMP_STRIDE = 16
SEL_BLOCK = 64
N_SEL = 16
WINDOW = 512
N_BRANCH = 3
FORCED_SCORE = 1.0e4
N_FORCED = 3
ROPE_THETA = 10000.0
LN_EPS = 1e-5
RMS_EPS = 1e-6

LANES = 128
LOG2E = 1.4426950408889634
NEG_BIG = -1e30
SWEEP_UNROLL = 4
TAKEN = -3.0e38
MASKED = -3e38
VMEM_LIMIT = 56 * 1024 * 1024

BF16 = jnp.bfloat16
F32 = jnp.float32


def _dot(a, b):
    return jnp.dot(a, b, preferred_element_type=F32)


def _params(sem, vmem=VMEM_LIMIT):
    return pltpu.CompilerParams(dimension_semantics=sem, vmem_limit_bytes=vmem)


_IN_SPLITS = (384, 384, 384, 256, 384, 128, 128, 128, 128, 128, 128, 18)
_OFFS = np.concatenate([[0], np.cumsum(_IN_SPLITS)]).astype(np.int64)
A_QSCALE = A_QK_DIM ** -0.5 * LOG2E
C_QSCALE = C_HEAD_DIM ** -0.5 * LOG2E
GATE_ROWS = 16
N_TOK = 384 + 256 + 4 * LANES
N_FEAT = 3 * 384 + 2 * LANES + C_KV_GROUPS * GATE_ROWS


def _prep_w_in(w):
    o = _OFFS
    L, D = w.shape[0], w.shape[1]
    seg = lambda i: w[:, :, o[i]:o[i + 1]]
    diff = lambda a: a.reshape(L, D, A_HEADS // 2, 2, 2, 2, A_QK_DIM // 2).transpose(0, 1, 2, 5, 3, 4, 6).reshape(L, D, -1)
    cq = lambda a: a.reshape(L, D, C_KV_GROUPS, C_HPG, 2, C_HEAD_DIM // 2).transpose(0, 1, 3, 4, 2, 5).reshape(L, D, -1)
    ck = lambda a: a.reshape(L, D, C_KV_GROUPS, 2, C_HEAD_DIM // 2).transpose(0, 1, 3, 2, 4).reshape(L, D, -1)
    w_tok = jnp.concatenate([diff(seg(1)), seg(3), ck(seg(5)), seg(6), ck(seg(7)), ck(seg(9))], axis=2)
    per = C_HPG * N_BRANCH
    gates = jnp.pad(seg(11).reshape(L, D, C_KV_GROUPS, per), ((0, 0), (0, 0), (0, 0), (0, GATE_ROWS - per)))
    w_feat = jnp.concatenate([diff(seg(0)), seg(2), cq(seg(4)), seg(8), seg(10),
                              gates.reshape(L, D, C_KV_GROUPS * GATE_ROWS)], axis=2)
    return w_tok.astype(BF16), jnp.swapaxes(w_feat, 1, 2).astype(BF16)


def _rope_tables(T):
    def tab(dim):
        half = dim // 2
        inv = 1.0 / (ROPE_THETA ** (jnp.arange(0, dim, 2, dtype=F32) / dim))
        ang = jnp.arange(T, dtype=F32)[:, None] * inv[None, :]
        lane = np.arange(LANES)
        j = lane % half
        sign = np.where(lane < 64, -1.0, 1.0).astype(np.float32)
        cos, sin = jnp.cos(ang), jnp.sin(ang)
        return cos[:, j], sin[:, j] * sign[None, :], cos[:, j[:64]].T, sin[:, j[:64]].T
    return tab(A_QK_DIM) + tab(C_HEAD_DIM)


def _inproj_kernel(x_ref, wt_ref, wf_ref, ca_ref, sa_ref, caT_ref, saT_ref, cc_ref, sc_ref, ccT_ref, scT_ref,
                   ka_ref, u_ref, kcm_ref, vcm_ref, ksl_ref, kwi_ref,
                   qa_ref, va_ref, qc_ref, vsl_ref, vwi_ref, gt_ref):
    xb = x_ref[...].astype(BF16)
    tok = _dot(xb, wt_ref[...])
    feat = lax.dot_general(wf_ref[...], xb, (((1,), (1,)), ((), ())), preferred_element_type=F32)

    def rope_tok(p, cos, sin):
        return p * cos + pltpu.roll(p, 64, axis=1) * sin

    def rope_feat(p, cos, sin):
        a, b = p[:64, :], p[64:, :]
        return jnp.concatenate([a * cos - b * sin, b * cos + a * sin], axis=0)

    ca, sa, cc, sc = ca_ref[...], sa_ref[...], cc_ref[...], sc_ref[...]
    caT, saT, ccT, scT = caT_ref[...], saT_ref[...], ccT_ref[...], scT_ref[...]
    for t in range(3):
        r = slice(t * LANES, (t + 1) * LANES)
        ka_ref[:, r] = rope_tok(tok[:, r], ca, sa).astype(BF16)
        qa_ref[r, :] = (rope_feat(feat[r, :], caT, saT) * A_QSCALE).astype(BF16)
        rq = slice(768 + t * LANES, 768 + (t + 1) * LANES)
        qc_ref[r, :] = (rope_feat(feat[rq, :], ccT, scT) * C_QSCALE).astype(BF16)
    u_ref[...] = tok[:, 384:640]
    kcm_ref[...] = rope_tok(tok[:, 640:768], cc, sc).astype(BF16)
    vcm_ref[...] = tok[:, 768:896].astype(BF16)
    ksl_ref[...] = rope_tok(tok[:, 896:1024], cc, sc).astype(BF16)
    kwi_ref[...] = rope_tok(tok[:, 1024:1152], cc, sc).astype(BF16)
    va_ref[...] = feat[384:768, :].astype(BF16)
    vsl_ref[...] = feat[1152:1280, :].astype(BF16)
    vwi_ref[...] = feat[1280:1408, :].astype(BF16)
    gt_ref[...] = jax.nn.sigmoid(feat[1408:, :])


def _layer_spec(a, l):
    return pl.BlockSpec((None,) + a.shape[1:], lambda *_: (l,) + (0,) * (a.ndim - 1))


def _inproj(x3, w_tok, w_feat, tabs, tm, l):
    B, T, D = x3.shape
    tok = lambda n: pl.BlockSpec((None, tm, n), lambda b, i: (b, i, 0))
    feat = lambda n: pl.BlockSpec((None, n, tm), lambda b, i: (b, 0, i))
    full = lambda a: _layer_spec(a, l)
    tab_tok = pl.BlockSpec((tm, LANES), lambda b, i: (i, 0))
    tab_feat = pl.BlockSpec((64, tm), lambda b, i: (0, i))
    tok_outs = [(384, BF16), (256, F32)] + [(LANES, BF16)] * 4
    feat_outs = [(384, BF16)] * 3 + [(LANES, BF16)] * 2 + [(C_KV_GROUPS * GATE_ROWS, F32)]
    return pl.pallas_call(
        _inproj_kernel,
        grid=(B, T // tm),
        in_specs=[tok(D), full(w_tok), full(w_feat)] + [tab_tok, tab_tok, tab_feat, tab_feat] * 2,
        out_specs=[tok(n) for n, _ in tok_outs] + [feat(n) for n, _ in feat_outs],
        out_shape=([jax.ShapeDtypeStruct((B, T, n), dt) for n, dt in tok_outs]
                   + [jax.ShapeDtypeStruct((B, n, T), dt) for n, dt in feat_outs]),
        compiler_params=_params(("parallel", "parallel")),
        name="inproj_rope",
    )(x3, w_tok, w_feat, *tabs)


ONES_ROWS = 16


def _sweep_scratch(tk, nq, dv, n_stats=1):
    stats = [pltpu.VMEM((1, nq), F32), pltpu.VMEM((dv + ONES_ROWS, nq), F32)]
    return [pltpu.VMEM((tk, nq), BF16), pltpu.VMEM((tk, nq), BF16),
            pltpu.VMEM((1, nq), F32), pltpu.VMEM((1, nq), F32)] + stats * n_stats


class _Sweep:
    def __init__(self, score, qpos, value, bufs, stats):
        self.score, self.qpos, self.value = score, qpos, value
        self.s_refs, self.cm_refs = bufs[0:2], bufs[2:4]
        self.m_ref, self.acc_ref = stats
        self.tk, self.nq = self.s_refs[0].shape
        self.dv = self.acc_ref.shape[0] - ONES_ROWS

    def init(self):
        self.m_ref[...] = jnp.full(self.m_ref.shape, NEG_BIG, F32)
        self.acc_ref[...] = jnp.zeros(self.acc_ref.shape, F32)

    def scores(self, j, slot, fn=None):
        s = self.score(j) if fn is None else fn()
        self.s_refs[slot][...] = s.astype(BF16)
        self.cm_refs[slot][...] = jnp.max(s, axis=0, keepdims=True)

    def softmax(self, slot, j, causal=False):
        s = self.s_refs[slot][...]
        if causal:
            kpos = j * self.tk + lax.broadcasted_iota(jnp.int32, (self.tk, self.nq), 0)
            s = jnp.where(kpos <= self.qpos, s, jnp.asarray(MASKED, s.dtype))
            cm = jnp.max(s, axis=0, keepdims=True).astype(F32)
        else:
            cm = self.cm_refs[slot][...]
        m = self.m_ref[...]
        m_new = jnp.maximum(m, cm)
        alpha = jnp.exp2(m - m_new)
        p = jnp.exp2(s - m_new.astype(s.dtype))
        self.m_ref[...] = m_new
        v = self.value(j)
        v1 = jnp.concatenate([v, jnp.ones((ONES_ROWS, v.shape[1]), v.dtype)], axis=0)
        self.acc_ref[...] = self.acc_ref[...] * alpha + _dot(v1, p)

    def full_tiles(self, count, par):
        def group(base, tiles):
            for u in range(tiles):
                self.scores(base + u + 1, (par + u + 1) % 2)
                self.softmax((par + u) % 2, base + u)

        def quad(i, c):
            group(SWEEP_UNROLL * i, SWEEP_UNROLL)
            return c

        def pair(i, c):
            group(done + 2 * i, 2)
            return c

        lax.fori_loop(0, count // SWEEP_UNROLL, quad, 0)
        done = (count // SWEEP_UNROLL) * SWEEP_UNROLL
        lax.fori_loop(0, (count - done) // 2, pair, 0)

    def result(self):
        return self.acc_ref[:self.dv, :] / self.acc_ref[self.dv:self.dv + 1, :]


def _causal_sweep(n_full, score, qpos, value, scratch, first_scores=None):
    sw = _Sweep(score, qpos, value, scratch[0:4], scratch[4:6])
    sw.init()
    sw.scores(0, 0, first_scores)
    sw.full_tiles(n_full - n_full % 2, 0)

    @pl.when(n_full % 2 == 0)
    def _():
        sw.softmax(0, n_full, causal=True)

    @pl.when(n_full % 2 == 1)
    def _():
        sw.scores(n_full, 1)
        sw.softmax(0, n_full - 1)
        sw.softmax(1, n_full, causal=True)

    return sw.result()


def _diff_attn_kernel(lq1_ref, lk1_ref, lq2_ref, lk2_ref, g_ref, q_ref, k_ref, v_ref, o_ref,
                      *scratch, tq, tk, lam_init):
    h = pl.program_id(1)
    i2 = pl.program_id(2)
    hl = h % 2
    nq = 2 * tq
    grp = (lax.broadcasted_iota(jnp.int32, (LANES, tq), 0) % 64) // 16
    lane_q = lax.broadcasted_iota(jnp.int32, (1, nq), 1) % tq

    def value(j):
        return v_ref[:, pl.ds(pl.multiple_of(j * tk, tk), tk)]

    def sweep(blk, stats):
        q = q_ref[:, blk * tq:(blk + 1) * tq]
        zero = jnp.zeros_like(q)
        q12 = jnp.concatenate([jnp.where(grp == 2 * hl, q, zero), jnp.where(grp == 2 * hl + 1, q, zero)], axis=1)
        qpos = (2 * i2 + blk) * tq + lane_q
        score = lambda j: _dot(k_ref[pl.ds(pl.multiple_of(j * tk, tk), tk), :], q12)
        return _Sweep(score, qpos, value, scratch[0:4], stats)

    dotp = lambda a, b: jnp.sum(a[...] * b[...], axis=1, keepdims=True)
    lam = jnp.exp(dotp(lq1_ref, lk1_ref)) - jnp.exp(dotp(lq2_ref, lk2_ref)) + lam_init

    def finish(sw, blk):
        on = sw.result()
        o = on[:, :tq] - lam * on[:, tq:]
        ms = jnp.mean(o * o, axis=0, keepdims=True)
        o = o * lax.rsqrt(ms + RMS_EPS) * g_ref[...] * (1.0 - lam_init)
        o_ref[:, blk * tq:(blk + 1) * tq] = o.astype(o_ref.dtype)

    a, b = sweep(0, scratch[4:6]), sweep(1, scratch[6:8])
    n_a = 2 * i2
    a.init()
    a.scores(0, 0)
    a.full_tiles(n_a, 0)
    b.init()
    b.scores(0, 1)
    a.softmax(0, n_a, causal=True)
    finish(a, 0)
    b.full_tiles(n_a, 1)
    b.scores(n_a + 1, 0)
    b.softmax(1, n_a)
    b.softmax(0, n_a + 1, causal=True)
    finish(b, 1)


def _diff_attn(qaT, ka, vaT, lq1, lk1, lq2, lk2, g_col, lam_init, tq, tk, l):
    B, _, T = qaT.shape
    assert tq == tk and T % (2 * tq) == 0, "query block i ends at key tile i; blocks are taken in pairs"
    nq = 2 * tq
    vec = _layer_spec(lq1, l)
    return pl.pallas_call(
        functools.partial(_diff_attn_kernel, tq=tq, tk=tk, lam_init=lam_init),
        grid=(B, A_HEADS, T // (2 * tq)),
        in_specs=[vec, vec, vec, vec, _layer_spec(g_col, l),
                  pl.BlockSpec((None, LANES, 2 * tq), lambda b, h, i: (b, h // 2, i)),
                  pl.BlockSpec((None, T, LANES), lambda b, h, i: (b, 0, h // 2)),
                  pl.BlockSpec((None, A_V_DIM, T), lambda b, h, i: (b, h, 0))],
        out_specs=pl.BlockSpec((None, A_V_DIM, 2 * tq), lambda b, h, i: (b, h, i)),
        out_shape=jax.ShapeDtypeStruct((B, A_WIDTH, T), BF16),
        scratch_shapes=_sweep_scratch(tk, nq, A_V_DIM, n_stats=2),
        compiler_params=_params(("parallel", "parallel", "arbitrary")),
        name="diff_attn",
    )(lq1, lk1, lq2, lk2, g_col, qaT, ka, vaT)


POOL_HALO = 16


def _pool_mix(halo_ref, cur_ref, w_ref, scale_ref, i, tm):
    cur = cur_ref[...]
    halo = halo_ref[...]
    halo = jnp.where(i == 0, jnp.zeros_like(halo), halo)
    arr = jnp.concatenate([halo, cur], axis=0)
    s2 = arr + pltpu.roll(arr, 1, axis=0)
    s4 = s2 + pltpu.roll(s2, 2, axis=0)
    s8 = s4 + pltpu.roll(s4, 4, axis=0)
    s16 = s8 + pltpu.roll(s8, 8, axis=0)
    sums = [s[POOL_HALO:, :] for s in (s2, s4, s8, s16)]
    lane = lax.broadcasted_iota(jnp.int32, (tm, POOL_WIDTH), 1)
    tpos = i * tm + lax.broadcasted_iota(jnp.int32, (tm, POOL_WIDTH), 0)
    grp = lane // POOL_GROUP_DIM
    win = jnp.where(grp == 0, sums[0], jnp.where(grp == 1, sums[1], jnp.where(grp == 2, sums[2], sums[3])))
    width = jnp.where(grp == 0, 2, jnp.where(grp == 1, 4, jnp.where(grp == 2, 8, 16)))
    cnt = jnp.minimum(tpos + 1, width).astype(F32)
    y = win / cnt - cur
    z = _dot(y.astype(BF16), w_ref[...])
    return (z * scale_ref[...]).astype(BF16)


def _prep_pool(w_pool, pool_scale):
    depth, G, C, _ = w_pool.shape
    eye = jnp.asarray(np.eye(G, dtype=np.float32))
    w_bd = (w_pool[:, :, :, None, :] * eye[None, :, None, :, None]).reshape(depth, G * C, G * C)
    return w_bd.astype(BF16), pool_scale[:, None, :]


def _cmp_kernel(k_ref, v_ref, pk_ref, pv_ref, wk_ref, wv_ref, kc_ref, vc_ref):
    n = k_ref.shape[0]
    rowid = lax.broadcasted_iota(jnp.int32, (n, LANES), 0)

    def compress(x_ref, pe_ref, w_ref):
        x = x_ref[...].astype(F32)
        lo = _dot((x + pe_ref[0:1, :]).astype(BF16), w_ref[0])
        hi = _dot((x + pe_ref[1:2, :]).astype(BF16), w_ref[1])
        out = lo + pltpu.roll(hi, n - 1, axis=0)
        return jnp.where(rowid < n - 1, out, 0.0)

    kc_ref[...] = compress(k_ref, pk_ref, wk_ref).astype(kc_ref.dtype)
    vc_ref[...] = compress(v_ref, pv_ref, wv_ref).astype(vc_ref.dtype)


def _compress(kr, vr, pe_k2, pe_v2, wk2, wv2, l):
    B, n, W = kr.shape
    blk = pl.BlockSpec((None, n, W), lambda b: (b, 0, 0))
    pe = _layer_spec(pe_k2, l)
    wsp = _layer_spec(wk2, l)
    osp = pl.BlockSpec((None, n, LANES), lambda b: (b, 0, 0))
    return pl.pallas_call(
        _cmp_kernel,
        grid=(B,),
        in_specs=[blk, blk, pe, pe, wsp, wsp],
        out_specs=[osp, osp],
        out_shape=[jax.ShapeDtypeStruct((B, n, LANES), BF16)] * 2,
        compiler_params=_params(("parallel",)),
        name="nsa_compress",
    )(kr, vr, pe_k2, pe_v2, wk2, wv2)


def _prep_cmp(pe_k, pe_v, w_ck, w_cv):
    L, G, d, h = w_ck.shape[0], C_KV_GROUPS, C_HEAD_DIM, C_HEAD_DIM // 2
    same = jnp.asarray(np.eye(G, dtype=np.float32))
    shape2 = (L, 2, 16 * LANES, G * d)
    wk = w_ck.reshape(L, CMP_BLOCK, 2, 1, h, 2, 1, h) * same[None, None, None, :, None, None, :, None]
    pk2 = jnp.broadcast_to(pe_k.reshape(L, CMP_BLOCK, 2, 1, h), (L, CMP_BLOCK, 2, G, h))
    wv = w_cv.reshape(L, CMP_BLOCK, 1, d, 1, d) * same[None, None, :, None, :, None]
    pv2 = jnp.broadcast_to(pe_v.reshape(L, CMP_BLOCK, 1, d), (L, CMP_BLOCK, G, d))
    flat = lambda pe: pe.reshape(L, 2, 16 * LANES)
    return flat(pk2), flat(pv2), wk.reshape(shape2).astype(BF16), wv.reshape(shape2).astype(BF16)


def _nsa_kernel(q_ref, gt_ref, kc_ref, vc_ref, ov_ref, wb_ref, ks_ref, vs_ref, kw_ref, vw_ref, o_ref, sel_ref,
                *scratch, tq, tk, T):
    g = pl.program_id(1)
    qi = pl.program_id(2)
    q0 = qi * tq
    n_cmp = kc_ref.shape[0]
    n_blk = T // SEL_BLOCK
    d = C_HEAD_DIM
    tpos = q0 + lax.broadcasted_iota(jnp.int32, (1, tq), 1)

    nq = C_HPG * tq
    tpos3 = q0 + lax.broadcasted_iota(jnp.int32, (1, nq), 1) % tq
    rowg = (lax.broadcasted_iota(jnp.int32, (LANES, tq), 0) % 64) // 32
    qs = []
    for hh in range(C_HPG):
        qt = q_ref[hh * LANES:(hh + 1) * LANES, :]
        qs.append(jnp.where(rowg == g, qt, jnp.zeros_like(qt)))
    q3 = jnp.concatenate(qs, axis=1)

    raw0_ref = scratch[-1]
    raw0_ref[...] = _dot(ks_ref[0:tk, :], q3)

    band = WINDOW + tq
    w0 = pl.multiple_of(jnp.maximum(q0 - WINDOW, 0), LANES)
    s = _dot(kw_ref[pl.ds(w0, band), :], q3) + jnp.concatenate([wb_ref[...]] * C_HPG, axis=1)
    e = jnp.exp2(s - jnp.max(s, axis=0, keepdims=True))
    o_win = _dot(vw_ref[:, pl.ds(w0, band)], e.astype(BF16)) / jnp.sum(e, axis=0, keepdims=True)

    cend = lax.broadcasted_iota(jnp.int32, (n_cmp, nq), 0) * CMP_STRIDE + (CMP_BLOCK - 1)
    s = jnp.where(cend <= tpos3, _dot(kc_ref[...], q3), MASKED)
    m = jnp.maximum(jnp.max(s, axis=0, keepdims=True), NEG_BIG)
    e = jnp.exp2(s - m)
    l = jnp.sum(e, axis=0, keepdims=True)
    p = e / jnp.where(l > 0.0, l, 1.0)
    o_cmp = _dot(vc_ref[...], p.astype(BF16))
    psum = p[:, :tq] + p[:, tq:2 * tq] + p[:, 2 * tq:]

    p_hi = psum.astype(BF16)
    p_lo = (psum - p_hi.astype(F32)).astype(BF16)
    imp = _dot(ov_ref[...], p_hi) + _dot(ov_ref[...], p_lo)
    jidx = lax.broadcasted_iota(jnp.int32, (n_blk, tq), 0).astype(F32)
    cur = (tpos // SEL_BLOCK).astype(F32)
    forced = (jidx == 0.0) | (jidx == cur) | (jidx == cur - 1.0)
    imp = jnp.where(jidx > cur, -1.0, imp)
    imp = jnp.where(forced, TAKEN, imp)
    rounds = min(N_SEL, n_blk) - N_FORCED

    def store_selection(taken_imp):
        bias = jnp.where(taken_imp == TAKEN, 0.0, MASKED)
        sel_ref[...] = jnp.concatenate([bias] * C_HPG, axis=1)

    quick = imp
    for _ in range(rounds):
        quick = jnp.where(quick == jnp.max(quick, axis=0, keepdims=True), TAKEN, quick)
    store_selection(quick)
    free = (jidx >= 1.0) & (jidx <= cur - 2.0)
    n_taken = jnp.sum(jnp.where(free & (quick == TAKEN), 1.0, 0.0), axis=0, keepdims=True)
    n_expected = jnp.minimum(jnp.maximum(cur - 2.0, 0.0), float(rounds))
    tie_in_top = jnp.max(jnp.abs(n_taken - n_expected)) > 0.0

    @pl.when(tie_in_top)
    def _():
        exact = imp
        for _ in range(rounds):
            mx = jnp.max(exact, axis=0, keepdims=True)
            first = jnp.min(jnp.where(exact == mx, jidx, float(n_blk)), axis=0, keepdims=True)
            exact = jnp.where(jidx == first, TAKEN, exact)
        store_selection(exact)

    bpt = tk // SEL_BLOCK

    def block_bias(j):
        rows = sel_ref[pl.ds(pl.multiple_of(j * bpt, bpt), bpt), :]
        return jnp.concatenate([jnp.broadcast_to(rows[i:i + 1, :], (SEL_BLOCK, nq)) for i in range(bpt)], axis=0)

    def score(j):
        return _dot(ks_ref[pl.ds(pl.multiple_of(j * tk, tk), tk), :], q3) + block_bias(j)

    def first_scores():
        return raw0_ref[...] + block_bias(0)

    def value(j):
        return vs_ref[:, pl.ds(pl.multiple_of(j * tk, tk), tk)]

    o_slc = _causal_sweep(q0 // tk, score, tpos3, value, scratch, first_scores)

    gts = gt_ref[...]
    for hh in range(C_HPG):
        r = hh * N_BRANCH
        c = slice(hh * tq, (hh + 1) * tq)
        out = (o_cmp[:, c] * gts[r:r + 1, :] + o_slc[:, c] * gts[r + 1:r + 2, :]
               + o_win[:, c] * gts[r + 2:r + 3, :])
        o_ref[hh * d:(hh + 1) * d, :] = out.astype(o_ref.dtype)


def _window_bias(tq):
    band, n_edge = WINDOW + tq, WINDOW // tq
    q0 = np.arange(n_edge + 1)[:, None, None] * tq
    wpos = np.maximum(q0 - WINDOW, 0) + np.arange(band)[None, :, None]
    tpos = q0 + np.arange(tq)[None, None, :]
    ok = (wpos <= tpos) & (wpos > tpos - WINDOW)
    return jnp.asarray(np.where(ok, 0.0, MASKED).astype(np.float32))


def _nsa(qcT, gtT, kc2, vcT, ovT, ksl, vslT, kwi, vwiT, tq, tk):
    B, _, T = qcT.shape
    n_cmp = kc2.shape[1]
    n_blk = T // SEL_BLOCK
    nq = C_HPG * tq
    assert tk % tq == 0, "the causal key tile must contain the whole query block"
    assert WINDOW % tq == 0 and T >= WINDOW + tq
    wb = _window_bias(tq)
    n_edge = WINDOW // tq
    kv_tok = pl.BlockSpec((None, T, LANES), lambda b, g, i: (b, 0, 0))
    kv_feat = pl.BlockSpec((None, C_HEAD_DIM, T), lambda b, g, i: (b, g, 0))
    return pl.pallas_call(
        functools.partial(_nsa_kernel, tq=tq, tk=tk, T=T),
        grid=(B, C_KV_GROUPS, T // tq),
        in_specs=[pl.BlockSpec((None, C_WIDTH, tq), lambda b, g, i: (b, 0, i)),
                  pl.BlockSpec((None, GATE_ROWS, tq), lambda b, g, i: (b, g, i)),
                  pl.BlockSpec((None, n_cmp, LANES), lambda b, g, i: (b, 0, 0)),
                  pl.BlockSpec((None, C_HEAD_DIM, n_cmp), lambda b, g, i: (b, g, 0)),
                  pl.BlockSpec((n_blk, n_cmp), lambda b, g, i: (0, 0)),
                  pl.BlockSpec((None, WINDOW + tq, tq), lambda b, g, i: (jnp.minimum(i, n_edge), 0, 0)),
                  kv_tok, kv_feat, kv_tok, kv_feat],
        out_specs=pl.BlockSpec((None, C_HPG * C_HEAD_DIM, tq), lambda b, g, i: (b, g, i)),
        out_shape=jax.ShapeDtypeStruct((B, C_WIDTH, T), BF16),
        scratch_shapes=([pltpu.VMEM((n_blk, nq), F32)] + _sweep_scratch(tk, nq, C_HEAD_DIM)
                        + [pltpu.VMEM((tk, nq), F32)]),
        compiler_params=_params(("parallel", "parallel", "arbitrary")),
        name="nsa_attn",
    )(qcT, gtT, kc2, vcT, ovT, wb, ksl, vslT, kwi, vwiT)


def _overlap_T(T):
    n_cmp = T // CMP_STRIDE
    n_blk = T // SEL_BLOCK
    cs = np.arange(n_cmp)[None, :] * CMP_STRIDE
    bs = np.arange(n_blk)[:, None] * SEL_BLOCK
    ov = (cs < bs + SEL_BLOCK) & (cs + CMP_BLOCK > bs) & (np.arange(n_cmp)[None, :] < n_cmp - 1)
    return jnp.asarray(ov.astype(np.float32), dtype=BF16)


def _layer_norm(y, g, b):
    mu = jnp.mean(y, axis=-1, keepdims=True)
    yc = y - mu
    var = jnp.mean(yc * yc, axis=-1, keepdims=True)
    return yc * lax.rsqrt(var + LN_EPS) * g + b


def _outproj_kernel(x_ref, oa_ref, uh_ref, u_ref, oc_ref, w_ref, wp_ref, ps_ref, g_ref, b_ref, o_ref, *, alpha, tm):
    tn = lambda a, w: lax.dot_general(a, w, (((0,), (0,)), ((), ())), preferred_element_type=F32)
    r1, r2 = A_WIDTH, A_WIDTH + POOL_WIDTH
    ob = _pool_mix(uh_ref, u_ref, wp_ref, ps_ref, pl.program_id(1), tm)
    mix = tn(oa_ref[...], w_ref[:r1, :]) + _dot(ob, w_ref[r1:r2, :]) + tn(oc_ref[...], w_ref[r2:, :])
    o_ref[...] = _layer_norm(alpha * x_ref[...] + mix, g_ref[...], b_ref[...])


def _outproj(x3, oaT, u, ocT, w, w_bd, p_scale, g, b, alpha, tm, l):
    B, T, D = x3.shape
    tok = lambda n: pl.BlockSpec((None, tm, n), lambda bb, i: (bb, i, 0))
    feat = lambda n: pl.BlockSpec((None, n, tm), lambda bb, i: (bb, 0, i))
    halo = pl.BlockSpec((None, POOL_HALO, POOL_WIDTH),
                        lambda bb, i: (bb, jnp.maximum(i * (tm // POOL_HALO) - 1, 0), 0))
    return pl.pallas_call(
        functools.partial(_outproj_kernel, alpha=alpha, tm=tm),
        grid=(B, T // tm),
        in_specs=[tok(D), feat(oaT.shape[1]), halo, tok(POOL_WIDTH), feat(ocT.shape[1]),
                  _layer_spec(w, l), _layer_spec(w_bd, l), _layer_spec(p_scale, l),
                  _layer_spec(g, l), _layer_spec(b, l)],
        out_specs=tok(D),
        out_shape=jax.ShapeDtypeStruct((B, T, D), F32),
        compiler_params=_params(("parallel", "parallel")),
        name="outproj_ln",
    )(x3, oaT, u, u, ocT, w, w_bd, p_scale, g, b)


def _mlp_kernel(x_ref, wu_ref, wd_ref, g_ref, b_ref, o_ref, acc_ref, *, alpha):
    f = pl.program_id(1)

    @pl.when(f == 0)
    def _():
        acc_ref[...] = jnp.zeros_like(acc_ref)

    hid = _dot(x_ref[...].astype(BF16), wu_ref[...])
    hid = jnp.square(jnp.maximum(hid, 0.0))
    acc_ref[...] += _dot(hid.astype(BF16), wd_ref[...])

    @pl.when(f == pl.num_programs(1) - 1)
    def _():
        o_ref[...] = _layer_norm(alpha * x_ref[...] + acc_ref[...], g_ref[...], b_ref[...])


def _mlp(x2, wu, wd, g, b, alpha, tm, tf, l):
    BT, D = x2.shape
    F = wu.shape[2]
    return pl.pallas_call(
        functools.partial(_mlp_kernel, alpha=alpha),
        grid=(BT // tm, F // tf),
        in_specs=[pl.BlockSpec((tm, D), lambda i, f: (i, 0)),
                  pl.BlockSpec((None, D, tf), lambda i, f: (l, 0, f)),
                  pl.BlockSpec((None, tf, D), lambda i, f: (l, f, 0)),
                  _layer_spec(g, l), _layer_spec(b, l)],
        out_specs=pl.BlockSpec((tm, D), lambda i, f: (i, 0)),
        out_shape=jax.ShapeDtypeStruct((BT, D), F32),
        scratch_shapes=[pltpu.VMEM((tm, D), F32)],
        compiler_params=_params(("parallel", "arbitrary")),
        name="mlp_ln",
    )(x2, wu, wd, g, b)


def _tiles(T):
    return dict(
        tm_proj=min(512, T), tm_out=min(512, T), tm_mlp=min(1024, T), tf_mlp=1024,
        a_tq=min(512, T), a_tk=min(512, T), c_tq=min(256, T), c_tk=min(512, T),
    )


def kernel(x, w_in, lam_q1, lam_k1, lam_q2, lam_k2, subln_g, w_pool, pool_scale, cmp_pe_k, cmp_pe_v,
           w_cmp_k, w_cmp_v, w_out, ln1_g, ln1_b, w_up, w_down, ln2_g, ln2_b):
    B, T, D = x.shape
    depth = w_in.shape[0]
    alpha = (2 * depth) ** 0.25
    tl = _tiles(T)
    tabs = _rope_tables(T)
    ovT = _overlap_T(T)
    w_tok, w_feat = _prep_w_in(w_in)
    w_bd, p_scale = _prep_pool(w_pool, pool_scale)
    pk2, pv2, wk2, wv2 = _prep_cmp(cmp_pe_k, cmp_pe_v, w_cmp_k, w_cmp_v)
    wo, wu, wd = w_out.astype(BF16), w_up.astype(BF16), w_down.astype(BF16)
    row = lambda a: a[:, None, :]
    lams = [row(a) for a in (lam_q1, lam_k1, lam_q2, lam_k2)]
    n16 = T // CMP_STRIDE
    x3 = x
    for l in range(depth):
        lam_init = 0.8 - 0.6 * math.exp(-0.3 * l)
        (ka, u, kcm, vcm, ksl, kwi, qaT, vaT, qcT, vslT, vwiT, gtT) = _inproj(x3, w_tok, w_feat, tabs, tl["tm_proj"], l)

        oaT = _diff_attn(qaT, ka, vaT, *lams, subln_g[:, :, None], lam_init, tl["a_tq"], tl["a_tk"], l)

        kc2, vc2 = _compress(kcm.reshape(B, n16, 16 * LANES), vcm.reshape(B, n16, 16 * LANES),
                             pk2, pv2, wk2, wv2, l)
        ocT = _nsa(qcT, gtT, kc2, vc2.transpose(0, 2, 1), ovT, ksl, vslT, kwi, vwiT, tl["c_tq"], tl["c_tk"])

        x3 = _outproj(x3, oaT, u, ocT, wo, w_bd, p_scale, row(ln1_g), row(ln1_b), alpha, tl["tm_out"], l)
        x3 = _mlp(x3.reshape(B * T, D), wu, wd, row(ln2_g), row(ln2_b), alpha, tl["tm_mlp"], tl["tf_mlp"],
                  l).reshape(B, T, D)
    return x3
```

```python
import functools
import math

import jax
import jax.numpy as jnp
import numpy as np
from jax import lax
from jax.experimental import pallas as pl
from jax.experimental.pallas import tpu as pltpu

A_HEADS = 6
A_QK_DIM = 32
A_V_DIM = 64
A_WIDTH = A_HEADS * A_V_DIM
POOL_WINDOWS = (2, 4, 8, 16)
POOL_GROUP_DIM = 64
POOL_WIDTH = 256
C_HEADS = 6
C_KV_GROUPS = 2
C_HPG = 3
C_HEAD_DIM = 64
C_WIDTH = C_HEADS * C_HEAD_DIM
CMP_BLOCK = 32
CMP_STRIDE = 16
SEL_BLOCK = 64
N_SEL = 16
WINDOW = 512
N_BRANCH = 3
FORCED_SCORE = 1.0e4
N_FORCED = 3
ROPE_THETA = 10000.0
LN_EPS = 1e-5
RMS_EPS = 1e-6

LANES = 128
LOG2E = 1.4426950408889634
NEG_BIG = -1e30
SWEEP_UNROLL = 4
A_BLOCKS = 4
TAKEN = -3.0e38
MASKED = -3e38
VMEM_LIMIT = 56 * 1024 * 1024

BF16 = jnp.bfloat16
F32 = jnp.float32


def _dot(a, b):
    return jnp.dot(a, b, preferred_element_type=F32)


def _params(sem, vmem=VMEM_LIMIT):
    return pltpu.CompilerParams(dimension_semantics=sem, vmem_limit_bytes=vmem)


_IN_SPLITS = (384, 384, 384, 256, 384, 128, 128, 128, 128, 128, 128, 18)
_OFFS = np.concatenate([[0], np.cumsum(_IN_SPLITS)]).astype(np.int64)
A_QSCALE = A_QK_DIM ** -0.5 * LOG2E
C_QSCALE = C_HEAD_DIM ** -0.5 * LOG2E
GATE_ROWS = 16
N_TOK = 384 + 256 + 4 * LANES
N_FEAT = 3 * 384 + 2 * LANES + C_KV_GROUPS * GATE_ROWS


def _prep_w_in(w):
    o = _OFFS
    L, D = w.shape[0], w.shape[1]
    seg = lambda i: w[:, :, o[i]:o[i + 1]]
    diff = lambda a: a.reshape(L, D, A_HEADS // 2, 2, 2, 2, A_QK_DIM // 2).transpose(0, 1, 2, 5, 3, 4, 6).reshape(L, D, -1)
    cq = lambda a: a.reshape(L, D, C_KV_GROUPS, C_HPG, 2, C_HEAD_DIM // 2).transpose(0, 1, 3, 4, 2, 5).reshape(L, D, -1)
    ck = lambda a: a.reshape(L, D, C_KV_GROUPS, 2, C_HEAD_DIM // 2).transpose(0, 1, 3, 2, 4).reshape(L, D, -1)
    w_tok = jnp.concatenate([diff(seg(1)), seg(3), ck(seg(5)), seg(6), ck(seg(7)), ck(seg(9))], axis=2)
    per = C_HPG * N_BRANCH
    gates = jnp.pad(seg(11).reshape(L, D, C_KV_GROUPS, per), ((0, 0), (0, 0), (0, 0), (0, GATE_ROWS - per)))
    w_feat = jnp.concatenate([diff(seg(0)), seg(2), cq(seg(4)), seg(8), seg(10),
                              gates.reshape(L, D, C_KV_GROUPS * GATE_ROWS)], axis=2)
    return w_tok.astype(BF16), jnp.swapaxes(w_feat, 1, 2).astype(BF16)


def _rope_tables(T):
    def tab(dim):
        half = dim // 2
        inv = 1.0 / (ROPE_THETA ** (jnp.arange(0, dim, 2, dtype=F32) / dim))
        ang = jnp.arange(T, dtype=F32)[:, None] * inv[None, :]
        lane = np.arange(LANES)
        j = lane % half
        sign = np.where(lane < 64, -1.0, 1.0).astype(np.float32)
        cos, sin = jnp.cos(ang), jnp.sin(ang)
        return cos[:, j], sin[:, j] * sign[None, :], cos[:, j[:64]].T, sin[:, j[:64]].T
    return tab(A_QK_DIM) + tab(C_HEAD_DIM)


def _inproj_kernel(x_ref, wt_ref, wf_ref, ca_ref, sa_ref, caT_ref, saT_ref, cc_ref, sc_ref, ccT_ref, scT_ref,
                   ka_ref, u_ref, kcm_ref, vcm_ref, ksl_ref, kwi_ref,
                   qa_ref, va_ref, qc_ref, vsl_ref, vwi_ref, gt_ref):
    xb = x_ref[...].astype(BF16)
    tok = _dot(xb, wt_ref[...])
    feat = lax.dot_general(wf_ref[...], xb, (((1,), (1,)), ((), ())), preferred_element_type=F32)

    def rope_tok(p, cos, sin):
        return p * cos + pltpu.roll(p, 64, axis=1) * sin

    def rope_feat(p, cos, sin):
        a, b = p[:64, :], p[64:, :]
        return jnp.concatenate([a * cos - b * sin, b * cos + a * sin], axis=0)

    ca, sa, cc, sc = ca_ref[...], sa_ref[...], cc_ref[...], sc_ref[...]
    caT, saT, ccT, scT = caT_ref[...], saT_ref[...], ccT_ref[...], scT_ref[...]
    for t in range(3):
        r = slice(t * LANES, (t + 1) * LANES)
        ka_ref[:, r] = rope_tok(tok[:, r], ca, sa).astype(BF16)
        qa_ref[r, :] = (rope_feat(feat[r, :], caT, saT) * A_QSCALE).astype(BF16)
        rq = slice(768 + t * LANES, 768 + (t + 1) * LANES)
        qc_ref[r, :] = (rope_feat(feat[rq, :], ccT, scT) * C_QSCALE).astype(BF16)
    u_ref[...] = tok[:, 384:640]
    kcm_ref[...] = rope_tok(tok[:, 640:768], cc, sc).astype(BF16)
    vcm_ref[...] = tok[:, 768:896].astype(BF16)
    ksl_ref[...] = rope_tok(tok[:, 896:1024], cc, sc).astype(BF16)
    kwi_ref[...] = rope_tok(tok[:, 1024:1152], cc, sc).astype(BF16)
    va_ref[...] = feat[384:768, :].astype(BF16)
    vsl_ref[...] = feat[1152:1280, :].astype(BF16)
    vwi_ref[...] = feat[1280:1408, :].astype(BF16)
    gt_ref[...] = jax.nn.sigmoid(feat[1408:, :])


def _layer_spec(a, l):
    return pl.BlockSpec((None,) + a.shape[1:], lambda *_: (l,) + (0,) * (a.ndim - 1))


def _inproj(x3, w_tok, w_feat, tabs, tm, l):
    B, T, D = x3.shape
    tok = lambda n: pl.BlockSpec((None, tm, n), lambda b, i: (b, i, 0))
    feat = lambda n: pl.BlockSpec((None, n, tm), lambda b, i: (b, 0, i))
    full = lambda a: _layer_spec(a, l)
    tab_tok = pl.BlockSpec((tm, LANES), lambda b, i: (i, 0))
    tab_feat = pl.BlockSpec((64, tm), lambda b, i: (0, i))
    tok_outs = [(384, BF16), (256, F32)] + [(LANES, BF16)] * 4
    feat_outs = [(384, BF16)] * 3 + [(LANES, BF16)] * 2 + [(C_KV_GROUPS * GATE_ROWS, F32)]
    return pl.pallas_call(
        _inproj_kernel,
        grid=(B, T // tm),
        in_specs=[tok(D), full(w_tok), full(w_feat)] + [tab_tok, tab_tok, tab_feat, tab_feat] * 2,
        out_specs=[tok(n) for n, _ in tok_outs] + [feat(n) for n, _ in feat_outs],
        out_shape=([jax.ShapeDtypeStruct((B, T, n), dt) for n, dt in tok_outs]
                   + [jax.ShapeDtypeStruct((B, n, T), dt) for n, dt in feat_outs]),
        compiler_params=_params(("parallel", "parallel")),
        name="inproj_rope",
    )(x3, w_tok, w_feat, *tabs)


ONES_ROWS = 16


def _sweep_scratch(tk, nq, dv, n_stats=1):
    stats = [pltpu.VMEM((1, nq), F32), pltpu.VMEM((dv + ONES_ROWS, nq), F32)]
    return [pltpu.VMEM((tk, nq), BF16), pltpu.VMEM((tk, nq), BF16),
            pltpu.VMEM((1, nq), F32), pltpu.VMEM((1, nq), F32)] + stats * n_stats


class _Sweep:
    def __init__(self, score, qpos, value, bufs, stats):
        self.score, self.qpos, self.value = score, qpos, value
        self.s_refs, self.cm_refs = bufs[0:2], bufs[2:4]
        self.m_ref, self.acc_ref = stats
        self.tk, self.nq = self.s_refs[0].shape
        self.dv = self.acc_ref.shape[0] - ONES_ROWS

    def init(self):
        self.m_ref[...] = jnp.full(self.m_ref.shape, NEG_BIG, F32)
        self.acc_ref[...] = jnp.zeros(self.acc_ref.shape, F32)

    def scores(self, j, slot, fn=None):
        s = self.score(j) if fn is None else fn()
        self.s_refs[slot][...] = s.astype(BF16)
        self.cm_refs[slot][...] = jnp.max(s, axis=0, keepdims=True)

    def softmax(self, slot, j, causal=False):
        s = self.s_refs[slot][...]
        if causal:
            kpos = j * self.tk + lax.broadcasted_iota(jnp.int32, (self.tk, self.nq), 0)
            s = jnp.where(kpos <= self.qpos, s, jnp.asarray(MASKED, s.dtype))
            cm = jnp.max(s, axis=0, keepdims=True).astype(F32)
        else:
            cm = self.cm_refs[slot][...]
        m = self.m_ref[...]
        m_new = jnp.maximum(m, cm)
        alpha = jnp.exp2(m - m_new)
        p = jnp.exp2(s - m_new.astype(s.dtype))
        self.m_ref[...] = m_new
        v = self.value(j)
        v1 = jnp.concatenate([v, jnp.ones((ONES_ROWS, v.shape[1]), v.dtype)], axis=0)
        self.acc_ref[...] = self.acc_ref[...] * alpha + _dot(v1, p)

    def full_tiles(self, count, par):
        def group(base, tiles):
            for u in range(tiles):
                self.scores(base + u + 1, (par + u + 1) % 2)
                self.softmax((par + u) % 2, base + u)

        def quad(i, c):
            group(SWEEP_UNROLL * i, SWEEP_UNROLL)
            return c

        def pair(i, c):
            group(done + 2 * i, 2)
            return c

        lax.fori_loop(0, count // SWEEP_UNROLL, quad, 0)
        done = (count // SWEEP_UNROLL) * SWEEP_UNROLL
        lax.fori_loop(0, (count - done) // 2, pair, 0)

    def result(self):
        return self.acc_ref[:self.dv, :] / self.acc_ref[self.dv:self.dv + 1, :]


def _causal_sweep(n_full, score, qpos, value, scratch, first_scores=None):
    sw = _Sweep(score, qpos, value, scratch[0:4], scratch[4:6])
    sw.init()
    sw.scores(0, 0, first_scores)
    sw.full_tiles(n_full - n_full % 2, 0)

    @pl.when(n_full % 2 == 0)
    def _():
        sw.softmax(0, n_full, causal=True)

    @pl.when(n_full % 2 == 1)
    def _():
        sw.scores(n_full, 1)
        sw.softmax(0, n_full - 1)
        sw.softmax(1, n_full, causal=True)

    return sw.result()


def _diff_attn_kernel(lq1_ref, lk1_ref, lq2_ref, lk2_ref, g_ref, q_ref, k_ref, v_ref, o_ref,
                      *scratch, tq, tk, lam_init):
    h = pl.program_id(1)
    step = pl.program_id(2)
    hl = h % 2
    nq = 2 * tq
    grp = (lax.broadcasted_iota(jnp.int32, (LANES, tq), 0) % 64) // 16
    lane_q = lax.broadcasted_iota(jnp.int32, (1, nq), 1) % tq

    def value(j):
        return v_ref[:, pl.ds(pl.multiple_of(j * tk, tk), tk)]

    def sweep(blk, stats):
        q = q_ref[:, blk * tq:(blk + 1) * tq]
        zero = jnp.zeros_like(q)
        q12 = jnp.concatenate([jnp.where(grp == 2 * hl, q, zero), jnp.where(grp == 2 * hl + 1, q, zero)], axis=1)
        qpos = (A_BLOCKS * step + blk) * tq + lane_q
        score = lambda j: _dot(k_ref[pl.ds(pl.multiple_of(j * tk, tk), tk), :], q12)
        return _Sweep(score, qpos, value, scratch[0:4], stats)

    dotp = lambda a, b: jnp.sum(a[...] * b[...], axis=1, keepdims=True)
    lam = jnp.exp(dotp(lq1_ref, lk1_ref)) - jnp.exp(dotp(lq2_ref, lk2_ref)) + lam_init

    def finish(sw, blk):
        on = sw.result()
        o = on[:, :tq] - lam * on[:, tq:]
        ms = jnp.mean(o * o, axis=0, keepdims=True)
        o = o * lax.rsqrt(ms + RMS_EPS) * g_ref[...] * (1.0 - lam_init)
        o_ref[:, blk * tq:(blk + 1) * tq] = o.astype(o_ref.dtype)

    sweeps = [sweep(blk, scratch[4 + 2 * blk:6 + 2 * blk]) for blk in range(A_BLOCKS)]
    first = 0
    sweeps[0].init()
    sweeps[0].scores(0, first)
    for blk, sw in enumerate(sweeps):
        n = A_BLOCKS * step + blk
        if blk % 2 == 0:
            sw.full_tiles(n, first)
            causal_buf = first
        else:
            sw.full_tiles(n - 1, first)
            sw.scores(n, 1 - first)
            sw.softmax(first, n - 1)
            causal_buf = 1 - first
        first = 1 - causal_buf
        if blk + 1 < A_BLOCKS:
            sweeps[blk + 1].init()
            sweeps[blk + 1].scores(0, first)
        sw.softmax(causal_buf, n, causal=True)
        finish(sw, blk)


def _diff_attn(qaT, ka, vaT, lq1, lk1, lq2, lk2, g_col, lam_init, tq, tk, l):
    B, _, T = qaT.shape
    assert tq == tk and T % (A_BLOCKS * tq) == 0, "query block i ends at key tile i; A_BLOCKS blocks per step"
    nq = 2 * tq
    vec = _layer_spec(lq1, l)
    return pl.pallas_call(
        functools.partial(_diff_attn_kernel, tq=tq, tk=tk, lam_init=lam_init),
        grid=(B, A_HEADS, T // (A_BLOCKS * tq)),
        in_specs=[vec, vec, vec, vec, _layer_spec(g_col, l),
                  pl.BlockSpec((None, LANES, A_BLOCKS * tq), lambda b, h, i: (b, h // 2, i)),
                  pl.BlockSpec((None, T, LANES), lambda b, h, i: (b, 0, h // 2)),
                  pl.BlockSpec((None, A_V_DIM, T), lambda b, h, i: (b, h, 0))],
        out_specs=pl.BlockSpec((None, A_V_DIM, A_BLOCKS * tq), lambda b, h, i: (b, h, i)),
        out_shape=jax.ShapeDtypeStruct((B, A_WIDTH, T), BF16),
        scratch_shapes=_sweep_scratch(tk, nq, A_V_DIM, n_stats=A_BLOCKS),
        compiler_params=_params(("parallel", "parallel", "arbitrary")),
        name="diff_attn",
    )(lq1, lk1, lq2, lk2, g_col, qaT, ka, vaT)


POOL_HALO = 16


def _pool_mix(halo_ref, cur_ref, w_ref, scale_ref, i, tm):
    cur = cur_ref[...]
    halo = halo_ref[...]
    halo = jnp.where(i == 0, jnp.zeros_like(halo), halo)
    arr = jnp.concatenate([halo, cur], axis=0)
    s2 = arr + pltpu.roll(arr, 1, axis=0)
    s4 = s2 + pltpu.roll(s2, 2, axis=0)
    s8 = s4 + pltpu.roll(s4, 4, axis=0)
    s16 = s8 + pltpu.roll(s8, 8, axis=0)
    sums = [s[POOL_HALO:, :] for s in (s2, s4, s8, s16)]
    lane = lax.broadcasted_iota(jnp.int32, (tm, POOL_WIDTH), 1)
    tpos = i * tm + lax.broadcasted_iota(jnp.int32, (tm, POOL_WIDTH), 0)
    grp = lane // POOL_GROUP_DIM
    win = jnp.where(grp == 0, sums[0], jnp.where(grp == 1, sums[1], jnp.where(grp == 2, sums[2], sums[3])))
    width = jnp.where(grp == 0, 2, jnp.where(grp == 1, 4, jnp.where(grp == 2, 8, 16)))
    cnt = jnp.minimum(tpos + 1, width).astype(F32)
    y = win / cnt - cur
    z = _dot(y.astype(BF16), w_ref[...])
    return (z * scale_ref[...]).astype(BF16)


def _prep_pool(w_pool, pool_scale):
    depth, G, C, _ = w_pool.shape
    eye = jnp.asarray(np.eye(G, dtype=np.float32))
    w_bd = (w_pool[:, :, :, None, :] * eye[None, :, None, :, None]).reshape(depth, G * C, G * C)
    return w_bd.astype(BF16), pool_scale[:, None, :]


def _cmp_kernel(k_ref, v_ref, pk_ref, pv_ref, wk_ref, wv_ref, kc_ref, vc_ref):
    n = k_ref.shape[0]
    rowid = lax.broadcasted_iota(jnp.int32, (n, LANES), 0)

    def compress(x_ref, pe_ref, w_ref):
        x = x_ref[...].astype(F32)
        lo = _dot((x + pe_ref[0:1, :]).astype(BF16), w_ref[0])
        hi = _dot((x + pe_ref[1:2, :]).astype(BF16), w_ref[1])
        out = lo + pltpu.roll(hi, n - 1, axis=0)
        return jnp.where(rowid < n - 1, out, 0.0)

    kc_ref[...] = compress(k_ref, pk_ref, wk_ref).astype(kc_ref.dtype)
    vc_ref[...] = compress(v_ref, pv_ref, wv_ref).astype(vc_ref.dtype)


def _compress(kr, vr, pe_k2, pe_v2, wk2, wv2, l):
    B, n, W = kr.shape
    blk = pl.BlockSpec((None, n, W), lambda b: (b, 0, 0))
    pe = _layer_spec(pe_k2, l)
    wsp = _layer_spec(wk2, l)
    osp = pl.BlockSpec((None, n, LANES), lambda b: (b, 0, 0))
    return pl.pallas_call(
        _cmp_kernel,
        grid=(B,),
        in_specs=[blk, blk, pe, pe, wsp, wsp],
        out_specs=[osp, osp],
        out_shape=[jax.ShapeDtypeStruct((B, n, LANES), BF16)] * 2,
        compiler_params=_params(("parallel",)),
        name="nsa_compress",
    )(kr, vr, pe_k2, pe_v2, wk2, wv2)


def _prep_cmp(pe_k, pe_v, w_ck, w_cv):
    L, G, d, h = w_ck.shape[0], C_KV_GROUPS, C_HEAD_DIM, C_HEAD_DIM // 2
    same = jnp.asarray(np.eye(G, dtype=np.float32))
    shape2 = (L, 2, 16 * LANES, G * d)
    wk = w_ck.reshape(L, CMP_BLOCK, 2, 1, h, 2, 1, h) * same[None, None, None, :, None, None, :, None]
    pk2 = jnp.broadcast_to(pe_k.reshape(L, CMP_BLOCK, 2, 1, h), (L, CMP_BLOCK, 2, G, h))
    wv = w_cv.reshape(L, CMP_BLOCK, 1, d, 1, d) * same[None, None, :, None, :, None]
    pv2 = jnp.broadcast_to(pe_v.reshape(L, CMP_BLOCK, 1, d), (L, CMP_BLOCK, G, d))
    flat = lambda pe: pe.reshape(L, 2, 16 * LANES)
    return flat(pk2), flat(pv2), wk.reshape(shape2).astype(BF16), wv.reshape(shape2).astype(BF16)


def _nsa_kernel(q_ref, gt_ref, kc_ref, vc_ref, ov_ref, wb_ref, ks_ref, vs_ref, kw_ref, vw_ref, o_ref, sel_ref,
                *scratch, tq, tk, T):
    g = pl.program_id(1)
    qi = pl.program_id(2)
    q0 = qi * tq
    n_cmp = kc_ref.shape[0]
    n_blk = T // SEL_BLOCK
    d = C_HEAD_DIM
    tpos = q0 + lax.broadcasted_iota(jnp.int32, (1, tq), 1)

    nq = C_HPG * tq
    tpos3 = q0 + lax.broadcasted_iota(jnp.int32, (1, nq), 1) % tq
    rowg = (lax.broadcasted_iota(jnp.int32, (LANES, tq), 0) % 64) // 32
    qs = []
    for hh in range(C_HPG):
        qt = q_ref[hh * LANES:(hh + 1) * LANES, :]
        qs.append(jnp.where(rowg == g, qt, jnp.zeros_like(qt)))
    q3 = jnp.concatenate(qs, axis=1)

    raw0_ref = scratch[-1]
    raw0_ref[...] = _dot(ks_ref[0:tk, :], q3)

    band = WINDOW + tq
    w0 = pl.multiple_of(jnp.maximum(q0 - WINDOW, 0), LANES)
    s = _dot(kw_ref[pl.ds(w0, band), :], q3) + jnp.concatenate([wb_ref[...]] * C_HPG, axis=1)
    e = jnp.exp2(s - jnp.max(s, axis=0, keepdims=True))
    o_win = _dot(vw_ref[:, pl.ds(w0, band)], e.astype(BF16)) / jnp.sum(e, axis=0, keepdims=True)

    cend = lax.broadcasted_iota(jnp.int32, (n_cmp, nq), 0) * CMP_STRIDE + (CMP_BLOCK - 1)
    s = jnp.where(cend <= tpos3, _dot(kc_ref[...], q3), MASKED)
    m = jnp.maximum(jnp.max(s, axis=0, keepdims=True), NEG_BIG)
    e = jnp.exp2(s - m)
    l = jnp.sum(e, axis=0, keepdims=True)
    p = e / jnp.where(l > 0.0, l, 1.0)
    o_cmp = _dot(vc_ref[...], p.astype(BF16))
    psum = p[:, :tq] + p[:, tq:2 * tq] + p[:, 2 * tq:]

    p_hi = psum.astype(BF16)
    p_lo = (psum - p_hi.astype(F32)).astype(BF16)
    imp = _dot(ov_ref[...], p_hi) + _dot(ov_ref[...], p_lo)
    jidx = lax.broadcasted_iota(jnp.int32, (n_blk, tq), 0).astype(F32)
    cur = (tpos // SEL_BLOCK).astype(F32)
    forced = (jidx == 0.0) | (jidx == cur) | (jidx == cur - 1.0)
    imp = jnp.where(jidx > cur, -1.0, imp)
    imp = jnp.where(forced, TAKEN, imp)
    rounds = min(N_SEL, n_blk) - N_FORCED

    def store_selection(taken_imp):
        bias = jnp.where(taken_imp == TAKEN, 0.0, MASKED)
        sel_ref[...] = jnp.concatenate([bias] * C_HPG, axis=1)

    quick = imp
    for _ in range(rounds):
        quick = jnp.where(quick == jnp.max(quick, axis=0, keepdims=True), TAKEN, quick)
    store_selection(quick)
    free = (jidx >= 1.0) & (jidx <= cur - 2.0)
    n_taken = jnp.sum(jnp.where(free & (quick == TAKEN), 1.0, 0.0), axis=0, keepdims=True)
    n_expected = jnp.minimum(jnp.maximum(cur - 2.0, 0.0), float(rounds))
    tie_in_top = jnp.max(jnp.abs(n_taken - n_expected)) > 0.0

    @pl.when(tie_in_top)
    def _():
        exact = imp
        for _ in range(rounds):
            mx = jnp.max(exact, axis=0, keepdims=True)
            first = jnp.min(jnp.where(exact == mx, jidx, float(n_blk)), axis=0, keepdims=True)
            exact = jnp.where(jidx == first, TAKEN, exact)
        store_selection(exact)

    bpt = tk // SEL_BLOCK

    def block_bias(j):
        rows = sel_ref[pl.ds(pl.multiple_of(j * bpt, bpt), bpt), :]
        return jnp.concatenate([jnp.broadcast_to(rows[i:i + 1, :], (SEL_BLOCK, nq)) for i in range(bpt)], axis=0)

    def score(j):
        return _dot(ks_ref[pl.ds(pl.multiple_of(j * tk, tk), tk), :], q3) + block_bias(j)

    def first_scores():
        return raw0_ref[...] + block_bias(0)

    def value(j):
        return vs_ref[:, pl.ds(pl.multiple_of(j * tk, tk), tk)]

    o_slc = _causal_sweep(q0 // tk, score, tpos3, value, scratch, first_scores)

    gts = gt_ref[...]
    for hh in range(C_HPG):
        r = hh * N_BRANCH
        c = slice(hh * tq, (hh + 1) * tq)
        out = (o_cmp[:, c] * gts[r:r + 1, :] + o_slc[:, c] * gts[r + 1:r + 2, :]
               + o_win[:, c] * gts[r + 2:r + 3, :])
        o_ref[hh * d:(hh + 1) * d, :] = out.astype(o_ref.dtype)


def _window_bias(tq):
    band, n_edge = WINDOW + tq, WINDOW // tq
    q0 = np.arange(n_edge + 1)[:, None, None] * tq
    wpos = np.maximum(q0 - WINDOW, 0) + np.arange(band)[None, :, None]
    tpos = q0 + np.arange(tq)[None, None, :]
    ok = (wpos <= tpos) & (wpos > tpos - WINDOW)
    return jnp.asarray(np.where(ok, 0.0, MASKED).astype(np.float32))


def _nsa(qcT, gtT, kc2, vcT, ovT, ksl, vslT, kwi, vwiT, tq, tk):
    B, _, T = qcT.shape
    n_cmp = kc2.shape[1]
    n_blk = T // SEL_BLOCK
    nq = C_HPG * tq
    assert tk % tq == 0, "the causal key tile must contain the whole query block"
    assert WINDOW % tq == 0 and T >= WINDOW + tq
    wb = _window_bias(tq)
    n_edge = WINDOW // tq
    kv_tok = pl.BlockSpec((None, T, LANES), lambda b, g, i: (b, 0, 0))
    kv_feat = pl.BlockSpec((None, C_HEAD_DIM, T), lambda b, g, i: (b, g, 0))
    return pl.pallas_call(
        functools.partial(_nsa_kernel, tq=tq, tk=tk, T=T),
        grid=(B, C_KV_GROUPS, T // tq),
        in_specs=[pl.BlockSpec((None, C_WIDTH, tq), lambda b, g, i: (b, 0, i)),
                  pl.BlockSpec((None, GATE_ROWS, tq), lambda b, g, i: (b, g, i)),
                  pl.BlockSpec((None, n_cmp, LANES), lambda b, g, i: (b, 0, 0)),
                  pl.BlockSpec((None, C_HEAD_DIM, n_cmp), lambda b, g, i: (b, g, 0)),
                  pl.BlockSpec((n_blk, n_cmp), lambda b, g, i: (0, 0)),
                  pl.BlockSpec((None, WINDOW + tq, tq), lambda b, g, i: (jnp.minimum(i, n_edge), 0, 0)),
                  kv_tok, kv_feat, kv_tok, kv_feat],
        out_specs=pl.BlockSpec((None, C_HPG * C_HEAD_DIM, tq), lambda b, g, i: (b, g, i)),
        out_shape=jax.ShapeDtypeStruct((B, C_WIDTH, T), BF16),
        scratch_shapes=([pltpu.VMEM((n_blk, nq), F32)] + _sweep_scratch(tk, nq, C_HEAD_DIM)
                        + [pltpu.VMEM((tk, nq), F32)]),
        compiler_params=_params(("parallel", "parallel", "arbitrary")),
        name="nsa_attn",
    )(qcT, gtT, kc2, vcT, ovT, wb, ksl, vslT, kwi, vwiT)


def _overlap_T(T):
    n_cmp = T // CMP_STRIDE
    n_blk = T // SEL_BLOCK
    cs = np.arange(n_cmp)[None, :] * CMP_STRIDE
    bs = np.arange(n_blk)[:, None] * SEL_BLOCK
    ov = (cs < bs + SEL_BLOCK) & (cs + CMP_BLOCK > bs) & (np.arange(n_cmp)[None, :] < n_cmp - 1)
    return jnp.asarray(ov.astype(np.float32), dtype=BF16)


def _layer_norm(y, g, b):
    mu = jnp.mean(y, axis=-1, keepdims=True)
    yc = y - mu
    var = jnp.mean(yc * yc, axis=-1, keepdims=True)
    return yc * lax.rsqrt(var + LN_EPS) * g + b


def _outproj_kernel(x_ref, oa_ref, uh_ref, u_ref, oc_ref, w_ref, wp_ref, ps_ref, g_ref, b_ref, o_ref, *, alpha, tm):
    tn = lambda a, w: lax.dot_general(a, w, (((0,), (0,)), ((), ())), preferred_element_type=F32)
    r1, r2 = A_WIDTH, A_WIDTH + POOL_WIDTH
    ob = _pool_mix(uh_ref, u_ref, wp_ref, ps_ref, pl.program_id(1), tm)
    mix = tn(oa_ref[...], w_ref[:r1, :]) + _dot(ob, w_ref[r1:r2, :]) + tn(oc_ref[...], w_ref[r2:, :])
    o_ref[...] = _layer_norm(alpha * x_ref[...] + mix, g_ref[...], b_ref[...])


def _outproj(x3, oaT, u, ocT, w, w_bd, p_scale, g, b, alpha, tm, l):
    B, T, D = x3.shape
    tok = lambda n: pl.BlockSpec((None, tm, n), lambda bb, i: (bb, i, 0))
    feat = lambda n: pl.BlockSpec((None, n, tm), lambda bb, i: (bb, 0, i))
    halo = pl.BlockSpec((None, POOL_HALO, POOL_WIDTH),
                        lambda bb, i: (bb, jnp.maximum(i * (tm // POOL_HALO) - 1, 0), 0))
    return pl.pallas_call(
        functools.partial(_outproj_kernel, alpha=alpha, tm=tm),
        grid=(B, T // tm),
        in_specs=[tok(D), feat(oaT.shape[1]), halo, tok(POOL_WIDTH), feat(ocT.shape[1]),
                  _layer_spec(w, l), _layer_spec(w_bd, l), _layer_spec(p_scale, l),
                  _layer_spec(g, l), _layer_spec(b, l)],
        out_specs=tok(D),
        out_shape=jax.ShapeDtypeStruct((B, T, D), F32),
        compiler_params=_params(("parallel", "parallel")),
        name="outproj_ln",
    )(x3, oaT, u, u, ocT, w, w_bd, p_scale, g, b)


def _mlp_kernel(x_ref, wu_ref, wd_ref, g_ref, b_ref, o_ref, acc_ref, *, alpha):
    f = pl.program_id(1)

    @pl.when(f == 0)
    def _():
        acc_ref[...] = jnp.zeros_like(acc_ref)

    hid = _dot(x_ref[...].astype(BF16), wu_ref[...])
    hid = jnp.square(jnp.maximum(hid, 0.0))
    acc_ref[...] += _dot(hid.astype(BF16), wd_ref[...])

    @pl.when(f == pl.num_programs(1) - 1)
    def _():
        o_ref[...] = _layer_norm(alpha * x_ref[...] + acc_ref[...], g_ref[...], b_ref[...])


def _mlp(x2, wu, wd, g, b, alpha, tm, tf, l):
    BT, D = x2.shape
    F = wu.shape[2]
    return pl.pallas_call(
        functools.partial(_mlp_kernel, alpha=alpha),
        grid=(BT // tm, F // tf),
        in_specs=[pl.BlockSpec((tm, D), lambda i, f: (i, 0)),
                  pl.BlockSpec((None, D, tf), lambda i, f: (l, 0, f)),
                  pl.BlockSpec((None, tf, D), lambda i, f: (l, f, 0)),
                  _layer_spec(g, l), _layer_spec(b, l)],
        out_specs=pl.BlockSpec((tm, D), lambda i, f: (i, 0)),
        out_shape=jax.ShapeDtypeStruct((BT, D), F32),
        scratch_shapes=[pltpu.VMEM((tm, D), F32)],
        compiler_params=_params(("parallel", "arbitrary")),
        name="mlp_ln",
    )(x2, wu, wd, g, b)


def _tiles(T):
    return dict(
        tm_proj=min(512, T), tm_out=min(512, T), tm_mlp=min(1024, T), tf_mlp=1024,
        a_tq=min(512, T), a_tk=min(512, T), c_tq=min(256, T), c_tk=min(512, T),
    )


def kernel(x, w_in, lam_q1, lam_k1, lam_q2, lam_k2, subln_g, w_pool, pool_scale, cmp_pe_k, cmp_pe_v,
           w_cmp_k, w_cmp_v, w_out, ln1_g, ln1_b, w_up, w_down, ln2_g, ln2_b):
    B, T, D = x.shape
    depth = w_in.shape[0]
    alpha = (2 * depth) ** 0.25
    tl = _tiles(T)
    tabs = _rope_tables(T)
    ovT = _overlap_T(T)
    w_tok, w_feat = _prep_w_in(w_in)
    w_bd, p_scale = _prep_pool(w_pool, pool_scale)
    pk2, pv2, wk2, wv2 = _prep_cmp(cmp_pe_k, cmp_pe_v, w_cmp_k, w_cmp_v)
    wo, wu, wd = w_out.astype(BF16), w_up.astype(BF16), w_down.astype(BF16)
    row = lambda a: a[:, None, :]
    lams = [row(a) for a in (lam_q1, lam_k1, lam_q2, lam_k2)]
    n16 = T // CMP_STRIDE
    x3 = x
    for l in range(depth):
        lam_init = 0.8 - 0.6 * math.exp(-0.3 * l)
        (ka, u, kcm, vcm, ksl, kwi, qaT, vaT, qcT, vslT, vwiT, gtT) = _inproj(x3, w_tok, w_feat, tabs, tl["tm_proj"], l)

        oaT = _diff_attn(qaT, ka, vaT, *lams, subln_g[:, :, None], lam_init, tl["a_tq"], tl["a_tk"], l)

        kc2, vc2 = _compress(kcm.reshape(B, n16, 16 * LANES), vcm.reshape(B, n16, 16 * LANES),
                             pk2, pv2, wk2, wv2, l)
        ocT = _nsa(qcT, gtT, kc2, vc2.transpose(0, 2, 1), ovT, ksl, vslT, kwi, vwiT, tl["c_tq"], tl["c_tk"])

        x3 = _outproj(x3, oaT, u, ocT, wo, w_bd, p_scale, row(ln1_g), row(ln1_b), alpha, tl["tm_out"], l)
        x3 = _mlp(x3.reshape(B * T, D), wu, wd, row(ln2_g), row(ln2_b), alpha, tl["tm_mlp"], tl["tf_mlp"],
                  l).reshape(B, T, D)
    return x3
```

```python
import functools
import math

import jax
import jax.numpy as jnp
import numpy as np
from jax import lax
from jax.experimental import pallas as pl
from jax.experimental.pallas import tpu as pltpu

A_HEADS = 6
A_QK_DIM = 32
A_V_DIM = 64
A_WIDTH = A_HEADS * A_V_DIM
POOL_WINDOWS = (2, 4, 8, 16)
POOL_GROUP_DIM = 64
POOL_WIDTH = 256
C_HEADS = 6
C_KV_GROUPS = 2
C_HPG = 3
C_HEAD_DIM = 64
C_WIDTH = C_HEADS * C_HEAD_DIM
CMP_BLOCK = 32
CMP_STRIDE = 16
SEL_BLOCK = 64
N_SEL = 16
WINDOW = 512
N_BRANCH = 3
FORCED_SCORE = 1.0e4
N_FORCED = 3
ROPE_THETA = 10000.0
LN_EPS = 1e-5
RMS_EPS = 1e-6

LANES = 128
LOG2E = 1.4426950408889634
NEG_BIG = -1e30
SWEEP_UNROLL = 4
A_BLOCKS = 4
TAKEN = -3.0e38
MASKED = -3e38
VMEM_LIMIT = 56 * 1024 * 1024

BF16 = jnp.bfloat16
F32 = jnp.float32


def _dot(a, b):
    return jnp.dot(a, b, preferred_element_type=F32)


def _params(sem, vmem=VMEM_LIMIT):
    return pltpu.CompilerParams(dimension_semantics=sem, vmem_limit_bytes=vmem)


_IN_SPLITS = (384, 384, 384, 256, 384, 128, 128, 128, 128, 128, 128, 18)
_OFFS = np.concatenate([[0], np.cumsum(_IN_SPLITS)]).astype(np.int64)
A_QSCALE = A_QK_DIM ** -0.5 * LOG2E
C_QSCALE = C_HEAD_DIM ** -0.5 * LOG2E
GATE_ROWS = 16
N_TOK = 384 + 256 + 4 * LANES
N_FEAT = 3 * 384 + 2 * LANES + C_KV_GROUPS * GATE_ROWS


def _prep_w_in(w):
    o = _OFFS
    L, D = w.shape[0], w.shape[1]
    seg = lambda i: w[:, :, o[i]:o[i + 1]]
    diff = lambda a: a.reshape(L, D, A_HEADS // 2, 2, 2, 2, A_QK_DIM // 2).transpose(0, 1, 2, 5, 3, 4, 6).reshape(L, D, -1)
    cq = lambda a: a.reshape(L, D, C_KV_GROUPS, C_HPG, 2, C_HEAD_DIM // 2).transpose(0, 1, 3, 4, 2, 5).reshape(L, D, -1)
    ck = lambda a: a.reshape(L, D, C_KV_GROUPS, 2, C_HEAD_DIM // 2).transpose(0, 1, 3, 2, 4).reshape(L, D, -1)
    w_tok = jnp.concatenate([diff(seg(1)), seg(3), ck(seg(5)), seg(6), ck(seg(7)), ck(seg(9))], axis=2)
    per = C_HPG * N_BRANCH
    gates = jnp.pad(seg(11).reshape(L, D, C_KV_GROUPS, per), ((0, 0), (0, 0), (0, 0), (0, GATE_ROWS - per)))
    w_feat = jnp.concatenate([diff(seg(0)), seg(2), cq(seg(4)), seg(8), seg(10),
                              gates.reshape(L, D, C_KV_GROUPS * GATE_ROWS)], axis=2)
    return w_tok.astype(BF16), jnp.swapaxes(w_feat, 1, 2).astype(BF16)


def _rope_tables(T):
    def tab(dim):
        half = dim // 2
        inv = 1.0 / (ROPE_THETA ** (jnp.arange(0, dim, 2, dtype=F32) / dim))
        ang = jnp.arange(T, dtype=F32)[:, None] * inv[None, :]
        lane = np.arange(LANES)
        j = lane % half
        sign = np.where(lane < 64, -1.0, 1.0).astype(np.float32)
        cos, sin = jnp.cos(ang), jnp.sin(ang)
        return cos[:, j], sin[:, j] * sign[None, :], cos[:, j[:64]].T, sin[:, j[:64]].T
    return tab(A_QK_DIM) + tab(C_HEAD_DIM)


def _inproj_kernel(x_ref, wt_ref, wf_ref, ca_ref, sa_ref, caT_ref, saT_ref, cc_ref, sc_ref, ccT_ref, scT_ref,
                   ka_ref, u_ref, kcm_ref, vcm_ref, ksl_ref, kwi_ref,
                   qa_ref, va_ref, qc_ref, vsl_ref, vwi_ref, gt_ref):
    xb = x_ref[...].astype(BF16)
    tok = _dot(xb, wt_ref[...])
    feat = lax.dot_general(wf_ref[...], xb, (((1,), (1,)), ((), ())), preferred_element_type=F32)

    def rope_tok(p, cos, sin):
        return p * cos + pltpu.roll(p, 64, axis=1) * sin

    def rope_feat(p, cos, sin):
        a, b = p[:64, :], p[64:, :]
        return jnp.concatenate([a * cos - b * sin, b * cos + a * sin], axis=0)

    ca, sa, cc, sc = ca_ref[...], sa_ref[...], cc_ref[...], sc_ref[...]
    caT, saT, ccT, scT = caT_ref[...], saT_ref[...], ccT_ref[...], scT_ref[...]
    for t in range(3):
        r = slice(t * LANES, (t + 1) * LANES)
        ka_ref[:, r] = rope_tok(tok[:, r], ca, sa).astype(BF16)
        qa_ref[r, :] = (rope_feat(feat[r, :], caT, saT) * A_QSCALE).astype(BF16)
        rq = slice(768 + t * LANES, 768 + (t + 1) * LANES)
        qc_ref[r, :] = (rope_feat(feat[rq, :], ccT, scT) * C_QSCALE).astype(BF16)
    u_ref[...] = tok[:, 384:640]
    kcm_ref[...] = rope_tok(tok[:, 640:768], cc, sc).astype(BF16)
    vcm_ref[...] = tok[:, 768:896].astype(BF16)
    ksl_ref[...] = rope_tok(tok[:, 896:1024], cc, sc).astype(BF16)
    kwi_ref[...] = rope_tok(tok[:, 1024:1152], cc, sc).astype(BF16)
    va_ref[...] = feat[384:768, :].astype(BF16)
    vsl_ref[...] = feat[1152:1280, :].astype(BF16)
    vwi_ref[...] = feat[1280:1408, :].astype(BF16)
    gt_ref[...] = jax.nn.sigmoid(feat[1408:, :])


def _layer_spec(a, l):
    return pl.BlockSpec((None,) + a.shape[1:], lambda *_: (l,) + (0,) * (a.ndim - 1))


def _inproj(x3, w_tok, w_feat, tabs, tm, l):
    B, T, D = x3.shape
    tok = lambda n: pl.BlockSpec((None, tm, n), lambda b, i: (b, i, 0))
    feat = lambda n: pl.BlockSpec((None, n, tm), lambda b, i: (b, 0, i))
    full = lambda a: _layer_spec(a, l)
    tab_tok = pl.BlockSpec((tm, LANES), lambda b, i: (i, 0))
    tab_feat = pl.BlockSpec((64, tm), lambda b, i: (0, i))
    tok_outs = [(384, BF16), (256, F32)] + [(LANES, BF16)] * 4
    feat_outs = [(384, BF16)] * 3 + [(LANES, BF16)] * 2 + [(C_KV_GROUPS * GATE_ROWS, F32)]
    return pl.pallas_call(
        _inproj_kernel,
        grid=(B, T // tm),
        in_specs=[tok(D), full(w_tok), full(w_feat)] + [tab_tok, tab_tok, tab_feat, tab_feat] * 2,
        out_specs=[tok(n) for n, _ in tok_outs] + [feat(n) for n, _ in feat_outs],
        out_shape=([jax.ShapeDtypeStruct((B, T, n), dt) for n, dt in tok_outs]
                   + [jax.ShapeDtypeStruct((B, n, T), dt) for n, dt in feat_outs]),
        compiler_params=_params(("parallel", "parallel")),
        name="inproj_rope",
    )(x3, w_tok, w_feat, *tabs)


ONES_ROWS = 16


def _sweep_scratch(tk, nq, dv, n_stats=1):
    stats = [pltpu.VMEM((1, nq), F32), pltpu.VMEM((dv + ONES_ROWS, nq), F32)]
    return [pltpu.VMEM((tk, nq), BF16), pltpu.VMEM((tk, nq), BF16),
            pltpu.VMEM((1, nq), F32), pltpu.VMEM((1, nq), F32)] + stats * n_stats


class _Sweep:
    def __init__(self, score, qpos, value, bufs, stats):
        self.score, self.qpos, self.value = score, qpos, value
        self.s_refs, self.cm_refs = bufs[0:2], bufs[2:4]
        self.m_ref, self.acc_ref = stats
        self.tk, self.nq = self.s_refs[0].shape
        self.dv = self.acc_ref.shape[0] - ONES_ROWS

    def init(self):
        self.m_ref[...] = jnp.full(self.m_ref.shape, NEG_BIG, F32)
        self.acc_ref[...] = jnp.zeros(self.acc_ref.shape, F32)

    def scores(self, j, slot, fn=None):
        s = self.score(j) if fn is None else fn()
        self.s_refs[slot][...] = s.astype(BF16)
        self.cm_refs[slot][...] = jnp.max(s, axis=0, keepdims=True)

    def softmax(self, slot, j, causal=False):
        s = self.s_refs[slot][...]
        if causal:
            kpos = j * self.tk + lax.broadcasted_iota(jnp.int32, (self.tk, self.nq), 0)
            s = jnp.where(kpos <= self.qpos, s, jnp.asarray(MASKED, s.dtype))
            cm = jnp.max(s, axis=0, keepdims=True).astype(F32)
        else:
            cm = self.cm_refs[slot][...]
        m = self.m_ref[...]
        m_new = jnp.maximum(m, cm)
        alpha = jnp.exp2(m - m_new)
        p = jnp.exp2(s - m_new.astype(s.dtype))
        self.m_ref[...] = m_new
        v = self.value(j)
        v1 = jnp.concatenate([v, jnp.ones((ONES_ROWS, v.shape[1]), v.dtype)], axis=0)
        self.acc_ref[...] = self.acc_ref[...] * alpha + _dot(v1, p)

    def full_tiles(self, count, par):
        def group(base, tiles):
            for u in range(tiles):
                self.scores(base + u + 1, (par + u + 1) % 2)
                self.softmax((par + u) % 2, base + u)

        def quad(i, c):
            group(SWEEP_UNROLL * i, SWEEP_UNROLL)
            return c

        def pair(i, c):
            group(done + 2 * i, 2)
            return c

        lax.fori_loop(0, count // SWEEP_UNROLL, quad, 0)
        done = (count // SWEEP_UNROLL) * SWEEP_UNROLL
        lax.fori_loop(0, (count - done) // 2, pair, 0)

    def result(self):
        return self.acc_ref[:self.dv, :] / self.acc_ref[self.dv:self.dv + 1, :]


def _causal_sweep(n_full, score, qpos, value, scratch, first_scores=None):
    sw = _Sweep(score, qpos, value, scratch[0:4], scratch[4:6])
    sw.init()
    sw.scores(0, 0, first_scores)
    sw.full_tiles(n_full - n_full % 2, 0)

    @pl.when(n_full % 2 == 0)
    def _():
        sw.softmax(0, n_full, causal=True)

    @pl.when(n_full % 2 == 1)
    def _():
        sw.scores(n_full, 1)
        sw.softmax(0, n_full - 1)
        sw.softmax(1, n_full, causal=True)

    return sw.result()


def _diff_attn_kernel(lq1_ref, lk1_ref, lq2_ref, lk2_ref, g_ref, q_ref, k_ref, v_ref, o_ref,
                      *scratch, tq, tk, lam_init):
    h = pl.program_id(1)
    step = pl.program_id(2)
    hl = h % 2
    nq = 2 * tq
    grp = (lax.broadcasted_iota(jnp.int32, (LANES, tq), 0) % 64) // 16
    lane_q = lax.broadcasted_iota(jnp.int32, (1, nq), 1) % tq

    def value(j):
        return v_ref[:, pl.ds(pl.multiple_of(j * tk, tk), tk)]

    def sweep(blk, stats):
        q = q_ref[:, blk * tq:(blk + 1) * tq]
        zero = jnp.zeros_like(q)
        q12 = jnp.concatenate([jnp.where(grp == 2 * hl, q, zero), jnp.where(grp == 2 * hl + 1, q, zero)], axis=1)
        qpos = (A_BLOCKS * step + blk) * tq + lane_q
        score = lambda j: _dot(k_ref[pl.ds(pl.multiple_of(j * tk, tk), tk), :], q12)
        return _Sweep(score, qpos, value, scratch[0:4], stats)

    dotp = lambda a, b: jnp.sum(a[...] * b[...], axis=1, keepdims=True)
    lam = jnp.exp(dotp(lq1_ref, lk1_ref)) - jnp.exp(dotp(lq2_ref, lk2_ref)) + lam_init

    def finish(sw, blk):
        on = sw.result()
        o = on[:, :tq] - lam * on[:, tq:]
        ms = jnp.mean(o * o, axis=0, keepdims=True)
        o = o * lax.rsqrt(ms + RMS_EPS) * g_ref[...] * (1.0 - lam_init)
        o_ref[:, blk * tq:(blk + 1) * tq] = o.astype(o_ref.dtype)

    sweeps = [sweep(blk, scratch[4 + 2 * blk:6 + 2 * blk]) for blk in range(A_BLOCKS)]
    first = 0
    sweeps[0].init()
    sweeps[0].scores(0, first)
    for blk, sw in enumerate(sweeps):
        n = A_BLOCKS * step + blk
        if blk % 2 == 0:
            sw.full_tiles(n, first)
            causal_buf = first
        else:
            sw.full_tiles(n - 1, first)
            sw.scores(n, 1 - first)
            sw.softmax(first, n - 1)
            causal_buf = 1 - first
        first = 1 - causal_buf
        if blk + 1 < A_BLOCKS:
            sweeps[blk + 1].init()
            sweeps[blk + 1].scores(0, first)
        sw.softmax(causal_buf, n, causal=True)
        finish(sw, blk)


def _diff_attn(qaT, ka, vaT, lq1, lk1, lq2, lk2, g_col, lam_init, tq, tk, l):
    B, _, T = qaT.shape
    assert tq == tk and T % (A_BLOCKS * tq) == 0, "query block i ends at key tile i; A_BLOCKS blocks per step"
    nq = 2 * tq
    vec = _layer_spec(lq1, l)
    return pl.pallas_call(
        functools.partial(_diff_attn_kernel, tq=tq, tk=tk, lam_init=lam_init),
        grid=(B, A_HEADS, T // (A_BLOCKS * tq)),
        in_specs=[vec, vec, vec, vec, _layer_spec(g_col, l),
                  pl.BlockSpec((None, LANES, A_BLOCKS * tq), lambda b, h, i: (b, h // 2, i)),
                  pl.BlockSpec((None, T, LANES), lambda b, h, i: (b, 0, h // 2)),
                  pl.BlockSpec((None, A_V_DIM, T), lambda b, h, i: (b, h, 0))],
        out_specs=pl.BlockSpec((None, A_V_DIM, A_BLOCKS * tq), lambda b, h, i: (b, h, i)),
        out_shape=jax.ShapeDtypeStruct((B, A_WIDTH, T), BF16),
        scratch_shapes=_sweep_scratch(tk, nq, A_V_DIM, n_stats=A_BLOCKS),
        compiler_params=_params(("parallel", "parallel", "arbitrary")),
        name="diff_attn",
    )(lq1, lk1, lq2, lk2, g_col, qaT, ka, vaT)


POOL_HALO = 16


def _pool_mix(halo, cur, w_ref, scale_ref, t0):
    tm = cur.shape[0]
    arr = jnp.concatenate([halo, cur], axis=0)
    s2 = arr + pltpu.roll(arr, 1, axis=0)
    s4 = s2 + pltpu.roll(s2, 2, axis=0)
    s8 = s4 + pltpu.roll(s4, 4, axis=0)
    s16 = s8 + pltpu.roll(s8, 8, axis=0)
    sums = [s[POOL_HALO:, :] for s in (s2, s4, s8, s16)]
    lane = lax.broadcasted_iota(jnp.int32, (tm, POOL_WIDTH), 1)
    tpos = t0 + lax.broadcasted_iota(jnp.int32, (tm, POOL_WIDTH), 0)
    grp = lane // POOL_GROUP_DIM
    win = jnp.where(grp == 0, sums[0], jnp.where(grp == 1, sums[1], jnp.where(grp == 2, sums[2], sums[3])))
    width = jnp.where(grp == 0, 2, jnp.where(grp == 1, 4, jnp.where(grp == 2, 8, 16)))
    cnt = jnp.minimum(tpos + 1, width).astype(F32)
    y = win / cnt - cur
    z = _dot(y.astype(BF16), w_ref[...])
    return (z * scale_ref[...]).astype(BF16)


def _prep_pool(w_pool, pool_scale):
    depth, G, C, _ = w_pool.shape
    eye = jnp.asarray(np.eye(G, dtype=np.float32))
    w_bd = (w_pool[:, :, :, None, :] * eye[None, :, None, :, None]).reshape(depth, G * C, G * C)
    return w_bd.astype(BF16), pool_scale[:, None, :]


def _cmp_kernel(k_ref, v_ref, pk_ref, pv_ref, wk_ref, wv_ref, kc_ref, vc_ref):
    n = k_ref.shape[0]
    rowid = lax.broadcasted_iota(jnp.int32, (n, LANES), 0)

    def compress(x_ref, pe_ref, w_ref):
        x = x_ref[...].astype(F32)
        lo = _dot((x + pe_ref[0:1, :]).astype(BF16), w_ref[0])
        hi = _dot((x + pe_ref[1:2, :]).astype(BF16), w_ref[1])
        out = lo + pltpu.roll(hi, n - 1, axis=0)
        return jnp.where(rowid < n - 1, out, 0.0)

    kc_ref[...] = compress(k_ref, pk_ref, wk_ref).astype(kc_ref.dtype)
    vc_ref[...] = compress(v_ref, pv_ref, wv_ref).astype(vc_ref.dtype)


def _compress(kr, vr, pe_k2, pe_v2, wk2, wv2, l):
    B, n, W = kr.shape
    blk = pl.BlockSpec((None, n, W), lambda b: (b, 0, 0))
    pe = _layer_spec(pe_k2, l)
    wsp = _layer_spec(wk2, l)
    osp = pl.BlockSpec((None, n, LANES), lambda b: (b, 0, 0))
    return pl.pallas_call(
        _cmp_kernel,
        grid=(B,),
        in_specs=[blk, blk, pe, pe, wsp, wsp],
        out_specs=[osp, osp],
        out_shape=[jax.ShapeDtypeStruct((B, n, LANES), BF16)] * 2,
        compiler_params=_params(("parallel",)),
        name="nsa_compress",
    )(kr, vr, pe_k2, pe_v2, wk2, wv2)


def _prep_cmp(pe_k, pe_v, w_ck, w_cv):
    L, G, d, h = w_ck.shape[0], C_KV_GROUPS, C_HEAD_DIM, C_HEAD_DIM // 2
    same = jnp.asarray(np.eye(G, dtype=np.float32))
    shape2 = (L, 2, 16 * LANES, G * d)
    wk = w_ck.reshape(L, CMP_BLOCK, 2, 1, h, 2, 1, h) * same[None, None, None, :, None, None, :, None]
    pk2 = jnp.broadcast_to(pe_k.reshape(L, CMP_BLOCK, 2, 1, h), (L, CMP_BLOCK, 2, G, h))
    wv = w_cv.reshape(L, CMP_BLOCK, 1, d, 1, d) * same[None, None, :, None, :, None]
    pv2 = jnp.broadcast_to(pe_v.reshape(L, CMP_BLOCK, 1, d), (L, CMP_BLOCK, G, d))
    flat = lambda pe: pe.reshape(L, 2, 16 * LANES)
    return flat(pk2), flat(pv2), wk.reshape(shape2).astype(BF16), wv.reshape(shape2).astype(BF16)


def _nsa_kernel(q_ref, gt_ref, kc_ref, vc_ref, ov_ref, wb_ref, ks_ref, vs_ref, kw_ref, vw_ref, o_ref, sel_ref,
                *scratch, tq, tk, T):
    g = pl.program_id(1)
    qi = pl.program_id(2)
    q0 = qi * tq
    n_cmp = kc_ref.shape[0]
    n_blk = T // SEL_BLOCK
    d = C_HEAD_DIM
    tpos = q0 + lax.broadcasted_iota(jnp.int32, (1, tq), 1)

    nq = C_HPG * tq
    tpos3 = q0 + lax.broadcasted_iota(jnp.int32, (1, nq), 1) % tq
    rowg = (lax.broadcasted_iota(jnp.int32, (LANES, tq), 0) % 64) // 32
    qs = []
    for hh in range(C_HPG):
        qt = q_ref[hh * LANES:(hh + 1) * LANES, :]
        qs.append(jnp.where(rowg == g, qt, jnp.zeros_like(qt)))
    q3 = jnp.concatenate(qs, axis=1)

    raw0_ref, ocw_ref = scratch[-2], scratch[-1]
    cur = (tpos // SEL_BLOCK).astype(F32)

    def prelude(n_c, n_b):
        cend = lax.broadcasted_iota(jnp.int32, (n_c, nq), 0) * CMP_STRIDE + (CMP_BLOCK - 1)
        s = jnp.where(cend <= tpos3, _dot(kc_ref[0:n_c, :], q3), MASKED)
        m = jnp.maximum(jnp.max(s, axis=0, keepdims=True), NEG_BIG)
        e = jnp.exp2(s - m)
        l = jnp.sum(e, axis=0, keepdims=True)
        p = e / jnp.where(l > 0.0, l, 1.0)
        ocw_ref[0:d, :] = _dot(vc_ref[:, 0:n_c], p.astype(BF16))
        psum = p[:, :tq] + p[:, tq:2 * tq] + p[:, 2 * tq:]

        p_hi = psum.astype(BF16)
        p_lo = (psum - p_hi.astype(F32)).astype(BF16)
        ov = ov_ref[0:n_b, 0:n_c]
        imp = _dot(ov, p_hi) + _dot(ov, p_lo)
        jidx = lax.broadcasted_iota(jnp.int32, (n_b, tq), 0).astype(F32)
        forced = (jidx == 0.0) | (jidx == cur) | (jidx == cur - 1.0)
        imp = jnp.where(jidx > cur, -1.0, imp)
        imp = jnp.where(forced, TAKEN, imp)
        rounds = min(N_SEL, n_blk) - N_FORCED

        band = WINDOW + tq
        w0 = pl.multiple_of(jnp.maximum(q0 - WINDOW, 0), LANES)
        s = _dot(kw_ref[pl.ds(w0, band), :], q3) + jnp.concatenate([wb_ref[...]] * C_HPG, axis=1)
        e = jnp.exp2(s - jnp.max(s, axis=0, keepdims=True))
        ocw_ref[d:, :] = _dot(vw_ref[:, pl.ds(w0, band)], e.astype(BF16)) / jnp.sum(e, axis=0, keepdims=True)

        raw0_ref[...] = _dot(ks_ref[0:tk, :], q3)

        def store_selection(taken_imp):
            bias = jnp.where(taken_imp == TAKEN, 0.0, MASKED)
            sel_ref[0:n_b, :] = jnp.concatenate([bias] * C_HPG, axis=1)

        if n_b < n_blk:
            sel_ref[n_b:, :] = jnp.full((n_blk - n_b, nq), MASKED, F32)
        quick = imp
        for _ in range(rounds):
            quick = jnp.where(quick == jnp.max(quick, axis=0, keepdims=True), TAKEN, quick)
        store_selection(quick)
        free = (jidx >= 1.0) & (jidx <= cur - 2.0)
        n_taken = jnp.sum(jnp.where(free & (quick == TAKEN), 1.0, 0.0), axis=0, keepdims=True)
        n_expected = jnp.minimum(jnp.maximum(cur - 2.0, 0.0), float(rounds))
        tie_in_top = jnp.max(jnp.abs(n_taken - n_expected)) > 0.0

        @pl.when(tie_in_top)
        def _():
            exact = imp
            for _ in range(rounds):
                mx = jnp.max(exact, axis=0, keepdims=True)
                first = jnp.min(jnp.where(exact == mx, jidx, float(n_blk)), axis=0, keepdims=True)
                exact = jnp.where(jidx == first, TAKEN, exact)
            store_selection(exact)

    early = q0 + tq <= T // 2
    pl.when(early)(lambda: prelude(n_cmp // 2, n_blk // 2))
    pl.when(jnp.logical_not(early))(lambda: prelude(n_cmp, n_blk))

    bpt = tk // SEL_BLOCK

    def block_bias(j):
        rows = sel_ref[pl.ds(pl.multiple_of(j * bpt, bpt), bpt), :]
        return jnp.concatenate([jnp.broadcast_to(rows[i:i + 1, :], (SEL_BLOCK, nq)) for i in range(bpt)], axis=0)

    def score(j):
        return _dot(ks_ref[pl.ds(pl.multiple_of(j * tk, tk), tk), :], q3) + block_bias(j)

    def first_scores():
        return raw0_ref[...] + block_bias(0)

    def value(j):
        return vs_ref[:, pl.ds(pl.multiple_of(j * tk, tk), tk)]

    o_slc = _causal_sweep(q0 // tk, score, tpos3, value, scratch, first_scores)

    gts = gt_ref[...]
    for hh in range(C_HPG):
        r = hh * N_BRANCH
        c = slice(hh * tq, (hh + 1) * tq)
        out = (ocw_ref[0:d, c] * gts[r:r + 1, :] + o_slc[:, c] * gts[r + 1:r + 2, :]
               + ocw_ref[d:, c] * gts[r + 2:r + 3, :])
        o_ref[hh * d:(hh + 1) * d, :] = out.astype(o_ref.dtype)


def _window_bias(tq):
    band, n_edge = WINDOW + tq, WINDOW // tq
    q0 = np.arange(n_edge + 1)[:, None, None] * tq
    wpos = np.maximum(q0 - WINDOW, 0) + np.arange(band)[None, :, None]
    tpos = q0 + np.arange(tq)[None, None, :]
    ok = (wpos <= tpos) & (wpos > tpos - WINDOW)
    return jnp.asarray(np.where(ok, 0.0, MASKED).astype(np.float32))


def _nsa(qcT, gtT, kc2, vcT, ovT, ksl, vslT, kwi, vwiT, tq, tk):
    B, _, T = qcT.shape
    n_cmp = kc2.shape[1]
    n_blk = T // SEL_BLOCK
    nq = C_HPG * tq
    assert tk % tq == 0, "the causal key tile must contain the whole query block"
    assert WINDOW % tq == 0 and T >= WINDOW + tq
    wb = _window_bias(tq)
    n_edge = WINDOW // tq
    kv_tok = pl.BlockSpec((None, T, LANES), lambda b, g, i: (b, 0, 0))
    kv_feat = pl.BlockSpec((None, C_HEAD_DIM, T), lambda b, g, i: (b, g, 0))
    return pl.pallas_call(
        functools.partial(_nsa_kernel, tq=tq, tk=tk, T=T),
        grid=(B, C_KV_GROUPS, T // tq),
        in_specs=[pl.BlockSpec((None, C_WIDTH, tq), lambda b, g, i: (b, 0, i)),
                  pl.BlockSpec((None, GATE_ROWS, tq), lambda b, g, i: (b, g, i)),
                  pl.BlockSpec((None, n_cmp, LANES), lambda b, g, i: (b, 0, 0)),
                  pl.BlockSpec((None, C_HEAD_DIM, n_cmp), lambda b, g, i: (b, g, 0)),
                  pl.BlockSpec((n_blk, n_cmp), lambda b, g, i: (0, 0)),
                  pl.BlockSpec((None, WINDOW + tq, tq), lambda b, g, i: (jnp.minimum(i, n_edge), 0, 0)),
                  kv_tok, kv_feat, kv_tok, kv_feat],
        out_specs=pl.BlockSpec((None, C_HPG * C_HEAD_DIM, tq), lambda b, g, i: (b, g, i)),
        out_shape=jax.ShapeDtypeStruct((B, C_WIDTH, T), BF16),
        scratch_shapes=([pltpu.VMEM((n_blk, nq), F32)] + _sweep_scratch(tk, nq, C_HEAD_DIM)
                        + [pltpu.VMEM((tk, nq), F32), pltpu.VMEM((2 * C_HEAD_DIM, nq), F32)]),
        compiler_params=_params(("parallel", "parallel", "arbitrary")),
        name="nsa_attn",
    )(qcT, gtT, kc2, vcT, ovT, wb, ksl, vslT, kwi, vwiT)


def _overlap_T(T):
    n_cmp = T // CMP_STRIDE
    n_blk = T // SEL_BLOCK
    cs = np.arange(n_cmp)[None, :] * CMP_STRIDE
    bs = np.arange(n_blk)[:, None] * SEL_BLOCK
    ov = (cs < bs + SEL_BLOCK) & (cs + CMP_BLOCK > bs) & (np.arange(n_cmp)[None, :] < n_cmp - 1)
    return jnp.asarray(ov.astype(np.float32), dtype=BF16)


def _layer_norm(y, g, b):
    mu = jnp.mean(y, axis=-1, keepdims=True)
    yc = y - mu
    var = jnp.mean(yc * yc, axis=-1, keepdims=True)
    return yc * lax.rsqrt(var + LN_EPS) * g + b


def _outproj_kernel(x_ref, oa_ref, uh_ref, u_ref, oc_ref, w_ref, wp_ref, ps_ref, g_ref, b_ref, o_ref, *, alpha, sub):
    i, tm = pl.program_id(1), x_ref.shape[0]
    tn = lambda a, w: lax.dot_general(a, w, (((0,), (0,)), ((), ())), preferred_element_type=F32)
    r1, r2 = A_WIDTH, A_WIDTH + POOL_WIDTH
    for r0 in range(0, tm, sub):
        rows = slice(r0, r0 + sub)
        if r0 == 0:
            halo = jnp.where(i == 0, jnp.zeros(uh_ref.shape, F32), uh_ref[...])
        else:
            halo = u_ref[r0 - POOL_HALO:r0, :]
        ob = _pool_mix(halo, u_ref[rows, :], wp_ref, ps_ref, i * tm + r0)
        mix = (tn(oa_ref[:, rows], w_ref[:r1, :]) + _dot(ob, w_ref[r1:r2, :]) + tn(oc_ref[:, rows], w_ref[r2:, :]))
        o_ref[rows, :] = _layer_norm(alpha * x_ref[rows, :] + mix, g_ref[...], b_ref[...])


def _outproj(x3, oaT, u, ocT, w, w_bd, p_scale, g, b, alpha, tm, sub, l):
    B, T, D = x3.shape
    tok = lambda n: pl.BlockSpec((None, tm, n), lambda bb, i: (bb, i, 0))
    feat = lambda n: pl.BlockSpec((None, n, tm), lambda bb, i: (bb, 0, i))
    halo = pl.BlockSpec((None, POOL_HALO, POOL_WIDTH),
                        lambda bb, i: (bb, jnp.maximum(i * (tm // POOL_HALO) - 1, 0), 0))
    return pl.pallas_call(
        functools.partial(_outproj_kernel, alpha=alpha, sub=sub),
        grid=(B, T // tm),
        in_specs=[tok(D), feat(oaT.shape[1]), halo, tok(POOL_WIDTH), feat(ocT.shape[1]),
                  _layer_spec(w, l), _layer_spec(w_bd, l), _layer_spec(p_scale, l),
                  _layer_spec(g, l), _layer_spec(b, l)],
        out_specs=tok(D),
        out_shape=jax.ShapeDtypeStruct((B, T, D), F32),
        compiler_params=_params(("parallel", "parallel")),
        name="outproj_ln",
    )(x3, oaT, u, u, ocT, w, w_bd, p_scale, g, b)


def _mlp_kernel(x_ref, wu_ref, wd_ref, g_ref, b_ref, o_ref, *, alpha, sub, tf):
    tm, F = x_ref.shape[0], wu_ref.shape[1]
    for r0 in range(0, tm, sub):
        x = x_ref[r0:r0 + sub, :]
        xb = x.astype(BF16)
        y = alpha * x
        for f0 in range(0, F, tf):
            hid = jnp.square(jnp.maximum(_dot(xb, wu_ref[:, f0:f0 + tf]), 0.0))
            y = y + _dot(hid.astype(BF16), wd_ref[f0:f0 + tf, :])
        o_ref[r0:r0 + sub, :] = _layer_norm(y, g_ref[...], b_ref[...])


def _mlp(x2, wu, wd, g, b, alpha, tm, sub, tf, l):
    BT, D = x2.shape
    once = lambda a: pl.BlockSpec((None,) + a.shape[1:], lambda i: (l, 0, 0), pipeline_mode=pl.Buffered(1))
    return pl.pallas_call(
        functools.partial(_mlp_kernel, alpha=alpha, sub=sub, tf=tf),
        grid=(BT // tm,),
        in_specs=[pl.BlockSpec((tm, D), lambda i: (i, 0)), once(wu), once(wd), _layer_spec(g, l), _layer_spec(b, l)],
        out_specs=pl.BlockSpec((tm, D), lambda i: (i, 0)),
        out_shape=jax.ShapeDtypeStruct((BT, D), F32),
        compiler_params=_params(("parallel",)),
        name="mlp_ln",
    )(x2, wu, wd, g, b)


def _tiles(T):
    return dict(
        tm_proj=min(512, T), tm_out=min(1024, T), sub_out=min(512, T), tm_mlp=min(1024, T), sub_mlp=min(512, T), tf_mlp=1024,
        a_tq=min(512, T), a_tk=min(512, T), c_tq=min(256, T), c_tk=min(512, T),
    )


def kernel(x, w_in, lam_q1, lam_k1, lam_q2, lam_k2, subln_g, w_pool, pool_scale, cmp_pe_k, cmp_pe_v,
           w_cmp_k, w_cmp_v, w_out, ln1_g, ln1_b, w_up, w_down, ln2_g, ln2_b):
    B, T, D = x.shape
    depth = w_in.shape[0]
    alpha = (2 * depth) ** 0.25
    tl = _tiles(T)
    tabs = _rope_tables(T)
    ovT = _overlap_T(T)
    w_tok, w_feat = _prep_w_in(w_in)
    w_bd, p_scale = _prep_pool(w_pool, pool_scale)
    pk2, pv2, wk2, wv2 = _prep_cmp(cmp_pe_k, cmp_pe_v, w_cmp_k, w_cmp_v)
    wo, wu, wd = w_out.astype(BF16), w_up.astype(BF16), w_down.astype(BF16)
    row = lambda a: a[:, None, :]
    lams = [row(a) for a in (lam_q1, lam_k1, lam_q2, lam_k2)]
    n16 = T // CMP_STRIDE
    x3 = x
    for l in range(depth):
        lam_init = 0.8 - 0.6 * math.exp(-0.3 * l)
        (ka, u, kcm, vcm, ksl, kwi, qaT, vaT, qcT, vslT, vwiT, gtT) = _inproj(x3, w_tok, w_feat, tabs, tl["tm_proj"], l)

        oaT = _diff_attn(qaT, ka, vaT, *lams, subln_g[:, :, None], lam_init, tl["a_tq"], tl["a_tk"], l)

        kc2, vc2 = _compress(kcm.reshape(B, n16, 16 * LANES), vcm.reshape(B, n16, 16 * LANES),
                             pk2, pv2, wk2, wv2, l)
        ocT = _nsa(qcT, gtT, kc2, vc2.transpose(0, 2, 1), ovT, ksl, vslT, kwi, vwiT, tl["c_tq"], tl["c_tk"])

        x3 = _outproj(x3, oaT, u, ocT, wo, w_bd, p_scale, row(ln1_g), row(ln1_b), alpha, tl["tm_out"], tl["sub_out"], l)
        x3 = _mlp(x3.reshape(B * T, D), wu, wd, row(ln2_g), row(ln2_b), alpha, tl["tm_mlp"], tl["sub_mlp"], tl["tf_mlp"],
                  l).reshape(B, T, D)
    return x3
```

```python
import functools
import math

import jax
import jax.numpy as jnp
import numpy as np
from jax import lax
from jax.experimental import pallas as pl
from jax.experimental.pallas import tpu as pltpu

A_HEADS = 6
A_QK_DIM = 32
A_V_DIM = 64
A_WIDTH = A_HEADS * A_V_DIM
POOL_WINDOWS = (2, 4, 8, 16)
POOL_GROUP_DIM = 64
POOL_WIDTH = 256
C_HEADS = 6
C_KV_GROUPS = 2
C_HPG = 3
C_HEAD_DIM = 64
C_WIDTH = C_HEADS * C_HEAD_DIM
CMP_BLOCK = 32
CMP_STRIDE = 16
SEL_BLOCK = 64
N_SEL = 16
WINDOW = 512
N_BRANCH = 3
FORCED_SCORE = 1.0e4
N_FORCED = 3
ROPE_THETA = 10000.0
LN_EPS = 1e-5
RMS_EPS = 1e-6

LANES = 128
LOG2E = 1.4426950408889634
NEG_BIG = -1e30
SWEEP_UNROLL = 4
A_BLOCKS = 4
C_BLOCKS = 2
TAKEN = -3.0e38
MASKED = -3e38
VMEM_LIMIT = 56 * 1024 * 1024

BF16 = jnp.bfloat16
F32 = jnp.float32


def _dot(a, b):
    return jnp.dot(a, b, preferred_element_type=F32)


def _params(sem, vmem=VMEM_LIMIT):
    return pltpu.CompilerParams(dimension_semantics=sem, vmem_limit_bytes=vmem)


_IN_SPLITS = (384, 384, 384, 256, 384, 128, 128, 128, 128, 128, 128, 18)
_OFFS = np.concatenate([[0], np.cumsum(_IN_SPLITS)]).astype(np.int64)
A_QSCALE = A_QK_DIM ** -0.5 * LOG2E
C_QSCALE = C_HEAD_DIM ** -0.5 * LOG2E
GATE_ROWS = 16
N_TOK = 384 + 256 + 4 * LANES
N_FEAT = 3 * 384 + 2 * LANES + C_KV_GROUPS * GATE_ROWS


def _prep_w_in(w):
    o = _OFFS
    L, D = w.shape[0], w.shape[1]
    seg = lambda i: w[:, :, o[i]:o[i + 1]]
    diff = lambda a: a.reshape(L, D, A_HEADS // 2, 2, 2, 2, A_QK_DIM // 2).transpose(0, 1, 2, 5, 3, 4, 6).reshape(L, D, -1)
    cq = lambda a: a.reshape(L, D, C_KV_GROUPS, C_HPG, 2, C_HEAD_DIM // 2).transpose(0, 1, 3, 4, 2, 5).reshape(L, D, -1)
    ck = lambda a: a.reshape(L, D, C_KV_GROUPS, 2, C_HEAD_DIM // 2).transpose(0, 1, 3, 2, 4).reshape(L, D, -1)
    w_tok = jnp.concatenate([diff(seg(1)), seg(3), ck(seg(5)), seg(6), ck(seg(7)), ck(seg(9))], axis=2)
    per = C_HPG * N_BRANCH
    gates = jnp.pad(seg(11).reshape(L, D, C_KV_GROUPS, per), ((0, 0), (0, 0), (0, 0), (0, GATE_ROWS - per)))
    w_feat = jnp.concatenate([diff(seg(0)), seg(2), cq(seg(4)), seg(8), seg(10),
                              gates.reshape(L, D, C_KV_GROUPS * GATE_ROWS)], axis=2)
    return w_tok.astype(BF16), jnp.swapaxes(w_feat, 1, 2).astype(BF16)


def _rope_tables(T):
    def tab(dim):
        half = dim // 2
        inv = 1.0 / (ROPE_THETA ** (jnp.arange(0, dim, 2, dtype=F32) / dim))
        ang = jnp.arange(T, dtype=F32)[:, None] * inv[None, :]
        lane = np.arange(LANES)
        j = lane % half
        sign = np.where(lane < 64, -1.0, 1.0).astype(np.float32)
        cos, sin = jnp.cos(ang), jnp.sin(ang)
        return cos[:, j], sin[:, j] * sign[None, :], cos[:, j[:64]].T, sin[:, j[:64]].T
    return tab(A_QK_DIM) + tab(C_HEAD_DIM)


def _inproj_kernel(x_ref, wt_ref, wf_ref, ca_ref, sa_ref, caT_ref, saT_ref, cc_ref, sc_ref, ccT_ref, scT_ref,
                   ka_ref, u_ref, kcm_ref, vcm_ref, ksl_ref, kwi_ref,
                   qa_ref, va_ref, qc_ref, vsl_ref, vwi_ref, gt_ref):
    xb = x_ref[...].astype(BF16)
    tok = _dot(xb, wt_ref[...])
    feat = lax.dot_general(wf_ref[...], xb, (((1,), (1,)), ((), ())), preferred_element_type=F32)

    def rope_tok(p, cos, sin):
        return p * cos + pltpu.roll(p, 64, axis=1) * sin

    def rope_feat(p, cos, sin):
        a, b = p[:64, :], p[64:, :]
        return jnp.concatenate([a * cos - b * sin, b * cos + a * sin], axis=0)

    ca, sa, cc, sc = ca_ref[...], sa_ref[...], cc_ref[...], sc_ref[...]
    caT, saT, ccT, scT = caT_ref[...], saT_ref[...], ccT_ref[...], scT_ref[...]
    for t in range(3):
        r = slice(t * LANES, (t + 1) * LANES)
        ka_ref[:, r] = rope_tok(tok[:, r], ca, sa).astype(BF16)
        qa_ref[r, :] = (rope_feat(feat[r, :], caT, saT) * A_QSCALE).astype(BF16)
        rq = slice(768 + t * LANES, 768 + (t + 1) * LANES)
        qc_ref[r, :] = (rope_feat(feat[rq, :], ccT, scT) * C_QSCALE).astype(BF16)
    u_ref[...] = tok[:, 384:640]
    kcm_ref[...] = rope_tok(tok[:, 640:768], cc, sc).astype(BF16)
    vcm_ref[...] = tok[:, 768:896].astype(BF16)
    ksl_ref[...] = rope_tok(tok[:, 896:1024], cc, sc).astype(BF16)
    kwi_ref[...] = rope_tok(tok[:, 1024:1152], cc, sc).astype(BF16)
    va_ref[...] = feat[384:768, :].astype(BF16)
    vsl_ref[...] = feat[1152:1280, :].astype(BF16)
    vwi_ref[...] = feat[1280:1408, :].astype(BF16)
    gt_ref[...] = jax.nn.sigmoid(feat[1408:, :])


def _layer_spec(a, l):
    return pl.BlockSpec((None,) + a.shape[1:], lambda *_: (l,) + (0,) * (a.ndim - 1))


def _inproj(x3, w_tok, w_feat, tabs, tm, l):
    B, T, D = x3.shape
    tok = lambda n: pl.BlockSpec((None, tm, n), lambda b, i: (b, i, 0))
    feat = lambda n: pl.BlockSpec((None, n, tm), lambda b, i: (b, 0, i))
    full = lambda a: _layer_spec(a, l)
    tab_tok = pl.BlockSpec((tm, LANES), lambda b, i: (i, 0))
    tab_feat = pl.BlockSpec((64, tm), lambda b, i: (0, i))
    tok_outs = [(384, BF16), (256, F32)] + [(LANES, BF16)] * 4
    feat_outs = [(384, BF16)] * 3 + [(LANES, BF16)] * 2 + [(C_KV_GROUPS * GATE_ROWS, F32)]
    return pl.pallas_call(
        _inproj_kernel,
        grid=(B, T // tm),
        in_specs=[tok(D), full(w_tok), full(w_feat)] + [tab_tok, tab_tok, tab_feat, tab_feat] * 2,
        out_specs=[tok(n) for n, _ in tok_outs] + [feat(n) for n, _ in feat_outs],
        out_shape=([jax.ShapeDtypeStruct((B, T, n), dt) for n, dt in tok_outs]
                   + [jax.ShapeDtypeStruct((B, n, T), dt) for n, dt in feat_outs]),
        compiler_params=_params(("parallel", "parallel")),
        name="inproj_rope",
    )(x3, w_tok, w_feat, *tabs)


ONES_ROWS = 16


def _sweep_scratch(tk, nq, dv, n_stats=1):
    stats = [pltpu.VMEM((1, nq), F32), pltpu.VMEM((dv + ONES_ROWS, nq), F32)]
    return [pltpu.VMEM((tk, nq), BF16), pltpu.VMEM((tk, nq), BF16),
            pltpu.VMEM((1, nq), F32), pltpu.VMEM((1, nq), F32)] + stats * n_stats


class _Sweep:
    def __init__(self, score, qpos, value, bufs, stats):
        self.score, self.qpos, self.value = score, qpos, value
        self.s_refs, self.cm_refs = bufs[0:2], bufs[2:4]
        self.m_ref, self.acc_ref = stats
        self.tk, self.nq = self.s_refs[0].shape
        self.dv = self.acc_ref.shape[0] - ONES_ROWS

    def init(self):
        self.m_ref[...] = jnp.full(self.m_ref.shape, NEG_BIG, F32)
        self.acc_ref[...] = jnp.zeros(self.acc_ref.shape, F32)

    def scores(self, j, slot, fn=None):
        s = self.score(j) if fn is None else fn()
        self.s_refs[slot][...] = s.astype(BF16)
        self.cm_refs[slot][...] = jnp.max(s, axis=0, keepdims=True)

    def softmax(self, slot, j, causal=False):
        s = self.s_refs[slot][...]
        if causal:
            kpos = j * self.tk + lax.broadcasted_iota(jnp.int32, (self.tk, self.nq), 0)
            s = jnp.where(kpos <= self.qpos, s, jnp.asarray(MASKED, s.dtype))
            cm = jnp.max(s, axis=0, keepdims=True).astype(F32)
        else:
            cm = self.cm_refs[slot][...]
        m = self.m_ref[...]
        m_new = jnp.maximum(m, cm)
        alpha = jnp.exp2(m - m_new)
        p = jnp.exp2(s - m_new.astype(s.dtype))
        self.m_ref[...] = m_new
        v = self.value(j)
        v1 = jnp.concatenate([v, jnp.ones((ONES_ROWS, v.shape[1]), v.dtype)], axis=0)
        self.acc_ref[...] = self.acc_ref[...] * alpha + _dot(v1, p)

    def full_tiles(self, count, par):
        def group(base, tiles):
            for u in range(tiles):
                self.scores(base + u + 1, (par + u + 1) % 2)
                self.softmax((par + u) % 2, base + u)

        def quad(i, c):
            group(SWEEP_UNROLL * i, SWEEP_UNROLL)
            return c

        def pair(i, c):
            group(done + 2 * i, 2)
            return c

        lax.fori_loop(0, count // SWEEP_UNROLL, quad, 0)
        done = (count // SWEEP_UNROLL) * SWEEP_UNROLL
        lax.fori_loop(0, (count - done) // 2, pair, 0)

    def result(self):
        return self.acc_ref[:self.dv, :] / self.acc_ref[self.dv:self.dv + 1, :]


def _causal_sweep(n_full, score, qpos, value, scratch, first_scores=None):
    sw = _Sweep(score, qpos, value, tuple(scratch[0:4]), tuple(scratch[4:6]))
    sw.init()
    sw.scores(0, 0, first_scores)
    sw.full_tiles(n_full - n_full % 2, 0)

    @pl.when(n_full % 2 == 0)
    def _():
        sw.softmax(0, n_full, causal=True)

    @pl.when(n_full % 2 == 1)
    def _():
        sw.scores(n_full, 1)
        sw.softmax(0, n_full - 1)
        sw.softmax(1, n_full, causal=True)

    return sw.result()


def _diff_attn_kernel(lq1_ref, lk1_ref, lq2_ref, lk2_ref, g_ref, q_ref, k_ref, v_ref, o_ref,
                      *scratch, tq, tk, lam_init):
    h = pl.program_id(1)
    step = pl.program_id(2)
    hl = h % 2
    nq = 2 * tq
    grp = (lax.broadcasted_iota(jnp.int32, (LANES, tq), 0) % 64) // 16
    lane_q = lax.broadcasted_iota(jnp.int32, (1, nq), 1) % tq

    def value(j):
        return v_ref[:, pl.ds(pl.multiple_of(j * tk, tk), tk)]

    def sweep(blk, stats):
        q = q_ref[:, blk * tq:(blk + 1) * tq]
        zero = jnp.zeros_like(q)
        q12 = jnp.concatenate([jnp.where(grp == 2 * hl, q, zero), jnp.where(grp == 2 * hl + 1, q, zero)], axis=1)
        qpos = (A_BLOCKS * step + blk) * tq + lane_q
        score = lambda j: _dot(k_ref[pl.ds(pl.multiple_of(j * tk, tk), tk), :], q12)
        return _Sweep(score, qpos, value, scratch[0:4], stats)

    dotp = lambda a, b: jnp.sum(a[...] * b[...], axis=1, keepdims=True)
    lam = jnp.exp(dotp(lq1_ref, lk1_ref)) - jnp.exp(dotp(lq2_ref, lk2_ref)) + lam_init

    def finish(sw, blk):
        on = sw.result()
        o = on[:, :tq] - lam * on[:, tq:]
        ms = jnp.mean(o * o, axis=0, keepdims=True)
        o = o * lax.rsqrt(ms + RMS_EPS) * g_ref[...] * (1.0 - lam_init)
        o_ref[:, blk * tq:(blk + 1) * tq] = o.astype(o_ref.dtype)

    sweeps = [sweep(blk, scratch[4 + 2 * blk:6 + 2 * blk]) for blk in range(A_BLOCKS)]
    first = 0
    sweeps[0].init()
    sweeps[0].scores(0, first)
    for blk, sw in enumerate(sweeps):
        n = A_BLOCKS * step + blk
        if blk % 2 == 0:
            sw.full_tiles(n, first)
            causal_buf = first
        else:
            sw.full_tiles(n - 1, first)
            sw.scores(n, 1 - first)
            sw.softmax(first, n - 1)
            causal_buf = 1 - first
        first = 1 - causal_buf
        if blk + 1 < A_BLOCKS:
            sweeps[blk + 1].init()
            sweeps[blk + 1].scores(0, first)
        sw.softmax(causal_buf, n, causal=True)
        finish(sw, blk)


def _diff_attn(qaT, ka, vaT, lq1, lk1, lq2, lk2, g_col, lam_init, tq, tk, l):
    B, _, T = qaT.shape
    assert tq == tk and T % (A_BLOCKS * tq) == 0, "query block i ends at key tile i; A_BLOCKS blocks per step"
    nq = 2 * tq
    vec = _layer_spec(lq1, l)
    return pl.pallas_call(
        functools.partial(_diff_attn_kernel, tq=tq, tk=tk, lam_init=lam_init),
        grid=(B, A_HEADS, T // (A_BLOCKS * tq)),
        in_specs=[vec, vec, vec, vec, _layer_spec(g_col, l),
                  pl.BlockSpec((None, LANES, A_BLOCKS * tq), lambda b, h, i: (b, h // 2, i)),
                  pl.BlockSpec((None, T, LANES), lambda b, h, i: (b, 0, h // 2)),
                  pl.BlockSpec((None, A_V_DIM, T), lambda b, h, i: (b, h, 0))],
        out_specs=pl.BlockSpec((None, A_V_DIM, A_BLOCKS * tq), lambda b, h, i: (b, h, i)),
        out_shape=jax.ShapeDtypeStruct((B, A_WIDTH, T), BF16),
        scratch_shapes=_sweep_scratch(tk, nq, A_V_DIM, n_stats=A_BLOCKS),
        compiler_params=_params(("parallel", "parallel", "arbitrary")),
        name="diff_attn",
    )(lq1, lk1, lq2, lk2, g_col, qaT, ka, vaT)


POOL_HALO = 16


def _pool_mix(halo, cur, w_ref, scale_ref, t0):
    tm = cur.shape[0]
    arr = jnp.concatenate([halo, cur], axis=0)
    s2 = arr + pltpu.roll(arr, 1, axis=0)
    s4 = s2 + pltpu.roll(s2, 2, axis=0)
    s8 = s4 + pltpu.roll(s4, 4, axis=0)
    s16 = s8 + pltpu.roll(s8, 8, axis=0)
    sums = [s[POOL_HALO:, :] for s in (s2, s4, s8, s16)]
    lane = lax.broadcasted_iota(jnp.int32, (tm, POOL_WIDTH), 1)
    tpos = t0 + lax.broadcasted_iota(jnp.int32, (tm, POOL_WIDTH), 0)
    grp = lane // POOL_GROUP_DIM
    win = jnp.where(grp == 0, sums[0], jnp.where(grp == 1, sums[1], jnp.where(grp == 2, sums[2], sums[3])))
    width = jnp.where(grp == 0, 2, jnp.where(grp == 1, 4, jnp.where(grp == 2, 8, 16)))
    cnt = jnp.minimum(tpos + 1, width).astype(F32)
    y = win / cnt - cur
    z = _dot(y.astype(BF16), w_ref[...])
    return (z * scale_ref[...]).astype(BF16)


def _prep_pool(w_pool, pool_scale):
    depth, G, C, _ = w_pool.shape
    eye = jnp.asarray(np.eye(G, dtype=np.float32))
    w_bd = (w_pool[:, :, :, None, :] * eye[None, :, None, :, None]).reshape(depth, G * C, G * C)
    return w_bd.astype(BF16), pool_scale[:, None, :]


def _cmp_kernel(k_ref, v_ref, pk_ref, pv_ref, wk_ref, wv_ref, kc_ref, vc_ref):
    n = k_ref.shape[0]
    rowid = lax.broadcasted_iota(jnp.int32, (n, LANES), 0)

    def compress(x_ref, pe_ref, w_ref):
        x = x_ref[...].astype(F32)
        lo = _dot((x + pe_ref[0:1, :]).astype(BF16), w_ref[0])
        hi = _dot((x + pe_ref[1:2, :]).astype(BF16), w_ref[1])
        out = lo + pltpu.roll(hi, n - 1, axis=0)
        return jnp.where(rowid < n - 1, out, 0.0)

    kc_ref[...] = compress(k_ref, pk_ref, wk_ref).astype(kc_ref.dtype)
    vc_ref[...] = compress(v_ref, pv_ref, wv_ref).astype(vc_ref.dtype)


def _compress(kr, vr, pe_k2, pe_v2, wk2, wv2, l):
    B, n, W = kr.shape
    blk = pl.BlockSpec((None, n, W), lambda b: (b, 0, 0))
    pe = _layer_spec(pe_k2, l)
    wsp = _layer_spec(wk2, l)
    osp = pl.BlockSpec((None, n, LANES), lambda b: (b, 0, 0))
    return pl.pallas_call(
        _cmp_kernel,
        grid=(B,),
        in_specs=[blk, blk, pe, pe, wsp, wsp],
        out_specs=[osp, osp],
        out_shape=[jax.ShapeDtypeStruct((B, n, LANES), BF16)] * 2,
        compiler_params=_params(("parallel",)),
        name="nsa_compress",
    )(kr, vr, pe_k2, pe_v2, wk2, wv2)


def _prep_cmp(pe_k, pe_v, w_ck, w_cv):
    L, G, d, h = w_ck.shape[0], C_KV_GROUPS, C_HEAD_DIM, C_HEAD_DIM // 2
    same = jnp.asarray(np.eye(G, dtype=np.float32))
    shape2 = (L, 2, 16 * LANES, G * d)
    wk = w_ck.reshape(L, CMP_BLOCK, 2, 1, h, 2, 1, h) * same[None, None, None, :, None, None, :, None]
    pk2 = jnp.broadcast_to(pe_k.reshape(L, CMP_BLOCK, 2, 1, h), (L, CMP_BLOCK, 2, G, h))
    wv = w_cv.reshape(L, CMP_BLOCK, 1, d, 1, d) * same[None, None, :, None, :, None]
    pv2 = jnp.broadcast_to(pe_v.reshape(L, CMP_BLOCK, 1, d), (L, CMP_BLOCK, G, d))
    flat = lambda pe: pe.reshape(L, 2, 16 * LANES)
    return flat(pk2), flat(pv2), wk.reshape(shape2).astype(BF16), wv.reshape(shape2).astype(BF16)


def _nsa_kernel(q_ref, gt_ref, kc_ref, vc_ref, ov_ref, *refs, tq, tk, T):
    wb_refs, (ks_ref, vs_ref, kw_ref, vw_ref, o_ref) = refs[:C_BLOCKS], refs[C_BLOCKS:C_BLOCKS + 5]
    scratch = refs[C_BLOCKS + 5:]
    sweep_bufs, per_block = scratch[0:4], scratch[4:]
    g = pl.program_id(1)
    step = pl.program_id(2)
    n_cmp = kc_ref.shape[0]
    n_blk = T // SEL_BLOCK
    d = C_HEAD_DIM
    nq = C_HPG * tq
    bpt = tk // SEL_BLOCK
    rowg = (lax.broadcasted_iota(jnp.int32, (LANES, tq), 0) % 64) // 32

    class Block:
        def __init__(self, blk):
            self.cols = slice(blk * tq, (blk + 1) * tq)
            self.q0 = (C_BLOCKS * step + blk) * tq
            self.tpos = self.q0 + lax.broadcasted_iota(jnp.int32, (1, tq), 1)
            self.tpos3 = self.q0 + lax.broadcasted_iota(jnp.int32, (1, nq), 1) % tq
            self.cur = (self.tpos // SEL_BLOCK).astype(F32)
            qs = []
            for hh in range(C_HPG):
                qt = q_ref[hh * LANES:(hh + 1) * LANES, self.cols]
                qs.append(jnp.where(rowg == g, qt, jnp.zeros_like(qt)))
            self.q3 = jnp.concatenate(qs, axis=1)
            self.wb_ref = wb_refs[blk]
            self.sel_ref, m_ref, acc_ref, self.raw0_ref, self.ocw_ref = per_block[5 * blk:5 * blk + 5]
            self.stats = (m_ref, acc_ref)

    blocks = [Block(blk) for blk in range(C_BLOCKS)]

    def prelude(bk, n_c, n_b):
        q3, tpos3, cur, q0 = bk.q3, bk.tpos3, bk.cur, bk.q0
        cend = lax.broadcasted_iota(jnp.int32, (n_c, nq), 0) * CMP_STRIDE + (CMP_BLOCK - 1)
        s = jnp.where(cend <= tpos3, _dot(kc_ref[0:n_c, :], q3), MASKED)
        m = jnp.maximum(jnp.max(s, axis=0, keepdims=True), NEG_BIG)
        e = jnp.exp2(s - m)
        l = jnp.sum(e, axis=0, keepdims=True)
        p = e / jnp.where(l > 0.0, l, 1.0)
        bk.ocw_ref[0:d, :] = _dot(vc_ref[:, 0:n_c], p.astype(BF16))
        psum = p[:, :tq] + p[:, tq:2 * tq] + p[:, 2 * tq:]

        p_hi = psum.astype(BF16)
        p_lo = (psum - p_hi.astype(F32)).astype(BF16)
        ov = ov_ref[0:n_b, 0:n_c]
        imp = _dot(ov, p_hi) + _dot(ov, p_lo)
        jidx = lax.broadcasted_iota(jnp.int32, (n_b, tq), 0).astype(F32)
        forced = (jidx == 0.0) | (jidx == cur) | (jidx == cur - 1.0)
        imp = jnp.where(jidx > cur, -1.0, imp)
        imp = jnp.where(forced, TAKEN, imp)
        rounds = min(N_SEL, n_blk) - N_FORCED

        band = WINDOW + tq
        w0 = pl.multiple_of(jnp.maximum(q0 - WINDOW, 0), LANES)
        s = _dot(kw_ref[pl.ds(w0, band), :], q3) + jnp.concatenate([bk.wb_ref[...]] * C_HPG, axis=1)
        e = jnp.exp2(s - jnp.max(s, axis=0, keepdims=True))
        bk.ocw_ref[d:, :] = _dot(vw_ref[:, pl.ds(w0, band)], e.astype(BF16)) / jnp.sum(e, axis=0, keepdims=True)

        bk.raw0_ref[...] = _dot(ks_ref[0:tk, :], q3)

        def store_selection(taken_imp):
            bias = jnp.where(taken_imp == TAKEN, 0.0, MASKED)
            bk.sel_ref[0:n_b, :] = jnp.concatenate([bias] * C_HPG, axis=1)

        if n_b < n_blk:
            bk.sel_ref[n_b:, :] = jnp.full((n_blk - n_b, nq), MASKED, F32)
        quick = imp
        for _ in range(rounds):
            quick = jnp.where(quick == jnp.max(quick, axis=0, keepdims=True), TAKEN, quick)
        store_selection(quick)
        free = (jidx >= 1.0) & (jidx <= cur - 2.0)
        n_taken = jnp.sum(jnp.where(free & (quick == TAKEN), 1.0, 0.0), axis=0, keepdims=True)
        n_expected = jnp.minimum(jnp.maximum(cur - 2.0, 0.0), float(rounds))
        tie = jnp.max(jnp.abs(n_taken - n_expected)) > 0.0

        def exact_rounds():
            exact = imp
            for _ in range(rounds):
                mx = jnp.max(exact, axis=0, keepdims=True)
                first = jnp.min(jnp.where(exact == mx, jidx, float(n_blk)), axis=0, keepdims=True)
                exact = jnp.where(jidx == first, TAKEN, exact)
            store_selection(exact)

        return tie, exact_rounds

    def preludes(n_c, n_b):
        pending = [prelude(bk, n_c, n_b) for bk in blocks]
        for tie, exact_rounds in pending:
            pl.when(tie)(exact_rounds)

    early = (C_BLOCKS * step + C_BLOCKS) * tq <= T // 2
    pl.when(early)(lambda: preludes(n_cmp // 2, n_blk // 2))
    pl.when(jnp.logical_not(early))(lambda: preludes(n_cmp, n_blk))

    gts = gt_ref[...]
    for bk in blocks:
        def block_bias(j, bk=bk):
            rows = bk.sel_ref[pl.ds(pl.multiple_of(j * bpt, bpt), bpt), :]
            return jnp.concatenate([jnp.broadcast_to(rows[i:i + 1, :], (SEL_BLOCK, nq)) for i in range(bpt)], axis=0)

        def score(j, bk=bk, block_bias=block_bias):
            return _dot(ks_ref[pl.ds(pl.multiple_of(j * tk, tk), tk), :], bk.q3) + block_bias(j)

        def first_scores(bk=bk, block_bias=block_bias):
            return bk.raw0_ref[...] + block_bias(0)

        def value(j):
            return vs_ref[:, pl.ds(pl.multiple_of(j * tk, tk), tk)]

        o_slc = _causal_sweep(bk.q0 // tk, score, bk.tpos3, value, sweep_bufs + bk.stats, first_scores)
        for hh in range(C_HPG):
            r = hh * N_BRANCH
            c = slice(hh * tq, (hh + 1) * tq)
            gc = gts[:, bk.cols]
            out = (bk.ocw_ref[0:d, c] * gc[r:r + 1, :] + o_slc[:, c] * gc[r + 1:r + 2, :]
                   + bk.ocw_ref[d:, c] * gc[r + 2:r + 3, :])
            o_ref[hh * d:(hh + 1) * d, bk.cols] = out.astype(o_ref.dtype)


def _window_bias(tq):
    band, n_edge = WINDOW + tq, WINDOW // tq
    q0 = np.arange(n_edge + 1)[:, None, None] * tq
    wpos = np.maximum(q0 - WINDOW, 0) + np.arange(band)[None, :, None]
    tpos = q0 + np.arange(tq)[None, None, :]
    ok = (wpos <= tpos) & (wpos > tpos - WINDOW)
    return jnp.asarray(np.where(ok, 0.0, MASKED).astype(np.float32))


def _nsa(qcT, gtT, kc2, vcT, ovT, ksl, vslT, kwi, vwiT, tq, tk):
    B, _, T = qcT.shape
    n_cmp = kc2.shape[1]
    n_blk = T // SEL_BLOCK
    nq = C_HPG * tq
    tqs = C_BLOCKS * tq
    assert tk % tqs == 0 or tqs % tk == 0, "a step's blocks must not straddle a partial key tile boundary"
    assert tk % tq == 0, "the causal key tile must contain the whole query block"
    assert WINDOW % tq == 0 and T >= WINDOW + tq and (T // 2) % tqs == 0
    wb = _window_bias(tq)
    n_edge = WINDOW // tq
    kv_tok = pl.BlockSpec((None, T, LANES), lambda b, g, i: (b, 0, 0))
    kv_feat = pl.BlockSpec((None, C_HEAD_DIM, T), lambda b, g, i: (b, g, 0))
    wb_spec = lambda blk: pl.BlockSpec((None, WINDOW + tq, tq),
                                       lambda b, g, i: (jnp.minimum(C_BLOCKS * i + blk, n_edge), 0, 0))
    sweep = _sweep_scratch(tk, nq, C_HEAD_DIM)
    per_block = [pltpu.VMEM((n_blk, nq), F32)] + sweep[4:6] + [pltpu.VMEM((tk, nq), F32),
                                                              pltpu.VMEM((2 * C_HEAD_DIM, nq), F32)]
    return pl.pallas_call(
        functools.partial(_nsa_kernel, tq=tq, tk=tk, T=T),
        grid=(B, C_KV_GROUPS, T // tqs),
        in_specs=[pl.BlockSpec((None, C_WIDTH, tqs), lambda b, g, i: (b, 0, i)),
                  pl.BlockSpec((None, GATE_ROWS, tqs), lambda b, g, i: (b, g, i)),
                  pl.BlockSpec((None, n_cmp, LANES), lambda b, g, i: (b, 0, 0)),
                  pl.BlockSpec((None, C_HEAD_DIM, n_cmp), lambda b, g, i: (b, g, 0)),
                  pl.BlockSpec((n_blk, n_cmp), lambda b, g, i: (0, 0))]
                 + [wb_spec(blk) for blk in range(C_BLOCKS)] + [kv_tok, kv_feat, kv_tok, kv_feat],
        out_specs=pl.BlockSpec((None, C_HPG * C_HEAD_DIM, tqs), lambda b, g, i: (b, g, i)),
        out_shape=jax.ShapeDtypeStruct((B, C_WIDTH, T), BF16),
        scratch_shapes=sweep[0:4] + per_block * C_BLOCKS,
        compiler_params=_params(("parallel", "parallel", "arbitrary")),
        name="nsa_attn",
    )(qcT, gtT, kc2, vcT, ovT, *([wb] * C_BLOCKS), ksl, vslT, kwi, vwiT)


def _overlap_T(T):
    n_cmp = T // CMP_STRIDE
    n_blk = T // SEL_BLOCK
    cs = np.arange(n_cmp)[None, :] * CMP_STRIDE
    bs = np.arange(n_blk)[:, None] * SEL_BLOCK
    ov = (cs < bs + SEL_BLOCK) & (cs + CMP_BLOCK > bs) & (np.arange(n_cmp)[None, :] < n_cmp - 1)
    return jnp.asarray(ov.astype(np.float32), dtype=BF16)


def _layer_norm(y, g, b):
    mu = jnp.mean(y, axis=-1, keepdims=True)
    yc = y - mu
    var = jnp.mean(yc * yc, axis=-1, keepdims=True)
    return yc * lax.rsqrt(var + LN_EPS) * g + b


def _outproj_kernel(x_ref, oa_ref, uh_ref, u_ref, oc_ref, w_ref, wp_ref, ps_ref, g_ref, b_ref, o_ref, *, alpha, sub):
    i, tm = pl.program_id(1), x_ref.shape[0]
    tn = lambda a, w: lax.dot_general(a, w, (((0,), (0,)), ((), ())), preferred_element_type=F32)
    r1, r2 = A_WIDTH, A_WIDTH + POOL_WIDTH
    for r0 in range(0, tm, sub):
        rows = slice(r0, r0 + sub)
        if r0 == 0:
            halo = jnp.where(i == 0, jnp.zeros(uh_ref.shape, F32), uh_ref[...])
        else:
            halo = u_ref[r0 - POOL_HALO:r0, :]
        ob = _pool_mix(halo, u_ref[rows, :], wp_ref, ps_ref, i * tm + r0)
        mix = (tn(oa_ref[:, rows], w_ref[:r1, :]) + _dot(ob, w_ref[r1:r2, :]) + tn(oc_ref[:, rows], w_ref[r2:, :]))
        o_ref[rows, :] = _layer_norm(alpha * x_ref[rows, :] + mix, g_ref[...], b_ref[...])


def _outproj(x3, oaT, u, ocT, w, w_bd, p_scale, g, b, alpha, tm, sub, l):
    B, T, D = x3.shape
    tok = lambda n: pl.BlockSpec((None, tm, n), lambda bb, i: (bb, i, 0))
    feat = lambda n: pl.BlockSpec((None, n, tm), lambda bb, i: (bb, 0, i))
    halo = pl.BlockSpec((None, POOL_HALO, POOL_WIDTH),
                        lambda bb, i: (bb, jnp.maximum(i * (tm // POOL_HALO) - 1, 0), 0))
    return pl.pallas_call(
        functools.partial(_outproj_kernel, alpha=alpha, sub=sub),
        grid=(B, T // tm),
        in_specs=[tok(D), feat(oaT.shape[1]), halo, tok(POOL_WIDTH), feat(ocT.shape[1]),
                  _layer_spec(w, l), _layer_spec(w_bd, l), _layer_spec(p_scale, l),
                  _layer_spec(g, l), _layer_spec(b, l)],
        out_specs=tok(D),
        out_shape=jax.ShapeDtypeStruct((B, T, D), F32),
        compiler_params=_params(("parallel", "parallel")),
        name="outproj_ln",
    )(x3, oaT, u, u, ocT, w, w_bd, p_scale, g, b)


def _mlp_kernel(x_ref, wu_ref, wd_ref, g_ref, b_ref, o_ref, *, alpha, sub, tf):
    tm, F = x_ref.shape[0], wu_ref.shape[1]
    for r0 in range(0, tm, sub):
        x = x_ref[r0:r0 + sub, :]
        xb = x.astype(BF16)
        y = alpha * x
        for f0 in range(0, F, tf):
            hid = jnp.square(jnp.maximum(_dot(xb, wu_ref[:, f0:f0 + tf]), 0.0))
            y = y + _dot(hid.astype(BF16), wd_ref[f0:f0 + tf, :])
        o_ref[r0:r0 + sub, :] = _layer_norm(y, g_ref[...], b_ref[...])


def _mlp(x2, wu, wd, g, b, alpha, tm, sub, tf, l):
    BT, D = x2.shape
    once = lambda a: pl.BlockSpec((None,) + a.shape[1:], lambda i: (l, 0, 0), pipeline_mode=pl.Buffered(1))
    return pl.pallas_call(
        functools.partial(_mlp_kernel, alpha=alpha, sub=sub, tf=tf),
        grid=(BT // tm,),
        in_specs=[pl.BlockSpec((tm, D), lambda i: (i, 0)), once(wu), once(wd), _layer_spec(g, l), _layer_spec(b, l)],
        out_specs=pl.BlockSpec((tm, D), lambda i: (i, 0)),
        out_shape=jax.ShapeDtypeStruct((BT, D), F32),
        compiler_params=_params(("parallel",)),
        name="mlp_ln",
    )(x2, wu, wd, g, b)


def _tiles(T):
    return dict(
        tm_proj=min(512, T), tm_out=min(1024, T), sub_out=min(512, T), tm_mlp=min(1024, T), sub_mlp=min(512, T), tf_mlp=1024,
        a_tq=min(512, T), a_tk=min(512, T), c_tq=min(256, T), c_tk=min(512, T),
    )


def kernel(x, w_in, lam_q1, lam_k1, lam_q2, lam_k2, subln_g, w_pool, pool_scale, cmp_pe_k, cmp_pe_v,
           w_cmp_k, w_cmp_v, w_out, ln1_g, ln1_b, w_up, w_down, ln2_g, ln2_b):
    B, T, D = x.shape
    depth = w_in.shape[0]
    alpha = (2 * depth) ** 0.25
    tl = _tiles(T)
    tabs = _rope_tables(T)
    ovT = _overlap_T(T)
    w_tok, w_feat = _prep_w_in(w_in)
    w_bd, p_scale = _prep_pool(w_pool, pool_scale)
    pk2, pv2, wk2, wv2 = _prep_cmp(cmp_pe_k, cmp_pe_v, w_cmp_k, w_cmp_v)
    wo, wu, wd = w_out.astype(BF16), w_up.astype(BF16), w_down.astype(BF16)
    row = lambda a: a[:, None, :]
    lams = [row(a) for a in (lam_q1, lam_k1, lam_q2, lam_k2)]
    n16 = T // CMP_STRIDE
    x3 = x
    for l in range(depth):
        lam_init = 0.8 - 0.6 * math.exp(-0.3 * l)
        (ka, u, kcm, vcm, ksl, kwi, qaT, vaT, qcT, vslT, vwiT, gtT) = _inproj(x3, w_tok, w_feat, tabs, tl["tm_proj"], l)

        oaT = _diff_attn(qaT, ka, vaT, *lams, subln_g[:, :, None], lam_init, tl["a_tq"], tl["a_tk"], l)

        kc2, vc2 = _compress(kcm.reshape(B, n16, 16 * LANES), vcm.reshape(B, n16, 16 * LANES),
                             pk2, pv2, wk2, wv2, l)
        ocT = _nsa(qcT, gtT, kc2, vc2.transpose(0, 2, 1), ovT, ksl, vslT, kwi, vwiT, tl["c_tq"], tl["c_tk"])

        x3 = _outproj(x3, oaT, u, ocT, wo, w_bd, p_scale, row(ln1_g), row(ln1_b), alpha, tl["tm_out"], tl["sub_out"], l)
        x3 = _mlp(x3.reshape(B * T, D), wu, wd, row(ln2_g), row(ln2_b), alpha, tl["tm_mlp"], tl["sub_mlp"], tl["tf_mlp"],
                  l).reshape(B, T, D)
    return x3
```

```python
import functools
import math

import jax
import jax.numpy as jnp
import numpy as np
from jax import lax
from jax.experimental import pallas as pl
from jax.experimental.pallas import tpu as pltpu

A_HEADS = 6
A_QK_DIM = 32
A_V_DIM = 64
A_WIDTH = A_HEADS * A_V_DIM
POOL_WINDOWS = (2, 4, 8, 16)
POOL_GROUP_DIM = 64
POOL_WIDTH = 256
C_HEADS = 6
C_KV_GROUPS = 2
C_HPG = 3
C_HEAD_DIM = 64
C_WIDTH = C_HEADS * C_HEAD_DIM
CMP_BLOCK = 32
CMP_STRIDE = 16
SEL_BLOCK = 64
N_SEL = 16
WINDOW = 512
N_BRANCH = 3
FORCED_SCORE = 1.0e4
N_FORCED = 3
ROPE_THETA = 10000.0
LN_EPS = 1e-5
RMS_EPS = 1e-6

LANES = 128
LOG2E = 1.4426950408889634
NEG_BIG = -1e30
SWEEP_UNROLL = 4
MXU_TILE = 256
A_BLOCKS = 4
C_BLOCKS = 2
TAKEN = -3.0e38
MASKED = -3e38
VMEM_LIMIT = 56 * 1024 * 1024

BF16 = jnp.bfloat16
F32 = jnp.float32


def _dot(a, b):
    return jnp.dot(a, b, preferred_element_type=F32)


def _params(sem, vmem=VMEM_LIMIT):
    return pltpu.CompilerParams(dimension_semantics=sem, vmem_limit_bytes=vmem)


_IN_SPLITS = (384, 384, 384, 256, 384, 128, 128, 128, 128, 128, 128, 18)
_OFFS = np.concatenate([[0], np.cumsum(_IN_SPLITS)]).astype(np.int64)
A_QSCALE = A_QK_DIM ** -0.5 * LOG2E
C_QSCALE = C_HEAD_DIM ** -0.5 * LOG2E
GATE_ROWS = 16
N_TOK = 384 + 256 + 4 * LANES
N_FEAT = 3 * 384 + 2 * LANES + C_KV_GROUPS * GATE_ROWS


def _prep_w_in(w):
    o = _OFFS
    L, D = w.shape[0], w.shape[1]
    seg = lambda i: w[:, :, o[i]:o[i + 1]]
    diff = lambda a: a.reshape(L, D, A_HEADS // 2, 2, 2, 2, A_QK_DIM // 2).transpose(0, 1, 2, 5, 3, 4, 6).reshape(L, D, -1)
    cq = lambda a: a.reshape(L, D, C_KV_GROUPS, C_HPG, 2, C_HEAD_DIM // 2).transpose(0, 1, 3, 4, 2, 5).reshape(L, D, -1)
    ck = lambda a: a.reshape(L, D, C_KV_GROUPS, 2, C_HEAD_DIM // 2).transpose(0, 1, 3, 2, 4).reshape(L, D, -1)
    w_tok = jnp.concatenate([diff(seg(1)), seg(3), ck(seg(5)), seg(6), ck(seg(7)), ck(seg(9))], axis=2)
    per = C_HPG * N_BRANCH
    gates = jnp.pad(seg(11).reshape(L, D, C_KV_GROUPS, per), ((0, 0), (0, 0), (0, 0), (0, GATE_ROWS - per)))
    w_feat = jnp.concatenate([diff(seg(0)), seg(2), cq(seg(4)), seg(8), seg(10),
                              gates.reshape(L, D, C_KV_GROUPS * GATE_ROWS)], axis=2)
    return w_tok.astype(BF16), jnp.swapaxes(w_feat, 1, 2).astype(BF16)


def _rope_tables(T):
    def tab(dim):
        half = dim // 2
        inv = 1.0 / (ROPE_THETA ** (jnp.arange(0, dim, 2, dtype=F32) / dim))
        ang = jnp.arange(T, dtype=F32)[:, None] * inv[None, :]
        lane = np.arange(LANES)
        j = lane % half
        sign = np.where(lane < 64, -1.0, 1.0).astype(np.float32)
        cos, sin = jnp.cos(ang), jnp.sin(ang)
        return cos[:, j], sin[:, j] * sign[None, :], cos[:, j[:64]].T, sin[:, j[:64]].T
    return tab(A_QK_DIM) + tab(C_HEAD_DIM)


def _inproj_kernel(x_ref, wt_ref, wf_ref, ca_ref, sa_ref, caT_ref, saT_ref, cc_ref, sc_ref, ccT_ref, scT_ref,
                   ka_ref, u_ref, kcm_ref, vcm_ref, ksl_ref, kwi_ref,
                   qa_ref, va_ref, qc_ref, vsl_ref, vwi_ref, gt_ref):
    xb = x_ref[...].astype(BF16)
    tok = _dot(xb, wt_ref[...])
    feat = lax.dot_general(wf_ref[...], xb, (((1,), (1,)), ((), ())), preferred_element_type=F32)

    def rope_tok(p, cos, sin):
        return p * cos + pltpu.roll(p, 64, axis=1) * sin

    def rope_feat(p, cos, sin):
        a, b = p[:64, :], p[64:, :]
        return jnp.concatenate([a * cos - b * sin, b * cos + a * sin], axis=0)

    ca, sa, cc, sc = ca_ref[...], sa_ref[...], cc_ref[...], sc_ref[...]
    caT, saT, ccT, scT = caT_ref[...], saT_ref[...], ccT_ref[...], scT_ref[...]
    for t in range(3):
        r = slice(t * LANES, (t + 1) * LANES)
        ka_ref[:, r] = rope_tok(tok[:, r], ca, sa).astype(BF16)
        qa_ref[r, :] = (rope_feat(feat[r, :], caT, saT) * A_QSCALE).astype(BF16)
        rq = slice(768 + t * LANES, 768 + (t + 1) * LANES)
        qc_ref[r, :] = (rope_feat(feat[rq, :], ccT, scT) * C_QSCALE).astype(BF16)
    u_ref[...] = tok[:, 384:640]
    kcm_ref[...] = rope_tok(tok[:, 640:768], cc, sc).astype(BF16)
    vcm_ref[...] = tok[:, 768:896].astype(BF16)
    ksl_ref[...] = rope_tok(tok[:, 896:1024], cc, sc).astype(BF16)
    kwi_ref[...] = rope_tok(tok[:, 1024:1152], cc, sc).astype(BF16)
    va_ref[...] = feat[384:768, :].astype(BF16)
    vsl_ref[...] = feat[1152:1280, :].astype(BF16)
    vwi_ref[...] = feat[1280:1408, :].astype(BF16)
    gt_ref[...] = jax.nn.sigmoid(feat[1408:, :])


def _layer_spec(a, l):
    return pl.BlockSpec((None,) + a.shape[1:], lambda *_: (l,) + (0,) * (a.ndim - 1))


def _inproj(x3, w_tok, w_feat, tabs, tm, l):
    B, T, D = x3.shape
    tok = lambda n: pl.BlockSpec((None, tm, n), lambda b, i: (b, i, 0))
    feat = lambda n: pl.BlockSpec((None, n, tm), lambda b, i: (b, 0, i))
    full = lambda a: _layer_spec(a, l)
    tab_tok = pl.BlockSpec((tm, LANES), lambda b, i: (i, 0))
    tab_feat = pl.BlockSpec((64, tm), lambda b, i: (0, i))
    tok_outs = [(384, BF16), (256, F32)] + [(LANES, BF16)] * 4
    feat_outs = [(384, BF16)] * 3 + [(LANES, BF16)] * 2 + [(C_KV_GROUPS * GATE_ROWS, F32)]
    return pl.pallas_call(
        _inproj_kernel,
        grid=(B, T // tm),
        in_specs=[tok(D), full(w_tok), full(w_feat)] + [tab_tok, tab_tok, tab_feat, tab_feat] * 2,
        out_specs=[tok(n) for n, _ in tok_outs] + [feat(n) for n, _ in feat_outs],
        out_shape=([jax.ShapeDtypeStruct((B, T, n), dt) for n, dt in tok_outs]
                   + [jax.ShapeDtypeStruct((B, n, T), dt) for n, dt in feat_outs]),
        compiler_params=_params(("parallel", "parallel")),
        name="inproj_rope",
    )(x3, w_tok, w_feat, *tabs)


ONES_ROWS = 16


def _sweep_scratch(tk, nq, dv, n_stats=1):
    stats = [pltpu.VMEM((1, nq), F32), pltpu.VMEM((dv + ONES_ROWS, nq), F32)]
    return [pltpu.VMEM((tk, nq), BF16), pltpu.VMEM((tk, nq), BF16),
            pltpu.VMEM((1, nq), F32), pltpu.VMEM((1, nq), F32)] + stats * n_stats


class _Sweep:
    def __init__(self, score, qpos, value, bufs, stats):
        self.score, self.qpos, self.value = score, qpos, value
        self.s_refs, self.cm_refs = bufs[0:2], bufs[2:4]
        self.m_ref, self.acc_ref = stats
        self.tk, self.nq = self.s_refs[0].shape
        self.dv = self.acc_ref.shape[0] - ONES_ROWS

    def init(self):
        self.m_ref[...] = jnp.full(self.m_ref.shape, NEG_BIG, F32)
        self.acc_ref[...] = jnp.zeros(self.acc_ref.shape, F32)

    def scores(self, j, slot, fn=None):
        s = self.score(j, slice(None)) if fn is None else fn()
        self.s_refs[slot][...] = s.astype(BF16)
        self.cm_refs[slot][...] = jnp.max(s, axis=0, keepdims=True)

    def softmax(self, slot, j, causal=False):
        s = self.s_refs[slot][...]
        if causal:
            kpos = j * self.tk + lax.broadcasted_iota(jnp.int32, (self.tk, self.nq), 0)
            s = jnp.where(kpos <= self.qpos, s, jnp.asarray(MASKED, s.dtype))
            cm = jnp.max(s, axis=0, keepdims=True).astype(F32)
        else:
            cm = self.cm_refs[slot][...]
        m = self.m_ref[...]
        m_new = jnp.maximum(m, cm)
        alpha = jnp.exp2(m - m_new)
        p = jnp.exp2(s - m_new.astype(s.dtype))
        self.m_ref[...] = m_new
        v = self.value(j)
        v1 = jnp.concatenate([v, jnp.ones((ONES_ROWS, v.shape[1]), v.dtype)], axis=0)
        self.acc_ref[...] = self.acc_ref[...] * alpha + _dot(v1, p)

    def step(self, j, cur, nxt):
        v = self.value(j)
        v1 = jnp.concatenate([v, jnp.ones((ONES_ROWS, v.shape[1]), v.dtype)], axis=0)
        for c0 in range(0, self.nq, MXU_TILE):
            cs = slice(c0, c0 + MXU_TILE)
            s_next = self.score(j + 1, cs)
            self.s_refs[nxt][:, cs] = s_next.astype(BF16)
            self.cm_refs[nxt][:, cs] = jnp.max(s_next, axis=0, keepdims=True)
            s = self.s_refs[cur][:, cs]
            m = self.m_ref[:, cs]
            m_new = jnp.maximum(m, self.cm_refs[cur][:, cs])
            alpha = jnp.exp2(m - m_new)
            p = jnp.exp2(s - m_new.astype(s.dtype))
            self.m_ref[:, cs] = m_new
            self.acc_ref[:, cs] = self.acc_ref[:, cs] * alpha + _dot(v1, p)

    def full_tiles(self, count, par):
        def group(base, tiles):
            for u in range(tiles):
                self.step(base + u, (par + u) % 2, (par + u + 1) % 2)

        def quad(i, c):
            group(SWEEP_UNROLL * i, SWEEP_UNROLL)
            return c

        def pair(i, c):
            group(done + 2 * i, 2)
            return c

        lax.fori_loop(0, count // SWEEP_UNROLL, quad, 0)
        done = (count // SWEEP_UNROLL) * SWEEP_UNROLL
        lax.fori_loop(0, (count - done) // 2, pair, 0)

    def result(self):
        return self.acc_ref[:self.dv, :] / self.acc_ref[self.dv:self.dv + 1, :]


def _causal_sweep(n_full, score, qpos, value, scratch, first_scores=None):
    sw = _Sweep(score, qpos, value, tuple(scratch[0:4]), tuple(scratch[4:6]))
    sw.init()
    sw.scores(0, 0, first_scores)
    sw.full_tiles(n_full - n_full % 2, 0)

    @pl.when(n_full % 2 == 0)
    def _():
        sw.softmax(0, n_full, causal=True)

    @pl.when(n_full % 2 == 1)
    def _():
        sw.scores(n_full, 1)
        sw.softmax(0, n_full - 1)
        sw.softmax(1, n_full, causal=True)

    return sw.result()


def _diff_attn_kernel(lq1_ref, lk1_ref, lq2_ref, lk2_ref, g_ref, q_ref, k_ref, v_ref, o_ref,
                      *scratch, tq, tk, lam_init):
    h = pl.program_id(1)
    step = pl.program_id(2)
    hl = h % 2
    nq = 2 * tq
    grp = (lax.broadcasted_iota(jnp.int32, (LANES, tq), 0) % 64) // 16
    lane_q = lax.broadcasted_iota(jnp.int32, (1, nq), 1) % tq

    def value(j):
        return v_ref[:, pl.ds(pl.multiple_of(j * tk, tk), tk)]

    def sweep(blk, stats):
        q = q_ref[:, blk * tq:(blk + 1) * tq]
        zero = jnp.zeros_like(q)
        q12 = jnp.concatenate([jnp.where(grp == 2 * hl, q, zero), jnp.where(grp == 2 * hl + 1, q, zero)], axis=1)
        qpos = (A_BLOCKS * step + blk) * tq + lane_q
        score = lambda j, cs: _dot(k_ref[pl.ds(pl.multiple_of(j * tk, tk), tk), :], q12[:, cs])
        return _Sweep(score, qpos, value, scratch[0:4], stats)

    dotp = lambda a, b: jnp.sum(a[...] * b[...], axis=1, keepdims=True)
    lam = jnp.exp(dotp(lq1_ref, lk1_ref)) - jnp.exp(dotp(lq2_ref, lk2_ref)) + lam_init

    def finish(sw, blk):
        on = sw.result()
        o = on[:, :tq] - lam * on[:, tq:]
        ms = jnp.mean(o * o, axis=0, keepdims=True)
        o = o * lax.rsqrt(ms + RMS_EPS) * g_ref[...] * (1.0 - lam_init)
        o_ref[:, blk * tq:(blk + 1) * tq] = o.astype(o_ref.dtype)

    sweeps = [sweep(blk, scratch[4 + 2 * blk:6 + 2 * blk]) for blk in range(A_BLOCKS)]
    first = 0
    sweeps[0].init()
    sweeps[0].scores(0, first)
    for blk, sw in enumerate(sweeps):
        n = A_BLOCKS * step + blk
        if blk % 2 == 0:
            sw.full_tiles(n, first)
            causal_buf = first
        else:
            sw.full_tiles(n - 1, first)
            sw.scores(n, 1 - first)
            sw.softmax(first, n - 1)
            causal_buf = 1 - first
        first = 1 - causal_buf
        if blk + 1 < A_BLOCKS:
            sweeps[blk + 1].init()
            sweeps[blk + 1].scores(0, first)
        sw.softmax(causal_buf, n, causal=True)
        finish(sw, blk)


def _diff_attn(qaT, ka, vaT, lq1, lk1, lq2, lk2, g_col, lam_init, tq, tk, l):
    B, _, T = qaT.shape
    assert tq == tk and T % (A_BLOCKS * tq) == 0, "query block i ends at key tile i; A_BLOCKS blocks per step"
    nq = 2 * tq
    vec = _layer_spec(lq1, l)
    return pl.pallas_call(
        functools.partial(_diff_attn_kernel, tq=tq, tk=tk, lam_init=lam_init),
        grid=(B, A_HEADS, T // (A_BLOCKS * tq)),
        in_specs=[vec, vec, vec, vec, _layer_spec(g_col, l),
                  pl.BlockSpec((None, LANES, A_BLOCKS * tq), lambda b, h, i: (b, h // 2, i)),
                  pl.BlockSpec((None, T, LANES), lambda b, h, i: (b, 0, h // 2)),
                  pl.BlockSpec((None, A_V_DIM, T), lambda b, h, i: (b, h, 0))],
        out_specs=pl.BlockSpec((None, A_V_DIM, A_BLOCKS * tq), lambda b, h, i: (b, h, i)),
        out_shape=jax.ShapeDtypeStruct((B, A_WIDTH, T), BF16),
        scratch_shapes=_sweep_scratch(tk, nq, A_V_DIM, n_stats=A_BLOCKS),
        compiler_params=_params(("parallel", "parallel", "arbitrary")),
        name="diff_attn",
    )(lq1, lk1, lq2, lk2, g_col, qaT, ka, vaT)


POOL_HALO = 16


def _pool_mix(halo, cur, w_ref, scale_ref, t0):
    tm = cur.shape[0]
    arr = jnp.concatenate([halo, cur], axis=0)
    s2 = arr + pltpu.roll(arr, 1, axis=0)
    s4 = s2 + pltpu.roll(s2, 2, axis=0)
    s8 = s4 + pltpu.roll(s4, 4, axis=0)
    s16 = s8 + pltpu.roll(s8, 8, axis=0)
    sums = [s[POOL_HALO:, :] for s in (s2, s4, s8, s16)]
    lane = lax.broadcasted_iota(jnp.int32, (tm, POOL_WIDTH), 1)
    tpos = t0 + lax.broadcasted_iota(jnp.int32, (tm, POOL_WIDTH), 0)
    grp = lane // POOL_GROUP_DIM
    win = jnp.where(grp == 0, sums[0], jnp.where(grp == 1, sums[1], jnp.where(grp == 2, sums[2], sums[3])))
    width = jnp.where(grp == 0, 2, jnp.where(grp == 1, 4, jnp.where(grp == 2, 8, 16)))
    cnt = jnp.minimum(tpos + 1, width).astype(F32)
    y = win / cnt - cur
    z = _dot(y.astype(BF16), w_ref[...])
    return (z * scale_ref[...]).astype(BF16)


def _prep_pool(w_pool, pool_scale):
    depth, G, C, _ = w_pool.shape
    eye = jnp.asarray(np.eye(G, dtype=np.float32))
    w_bd = (w_pool[:, :, :, None, :] * eye[None, :, None, :, None]).reshape(depth, G * C, G * C)
    return w_bd.astype(BF16), pool_scale[:, None, :]


def _cmp_kernel(k_ref, v_ref, pk_ref, pv_ref, wk_ref, wv_ref, kc_ref, vc_ref):
    n = k_ref.shape[0]
    rowid = lax.broadcasted_iota(jnp.int32, (n, LANES), 0)

    def compress(x_ref, pe_ref, w_ref):
        x = x_ref[...].astype(F32)
        lo = _dot((x + pe_ref[0:1, :]).astype(BF16), w_ref[0])
        hi = _dot((x + pe_ref[1:2, :]).astype(BF16), w_ref[1])
        out = lo + pltpu.roll(hi, n - 1, axis=0)
        return jnp.where(rowid < n - 1, out, 0.0)

    kc_ref[...] = compress(k_ref, pk_ref, wk_ref).astype(kc_ref.dtype)
    vc_ref[...] = compress(v_ref, pv_ref, wv_ref).astype(vc_ref.dtype)


def _compress(kr, vr, pe_k2, pe_v2, wk2, wv2, l):
    B, n, W = kr.shape
    blk = pl.BlockSpec((None, n, W), lambda b: (b, 0, 0))
    pe = _layer_spec(pe_k2, l)
    wsp = _layer_spec(wk2, l)
    osp = pl.BlockSpec((None, n, LANES), lambda b: (b, 0, 0))
    return pl.pallas_call(
        _cmp_kernel,
        grid=(B,),
        in_specs=[blk, blk, pe, pe, wsp, wsp],
        out_specs=[osp, osp],
        out_shape=[jax.ShapeDtypeStruct((B, n, LANES), BF16)] * 2,
        compiler_params=_params(("parallel",)),
        name="nsa_compress",
    )(kr, vr, pe_k2, pe_v2, wk2, wv2)


def _prep_cmp(pe_k, pe_v, w_ck, w_cv):
    L, G, d, h = w_ck.shape[0], C_KV_GROUPS, C_HEAD_DIM, C_HEAD_DIM // 2
    same = jnp.asarray(np.eye(G, dtype=np.float32))
    shape2 = (L, 2, 16 * LANES, G * d)
    wk = w_ck.reshape(L, CMP_BLOCK, 2, 1, h, 2, 1, h) * same[None, None, None, :, None, None, :, None]
    pk2 = jnp.broadcast_to(pe_k.reshape(L, CMP_BLOCK, 2, 1, h), (L, CMP_BLOCK, 2, G, h))
    wv = w_cv.reshape(L, CMP_BLOCK, 1, d, 1, d) * same[None, None, :, None, :, None]
    pv2 = jnp.broadcast_to(pe_v.reshape(L, CMP_BLOCK, 1, d), (L, CMP_BLOCK, G, d))
    flat = lambda pe: pe.reshape(L, 2, 16 * LANES)
    return flat(pk2), flat(pv2), wk.reshape(shape2).astype(BF16), wv.reshape(shape2).astype(BF16)


def _nsa_kernel(q_ref, gt_ref, kc_ref, vc_ref, ov_ref, *refs, tq, tk, T):
    wb_refs, (ks_ref, vs_ref, kw_ref, vw_ref, o_ref) = refs[:C_BLOCKS], refs[C_BLOCKS:C_BLOCKS + 5]
    scratch = refs[C_BLOCKS + 5:]
    sweep_bufs, per_block = scratch[0:4], scratch[4:]
    g = pl.program_id(1)
    step = pl.program_id(2)
    n_cmp = kc_ref.shape[0]
    n_blk = T // SEL_BLOCK
    d = C_HEAD_DIM
    nq = C_HPG * tq
    bpt = tk // SEL_BLOCK
    rowg = (lax.broadcasted_iota(jnp.int32, (LANES, tq), 0) % 64) // 32

    class Block:
        def __init__(self, blk):
            self.cols = slice(blk * tq, (blk + 1) * tq)
            self.q0 = (C_BLOCKS * step + blk) * tq
            self.tpos = self.q0 + lax.broadcasted_iota(jnp.int32, (1, tq), 1)
            self.tpos3 = self.q0 + lax.broadcasted_iota(jnp.int32, (1, nq), 1) % tq
            self.cur = (self.tpos // SEL_BLOCK).astype(F32)
            qs = []
            for hh in range(C_HPG):
                qt = q_ref[hh * LANES:(hh + 1) * LANES, self.cols]
                qs.append(jnp.where(rowg == g, qt, jnp.zeros_like(qt)))
            self.q3 = jnp.concatenate(qs, axis=1)
            self.wb_ref = wb_refs[blk]
            self.sel_ref, m_ref, acc_ref, self.raw0_ref, self.ocw_ref = per_block[5 * blk:5 * blk + 5]
            self.stats = (m_ref, acc_ref)

    blocks = [Block(blk) for blk in range(C_BLOCKS)]

    def prelude(bk, n_c, n_b):
        q3, tpos3, cur, q0 = bk.q3, bk.tpos3, bk.cur, bk.q0
        cend = lax.broadcasted_iota(jnp.int32, (n_c, nq), 0) * CMP_STRIDE + (CMP_BLOCK - 1)
        s = jnp.where(cend <= tpos3, _dot(kc_ref[0:n_c, :], q3), MASKED)
        m = jnp.maximum(jnp.max(s, axis=0, keepdims=True), NEG_BIG)
        e = jnp.exp2(s - m)
        l = jnp.sum(e, axis=0, keepdims=True)
        p = e / jnp.where(l > 0.0, l, 1.0)
        bk.ocw_ref[0:d, :] = _dot(vc_ref[:, 0:n_c], p.astype(BF16))
        psum = p[:, :tq] + p[:, tq:2 * tq] + p[:, 2 * tq:]

        p_hi = psum.astype(BF16)
        p_lo = (psum - p_hi.astype(F32)).astype(BF16)
        ov = ov_ref[0:n_b, 0:n_c]
        imp = _dot(ov, p_hi) + _dot(ov, p_lo)
        jidx = lax.broadcasted_iota(jnp.int32, (n_b, tq), 0).astype(F32)
        forced = (jidx == 0.0) | (jidx == cur) | (jidx == cur - 1.0)
        imp = jnp.where(jidx > cur, -1.0, imp)
        imp = jnp.where(forced, TAKEN, imp)
        rounds = min(N_SEL, n_blk) - N_FORCED

        band = WINDOW + tq
        w0 = pl.multiple_of(jnp.maximum(q0 - WINDOW, 0), LANES)
        s = _dot(kw_ref[pl.ds(w0, band), :], q3) + jnp.concatenate([bk.wb_ref[...]] * C_HPG, axis=1)
        e = jnp.exp2(s - jnp.max(s, axis=0, keepdims=True))
        bk.ocw_ref[d:, :] = _dot(vw_ref[:, pl.ds(w0, band)], e.astype(BF16)) / jnp.sum(e, axis=0, keepdims=True)

        bk.raw0_ref[...] = _dot(ks_ref[0:tk, :], q3)

        def store_selection(taken_imp):
            bias = jnp.where(taken_imp == TAKEN, 0.0, MASKED)
            bk.sel_ref[0:n_b, :] = jnp.concatenate([bias] * C_HPG, axis=1)

        if n_b < n_blk:
            bk.sel_ref[n_b:, :] = jnp.full((n_blk - n_b, nq), MASKED, F32)
        quick = imp
        for _ in range(rounds):
            quick = jnp.where(quick == jnp.max(quick, axis=0, keepdims=True), TAKEN, quick)
        store_selection(quick)
        free = (jidx >= 1.0) & (jidx <= cur - 2.0)
        n_taken = jnp.sum(jnp.where(free & (quick == TAKEN), 1.0, 0.0), axis=0, keepdims=True)
        n_expected = jnp.minimum(jnp.maximum(cur - 2.0, 0.0), float(rounds))
        tie = jnp.max(jnp.abs(n_taken - n_expected)) > 0.0

        def exact_rounds():
            exact = imp
            for _ in range(rounds):
                mx = jnp.max(exact, axis=0, keepdims=True)
                first = jnp.min(jnp.where(exact == mx, jidx, float(n_blk)), axis=0, keepdims=True)
                exact = jnp.where(jidx == first, TAKEN, exact)
            store_selection(exact)

        return tie, exact_rounds

    def preludes(n_c, n_b):
        pending = [prelude(bk, n_c, n_b) for bk in blocks]
        for tie, exact_rounds in pending:
            pl.when(tie)(exact_rounds)

    early = (C_BLOCKS * step + C_BLOCKS) * tq <= T // 2
    pl.when(early)(lambda: preludes(n_cmp // 2, n_blk // 2))
    pl.when(jnp.logical_not(early))(lambda: preludes(n_cmp, n_blk))

    gts = gt_ref[...]
    for bk in blocks:
        def block_bias(j, cs, bk=bk):
            rows = bk.sel_ref[pl.ds(pl.multiple_of(j * bpt, bpt), bpt), cs]
            return jnp.concatenate([jnp.broadcast_to(rows[i:i + 1, :], (SEL_BLOCK, rows.shape[1]))
                                    for i in range(bpt)], axis=0)

        def score(j, cs, bk=bk, block_bias=block_bias):
            return _dot(ks_ref[pl.ds(pl.multiple_of(j * tk, tk), tk), :], bk.q3[:, cs]) + block_bias(j, cs)

        def first_scores(bk=bk, block_bias=block_bias):
            return bk.raw0_ref[...] + block_bias(0, slice(None))

        def value(j):
            return vs_ref[:, pl.ds(pl.multiple_of(j * tk, tk), tk)]

        o_slc = _causal_sweep(bk.q0 // tk, score, bk.tpos3, value, sweep_bufs + bk.stats, first_scores)
        for hh in range(C_HPG):
            r = hh * N_BRANCH
            c = slice(hh * tq, (hh + 1) * tq)
            gc = gts[:, bk.cols]
            out = (bk.ocw_ref[0:d, c] * gc[r:r + 1, :] + o_slc[:, c] * gc[r + 1:r + 2, :]
                   + bk.ocw_ref[d:, c] * gc[r + 2:r + 3, :])
            o_ref[hh * d:(hh + 1) * d, bk.cols] = out.astype(o_ref.dtype)


def _window_bias(tq):
    band, n_edge = WINDOW + tq, WINDOW // tq
    q0 = np.arange(n_edge + 1)[:, None, None] * tq
    wpos = np.maximum(q0 - WINDOW, 0) + np.arange(band)[None, :, None]
    tpos = q0 + np.arange(tq)[None, None, :]
    ok = (wpos <= tpos) & (wpos > tpos - WINDOW)
    return jnp.asarray(np.where(ok, 0.0, MASKED).astype(np.float32))


def _nsa(qcT, gtT, kc2, vcT, ovT, ksl, vslT, kwi, vwiT, tq, tk):
    B, _, T = qcT.shape
    n_cmp = kc2.shape[1]
    n_blk = T // SEL_BLOCK
    nq = C_HPG * tq
    tqs = C_BLOCKS * tq
    assert tk % tqs == 0 or tqs % tk == 0, "a step's blocks must not straddle a partial key tile boundary"
    assert tk % tq == 0, "the causal key tile must contain the whole query block"
    assert WINDOW % tq == 0 and T >= WINDOW + tq and (T // 2) % tqs == 0
    wb = _window_bias(tq)
    n_edge = WINDOW // tq
    kv_tok = pl.BlockSpec((None, T, LANES), lambda b, g, i: (b, 0, 0))
    kv_feat = pl.BlockSpec((None, C_HEAD_DIM, T), lambda b, g, i: (b, g, 0))
    wb_spec = lambda blk: pl.BlockSpec((None, WINDOW + tq, tq),
                                       lambda b, g, i: (jnp.minimum(C_BLOCKS * i + blk, n_edge), 0, 0))
    sweep = _sweep_scratch(tk, nq, C_HEAD_DIM)
    per_block = [pltpu.VMEM((n_blk, nq), F32)] + sweep[4:6] + [pltpu.VMEM((tk, nq), F32),
                                                              pltpu.VMEM((2 * C_HEAD_DIM, nq), F32)]
    return pl.pallas_call(
        functools.partial(_nsa_kernel, tq=tq, tk=tk, T=T),
        grid=(B, C_KV_GROUPS, T // tqs),
        in_specs=[pl.BlockSpec((None, C_WIDTH, tqs), lambda b, g, i: (b, 0, i)),
                  pl.BlockSpec((None, GATE_ROWS, tqs), lambda b, g, i: (b, g, i)),
                  pl.BlockSpec((None, n_cmp, LANES), lambda b, g, i: (b, 0, 0)),
                  pl.BlockSpec((None, C_HEAD_DIM, n_cmp), lambda b, g, i: (b, g, 0)),
                  pl.BlockSpec((n_blk, n_cmp), lambda b, g, i: (0, 0))]
                 + [wb_spec(blk) for blk in range(C_BLOCKS)] + [kv_tok, kv_feat, kv_tok, kv_feat],
        out_specs=pl.BlockSpec((None, C_HPG * C_HEAD_DIM, tqs), lambda b, g, i: (b, g, i)),
        out_shape=jax.ShapeDtypeStruct((B, C_WIDTH, T), BF16),
        scratch_shapes=sweep[0:4] + per_block * C_BLOCKS,
        compiler_params=_params(("parallel", "parallel", "arbitrary")),
        name="nsa_attn",
    )(qcT, gtT, kc2, vcT, ovT, *([wb] * C_BLOCKS), ksl, vslT, kwi, vwiT)


def _overlap_T(T):
    n_cmp = T // CMP_STRIDE
    n_blk = T // SEL_BLOCK
    cs = np.arange(n_cmp)[None, :] * CMP_STRIDE
    bs = np.arange(n_blk)[:, None] * SEL_BLOCK
    ov = (cs < bs + SEL_BLOCK) & (cs + CMP_BLOCK > bs) & (np.arange(n_cmp)[None, :] < n_cmp - 1)
    return jnp.asarray(ov.astype(np.float32), dtype=BF16)


def _layer_norm(y, g, b):
    mu = jnp.mean(y, axis=-1, keepdims=True)
    yc = y - mu
    var = jnp.mean(yc * yc, axis=-1, keepdims=True)
    return yc * lax.rsqrt(var + LN_EPS) * g + b


def _outproj_kernel(x_ref, oa_ref, uh_ref, u_ref, oc_ref, w_ref, wp_ref, ps_ref, g_ref, b_ref, o_ref, *, alpha, sub):
    i, tm = pl.program_id(1), x_ref.shape[0]
    tn = lambda a, w: lax.dot_general(a, w, (((0,), (0,)), ((), ())), preferred_element_type=F32)
    r1, r2 = A_WIDTH, A_WIDTH + POOL_WIDTH
    for r0 in range(0, tm, sub):
        rows = slice(r0, r0 + sub)
        if r0 == 0:
            halo = jnp.where(i == 0, jnp.zeros(uh_ref.shape, F32), uh_ref[...])
        else:
            halo = u_ref[r0 - POOL_HALO:r0, :]
        ob = _pool_mix(halo, u_ref[rows, :], wp_ref, ps_ref, i * tm + r0)
        mix = (tn(oa_ref[:, rows], w_ref[:r1, :]) + _dot(ob, w_ref[r1:r2, :]) + tn(oc_ref[:, rows], w_ref[r2:, :]))
        o_ref[rows, :] = _layer_norm(alpha * x_ref[rows, :] + mix, g_ref[...], b_ref[...])


def _outproj(x3, oaT, u, ocT, w, w_bd, p_scale, g, b, alpha, tm, sub, l):
    B, T, D = x3.shape
    tok = lambda n: pl.BlockSpec((None, tm, n), lambda bb, i: (bb, i, 0))
    feat = lambda n: pl.BlockSpec((None, n, tm), lambda bb, i: (bb, 0, i))
    halo = pl.BlockSpec((None, POOL_HALO, POOL_WIDTH),
                        lambda bb, i: (bb, jnp.maximum(i * (tm // POOL_HALO) - 1, 0), 0))
    return pl.pallas_call(
        functools.partial(_outproj_kernel, alpha=alpha, sub=sub),
        grid=(B, T // tm),
        in_specs=[tok(D), feat(oaT.shape[1]), halo, tok(POOL_WIDTH), feat(ocT.shape[1]),
                  _layer_spec(w, l), _layer_spec(w_bd, l), _layer_spec(p_scale, l),
                  _layer_spec(g, l), _layer_spec(b, l)],
        out_specs=tok(D),
        out_shape=jax.ShapeDtypeStruct((B, T, D), F32),
        compiler_params=_params(("parallel", "parallel")),
        name="outproj_ln",
    )(x3, oaT, u, u, ocT, w, w_bd, p_scale, g, b)


def _mlp_kernel(x_ref, wu_ref, wd_ref, g_ref, b_ref, o_ref, *, alpha, sub, tf):
    tm, F = x_ref.shape[0], wu_ref.shape[1]
    for r0 in range(0, tm, sub):
        x = x_ref[r0:r0 + sub, :]
        xb = x.astype(BF16)
        y = alpha * x
        for f0 in range(0, F, tf):
            hid = jnp.square(jnp.maximum(_dot(xb, wu_ref[:, f0:f0 + tf]), 0.0))
            y = y + _dot(hid.astype(BF16), wd_ref[f0:f0 + tf, :])
        o_ref[r0:r0 + sub, :] = _layer_norm(y, g_ref[...], b_ref[...])


def _mlp(x2, wu, wd, g, b, alpha, tm, sub, tf, l):
    BT, D = x2.shape
    once = lambda a: pl.BlockSpec((None,) + a.shape[1:], lambda i: (l, 0, 0), pipeline_mode=pl.Buffered(1))
    return pl.pallas_call(
        functools.partial(_mlp_kernel, alpha=alpha, sub=sub, tf=tf),
        grid=(BT // tm,),
        in_specs=[pl.BlockSpec((tm, D), lambda i: (i, 0)), once(wu), once(wd), _layer_spec(g, l), _layer_spec(b, l)],
        out_specs=pl.BlockSpec((tm, D), lambda i: (i, 0)),
        out_shape=jax.ShapeDtypeStruct((BT, D), F32),
        compiler_params=_params(("parallel",)),
        name="mlp_ln",
    )(x2, wu, wd, g, b)


def _tiles(T):
    return dict(
        tm_proj=min(512, T), tm_out=min(1024, T), sub_out=min(512, T), tm_mlp=min(1024, T), sub_mlp=min(512, T), tf_mlp=1024,
        a_tq=min(512, T), a_tk=min(512, T), c_tq=min(256, T), c_tk=min(512, T),
    )


def kernel(x, w_in, lam_q1, lam_k1, lam_q2, lam_k2, subln_g, w_pool, pool_scale, cmp_pe_k, cmp_pe_v,
           w_cmp_k, w_cmp_v, w_out, ln1_g, ln1_b, w_up, w_down, ln2_g, ln2_b):
    B, T, D = x.shape
    depth = w_in.shape[0]
    alpha = (2 * depth) ** 0.25
    tl = _tiles(T)
    tabs = _rope_tables(T)
    ovT = _overlap_T(T)
    w_tok, w_feat = _prep_w_in(w_in)
    w_bd, p_scale = _prep_pool(w_pool, pool_scale)
    pk2, pv2, wk2, wv2 = _prep_cmp(cmp_pe_k, cmp_pe_v, w_cmp_k, w_cmp_v)
    wo, wu, wd = w_out.astype(BF16), w_up.astype(BF16), w_down.astype(BF16)
    row = lambda a: a[:, None, :]
    lams = [row(a) for a in (lam_q1, lam_k1, lam_q2, lam_k2)]
    n16 = T // CMP_STRIDE
    x3 = x
    for l in range(depth):
        lam_init = 0.8 - 0.6 * math.exp(-0.3 * l)
        (ka, u, kcm, vcm, ksl, kwi, qaT, vaT, qcT, vslT, vwiT, gtT) = _inproj(x3, w_tok, w_feat, tabs, tl["tm_proj"], l)

        oaT = _diff_attn(qaT, ka, vaT, *lams, subln_g[:, :, None], lam_init, tl["a_tq"], tl["a_tk"], l)

        kc2, vc2 = _compress(kcm.reshape(B, n16, 16 * LANES), vcm.reshape(B, n16, 16 * LANES),
                             pk2, pv2, wk2, wv2, l)
        ocT = _nsa(qcT, gtT, kc2, vc2.transpose(0, 2, 1), ovT, ksl, vslT, kwi, vwiT, tl["c_tq"], tl["c_tk"])

        x3 = _outproj(x3, oaT, u, ocT, wo, w_bd, p_scale, row(ln1_g), row(ln1_b), alpha, tl["tm_out"], tl["sub_out"], l)
        x3 = _mlp(x3.reshape(B * T, D), wu, wd, row(ln2_g), row(ln2_b), alpha, tl["tm_mlp"], tl["sub_mlp"], tl["tf_mlp"],
                  l).reshape(B, T, D)
    return x3
```

```python
import functools
import math

import jax
import jax.numpy as jnp
import numpy as np
from jax import lax
from jax.experimental import pallas as pl
from jax.experimental.pallas import tpu as pltpu

A_HEADS = 6
A_QK_DIM = 32
A_V_DIM = 64
A_WIDTH = A_HEADS * A_V_DIM
POOL_WINDOWS = (2, 4, 8, 16)
POOL_GROUP_DIM = 64
POOL_WIDTH = 256
C_HEADS = 6
C_KV_GROUPS = 2
C_HPG = 3
C_HEAD_DIM = 64
C_WIDTH = C_HEADS * C_HEAD_DIM
CMP_BLOCK = 32
CMP_STRIDE = 16
SEL_BLOCK = 64
N_SEL = 16
WINDOW = 512
N_BRANCH = 3
FORCED_SCORE = 1.0e4
N_FORCED = 3
ROPE_THETA = 10000.0
LN_EPS = 1e-5
RMS_EPS = 1e-6

LANES = 128
LOG2E = 1.4426950408889634
NEG_BIG = -1e30
SWEEP_UNROLL = 4
MXU_TILE = 256
A_BLOCKS = 4
C_BLOCKS = 2
TAKEN = -3.0e38
MASKED = -3e38
VMEM_LIMIT = 56 * 1024 * 1024

BF16 = jnp.bfloat16
F32 = jnp.float32


def _dot(a, b):
    return jnp.dot(a, b, preferred_element_type=F32)


def _params(sem, vmem=VMEM_LIMIT):
    return pltpu.CompilerParams(dimension_semantics=sem, vmem_limit_bytes=vmem)


_IN_SPLITS = (384, 384, 384, 256, 384, 128, 128, 128, 128, 128, 128, 18)
_OFFS = np.concatenate([[0], np.cumsum(_IN_SPLITS)]).astype(np.int64)
A_QSCALE = A_QK_DIM ** -0.5 * LOG2E
C_QSCALE = C_HEAD_DIM ** -0.5 * LOG2E
GATE_ROWS = 16
N_TOK = 384 + 256 + 4 * LANES
N_FEAT = 3 * 384 + 2 * LANES + C_KV_GROUPS * GATE_ROWS


def _prep_w_in(w):
    o = _OFFS
    L, D = w.shape[0], w.shape[1]
    seg = lambda i: w[:, :, o[i]:o[i + 1]]
    diff = lambda a: a.reshape(L, D, A_HEADS // 2, 2, 2, 2, A_QK_DIM // 2).transpose(0, 1, 2, 5, 3, 4, 6).reshape(L, D, -1)
    cq = lambda a: a.reshape(L, D, C_KV_GROUPS, C_HPG, 2, C_HEAD_DIM // 2).transpose(0, 1, 3, 4, 2, 5).reshape(L, D, -1)
    ck = lambda a: a.reshape(L, D, C_KV_GROUPS, 2, C_HEAD_DIM // 2).transpose(0, 1, 3, 2, 4).reshape(L, D, -1)
    w_tok = jnp.concatenate([diff(seg(1)), seg(3), ck(seg(5)), seg(6), ck(seg(7)), ck(seg(9))], axis=2)
    per = C_HPG * N_BRANCH
    gates = jnp.pad(seg(11).reshape(L, D, C_KV_GROUPS, per), ((0, 0), (0, 0), (0, 0), (0, GATE_ROWS - per)))
    w_feat = jnp.concatenate([diff(seg(0)), seg(2), cq(seg(4)), seg(8), seg(10),
                              gates.reshape(L, D, C_KV_GROUPS * GATE_ROWS)], axis=2)
    return w_tok.astype(BF16), jnp.swapaxes(w_feat, 1, 2).astype(BF16)


def _rope_tables(T):
    def tab(dim):
        half = dim // 2
        inv = 1.0 / (ROPE_THETA ** (jnp.arange(0, dim, 2, dtype=F32) / dim))
        ang = jnp.arange(T, dtype=F32)[:, None] * inv[None, :]
        lane = np.arange(LANES)
        j = lane % half
        sign = np.where(lane < 64, -1.0, 1.0).astype(np.float32)
        cos, sin = jnp.cos(ang), jnp.sin(ang)
        return cos[:, j], sin[:, j] * sign[None, :], cos[:, j[:64]].T, sin[:, j[:64]].T
    return tab(A_QK_DIM) + tab(C_HEAD_DIM)


def _inproj_kernel(x_ref, wt_ref, wf_ref, ca_ref, sa_ref, caT_ref, saT_ref, cc_ref, sc_ref, ccT_ref, scT_ref,
                   ka_ref, u_ref, kcm_ref, vcm_ref, ksl_ref, kwi_ref,
                   qa_ref, va_ref, qc_ref, vsl_ref, vwi_ref, gt_ref):
    xb = x_ref[...].astype(BF16)
    tok = _dot(xb, wt_ref[...])
    feat = lax.dot_general(wf_ref[...], xb, (((1,), (1,)), ((), ())), preferred_element_type=F32)

    def rope_tok(p, cos, sin):
        return p * cos + pltpu.roll(p, 64, axis=1) * sin

    def rope_feat(p, cos, sin):
        a, b = p[:64, :], p[64:, :]
        return jnp.concatenate([a * cos - b * sin, b * cos + a * sin], axis=0)

    ca, sa, cc, sc = ca_ref[...], sa_ref[...], cc_ref[...], sc_ref[...]
    caT, saT, ccT, scT = caT_ref[...], saT_ref[...], ccT_ref[...], scT_ref[...]
    for t in range(3):
        r = slice(t * LANES, (t + 1) * LANES)
        ka_ref[:, r] = rope_tok(tok[:, r], ca, sa).astype(BF16)
        qa_ref[r, :] = (rope_feat(feat[r, :], caT, saT) * A_QSCALE).astype(BF16)
        rq = slice(768 + t * LANES, 768 + (t + 1) * LANES)
        qc_ref[r, :] = (rope_feat(feat[rq, :], ccT, scT) * C_QSCALE).astype(BF16)
    u_ref[...] = tok[:, 384:640]
    kcm_ref[...] = rope_tok(tok[:, 640:768], cc, sc).astype(BF16)
    vcm_ref[...] = tok[:, 768:896].astype(BF16)
    ksl_ref[...] = rope_tok(tok[:, 896:1024], cc, sc).astype(BF16)
    kwi_ref[...] = rope_tok(tok[:, 1024:1152], cc, sc).astype(BF16)
    va_ref[...] = feat[384:768, :].astype(BF16)
    vsl_ref[...] = feat[1152:1280, :].astype(BF16)
    vwi_ref[...] = feat[1280:1408, :].astype(BF16)
    gt_ref[...] = jax.nn.sigmoid(feat[1408:, :])


def _layer_spec(a, l):
    return pl.BlockSpec((None,) + a.shape[1:], lambda *_: (l,) + (0,) * (a.ndim - 1))


def _inproj(x3, w_tok, w_feat, tabs, tm, l):
    B, T, D = x3.shape
    tok = lambda n: pl.BlockSpec((None, tm, n), lambda b, i: (b, i, 0))
    feat = lambda n: pl.BlockSpec((None, n, tm), lambda b, i: (b, 0, i))
    full = lambda a: _layer_spec(a, l)
    tab_tok = pl.BlockSpec((tm, LANES), lambda b, i: (i, 0))
    tab_feat = pl.BlockSpec((64, tm), lambda b, i: (0, i))
    tok_outs = [(384, BF16), (256, F32)] + [(LANES, BF16)] * 4
    feat_outs = [(384, BF16)] * 3 + [(LANES, BF16)] * 2 + [(C_KV_GROUPS * GATE_ROWS, F32)]
    return pl.pallas_call(
        _inproj_kernel,
        grid=(B, T // tm),
        in_specs=[tok(D), full(w_tok), full(w_feat)] + [tab_tok, tab_tok, tab_feat, tab_feat] * 2,
        out_specs=[tok(n) for n, _ in tok_outs] + [feat(n) for n, _ in feat_outs],
        out_shape=([jax.ShapeDtypeStruct((B, T, n), dt) for n, dt in tok_outs]
                   + [jax.ShapeDtypeStruct((B, n, T), dt) for n, dt in feat_outs]),
        compiler_params=_params(("parallel", "parallel")),
        name="inproj_rope",
    )(x3, w_tok, w_feat, *tabs)


ONES_ROWS = 16


def _sweep_scratch(tk, nq, dv, n_stats=1):
    stats = [pltpu.VMEM((1, nq), F32), pltpu.VMEM((dv + ONES_ROWS, nq), F32)]
    return [pltpu.VMEM((tk, nq), BF16), pltpu.VMEM((tk, nq), BF16),
            pltpu.VMEM((1, nq), F32), pltpu.VMEM((1, nq), F32)] + stats * n_stats


class _Sweep:
    def __init__(self, score, qpos, value, bufs, stats):
        self.score, self.qpos, self.value = score, qpos, value
        self.s_refs, self.cm_refs = bufs[0:2], bufs[2:4]
        self.m_ref, self.acc_ref = stats
        self.tk, self.nq = self.s_refs[0].shape
        self.dv = self.acc_ref.shape[0] - ONES_ROWS

    def init(self):
        self.m_ref[...] = jnp.full(self.m_ref.shape, NEG_BIG, F32)
        self.acc_ref[...] = jnp.zeros(self.acc_ref.shape, F32)

    def scores(self, j, slot, fn=None):
        s = self.score(j, slice(None)) if fn is None else fn()
        self.s_refs[slot][...] = s.astype(BF16)
        self.cm_refs[slot][...] = jnp.max(s, axis=0, keepdims=True)

    def softmax(self, slot, j, causal=False):
        s = self.s_refs[slot][...]
        if causal:
            kpos = j * self.tk + lax.broadcasted_iota(jnp.int32, (self.tk, self.nq), 0)
            s = jnp.where(kpos <= self.qpos, s, jnp.asarray(MASKED, s.dtype))
            cm = jnp.max(s, axis=0, keepdims=True).astype(F32)
        else:
            cm = self.cm_refs[slot][...]
        m = self.m_ref[...]
        m_new = jnp.maximum(m, cm)
        alpha = jnp.exp2(m - m_new)
        p = jnp.exp2(s - m_new.astype(s.dtype))
        self.m_ref[...] = m_new
        v = self.value(j)
        v1 = jnp.concatenate([v, jnp.ones((ONES_ROWS, v.shape[1]), v.dtype)], axis=0)
        self.acc_ref[...] = self.acc_ref[...] * alpha + _dot(v1, p)

    def step(self, j, cur, nxt, causal=False, then=None):
        scorer, j_next = (self, j + 1) if then is None else (then, 0)
        v = self.value(j)
        v1 = jnp.concatenate([v, jnp.ones((ONES_ROWS, v.shape[1]), v.dtype)], axis=0)
        for c0 in range(0, self.nq, MXU_TILE):
            cs = slice(c0, c0 + MXU_TILE)
            s_next = scorer.score(j_next, cs)
            self.s_refs[nxt][:, cs] = s_next.astype(BF16)
            self.cm_refs[nxt][:, cs] = jnp.max(s_next, axis=0, keepdims=True)
            s = self.s_refs[cur][:, cs]
            if causal:
                kpos = j * self.tk + lax.broadcasted_iota(jnp.int32, s.shape, 0)
                s = jnp.where(kpos <= self.qpos[:, cs], s, jnp.asarray(MASKED, s.dtype))
                cm = jnp.max(s, axis=0, keepdims=True).astype(F32)
            else:
                cm = self.cm_refs[cur][:, cs]
            m = self.m_ref[:, cs]
            m_new = jnp.maximum(m, cm)
            alpha = jnp.exp2(m - m_new)
            p = jnp.exp2(s - m_new.astype(s.dtype))
            self.m_ref[:, cs] = m_new
            self.acc_ref[:, cs] = self.acc_ref[:, cs] * alpha + _dot(v1, p)

    def full_tiles(self, count, par):
        def group(base, tiles):
            for u in range(tiles):
                self.step(base + u, (par + u) % 2, (par + u + 1) % 2)

        def quad(i, c):
            group(SWEEP_UNROLL * i, SWEEP_UNROLL)
            return c

        def pair(i, c):
            group(done + 2 * i, 2)
            return c

        lax.fori_loop(0, count // SWEEP_UNROLL, quad, 0)
        done = (count // SWEEP_UNROLL) * SWEEP_UNROLL
        lax.fori_loop(0, (count - done) // 2, pair, 0)

    def result(self):
        return self.acc_ref[:self.dv, :] / self.acc_ref[self.dv:self.dv + 1, :]


def _causal_sweep(n_full, score, qpos, value, scratch, first_scores=None):
    sw = _Sweep(score, qpos, value, tuple(scratch[0:4]), tuple(scratch[4:6]))
    sw.init()
    sw.scores(0, 0, first_scores)
    sw.full_tiles(n_full - n_full % 2, 0)

    @pl.when(n_full % 2 == 0)
    def _():
        sw.softmax(0, n_full, causal=True)

    @pl.when(n_full % 2 == 1)
    def _():
        sw.step(n_full - 1, 0, 1)
        sw.softmax(1, n_full, causal=True)

    return sw.result()


def _diff_attn_kernel(lq1_ref, lk1_ref, lq2_ref, lk2_ref, g_ref, q_ref, k_ref, v_ref, o_ref,
                      *scratch, tq, tk, lam_init):
    h = pl.program_id(1)
    step = pl.program_id(2)
    hl = h % 2
    nq = 2 * tq
    grp = (lax.broadcasted_iota(jnp.int32, (LANES, tq), 0) % 64) // 16
    lane_q = lax.broadcasted_iota(jnp.int32, (1, nq), 1) % tq

    def value(j):
        return v_ref[:, pl.ds(pl.multiple_of(j * tk, tk), tk)]

    def sweep(blk, stats):
        q = q_ref[:, blk * tq:(blk + 1) * tq]
        zero = jnp.zeros_like(q)
        q12 = jnp.concatenate([jnp.where(grp == 2 * hl, q, zero), jnp.where(grp == 2 * hl + 1, q, zero)], axis=1)
        qpos = (A_BLOCKS * step + blk) * tq + lane_q
        score = lambda j, cs: _dot(k_ref[pl.ds(pl.multiple_of(j * tk, tk), tk), :], q12[:, cs])
        return _Sweep(score, qpos, value, scratch[0:4], stats)

    dotp = lambda a, b: jnp.sum(a[...] * b[...], axis=1, keepdims=True)
    lam = jnp.exp(dotp(lq1_ref, lk1_ref)) - jnp.exp(dotp(lq2_ref, lk2_ref)) + lam_init

    def finish(sw, blk):
        on = sw.result()
        o = on[:, :tq] - lam * on[:, tq:]
        ms = jnp.mean(o * o, axis=0, keepdims=True)
        o = o * lax.rsqrt(ms + RMS_EPS) * g_ref[...] * (1.0 - lam_init)
        o_ref[:, blk * tq:(blk + 1) * tq] = o.astype(o_ref.dtype)

    sweeps = [sweep(blk, scratch[4 + 2 * blk:6 + 2 * blk]) for blk in range(A_BLOCKS)]
    first = 0
    sweeps[0].init()
    sweeps[0].scores(0, first)
    for blk, sw in enumerate(sweeps):
        n = A_BLOCKS * step + blk
        if blk % 2 == 0:
            sw.full_tiles(n, first)
            causal_buf = first
        else:
            sw.full_tiles(n - 1, first)
            sw.step(n - 1, first, 1 - first)
            causal_buf = 1 - first
        first = 1 - causal_buf
        if blk + 1 < A_BLOCKS:
            sweeps[blk + 1].init()
            sw.step(n, causal_buf, first, causal=True, then=sweeps[blk + 1])
        else:
            sw.softmax(causal_buf, n, causal=True)
        finish(sw, blk)


def _diff_attn(qaT, ka, vaT, lq1, lk1, lq2, lk2, g_col, lam_init, tq, tk, l):
    B, _, T = qaT.shape
    assert tq == tk and T % (A_BLOCKS * tq) == 0, "query block i ends at key tile i; A_BLOCKS blocks per step"
    nq = 2 * tq
    vec = _layer_spec(lq1, l)
    return pl.pallas_call(
        functools.partial(_diff_attn_kernel, tq=tq, tk=tk, lam_init=lam_init),
        grid=(B, A_HEADS, T // (A_BLOCKS * tq)),
        in_specs=[vec, vec, vec, vec, _layer_spec(g_col, l),
                  pl.BlockSpec((None, LANES, A_BLOCKS * tq), lambda b, h, i: (b, h // 2, i)),
                  pl.BlockSpec((None, T, LANES), lambda b, h, i: (b, 0, h // 2)),
                  pl.BlockSpec((None, A_V_DIM, T), lambda b, h, i: (b, h, 0))],
        out_specs=pl.BlockSpec((None, A_V_DIM, A_BLOCKS * tq), lambda b, h, i: (b, h, i)),
        out_shape=jax.ShapeDtypeStruct((B, A_WIDTH, T), BF16),
        scratch_shapes=_sweep_scratch(tk, nq, A_V_DIM, n_stats=A_BLOCKS),
        compiler_params=_params(("parallel", "parallel", "arbitrary")),
        name="diff_attn",
    )(lq1, lk1, lq2, lk2, g_col, qaT, ka, vaT)


POOL_HALO = 16


def _pool_mix(halo, cur, w_ref, scale_ref, t0):
    tm = cur.shape[0]
    arr = jnp.concatenate([halo, cur], axis=0)
    s2 = arr + pltpu.roll(arr, 1, axis=0)
    s4 = s2 + pltpu.roll(s2, 2, axis=0)
    s8 = s4 + pltpu.roll(s4, 4, axis=0)
    s16 = s8 + pltpu.roll(s8, 8, axis=0)
    sums = [s[POOL_HALO:, :] for s in (s2, s4, s8, s16)]
    lane = lax.broadcasted_iota(jnp.int32, (tm, POOL_WIDTH), 1)
    tpos = t0 + lax.broadcasted_iota(jnp.int32, (tm, POOL_WIDTH), 0)
    grp = lane // POOL_GROUP_DIM
    win = jnp.where(grp == 0, sums[0], jnp.where(grp == 1, sums[1], jnp.where(grp == 2, sums[2], sums[3])))
    width = jnp.where(grp == 0, 2, jnp.where(grp == 1, 4, jnp.where(grp == 2, 8, 16)))
    cnt = jnp.minimum(tpos + 1, width).astype(F32)
    y = win / cnt - cur
    z = _dot(y.astype(BF16), w_ref[...])
    return (z * scale_ref[...]).astype(BF16)


def _prep_pool(w_pool, pool_scale):
    depth, G, C, _ = w_pool.shape
    eye = jnp.asarray(np.eye(G, dtype=np.float32))
    w_bd = (w_pool[:, :, :, None, :] * eye[None, :, None, :, None]).reshape(depth, G * C, G * C)
    return w_bd.astype(BF16), pool_scale[:, None, :]


def _cmp_kernel(k_ref, v_ref, pk_ref, pv_ref, wk_ref, wv_ref, kc_ref, vc_ref):
    n = k_ref.shape[0]
    rowid = lax.broadcasted_iota(jnp.int32, (n, LANES), 0)

    def compress(x_ref, pe_ref, w_ref):
        x = x_ref[...].astype(F32)
        lo = _dot((x + pe_ref[0:1, :]).astype(BF16), w_ref[0])
        hi = _dot((x + pe_ref[1:2, :]).astype(BF16), w_ref[1])
        out = lo + pltpu.roll(hi, n - 1, axis=0)
        return jnp.where(rowid < n - 1, out, 0.0)

    kc_ref[...] = compress(k_ref, pk_ref, wk_ref).astype(kc_ref.dtype)
    vc_ref[...] = compress(v_ref, pv_ref, wv_ref).astype(vc_ref.dtype)


def _compress(kr, vr, pe_k2, pe_v2, wk2, wv2, l):
    B, n, W = kr.shape
    blk = pl.BlockSpec((None, n, W), lambda b: (b, 0, 0))
    pe = _layer_spec(pe_k2, l)
    wsp = _layer_spec(wk2, l)
    osp = pl.BlockSpec((None, n, LANES), lambda b: (b, 0, 0))
    return pl.pallas_call(
        _cmp_kernel,
        grid=(B,),
        in_specs=[blk, blk, pe, pe, wsp, wsp],
        out_specs=[osp, osp],
        out_shape=[jax.ShapeDtypeStruct((B, n, LANES), BF16)] * 2,
        compiler_params=_params(("parallel",)),
        name="nsa_compress",
    )(kr, vr, pe_k2, pe_v2, wk2, wv2)


def _prep_cmp(pe_k, pe_v, w_ck, w_cv):
    L, G, d, h = w_ck.shape[0], C_KV_GROUPS, C_HEAD_DIM, C_HEAD_DIM // 2
    same = jnp.asarray(np.eye(G, dtype=np.float32))
    shape2 = (L, 2, 16 * LANES, G * d)
    wk = w_ck.reshape(L, CMP_BLOCK, 2, 1, h, 2, 1, h) * same[None, None, None, :, None, None, :, None]
    pk2 = jnp.broadcast_to(pe_k.reshape(L, CMP_BLOCK, 2, 1, h), (L, CMP_BLOCK, 2, G, h))
    wv = w_cv.reshape(L, CMP_BLOCK, 1, d, 1, d) * same[None, None, :, None, :, None]
    pv2 = jnp.broadcast_to(pe_v.reshape(L, CMP_BLOCK, 1, d), (L, CMP_BLOCK, G, d))
    flat = lambda pe: pe.reshape(L, 2, 16 * LANES)
    return flat(pk2), flat(pv2), wk.reshape(shape2).astype(BF16), wv.reshape(shape2).astype(BF16)


def _nsa_kernel(q_ref, gt_ref, kc_ref, vc_ref, ov_ref, *refs, tq, tk, T):
    wb_refs, (ks_ref, vs_ref, kw_ref, vw_ref, o_ref) = refs[:C_BLOCKS], refs[C_BLOCKS:C_BLOCKS + 5]
    scratch = refs[C_BLOCKS + 5:]
    sweep_bufs, per_block = scratch[0:4], scratch[4:]
    g = pl.program_id(1)
    step = pl.program_id(2)
    n_cmp = kc_ref.shape[0]
    n_blk = T // SEL_BLOCK
    d = C_HEAD_DIM
    nq = C_HPG * tq
    bpt = tk // SEL_BLOCK
    rowg = (lax.broadcasted_iota(jnp.int32, (LANES, tq), 0) % 64) // 32

    class Block:
        def __init__(self, blk):
            self.cols = slice(blk * tq, (blk + 1) * tq)
            self.q0 = (C_BLOCKS * step + blk) * tq
            self.tpos = self.q0 + lax.broadcasted_iota(jnp.int32, (1, tq), 1)
            self.tpos3 = self.q0 + lax.broadcasted_iota(jnp.int32, (1, nq), 1) % tq
            self.cur = (self.tpos // SEL_BLOCK).astype(F32)
            qs = []
            for hh in range(C_HPG):
                qt = q_ref[hh * LANES:(hh + 1) * LANES, self.cols]
                qs.append(jnp.where(rowg == g, qt, jnp.zeros_like(qt)))
            self.q3 = jnp.concatenate(qs, axis=1)
            self.wb_ref = wb_refs[blk]
            self.sel_ref, m_ref, acc_ref, self.raw0_ref, self.ocw_ref = per_block[5 * blk:5 * blk + 5]
            self.stats = (m_ref, acc_ref)

    blocks = [Block(blk) for blk in range(C_BLOCKS)]

    def prelude(bk, n_c, n_b):
        q3, tpos3, cur, q0 = bk.q3, bk.tpos3, bk.cur, bk.q0
        cend = lax.broadcasted_iota(jnp.int32, (n_c, nq), 0) * CMP_STRIDE + (CMP_BLOCK - 1)
        s = jnp.where(cend <= tpos3, _dot(kc_ref[0:n_c, :], q3), MASKED)
        m = jnp.maximum(jnp.max(s, axis=0, keepdims=True), NEG_BIG)
        e = jnp.exp2(s - m)
        l = jnp.sum(e, axis=0, keepdims=True)
        p = e / jnp.where(l > 0.0, l, 1.0)
        bk.ocw_ref[0:d, :] = _dot(vc_ref[:, 0:n_c], p.astype(BF16))
        psum = p[:, :tq] + p[:, tq:2 * tq] + p[:, 2 * tq:]

        p_hi = psum.astype(BF16)
        p_lo = (psum - p_hi.astype(F32)).astype(BF16)
        ov = ov_ref[0:n_b, 0:n_c]
        imp = _dot(ov, p_hi) + _dot(ov, p_lo)
        jidx = lax.broadcasted_iota(jnp.int32, (n_b, tq), 0).astype(F32)
        forced = (jidx == 0.0) | (jidx == cur) | (jidx == cur - 1.0)
        imp = jnp.where(jidx > cur, -1.0, imp)
        imp = jnp.where(forced, TAKEN, imp)
        rounds = min(N_SEL, n_blk) - N_FORCED

        band = WINDOW + tq
        w0 = pl.multiple_of(jnp.maximum(q0 - WINDOW, 0), LANES)
        s = _dot(kw_ref[pl.ds(w0, band), :], q3) + jnp.concatenate([bk.wb_ref[...]] * C_HPG, axis=1)
        e = jnp.exp2(s - jnp.max(s, axis=0, keepdims=True))
        bk.ocw_ref[d:, :] = _dot(vw_ref[:, pl.ds(w0, band)], e.astype(BF16)) / jnp.sum(e, axis=0, keepdims=True)

        bk.raw0_ref[...] = _dot(ks_ref[0:tk, :], q3)

        def store_selection(taken_imp):
            bias = jnp.where(taken_imp == TAKEN, 0.0, MASKED)
            bk.sel_ref[0:n_b, :] = jnp.concatenate([bias] * C_HPG, axis=1)

        if n_b < n_blk:
            bk.sel_ref[n_b:, :] = jnp.full((n_blk - n_b, nq), MASKED, F32)
        quick = imp
        for _ in range(rounds):
            quick = jnp.where(quick == jnp.max(quick, axis=0, keepdims=True), TAKEN, quick)
        store_selection(quick)
        free = (jidx >= 1.0) & (jidx <= cur - 2.0)
        n_taken = jnp.sum(jnp.where(free & (quick == TAKEN), 1.0, 0.0), axis=0, keepdims=True)
        n_expected = jnp.minimum(jnp.maximum(cur - 2.0, 0.0), float(rounds))
        tie = jnp.max(jnp.abs(n_taken - n_expected)) > 0.0

        def exact_rounds():
            exact = imp
            for _ in range(rounds):
                mx = jnp.max(exact, axis=0, keepdims=True)
                first = jnp.min(jnp.where(exact == mx, jidx, float(n_blk)), axis=0, keepdims=True)
                exact = jnp.where(jidx == first, TAKEN, exact)
            store_selection(exact)

        return tie, exact_rounds

    def preludes(n_c, n_b):
        pending = [prelude(bk, n_c, n_b) for bk in blocks]
        for tie, exact_rounds in pending:
            pl.when(tie)(exact_rounds)

    early = (C_BLOCKS * step + C_BLOCKS) * tq <= T // 2
    pl.when(early)(lambda: preludes(n_cmp // 2, n_blk // 2))
    pl.when(jnp.logical_not(early))(lambda: preludes(n_cmp, n_blk))

    gts = gt_ref[...]
    for bk in blocks:
        def block_bias(j, cs, bk=bk):
            rows = bk.sel_ref[pl.ds(pl.multiple_of(j * bpt, bpt), bpt), cs]
            return jnp.concatenate([jnp.broadcast_to(rows[i:i + 1, :], (SEL_BLOCK, rows.shape[1]))
                                    for i in range(bpt)], axis=0)

        def score(j, cs, bk=bk, block_bias=block_bias):
            return _dot(ks_ref[pl.ds(pl.multiple_of(j * tk, tk), tk), :], bk.q3[:, cs]) + block_bias(j, cs)

        def first_scores(bk=bk, block_bias=block_bias):
            return bk.raw0_ref[...] + block_bias(0, slice(None))

        def value(j):
            return vs_ref[:, pl.ds(pl.multiple_of(j * tk, tk), tk)]

        o_slc = _causal_sweep(bk.q0 // tk, score, bk.tpos3, value, sweep_bufs + bk.stats, first_scores)
        for hh in range(C_HPG):
            r = hh * N_BRANCH
            c = slice(hh * tq, (hh + 1) * tq)
            gc = gts[:, bk.cols]
            out = (bk.ocw_ref[0:d, c] * gc[r:r + 1, :] + o_slc[:, c] * gc[r + 1:r + 2, :]
                   + bk.ocw_ref[d:, c] * gc[r + 2:r + 3, :])
            o_ref[hh * d:(hh + 1) * d, bk.cols] = out.astype(o_ref.dtype)


def _window_bias(tq):
    band, n_edge = WINDOW + tq, WINDOW // tq
    q0 = np.arange(n_edge + 1)[:, None, None] * tq
    wpos = np.maximum(q0 - WINDOW, 0) + np.arange(band)[None, :, None]
    tpos = q0 + np.arange(tq)[None, None, :]
    ok = (wpos <= tpos) & (wpos > tpos - WINDOW)
    return jnp.asarray(np.where(ok, 0.0, MASKED).astype(np.float32))


def _nsa(qcT, gtT, kc2, vcT, ovT, ksl, vslT, kwi, vwiT, tq, tk):
    B, _, T = qcT.shape
    n_cmp = kc2.shape[1]
    n_blk = T // SEL_BLOCK
    nq = C_HPG * tq
    tqs = C_BLOCKS * tq
    assert tk % tqs == 0 or tqs % tk == 0, "a step's blocks must not straddle a partial key tile boundary"
    assert tk % tq == 0, "the causal key tile must contain the whole query block"
    assert WINDOW % tq == 0 and T >= WINDOW + tq and (T // 2) % tqs == 0
    wb = _window_bias(tq)
    n_edge = WINDOW // tq
    kv_tok = pl.BlockSpec((None, T, LANES), lambda b, g, i: (b, 0, 0))
    kv_feat = pl.BlockSpec((None, C_HEAD_DIM, T), lambda b, g, i: (b, g, 0))
    wb_spec = lambda blk: pl.BlockSpec((None, WINDOW + tq, tq),
                                       lambda b, g, i: (jnp.minimum(C_BLOCKS * i + blk, n_edge), 0, 0))
    sweep = _sweep_scratch(tk, nq, C_HEAD_DIM)
    per_block = [pltpu.VMEM((n_blk, nq), F32)] + sweep[4:6] + [pltpu.VMEM((tk, nq), F32),
                                                              pltpu.VMEM((2 * C_HEAD_DIM, nq), F32)]
    return pl.pallas_call(
        functools.partial(_nsa_kernel, tq=tq, tk=tk, T=T),
        grid=(B, C_KV_GROUPS, T // tqs),
        in_specs=[pl.BlockSpec((None, C_WIDTH, tqs), lambda b, g, i: (b, 0, i)),
                  pl.BlockSpec((None, GATE_ROWS, tqs), lambda b, g, i: (b, g, i)),
                  pl.BlockSpec((None, n_cmp, LANES), lambda b, g, i: (b, 0, 0)),
                  pl.BlockSpec((None, C_HEAD_DIM, n_cmp), lambda b, g, i: (b, g, 0)),
                  pl.BlockSpec((n_blk, n_cmp), lambda b, g, i: (0, 0))]
                 + [wb_spec(blk) for blk in range(C_BLOCKS)] + [kv_tok, kv_feat, kv_tok, kv_feat],
        out_specs=pl.BlockSpec((None, C_HPG * C_HEAD_DIM, tqs), lambda b, g, i: (b, g, i)),
        out_shape=jax.ShapeDtypeStruct((B, C_WIDTH, T), BF16),
        scratch_shapes=sweep[0:4] + per_block * C_BLOCKS,
        compiler_params=_params(("parallel", "parallel", "arbitrary")),
        name="nsa_attn",
    )(qcT, gtT, kc2, vcT, ovT, *([wb] * C_BLOCKS), ksl, vslT, kwi, vwiT)


def _overlap_T(T):
    n_cmp = T // CMP_STRIDE
    n_blk = T // SEL_BLOCK
    cs = np.arange(n_cmp)[None, :] * CMP_STRIDE
    bs = np.arange(n_blk)[:, None] * SEL_BLOCK
    ov = (cs < bs + SEL_BLOCK) & (cs + CMP_BLOCK > bs) & (np.arange(n_cmp)[None, :] < n_cmp - 1)
    return jnp.asarray(ov.astype(np.float32), dtype=BF16)


def _layer_norm(y, g, b):
    mu = jnp.mean(y, axis=-1, keepdims=True)
    yc = y - mu
    var = jnp.mean(yc * yc, axis=-1, keepdims=True)
    return yc * lax.rsqrt(var + LN_EPS) * g + b


def _outproj_kernel(x_ref, oa_ref, uh_ref, u_ref, oc_ref, w_ref, wp_ref, ps_ref, g_ref, b_ref, o_ref, *, alpha, sub):
    i, tm = pl.program_id(1), x_ref.shape[0]
    tn = lambda a, w: lax.dot_general(a, w, (((0,), (0,)), ((), ())), preferred_element_type=F32)
    r1, r2 = A_WIDTH, A_WIDTH + POOL_WIDTH
    for r0 in range(0, tm, sub):
        rows = slice(r0, r0 + sub)
        if r0 == 0:
            halo = jnp.where(i == 0, jnp.zeros(uh_ref.shape, F32), uh_ref[...])
        else:
            halo = u_ref[r0 - POOL_HALO:r0, :]
        ob = _pool_mix(halo, u_ref[rows, :], wp_ref, ps_ref, i * tm + r0)
        mix = (tn(oa_ref[:, rows], w_ref[:r1, :]) + _dot(ob, w_ref[r1:r2, :]) + tn(oc_ref[:, rows], w_ref[r2:, :]))
        o_ref[rows, :] = _layer_norm(alpha * x_ref[rows, :] + mix, g_ref[...], b_ref[...])


def _outproj(x3, oaT, u, ocT, w, w_bd, p_scale, g, b, alpha, tm, sub, l):
    B, T, D = x3.shape
    tok = lambda n: pl.BlockSpec((None, tm, n), lambda bb, i: (bb, i, 0))
    feat = lambda n: pl.BlockSpec((None, n, tm), lambda bb, i: (bb, 0, i))
    halo = pl.BlockSpec((None, POOL_HALO, POOL_WIDTH),
                        lambda bb, i: (bb, jnp.maximum(i * (tm // POOL_HALO) - 1, 0), 0))
    return pl.pallas_call(
        functools.partial(_outproj_kernel, alpha=alpha, sub=sub),
        grid=(B, T // tm),
        in_specs=[tok(D), feat(oaT.shape[1]), halo, tok(POOL_WIDTH), feat(ocT.shape[1]),
                  _layer_spec(w, l), _layer_spec(w_bd, l), _layer_spec(p_scale, l),
                  _layer_spec(g, l), _layer_spec(b, l)],
        out_specs=tok(D),
        out_shape=jax.ShapeDtypeStruct((B, T, D), F32),
        compiler_params=_params(("parallel", "parallel")),
        name="outproj_ln",
    )(x3, oaT, u, u, ocT, w, w_bd, p_scale, g, b)


def _mlp_kernel(x_ref, wu_ref, wd_ref, g_ref, b_ref, o_ref, *, alpha, sub, tf):
    tm, F = x_ref.shape[0], wu_ref.shape[1]
    for r0 in range(0, tm, sub):
        x = x_ref[r0:r0 + sub, :]
        xb = x.astype(BF16)
        y = alpha * x
        for f0 in range(0, F, tf):
            hid = jnp.square(jnp.maximum(_dot(xb, wu_ref[:, f0:f0 + tf]), 0.0))
            y = y + _dot(hid.astype(BF16), wd_ref[f0:f0 + tf, :])
        o_ref[r0:r0 + sub, :] = _layer_norm(y, g_ref[...], b_ref[...])


def _mlp(x2, wu, wd, g, b, alpha, tm, sub, tf, l):
    BT, D = x2.shape
    once = lambda a: pl.BlockSpec((None,) + a.shape[1:], lambda i: (l, 0, 0), pipeline_mode=pl.Buffered(1))
    return pl.pallas_call(
        functools.partial(_mlp_kernel, alpha=alpha, sub=sub, tf=tf),
        grid=(BT // tm,),
        in_specs=[pl.BlockSpec((tm, D), lambda i: (i, 0)), once(wu), once(wd), _layer_spec(g, l), _layer_spec(b, l)],
        out_specs=pl.BlockSpec((tm, D), lambda i: (i, 0)),
        out_shape=jax.ShapeDtypeStruct((BT, D), F32),
        compiler_params=_params(("parallel",)),
        name="mlp_ln",
    )(x2, wu, wd, g, b)


def _tiles(T):
    return dict(
        tm_proj=min(512, T), tm_out=min(1024, T), sub_out=min(512, T), tm_mlp=min(1024, T), sub_mlp=min(512, T), tf_mlp=1024,
        a_tq=min(512, T), a_tk=min(512, T), c_tq=min(256, T), c_tk=min(512, T),
    )


def kernel(x, w_in, lam_q1, lam_k1, lam_q2, lam_k2, subln_g, w_pool, pool_scale, cmp_pe_k, cmp_pe_v,
           w_cmp_k, w_cmp_v, w_out, ln1_g, ln1_b, w_up, w_down, ln2_g, ln2_b):
    B, T, D = x.shape
    depth = w_in.shape[0]
    alpha = (2 * depth) ** 0.25
    tl = _tiles(T)
    tabs = _rope_tables(T)
    ovT = _overlap_T(T)
    w_tok, w_feat = _prep_w_in(w_in)
    w_bd, p_scale = _prep_pool(w_pool, pool_scale)
    pk2, pv2, wk2, wv2 = _prep_cmp(cmp_pe_k, cmp_pe_v, w_cmp_k, w_cmp_v)
    wo, wu, wd = w_out.astype(BF16), w_up.astype(BF16), w_down.astype(BF16)
    row = lambda a: a[:, None, :]
    lams = [row(a) for a in (lam_q1, lam_k1, lam_q2, lam_k2)]
    n16 = T // CMP_STRIDE
    x3 = x
    for l in range(depth):
        lam_init = 0.8 - 0.6 * math.exp(-0.3 * l)
        (ka, u, kcm, vcm, ksl, kwi, qaT, vaT, qcT, vslT, vwiT, gtT) = _inproj(x3, w_tok, w_feat, tabs, tl["tm_proj"], l)

        oaT = _diff_attn(qaT, ka, vaT, *lams, subln_g[:, :, None], lam_init, tl["a_tq"], tl["a_tk"], l)

        kc2, vc2 = _compress(kcm.reshape(B, n16, 16 * LANES), vcm.reshape(B, n16, 16 * LANES),
                             pk2, pv2, wk2, wv2, l)
        ocT = _nsa(qcT, gtT, kc2, vc2.transpose(0, 2, 1), ovT, ksl, vslT, kwi, vwiT, tl["c_tq"], tl["c_tk"])

        x3 = _outproj(x3, oaT, u, ocT, wo, w_bd, p_scale, row(ln1_g), row(ln1_b), alpha, tl["tm_out"], tl["sub_out"], l)
        x3 = _mlp(x3.reshape(B * T, D), wu, wd, row(ln2_g), row(ln2_b), alpha, tl["tm_mlp"], tl["sub_mlp"], tl["tf_mlp"],
                  l).reshape(B, T, D)
    return x3
```

```python
import functools
import math

import jax
import jax.numpy as jnp
import numpy as np
from jax import lax
from jax.experimental import pallas as pl
from jax.experimental.pallas import tpu as pltpu

A_HEADS = 6
A_QK_DIM = 32
A_V_DIM = 64
A_WIDTH = A_HEADS * A_V_DIM
POOL_WINDOWS = (2, 4, 8, 16)
POOL_GROUP_DIM = 64
POOL_WIDTH = 256
C_HEADS = 6
C_KV_GROUPS = 2
C_HPG = 3
C_HEAD_DIM = 64
C_WIDTH = C_HEADS * C_HEAD_DIM
CMP_BLOCK = 32
CMP_STRIDE = 16
SEL_BLOCK = 64
N_SEL = 16
WINDOW = 512
N_BRANCH = 3
FORCED_SCORE = 1.0e4
N_FORCED = 3
ROPE_THETA = 10000.0
LN_EPS = 1e-5
RMS_EPS = 1e-6

LANES = 128
LOG2E = 1.4426950408889634
NEG_BIG = -1e30
SWEEP_UNROLL = 4
MXU_TILE = 256
A_BLOCKS = 4
C_BLOCKS = 2
TAKEN = -3.0e38
MASKED = -3e38
VMEM_LIMIT = 56 * 1024 * 1024

BF16 = jnp.bfloat16
F32 = jnp.float32


def _dot(a, b):
    return jnp.dot(a, b, preferred_element_type=F32)


def _params(sem, vmem=VMEM_LIMIT):
    return pltpu.CompilerParams(dimension_semantics=sem, vmem_limit_bytes=vmem)


_IN_SPLITS = (384, 384, 384, 256, 384, 128, 128, 128, 128, 128, 128, 18)
_OFFS = np.concatenate([[0], np.cumsum(_IN_SPLITS)]).astype(np.int64)
A_QSCALE = A_QK_DIM ** -0.5 * LOG2E
C_QSCALE = C_HEAD_DIM ** -0.5 * LOG2E
GATE_ROWS = 16
N_TOK = 384 + 256 + 4 * LANES
N_FEAT = 3 * 384 + 2 * LANES + C_KV_GROUPS * GATE_ROWS


def _prep_w_in(w):
    o = _OFFS
    L, D = w.shape[0], w.shape[1]
    seg = lambda i: w[:, :, o[i]:o[i + 1]]
    diff = lambda a: a.reshape(L, D, A_HEADS // 2, 2, 2, 2, A_QK_DIM // 2).transpose(0, 1, 2, 5, 3, 4, 6).reshape(L, D, -1)
    cq = lambda a: a.reshape(L, D, C_KV_GROUPS, C_HPG, 2, C_HEAD_DIM // 2).transpose(0, 1, 3, 4, 2, 5).reshape(L, D, -1)
    ck = lambda a: a.reshape(L, D, C_KV_GROUPS, 2, C_HEAD_DIM // 2).transpose(0, 1, 3, 2, 4).reshape(L, D, -1)
    w_tok = jnp.concatenate([diff(seg(1)), seg(3), ck(seg(5)), seg(6), ck(seg(7)), ck(seg(9))], axis=2)
    per = C_HPG * N_BRANCH
    gates = jnp.pad(seg(11).reshape(L, D, C_KV_GROUPS, per), ((0, 0), (0, 0), (0, 0), (0, GATE_ROWS - per)))
    w_feat = jnp.concatenate([diff(seg(0)), seg(2), cq(seg(4)), seg(8), seg(10),
                              gates.reshape(L, D, C_KV_GROUPS * GATE_ROWS)], axis=2)
    return w_tok.astype(BF16), jnp.swapaxes(w_feat, 1, 2).astype(BF16)


def _rope_tables(T):
    def tab(dim):
        half = dim // 2
        inv = 1.0 / (ROPE_THETA ** (jnp.arange(0, dim, 2, dtype=F32) / dim))
        ang = jnp.arange(T, dtype=F32)[:, None] * inv[None, :]
        lane = np.arange(LANES)
        j = lane % half
        sign = np.where(lane < 64, -1.0, 1.0).astype(np.float32)
        cos, sin = jnp.cos(ang), jnp.sin(ang)
        return cos[:, j], sin[:, j] * sign[None, :], cos[:, j[:64]].T, sin[:, j[:64]].T
    return tab(A_QK_DIM) + tab(C_HEAD_DIM)


def _inproj_kernel(x_ref, wt_ref, wf_ref, ca_ref, sa_ref, caT_ref, saT_ref, cc_ref, sc_ref, ccT_ref, scT_ref,
                   ka_ref, u_ref, kcm_ref, vcm_ref, ksl_ref, kwi_ref,
                   qa_ref, va_ref, qc_ref, vsl_ref, vwi_ref, gt_ref):
    xb = x_ref[...].astype(BF16)
    tok = _dot(xb, wt_ref[...])
    feat = lax.dot_general(wf_ref[...], xb, (((1,), (1,)), ((), ())), preferred_element_type=F32)

    def rope_tok(p, cos, sin):
        return p * cos + pltpu.roll(p, 64, axis=1) * sin

    def rope_feat(p, cos, sin):
        a, b = p[:64, :], p[64:, :]
        return jnp.concatenate([a * cos - b * sin, b * cos + a * sin], axis=0)

    ca, sa, cc, sc = ca_ref[...], sa_ref[...], cc_ref[...], sc_ref[...]
    caT, saT, ccT, scT = caT_ref[...], saT_ref[...], ccT_ref[...], scT_ref[...]
    for t in range(3):
        r = slice(t * LANES, (t + 1) * LANES)
        ka_ref[:, r] = rope_tok(tok[:, r], ca, sa).astype(BF16)
        qa_ref[r, :] = (rope_feat(feat[r, :], caT, saT) * A_QSCALE).astype(BF16)
        rq = slice(768 + t * LANES, 768 + (t + 1) * LANES)
        qc_ref[r, :] = (rope_feat(feat[rq, :], ccT, scT) * C_QSCALE).astype(BF16)
    u_ref[...] = tok[:, 384:640]
    kcm_ref[...] = rope_tok(tok[:, 640:768], cc, sc).astype(BF16)
    vcm_ref[...] = tok[:, 768:896].astype(BF16)
    ksl_ref[...] = rope_tok(tok[:, 896:1024], cc, sc).astype(BF16)
    kwi_ref[...] = rope_tok(tok[:, 1024:1152], cc, sc).astype(BF16)
    va_ref[...] = feat[384:768, :].astype(BF16)
    vsl_ref[...] = feat[1152:1280, :].astype(BF16)
    vwi_ref[...] = feat[1280:1408, :].astype(BF16)
    gt_ref[...] = jax.nn.sigmoid(feat[1408:, :])


def _layer_spec(a, l):
    return pl.BlockSpec((None,) + a.shape[1:], lambda *_: (l,) + (0,) * (a.ndim - 1))


def _inproj(x3, w_tok, w_feat, tabs, tm, l):
    B, T, D = x3.shape
    tok = lambda n: pl.BlockSpec((None, tm, n), lambda b, i: (b, i, 0))
    feat = lambda n: pl.BlockSpec((None, n, tm), lambda b, i: (b, 0, i))
    full = lambda a: _layer_spec(a, l)
    tab_tok = pl.BlockSpec((tm, LANES), lambda b, i: (i, 0))
    tab_feat = pl.BlockSpec((64, tm), lambda b, i: (0, i))
    tok_outs = [(384, BF16), (256, F32)] + [(LANES, BF16)] * 4
    feat_outs = [(384, BF16)] * 3 + [(LANES, BF16)] * 2 + [(C_KV_GROUPS * GATE_ROWS, F32)]
    return pl.pallas_call(
        _inproj_kernel,
        grid=(B, T // tm),
        in_specs=[tok(D), full(w_tok), full(w_feat)] + [tab_tok, tab_tok, tab_feat, tab_feat] * 2,
        out_specs=[tok(n) for n, _ in tok_outs] + [feat(n) for n, _ in feat_outs],
        out_shape=([jax.ShapeDtypeStruct((B, T, n), dt) for n, dt in tok_outs]
                   + [jax.ShapeDtypeStruct((B, n, T), dt) for n, dt in feat_outs]),
        compiler_params=_params(("parallel", "parallel")),
        name="inproj_rope",
    )(x3, w_tok, w_feat, *tabs)


ONES_ROWS = 16


def _sweep_scratch(tk, nq, dv, n_stats=1):
    stats = [pltpu.VMEM((1, nq), F32), pltpu.VMEM((dv + ONES_ROWS, nq), F32)]
    return [pltpu.VMEM((tk, nq), BF16), pltpu.VMEM((tk, nq), BF16),
            pltpu.VMEM((1, nq), F32), pltpu.VMEM((1, nq), F32)] + stats * n_stats


class _Sweep:
    def __init__(self, score, qpos, value, bufs, stats):
        self.score, self.qpos, self.value = score, qpos, value
        self.s_refs, self.cm_refs = bufs[0:2], bufs[2:4]
        self.m_ref, self.acc_ref = stats
        self.tk, self.nq = self.s_refs[0].shape
        self.dv = self.acc_ref.shape[0] - ONES_ROWS

    def init(self):
        self.m_ref[...] = jnp.full(self.m_ref.shape, NEG_BIG, F32)
        self.acc_ref[...] = jnp.zeros(self.acc_ref.shape, F32)

    def scores(self, j, slot, fn=None):
        s = self.score(j, slice(None)) if fn is None else fn()
        self.s_refs[slot][...] = s.astype(BF16)
        self.cm_refs[slot][...] = jnp.max(s, axis=0, keepdims=True)

    def softmax(self, slot, j, causal=False):
        self.step(j, slot, None, causal)

    def step(self, j, cur, nxt, causal=False, then=None):
        scorer, j_next = (self, j + 1) if then is None else (then, 0)
        v = self.value(j)
        v1 = jnp.concatenate([v, jnp.ones((ONES_ROWS, v.shape[1]), v.dtype)], axis=0)
        for c0 in range(0, self.nq, MXU_TILE):
            cs = slice(c0, c0 + MXU_TILE)
            if nxt is not None:
                s_next = scorer.score(j_next, cs)
                self.s_refs[nxt][:, cs] = s_next.astype(BF16)
                self.cm_refs[nxt][:, cs] = jnp.max(s_next, axis=0, keepdims=True)
            s = self.s_refs[cur][:, cs]
            if causal:
                kpos = j * self.tk + lax.broadcasted_iota(jnp.int32, s.shape, 0)
                s = jnp.where(kpos <= self.qpos[:, cs], s, jnp.asarray(MASKED, s.dtype))
                cm = jnp.max(s, axis=0, keepdims=True).astype(F32)
            else:
                cm = self.cm_refs[cur][:, cs]
            m = self.m_ref[:, cs]
            m_new = jnp.maximum(m, cm)
            alpha = jnp.exp2(m - m_new)
            p = jnp.exp2(s - m_new.astype(s.dtype))
            self.m_ref[:, cs] = m_new
            self.acc_ref[:, cs] = self.acc_ref[:, cs] * alpha + _dot(v1, p)

    def full_tiles(self, count, par):
        def group(base, tiles):
            for u in range(tiles):
                self.step(base + u, (par + u) % 2, (par + u + 1) % 2)

        def quad(i, c):
            group(SWEEP_UNROLL * i, SWEEP_UNROLL)
            return c

        def pair(i, c):
            group(done + 2 * i, 2)
            return c

        lax.fori_loop(0, count // SWEEP_UNROLL, quad, 0)
        done = (count // SWEEP_UNROLL) * SWEEP_UNROLL
        lax.fori_loop(0, (count - done) // 2, pair, 0)

    def result(self):
        return self.acc_ref[:self.dv, :] / self.acc_ref[self.dv:self.dv + 1, :]


def _causal_sweep(n_full, score, qpos, value, scratch, first_scores=None):
    sw = _Sweep(score, qpos, value, tuple(scratch[0:4]), tuple(scratch[4:6]))
    sw.init()
    sw.scores(0, 0, first_scores)
    sw.full_tiles(n_full - n_full % 2, 0)

    @pl.when(n_full % 2 == 0)
    def _():
        sw.softmax(0, n_full, causal=True)

    @pl.when(n_full % 2 == 1)
    def _():
        sw.step(n_full - 1, 0, 1)
        sw.softmax(1, n_full, causal=True)

    return sw.result()


def _diff_attn_kernel(lq1_ref, lk1_ref, lq2_ref, lk2_ref, g_ref, q_ref, k_ref, v_ref, o_ref,
                      *scratch, tq, tk, lam_init):
    h = pl.program_id(1)
    step = pl.program_id(2)
    hl = h % 2
    nq = 2 * tq
    grp = (lax.broadcasted_iota(jnp.int32, (LANES, tq), 0) % 64) // 16
    lane_q = lax.broadcasted_iota(jnp.int32, (1, nq), 1) % tq

    def value(j):
        return v_ref[:, pl.ds(pl.multiple_of(j * tk, tk), tk)]

    def sweep(blk, stats):
        q = q_ref[:, blk * tq:(blk + 1) * tq]
        zero = jnp.zeros_like(q)
        q12 = jnp.concatenate([jnp.where(grp == 2 * hl, q, zero), jnp.where(grp == 2 * hl + 1, q, zero)], axis=1)
        qpos = (A_BLOCKS * step + blk) * tq + lane_q
        score = lambda j, cs: _dot(k_ref[pl.ds(pl.multiple_of(j * tk, tk), tk), :], q12[:, cs])
        return _Sweep(score, qpos, value, scratch[0:4], stats)

    dotp = lambda a, b: jnp.sum(a[...] * b[...], axis=1, keepdims=True)
    lam = jnp.exp(dotp(lq1_ref, lk1_ref)) - jnp.exp(dotp(lq2_ref, lk2_ref)) + lam_init

    def finish(sw, blk):
        on = sw.result()
        o = on[:, :tq] - lam * on[:, tq:]
        ms = jnp.mean(o * o, axis=0, keepdims=True)
        o = o * lax.rsqrt(ms + RMS_EPS) * g_ref[...] * (1.0 - lam_init)
        o_ref[:, blk * tq:(blk + 1) * tq] = o.astype(o_ref.dtype)

    sweeps = [sweep(blk, scratch[4 + 2 * blk:6 + 2 * blk]) for blk in range(A_BLOCKS)]
    first = 0
    sweeps[0].init()
    sweeps[0].scores(0, first)
    for blk, sw in enumerate(sweeps):
        n = A_BLOCKS * step + blk
        if blk % 2 == 0:
            sw.full_tiles(n, first)
            causal_buf = first
        else:
            sw.full_tiles(n - 1, first)
            sw.step(n - 1, first, 1 - first)
            causal_buf = 1 - first
        first = 1 - causal_buf
        if blk + 1 < A_BLOCKS:
            sweeps[blk + 1].init()
            sw.step(n, causal_buf, first, causal=True, then=sweeps[blk + 1])
        else:
            sw.softmax(causal_buf, n, causal=True)
        finish(sw, blk)


def _diff_attn(qaT, ka, vaT, lq1, lk1, lq2, lk2, g_col, lam_init, tq, tk, l):
    B, _, T = qaT.shape
    assert tq == tk and T % (A_BLOCKS * tq) == 0, "query block i ends at key tile i; A_BLOCKS blocks per step"
    nq = 2 * tq
    vec = _layer_spec(lq1, l)
    return pl.pallas_call(
        functools.partial(_diff_attn_kernel, tq=tq, tk=tk, lam_init=lam_init),
        grid=(B, A_HEADS, T // (A_BLOCKS * tq)),
        in_specs=[vec, vec, vec, vec, _layer_spec(g_col, l),
                  pl.BlockSpec((None, LANES, A_BLOCKS * tq), lambda b, h, i: (b, h // 2, i)),
                  pl.BlockSpec((None, T, LANES), lambda b, h, i: (b, 0, h // 2)),
                  pl.BlockSpec((None, A_V_DIM, T), lambda b, h, i: (b, h, 0))],
        out_specs=pl.BlockSpec((None, A_V_DIM, A_BLOCKS * tq), lambda b, h, i: (b, h, i)),
        out_shape=jax.ShapeDtypeStruct((B, A_WIDTH, T), BF16),
        scratch_shapes=_sweep_scratch(tk, nq, A_V_DIM, n_stats=A_BLOCKS),
        compiler_params=_params(("parallel", "parallel", "arbitrary")),
        name="diff_attn",
    )(lq1, lk1, lq2, lk2, g_col, qaT, ka, vaT)


POOL_HALO = 16


def _pool_mix(halo, cur, w_ref, scale_ref, t0):
    tm = cur.shape[0]
    arr = jnp.concatenate([halo, cur], axis=0)
    s2 = arr + pltpu.roll(arr, 1, axis=0)
    s4 = s2 + pltpu.roll(s2, 2, axis=0)
    s8 = s4 + pltpu.roll(s4, 4, axis=0)
    s16 = s8 + pltpu.roll(s8, 8, axis=0)
    sums = [s[POOL_HALO:, :] for s in (s2, s4, s8, s16)]
    lane = lax.broadcasted_iota(jnp.int32, (tm, POOL_WIDTH), 1)
    tpos = t0 + lax.broadcasted_iota(jnp.int32, (tm, POOL_WIDTH), 0)
    grp = lane // POOL_GROUP_DIM
    win = jnp.where(grp == 0, sums[0], jnp.where(grp == 1, sums[1], jnp.where(grp == 2, sums[2], sums[3])))
    width = jnp.where(grp == 0, 2, jnp.where(grp == 1, 4, jnp.where(grp == 2, 8, 16)))
    cnt = jnp.minimum(tpos + 1, width).astype(F32)
    y = win / cnt - cur
    z = _dot(y.astype(BF16), w_ref[...])
    return (z * scale_ref[...]).astype(BF16)


def _prep_pool(w_pool, pool_scale):
    depth, G, C, _ = w_pool.shape
    eye = jnp.asarray(np.eye(G, dtype=np.float32))
    w_bd = (w_pool[:, :, :, None, :] * eye[None, :, None, :, None]).reshape(depth, G * C, G * C)
    return w_bd.astype(BF16), pool_scale[:, None, :]


def _cmp_kernel(k_ref, v_ref, pk_ref, pv_ref, wk_ref, wv_ref, kc_ref, vc_ref):
    n = k_ref.shape[0]
    rowid = lax.broadcasted_iota(jnp.int32, (n, LANES), 0)

    def compress(x_ref, pe_ref, w_ref):
        x = x_ref[...].astype(F32)
        lo = _dot((x + pe_ref[0:1, :]).astype(BF16), w_ref[0])
        hi = _dot((x + pe_ref[1:2, :]).astype(BF16), w_ref[1])
        out = lo + pltpu.roll(hi, n - 1, axis=0)
        return jnp.where(rowid < n - 1, out, 0.0)

    kc_ref[...] = compress(k_ref, pk_ref, wk_ref).astype(kc_ref.dtype)
    vc_ref[...] = compress(v_ref, pv_ref, wv_ref).astype(vc_ref.dtype)


def _compress(kr, vr, pe_k2, pe_v2, wk2, wv2, l):
    B, n, W = kr.shape
    blk = pl.BlockSpec((None, n, W), lambda b: (b, 0, 0))
    pe = _layer_spec(pe_k2, l)
    wsp = _layer_spec(wk2, l)
    osp = pl.BlockSpec((None, n, LANES), lambda b: (b, 0, 0))
    return pl.pallas_call(
        _cmp_kernel,
        grid=(B,),
        in_specs=[blk, blk, pe, pe, wsp, wsp],
        out_specs=[osp, osp],
        out_shape=[jax.ShapeDtypeStruct((B, n, LANES), BF16)] * 2,
        compiler_params=_params(("parallel",)),
        name="nsa_compress",
    )(kr, vr, pe_k2, pe_v2, wk2, wv2)


def _prep_cmp(pe_k, pe_v, w_ck, w_cv):
    L, G, d, h = w_ck.shape[0], C_KV_GROUPS, C_HEAD_DIM, C_HEAD_DIM // 2
    same = jnp.asarray(np.eye(G, dtype=np.float32))
    shape2 = (L, 2, 16 * LANES, G * d)
    wk = w_ck.reshape(L, CMP_BLOCK, 2, 1, h, 2, 1, h) * same[None, None, None, :, None, None, :, None]
    pk2 = jnp.broadcast_to(pe_k.reshape(L, CMP_BLOCK, 2, 1, h), (L, CMP_BLOCK, 2, G, h))
    wv = w_cv.reshape(L, CMP_BLOCK, 1, d, 1, d) * same[None, None, :, None, :, None]
    pv2 = jnp.broadcast_to(pe_v.reshape(L, CMP_BLOCK, 1, d), (L, CMP_BLOCK, G, d))
    flat = lambda pe: pe.reshape(L, 2, 16 * LANES)
    return flat(pk2), flat(pv2), wk.reshape(shape2).astype(BF16), wv.reshape(shape2).astype(BF16)


def _nsa_kernel(q_ref, gt_ref, kc_ref, vc_ref, ov_ref, *refs, tq, tk, T):
    wb_refs, (ks_ref, vs_ref, kw_ref, vw_ref, o_ref) = refs[:C_BLOCKS], refs[C_BLOCKS:C_BLOCKS + 5]
    scratch = refs[C_BLOCKS + 5:]
    sweep_bufs, per_block = scratch[0:4], scratch[4:]
    g = pl.program_id(1)
    step = pl.program_id(2)
    n_cmp = kc_ref.shape[0]
    n_blk = T // SEL_BLOCK
    d = C_HEAD_DIM
    nq = C_HPG * tq
    bpt = tk // SEL_BLOCK
    rowg = (lax.broadcasted_iota(jnp.int32, (LANES, tq), 0) % 64) // 32

    class Block:
        def __init__(self, blk):
            self.cols = slice(blk * tq, (blk + 1) * tq)
            self.q0 = (C_BLOCKS * step + blk) * tq
            self.tpos = self.q0 + lax.broadcasted_iota(jnp.int32, (1, tq), 1)
            self.tpos3 = self.q0 + lax.broadcasted_iota(jnp.int32, (1, nq), 1) % tq
            self.cur = (self.tpos // SEL_BLOCK).astype(F32)
            qs = []
            for hh in range(C_HPG):
                qt = q_ref[hh * LANES:(hh + 1) * LANES, self.cols]
                qs.append(jnp.where(rowg == g, qt, jnp.zeros_like(qt)))
            self.q3 = jnp.concatenate(qs, axis=1)
            self.wb_ref = wb_refs[blk]
            self.sel_ref, m_ref, acc_ref, self.raw0_ref, self.ocw_ref = per_block[5 * blk:5 * blk + 5]
            self.stats = (m_ref, acc_ref)

    blocks = [Block(blk) for blk in range(C_BLOCKS)]

    def prelude(bk, n_c, n_b):
        q3, tpos3, cur, q0 = bk.q3, bk.tpos3, bk.cur, bk.q0
        cend = lax.broadcasted_iota(jnp.int32, (n_c, nq), 0) * CMP_STRIDE + (CMP_BLOCK - 1)
        s = jnp.where(cend <= tpos3, _dot(kc_ref[0:n_c, :], q3), MASKED)
        m = jnp.maximum(jnp.max(s, axis=0, keepdims=True), NEG_BIG)
        e = jnp.exp2(s - m)
        l = jnp.sum(e, axis=0, keepdims=True)
        p = e / jnp.where(l > 0.0, l, 1.0)
        bk.ocw_ref[0:d, :] = _dot(vc_ref[:, 0:n_c], p.astype(BF16))
        psum = p[:, :tq] + p[:, tq:2 * tq] + p[:, 2 * tq:]

        p_hi = psum.astype(BF16)
        p_lo = (psum - p_hi.astype(F32)).astype(BF16)
        ov = ov_ref[0:n_b, 0:n_c]
        imp = _dot(ov, p_hi) + _dot(ov, p_lo)
        jidx = lax.broadcasted_iota(jnp.int32, (n_b, tq), 0).astype(F32)
        forced = (jidx == 0.0) | (jidx == cur) | (jidx == cur - 1.0)
        imp = jnp.where(jidx > cur, -1.0, imp)
        imp = jnp.where(forced, TAKEN, imp)
        rounds = min(N_SEL, n_blk) - N_FORCED

        band = WINDOW + tq
        w0 = pl.multiple_of(jnp.maximum(q0 - WINDOW, 0), LANES)
        s = _dot(kw_ref[pl.ds(w0, band), :], q3) + jnp.concatenate([bk.wb_ref[...]] * C_HPG, axis=1)
        e = jnp.exp2(s - jnp.max(s, axis=0, keepdims=True))
        bk.ocw_ref[d:, :] = _dot(vw_ref[:, pl.ds(w0, band)], e.astype(BF16)) / jnp.sum(e, axis=0, keepdims=True)

        bk.raw0_ref[...] = _dot(ks_ref[0:tk, :], q3)

        def store_selection(taken_imp):
            bias = jnp.where(taken_imp == TAKEN, 0.0, MASKED)
            bk.sel_ref[0:n_b, :] = jnp.concatenate([bias] * C_HPG, axis=1)

        if n_b < n_blk:
            bk.sel_ref[n_b:, :] = jnp.full((n_blk - n_b, nq), MASKED, F32)
        quick = imp
        for _ in range(rounds):
            quick = jnp.where(quick == jnp.max(quick, axis=0, keepdims=True), TAKEN, quick)
        store_selection(quick)
        free = (jidx >= 1.0) & (jidx <= cur - 2.0)
        n_taken = jnp.sum(jnp.where(free & (quick == TAKEN), 1.0, 0.0), axis=0, keepdims=True)
        n_expected = jnp.minimum(jnp.maximum(cur - 2.0, 0.0), float(rounds))
        tie = jnp.max(jnp.abs(n_taken - n_expected)) > 0.0

        def exact_rounds():
            exact = imp
            for _ in range(rounds):
                mx = jnp.max(exact, axis=0, keepdims=True)
                first = jnp.min(jnp.where(exact == mx, jidx, float(n_blk)), axis=0, keepdims=True)
                exact = jnp.where(jidx == first, TAKEN, exact)
            store_selection(exact)

        return tie, exact_rounds

    def preludes(n_c, n_b):
        pending = [prelude(bk, n_c, n_b) for bk in blocks]
        for tie, exact_rounds in pending:
            pl.when(tie)(exact_rounds)

    early = (C_BLOCKS * step + C_BLOCKS) * tq <= T // 2
    pl.when(early)(lambda: preludes(n_cmp // 2, n_blk // 2))
    pl.when(jnp.logical_not(early))(lambda: preludes(n_cmp, n_blk))

    gts = gt_ref[...]
    for bk in blocks:
        def block_bias(j, cs, bk=bk):
            rows = bk.sel_ref[pl.ds(pl.multiple_of(j * bpt, bpt), bpt), cs]
            return jnp.concatenate([jnp.broadcast_to(rows[i:i + 1, :], (SEL_BLOCK, rows.shape[1]))
                                    for i in range(bpt)], axis=0)

        def score(j, cs, bk=bk, block_bias=block_bias):
            return _dot(ks_ref[pl.ds(pl.multiple_of(j * tk, tk), tk), :], bk.q3[:, cs]) + block_bias(j, cs)

        def first_scores(bk=bk, block_bias=block_bias):
            return bk.raw0_ref[...] + block_bias(0, slice(None))

        def value(j):
            return vs_ref[:, pl.ds(pl.multiple_of(j * tk, tk), tk)]

        o_slc = _causal_sweep(bk.q0 // tk, score, bk.tpos3, value, sweep_bufs + bk.stats, first_scores)
        for hh in range(C_HPG):
            r = hh * N_BRANCH
            c = slice(hh * tq, (hh + 1) * tq)
            gc = gts[:, bk.cols]
            out = (bk.ocw_ref[0:d, c] * gc[r:r + 1, :] + o_slc[:, c] * gc[r + 1:r + 2, :]
                   + bk.ocw_ref[d:, c] * gc[r + 2:r + 3, :])
            o_ref[hh * d:(hh + 1) * d, bk.cols] = out.astype(o_ref.dtype)


def _window_bias(tq):
    band, n_edge = WINDOW + tq, WINDOW // tq
    q0 = np.arange(n_edge + 1)[:, None, None] * tq
    wpos = np.maximum(q0 - WINDOW, 0) + np.arange(band)[None, :, None]
    tpos = q0 + np.arange(tq)[None, None, :]
    ok = (wpos <= tpos) & (wpos > tpos - WINDOW)
    return jnp.asarray(np.where(ok, 0.0, MASKED).astype(np.float32))


def _nsa(qcT, gtT, kc2, vcT, ovT, ksl, vslT, kwi, vwiT, tq, tk):
    B, _, T = qcT.shape
    n_cmp = kc2.shape[1]
    n_blk = T // SEL_BLOCK
    nq = C_HPG * tq
    tqs = C_BLOCKS * tq
    assert tk % tqs == 0 or tqs % tk == 0, "a step's blocks must not straddle a partial key tile boundary"
    assert tk % tq == 0, "the causal key tile must contain the whole query block"
    assert WINDOW % tq == 0 and T >= WINDOW + tq and (T // 2) % tqs == 0
    wb = _window_bias(tq)
    n_edge = WINDOW // tq
    kv_tok = pl.BlockSpec((None, T, LANES), lambda b, g, i: (b, 0, 0))
    kv_feat = pl.BlockSpec((None, C_HEAD_DIM, T), lambda b, g, i: (b, g, 0))
    wb_spec = lambda blk: pl.BlockSpec((None, WINDOW + tq, tq),
                                       lambda b, g, i: (jnp.minimum(C_BLOCKS * i + blk, n_edge), 0, 0))
    sweep = _sweep_scratch(tk, nq, C_HEAD_DIM)
    per_block = [pltpu.VMEM((n_blk, nq), F32)] + sweep[4:6] + [pltpu.VMEM((tk, nq), F32),
                                                              pltpu.VMEM((2 * C_HEAD_DIM, nq), F32)]
    return pl.pallas_call(
        functools.partial(_nsa_kernel, tq=tq, tk=tk, T=T),
        grid=(B, C_KV_GROUPS, T // tqs),
        in_specs=[pl.BlockSpec((None, C_WIDTH, tqs), lambda b, g, i: (b, 0, i)),
                  pl.BlockSpec((None, GATE_ROWS, tqs), lambda b, g, i: (b, g, i)),
                  pl.BlockSpec((None, n_cmp, LANES), lambda b, g, i: (b, 0, 0)),
                  pl.BlockSpec((None, C_HEAD_DIM, n_cmp), lambda b, g, i: (b, g, 0)),
                  pl.BlockSpec((n_blk, n_cmp), lambda b, g, i: (0, 0))]
                 + [wb_spec(blk) for blk in range(C_BLOCKS)] + [kv_tok, kv_feat, kv_tok, kv_feat],
        out_specs=pl.BlockSpec((None, C_HPG * C_HEAD_DIM, tqs), lambda b, g, i: (b, g, i)),
        out_shape=jax.ShapeDtypeStruct((B, C_WIDTH, T), BF16),
        scratch_shapes=sweep[0:4] + per_block * C_BLOCKS,
        compiler_params=_params(("parallel", "parallel", "arbitrary")),
        name="nsa_attn",
    )(qcT, gtT, kc2, vcT, ovT, *([wb] * C_BLOCKS), ksl, vslT, kwi, vwiT)


def _overlap_T(T):
    n_cmp = T // CMP_STRIDE
    n_blk = T // SEL_BLOCK
    cs = np.arange(n_cmp)[None, :] * CMP_STRIDE
    bs = np.arange(n_blk)[:, None] * SEL_BLOCK
    ov = (cs < bs + SEL_BLOCK) & (cs + CMP_BLOCK > bs) & (np.arange(n_cmp)[None, :] < n_cmp - 1)
    return jnp.asarray(ov.astype(np.float32), dtype=BF16)


def _layer_norm(y, g, b):
    mu = jnp.mean(y, axis=-1, keepdims=True)
    yc = y - mu
    var = jnp.mean(yc * yc, axis=-1, keepdims=True)
    return yc * lax.rsqrt(var + LN_EPS) * g + b


def _outproj_kernel(x_ref, oa_ref, uh_ref, u_ref, oc_ref, w_ref, wp_ref, ps_ref, g_ref, b_ref, o_ref, *, alpha, sub):
    i, tm = pl.program_id(1), x_ref.shape[0]
    tn = lambda a, w: lax.dot_general(a, w, (((0,), (0,)), ((), ())), preferred_element_type=F32)
    r1, r2 = A_WIDTH, A_WIDTH + POOL_WIDTH
    for r0 in range(0, tm, sub):
        rows = slice(r0, r0 + sub)
        if r0 == 0:
            halo = jnp.where(i == 0, jnp.zeros(uh_ref.shape, F32), uh_ref[...])
        else:
            halo = u_ref[r0 - POOL_HALO:r0, :]
        ob = _pool_mix(halo, u_ref[rows, :], wp_ref, ps_ref, i * tm + r0)
        mix = (tn(oa_ref[:, rows], w_ref[:r1, :]) + _dot(ob, w_ref[r1:r2, :]) + tn(oc_ref[:, rows], w_ref[r2:, :]))
        o_ref[rows, :] = _layer_norm(alpha * x_ref[rows, :] + mix, g_ref[...], b_ref[...])


def _outproj(x3, oaT, u, ocT, w, w_bd, p_scale, g, b, alpha, tm, sub, l):
    B, T, D = x3.shape
    tok = lambda n: pl.BlockSpec((None, tm, n), lambda bb, i: (bb, i, 0))
    feat = lambda n: pl.BlockSpec((None, n, tm), lambda bb, i: (bb, 0, i))
    halo = pl.BlockSpec((None, POOL_HALO, POOL_WIDTH),
                        lambda bb, i: (bb, jnp.maximum(i * (tm // POOL_HALO) - 1, 0), 0))
    return pl.pallas_call(
        functools.partial(_outproj_kernel, alpha=alpha, sub=sub),
        grid=(B, T // tm),
        in_specs=[tok(D), feat(oaT.shape[1]), halo, tok(POOL_WIDTH), feat(ocT.shape[1]),
                  _layer_spec(w, l), _layer_spec(w_bd, l), _layer_spec(p_scale, l),
                  _layer_spec(g, l), _layer_spec(b, l)],
        out_specs=tok(D),
        out_shape=jax.ShapeDtypeStruct((B, T, D), F32),
        compiler_params=_params(("parallel", "parallel")),
        name="outproj_ln",
    )(x3, oaT, u, u, ocT, w, w_bd, p_scale, g, b)


def _mlp_kernel(x_ref, wu_ref, wd_ref, g_ref, b_ref, o_ref, *, alpha, sub, tf):
    tm, F = x_ref.shape[0], wu_ref.shape[1]
    for r0 in range(0, tm, sub):
        x = x_ref[r0:r0 + sub, :]
        xb = x.astype(BF16)
        y = alpha * x
        for f0 in range(0, F, tf):
            hid = jnp.square(jnp.maximum(_dot(xb, wu_ref[:, f0:f0 + tf]), 0.0))
            y = y + _dot(hid.astype(BF16), wd_ref[f0:f0 + tf, :])
        o_ref[r0:r0 + sub, :] = _layer_norm(y, g_ref[...], b_ref[...])


def _mlp(x2, wu, wd, g, b, alpha, tm, sub, tf, l):
    BT, D = x2.shape
    once = lambda a: pl.BlockSpec((None,) + a.shape[1:], lambda i: (l, 0, 0), pipeline_mode=pl.Buffered(1))
    return pl.pallas_call(
        functools.partial(_mlp_kernel, alpha=alpha, sub=sub, tf=tf),
        grid=(BT // tm,),
        in_specs=[pl.BlockSpec((tm, D), lambda i: (i, 0)), once(wu), once(wd), _layer_spec(g, l), _layer_spec(b, l)],
        out_specs=pl.BlockSpec((tm, D), lambda i: (i, 0)),
        out_shape=jax.ShapeDtypeStruct((BT, D), F32),
        compiler_params=_params(("parallel",)),
        name="mlp_ln",
    )(x2, wu, wd, g, b)


def _tiles(T):
    return dict(
        tm_proj=min(512, T), tm_out=min(1024, T), sub_out=min(256, T), tm_mlp=min(1024, T), sub_mlp=min(256, T), tf_mlp=1024,
        a_tq=min(512, T), a_tk=min(512, T), c_tq=min(256, T), c_tk=min(512, T),
    )


def kernel(x, w_in, lam_q1, lam_k1, lam_q2, lam_k2, subln_g, w_pool, pool_scale, cmp_pe_k, cmp_pe_v,
           w_cmp_k, w_cmp_v, w_out, ln1_g, ln1_b, w_up, w_down, ln2_g, ln2_b):
    B, T, D = x.shape
    depth = w_in.shape[0]
    alpha = (2 * depth) ** 0.25
    tl = _tiles(T)
    tabs = _rope_tables(T)
    ovT = _overlap_T(T)
    w_tok, w_feat = _prep_w_in(w_in)
    w_bd, p_scale = _prep_pool(w_pool, pool_scale)
    pk2, pv2, wk2, wv2 = _prep_cmp(cmp_pe_k, cmp_pe_v, w_cmp_k, w_cmp_v)
    wo, wu, wd = w_out.astype(BF16), w_up.astype(BF16), w_down.astype(BF16)
    row = lambda a: a[:, None, :]
    lams = [row(a) for a in (lam_q1, lam_k1, lam_q2, lam_k2)]
    n16 = T // CMP_STRIDE
    x3 = x
    for l in range(depth):
        lam_init = 0.8 - 0.6 * math.exp(-0.3 * l)
        (ka, u, kcm, vcm, ksl, kwi, qaT, vaT, qcT, vslT, vwiT, gtT) = _inproj(x3, w_tok, w_feat, tabs, tl["tm_proj"], l)

        oaT = _diff_attn(qaT, ka, vaT, *lams, subln_g[:, :, None], lam_init, tl["a_tq"], tl["a_tk"], l)

        kc2, vc2 = _compress(kcm.reshape(B, n16, 16 * LANES), vcm.reshape(B, n16, 16 * LANES),
                             pk2, pv2, wk2, wv2, l)
        ocT = _nsa(qcT, gtT, kc2, vc2.transpose(0, 2, 1), ovT, ksl, vslT, kwi, vwiT, tl["c_tq"], tl["c_tk"])

        x3 = _outproj(x3, oaT, u, ocT, wo, w_bd, p_scale, row(ln1_g), row(ln1_b), alpha, tl["tm_out"], tl["sub_out"], l)
        x3 = _mlp(x3.reshape(B * T, D), wu, wd, row(ln2_g), row(ln2_b), alpha, tl["tm_mlp"], tl["sub_mlp"], tl["tf_mlp"],
                  l).reshape(B, T, D)
    return x3
```

```python
import functools
import math

import jax
import jax.numpy as jnp
import numpy as np
from jax import lax
from jax.experimental import pallas as pl
from jax.experimental.pallas import tpu as pltpu

A_HEADS = 6
A_QK_DIM = 32
A_V_DIM = 64
A_WIDTH = A_HEADS * A_V_DIM
POOL_WINDOWS = (2, 4, 8, 16)
POOL_GROUP_DIM = 64
POOL_WIDTH = 256
C_HEADS = 6
C_KV_GROUPS = 2
C_HPG = 3
C_HEAD_DIM = 64
C_WIDTH = C_HEADS * C_HEAD_DIM
CMP_BLOCK = 32
CMP_STRIDE = 16
SEL_BLOCK = 64
N_SEL = 16
WINDOW = 512
N_BRANCH = 3
FORCED_SCORE = 1.0e4
N_FORCED = 3
ROPE_THETA = 10000.0
LN_EPS = 1e-5
RMS_EPS = 1e-6

LANES = 128
LOG2E = 1.4426950408889634
NEG_BIG = -1e30
SWEEP_UNROLL = 4
MXU_TILE = 256
A_BLOCKS = 4
C_BLOCKS = 2
TAKEN = -3.0e38
MASKED = -3e38
VMEM_LIMIT = 56 * 1024 * 1024

BF16 = jnp.bfloat16
F32 = jnp.float32


def _dot(a, b):
    return jnp.dot(a, b, preferred_element_type=F32)


def _params(sem, vmem=VMEM_LIMIT):
    return pltpu.CompilerParams(dimension_semantics=sem, vmem_limit_bytes=vmem)


_IN_SPLITS = (384, 384, 384, 256, 384, 128, 128, 128, 128, 128, 128, 18)
_OFFS = np.concatenate([[0], np.cumsum(_IN_SPLITS)]).astype(np.int64)
A_QSCALE = A_QK_DIM ** -0.5 * LOG2E
C_QSCALE = C_HEAD_DIM ** -0.5 * LOG2E
GATE_ROWS = 16
N_TOK = 384 + 256 + 4 * LANES
N_FEAT = 3 * 384 + 2 * LANES + C_KV_GROUPS * GATE_ROWS


def _prep_w_in(w):
    o = _OFFS
    L, D = w.shape[0], w.shape[1]
    seg = lambda i: w[:, :, o[i]:o[i + 1]]
    diff = lambda a: a.reshape(L, D, A_HEADS // 2, 2, 2, 2, A_QK_DIM // 2).transpose(0, 1, 2, 5, 3, 4, 6).reshape(L, D, -1)
    cq = lambda a: a.reshape(L, D, C_KV_GROUPS, C_HPG, 2, C_HEAD_DIM // 2).transpose(0, 1, 3, 4, 2, 5).reshape(L, D, -1)
    ck = lambda a: a.reshape(L, D, C_KV_GROUPS, 2, C_HEAD_DIM // 2).transpose(0, 1, 3, 2, 4).reshape(L, D, -1)
    w_tok = jnp.concatenate([diff(seg(1)), seg(3), ck(seg(5)), seg(6), ck(seg(7)), ck(seg(9))], axis=2)
    per = C_HPG * N_BRANCH
    gates = jnp.pad(seg(11).reshape(L, D, C_KV_GROUPS, per), ((0, 0), (0, 0), (0, 0), (0, GATE_ROWS - per)))
    w_feat = jnp.concatenate([diff(seg(0)), seg(2), cq(seg(4)), seg(8), seg(10),
                              gates.reshape(L, D, C_KV_GROUPS * GATE_ROWS)], axis=2)
    return w_tok.astype(BF16), jnp.swapaxes(w_feat, 1, 2).astype(BF16)


def _rope_tables(T):
    def tab(dim):
        half = dim // 2
        inv = 1.0 / (ROPE_THETA ** (jnp.arange(0, dim, 2, dtype=F32) / dim))
        ang = jnp.arange(T, dtype=F32)[:, None] * inv[None, :]
        lane = np.arange(LANES)
        j = lane % half
        sign = np.where(lane < 64, -1.0, 1.0).astype(np.float32)
        cos, sin = jnp.cos(ang), jnp.sin(ang)
        return cos[:, j], sin[:, j] * sign[None, :], cos[:, j[:64]].T, sin[:, j[:64]].T
    return tab(A_QK_DIM) + tab(C_HEAD_DIM)


def _inproj_kernel(x_ref, wt_ref, wf_ref, ca_ref, sa_ref, caT_ref, saT_ref, cc_ref, sc_ref, ccT_ref, scT_ref,
                   ka_ref, u_ref, kcm_ref, vcm_ref, ksl_ref, kwi_ref,
                   qa_ref, va_ref, qc_ref, vsl_ref, vwi_ref, gt_ref):
    xb = x_ref[...].astype(BF16)
    tok = _dot(xb, wt_ref[...])
    feat = lax.dot_general(wf_ref[...], xb, (((1,), (1,)), ((), ())), preferred_element_type=F32)

    def rope_tok(p, cos, sin):
        return p * cos + pltpu.roll(p, 64, axis=1) * sin

    def rope_feat(p, cos, sin):
        a, b = p[:64, :], p[64:, :]
        return jnp.concatenate([a * cos - b * sin, b * cos + a * sin], axis=0)

    ca, sa, cc, sc = ca_ref[...], sa_ref[...], cc_ref[...], sc_ref[...]
    caT, saT, ccT, scT = caT_ref[...], saT_ref[...], ccT_ref[...], scT_ref[...]
    for t in range(3):
        r = slice(t * LANES, (t + 1) * LANES)
        ka_ref[:, r] = rope_tok(tok[:, r], ca, sa).astype(BF16)
        qa_ref[r, :] = (rope_feat(feat[r, :], caT, saT) * A_QSCALE).astype(BF16)
        rq = slice(768 + t * LANES, 768 + (t + 1) * LANES)
        qc_ref[r, :] = (rope_feat(feat[rq, :], ccT, scT) * C_QSCALE).astype(BF16)
    u_ref[...] = tok[:, 384:640]
    kcm_ref[...] = rope_tok(tok[:, 640:768], cc, sc).astype(BF16)
    vcm_ref[...] = tok[:, 768:896].astype(BF16)
    ksl_ref[...] = rope_tok(tok[:, 896:1024], cc, sc).astype(BF16)
    kwi_ref[...] = rope_tok(tok[:, 1024:1152], cc, sc).astype(BF16)
    va_ref[...] = feat[384:768, :].astype(BF16)
    vsl_ref[...] = feat[1152:1280, :].astype(BF16)
    vwi_ref[...] = feat[1280:1408, :].astype(BF16)
    gt_ref[...] = jax.nn.sigmoid(feat[1408:, :])


def _layer_spec(a, l):
    return pl.BlockSpec((None,) + a.shape[1:], lambda *_: (l,) + (0,) * (a.ndim - 1))


def _inproj(x3, w_tok, w_feat, tabs, tm, l):
    B, T, D = x3.shape
    tok = lambda n: pl.BlockSpec((None, tm, n), lambda b, i: (b, i, 0))
    feat = lambda n: pl.BlockSpec((None, n, tm), lambda b, i: (b, 0, i))
    full = lambda a: _layer_spec(a, l)
    tab_tok = pl.BlockSpec((tm, LANES), lambda b, i: (i, 0))
    tab_feat = pl.BlockSpec((64, tm), lambda b, i: (0, i))
    tok_outs = [(384, BF16), (256, F32)] + [(LANES, BF16)] * 4
    feat_outs = [(384, BF16)] * 3 + [(LANES, BF16)] * 2 + [(C_KV_GROUPS * GATE_ROWS, F32)]
    return pl.pallas_call(
        _inproj_kernel,
        grid=(B, T // tm),
        in_specs=[tok(D), full(w_tok), full(w_feat)] + [tab_tok, tab_tok, tab_feat, tab_feat] * 2,
        out_specs=[tok(n) for n, _ in tok_outs] + [feat(n) for n, _ in feat_outs],
        out_shape=([jax.ShapeDtypeStruct((B, T, n), dt) for n, dt in tok_outs]
                   + [jax.ShapeDtypeStruct((B, n, T), dt) for n, dt in feat_outs]),
        compiler_params=_params(("parallel", "parallel")),
        name="inproj_rope",
    )(x3, w_tok, w_feat, *tabs)


ONES_ROWS = 16


def _sweep_scratch(tk, nq, dv, n_stats=1):
    stats = [pltpu.VMEM((1, nq), F32), pltpu.VMEM((dv + ONES_ROWS, nq), F32)]
    return [pltpu.VMEM((tk, nq), F32), pltpu.VMEM((tk, nq), F32),
            pltpu.VMEM((1, nq), F32), pltpu.VMEM((1, nq), F32)] + stats * n_stats


class _Sweep:
    def __init__(self, score, qpos, value, bufs, stats):
        self.score, self.qpos, self.value = score, qpos, value
        self.s_refs, self.cm_refs = bufs[0:2], bufs[2:4]
        self.m_ref, self.acc_ref = stats
        self.tk, self.nq = self.s_refs[0].shape
        self.dv = self.acc_ref.shape[0] - ONES_ROWS

    def init(self):
        self.m_ref[...] = jnp.full(self.m_ref.shape, NEG_BIG, F32)
        self.acc_ref[...] = jnp.zeros(self.acc_ref.shape, F32)

    def scores(self, j, slot, fn=None):
        s = self.score(j, slice(None)) if fn is None else fn()
        self.s_refs[slot][...] = s
        self.cm_refs[slot][...] = jnp.max(s, axis=0, keepdims=True)

    def softmax(self, slot, j, causal=False):
        self.step(j, slot, None, causal)

    def step(self, j, cur, nxt, causal=False, then=None):
        scorer, j_next = (self, j + 1) if then is None else (then, 0)
        v = self.value(j)
        v1 = jnp.concatenate([v, jnp.ones((ONES_ROWS, v.shape[1]), v.dtype)], axis=0)
        for c0 in range(0, self.nq, MXU_TILE):
            cs = slice(c0, c0 + MXU_TILE)
            if nxt is not None:
                s_next = scorer.score(j_next, cs)
                self.s_refs[nxt][:, cs] = s_next
                self.cm_refs[nxt][:, cs] = jnp.max(s_next, axis=0, keepdims=True)
            s = self.s_refs[cur][:, cs]
            if causal:
                kpos = j * self.tk + lax.broadcasted_iota(jnp.int32, s.shape, 0)
                s = jnp.where(kpos <= self.qpos[:, cs], s, jnp.asarray(MASKED, s.dtype))
                cm = jnp.max(s, axis=0, keepdims=True).astype(F32)
            else:
                cm = self.cm_refs[cur][:, cs]
            m = self.m_ref[:, cs]
            m_new = jnp.maximum(m, cm)
            alpha = jnp.exp2(m - m_new)
            p = jnp.exp2(s - m_new.astype(s.dtype))
            self.m_ref[:, cs] = m_new
            self.acc_ref[:, cs] = self.acc_ref[:, cs] * alpha + _dot(v1, p.astype(BF16))

    def full_tiles(self, count, par):
        def group(base, tiles):
            for u in range(tiles):
                self.step(base + u, (par + u) % 2, (par + u + 1) % 2)

        def quad(i, c):
            group(SWEEP_UNROLL * i, SWEEP_UNROLL)
            return c

        def pair(i, c):
            group(done + 2 * i, 2)
            return c

        lax.fori_loop(0, count // SWEEP_UNROLL, quad, 0)
        done = (count // SWEEP_UNROLL) * SWEEP_UNROLL
        lax.fori_loop(0, (count - done) // 2, pair, 0)

    def result(self):
        return self.acc_ref[:self.dv, :] / self.acc_ref[self.dv:self.dv + 1, :]


def _causal_sweep(n_full, score, qpos, value, scratch, first_scores=None):
    sw = _Sweep(score, qpos, value, tuple(scratch[0:4]), tuple(scratch[4:6]))
    sw.init()
    sw.scores(0, 0, first_scores)
    sw.full_tiles(n_full - n_full % 2, 0)

    @pl.when(n_full % 2 == 0)
    def _():
        sw.softmax(0, n_full, causal=True)

    @pl.when(n_full % 2 == 1)
    def _():
        sw.step(n_full - 1, 0, 1)
        sw.softmax(1, n_full, causal=True)

    return sw.result()


def _diff_attn_kernel(lq1_ref, lk1_ref, lq2_ref, lk2_ref, g_ref, q_ref, k_ref, v_ref, o_ref,
                      *scratch, tq, tk, lam_init):
    h = pl.program_id(1)
    step = pl.program_id(2)
    hl = h % 2
    nq = 2 * tq
    grp = (lax.broadcasted_iota(jnp.int32, (LANES, tq), 0) % 64) // 16
    lane_q = lax.broadcasted_iota(jnp.int32, (1, nq), 1) % tq

    def value(j):
        return v_ref[:, pl.ds(pl.multiple_of(j * tk, tk), tk)]

    def sweep(blk, stats):
        q = q_ref[:, blk * tq:(blk + 1) * tq]
        zero = jnp.zeros_like(q)
        q12 = jnp.concatenate([jnp.where(grp == 2 * hl, q, zero), jnp.where(grp == 2 * hl + 1, q, zero)], axis=1)
        qpos = (A_BLOCKS * step + blk) * tq + lane_q
        score = lambda j, cs: _dot(k_ref[pl.ds(pl.multiple_of(j * tk, tk), tk), :], q12[:, cs])
        return _Sweep(score, qpos, value, scratch[0:4], stats)

    dotp = lambda a, b: jnp.sum(a[...] * b[...], axis=1, keepdims=True)
    lam = jnp.exp(dotp(lq1_ref, lk1_ref)) - jnp.exp(dotp(lq2_ref, lk2_ref)) + lam_init

    def finish(sw, blk):
        on = sw.result()
        o = on[:, :tq] - lam * on[:, tq:]
        ms = jnp.mean(o * o, axis=0, keepdims=True)
        o = o * lax.rsqrt(ms + RMS_EPS) * g_ref[...] * (1.0 - lam_init)
        o_ref[:, blk * tq:(blk + 1) * tq] = o.astype(o_ref.dtype)

    sweeps = [sweep(blk, scratch[4 + 2 * blk:6 + 2 * blk]) for blk in range(A_BLOCKS)]
    first = 0
    sweeps[0].init()
    sweeps[0].scores(0, first)
    for blk, sw in enumerate(sweeps):
        n = A_BLOCKS * step + blk
        if blk % 2 == 0:
            sw.full_tiles(n, first)
            causal_buf = first
        else:
            sw.full_tiles(n - 1, first)
            sw.step(n - 1, first, 1 - first)
            causal_buf = 1 - first
        first = 1 - causal_buf
        if blk + 1 < A_BLOCKS:
            sweeps[blk + 1].init()
            sw.step(n, causal_buf, first, causal=True, then=sweeps[blk + 1])
        else:
            sw.softmax(causal_buf, n, causal=True)
        finish(sw, blk)


def _diff_attn(qaT, ka, vaT, lq1, lk1, lq2, lk2, g_col, lam_init, tq, tk, l):
    B, _, T = qaT.shape
    assert tq == tk and T % (A_BLOCKS * tq) == 0, "query block i ends at key tile i; A_BLOCKS blocks per step"
    nq = 2 * tq
    vec = _layer_spec(lq1, l)
    return pl.pallas_call(
        functools.partial(_diff_attn_kernel, tq=tq, tk=tk, lam_init=lam_init),
        grid=(B, A_HEADS, T // (A_BLOCKS * tq)),
        in_specs=[vec, vec, vec, vec, _layer_spec(g_col, l),
                  pl.BlockSpec((None, LANES, A_BLOCKS * tq), lambda b, h, i: (b, h // 2, i)),
                  pl.BlockSpec((None, T, LANES), lambda b, h, i: (b, 0, h // 2)),
                  pl.BlockSpec((None, A_V_DIM, T), lambda b, h, i: (b, h, 0))],
        out_specs=pl.BlockSpec((None, A_V_DIM, A_BLOCKS * tq), lambda b, h, i: (b, h, i)),
        out_shape=jax.ShapeDtypeStruct((B, A_WIDTH, T), BF16),
        scratch_shapes=_sweep_scratch(tk, nq, A_V_DIM, n_stats=A_BLOCKS),
        compiler_params=_params(("parallel", "parallel", "arbitrary")),
        name="diff_attn",
    )(lq1, lk1, lq2, lk2, g_col, qaT, ka, vaT)


POOL_HALO = 16


def _pool_mix(halo, cur, w_ref, scale_ref, t0):
    tm = cur.shape[0]
    arr = jnp.concatenate([halo, cur], axis=0)
    s2 = arr + pltpu.roll(arr, 1, axis=0)
    s4 = s2 + pltpu.roll(s2, 2, axis=0)
    s8 = s4 + pltpu.roll(s4, 4, axis=0)
    s16 = s8 + pltpu.roll(s8, 8, axis=0)
    sums = [s[POOL_HALO:, :] for s in (s2, s4, s8, s16)]
    lane = lax.broadcasted_iota(jnp.int32, (tm, POOL_WIDTH), 1)
    tpos = t0 + lax.broadcasted_iota(jnp.int32, (tm, POOL_WIDTH), 0)
    grp = lane // POOL_GROUP_DIM
    win = jnp.where(grp == 0, sums[0], jnp.where(grp == 1, sums[1], jnp.where(grp == 2, sums[2], sums[3])))
    width = jnp.where(grp == 0, 2, jnp.where(grp == 1, 4, jnp.where(grp == 2, 8, 16)))
    cnt = jnp.minimum(tpos + 1, width).astype(F32)
    y = win / cnt - cur
    z = _dot(y.astype(BF16), w_ref[...])
    return (z * scale_ref[...]).astype(BF16)


def _prep_pool(w_pool, pool_scale):
    depth, G, C, _ = w_pool.shape
    eye = jnp.asarray(np.eye(G, dtype=np.float32))
    w_bd = (w_pool[:, :, :, None, :] * eye[None, :, None, :, None]).reshape(depth, G * C, G * C)
    return w_bd.astype(BF16), pool_scale[:, None, :]


def _cmp_kernel(k_ref, v_ref, pk_ref, pv_ref, wk_ref, wv_ref, kc_ref, vc_ref):
    n = k_ref.shape[0]
    rowid = lax.broadcasted_iota(jnp.int32, (n, LANES), 0)

    def compress(x_ref, pe_ref, w_ref):
        x = x_ref[...].astype(F32)
        lo = _dot((x + pe_ref[0:1, :]).astype(BF16), w_ref[0])
        hi = _dot((x + pe_ref[1:2, :]).astype(BF16), w_ref[1])
        out = lo + pltpu.roll(hi, n - 1, axis=0)
        return jnp.where(rowid < n - 1, out, 0.0)

    kc_ref[...] = compress(k_ref, pk_ref, wk_ref).astype(kc_ref.dtype)
    vc_ref[...] = compress(v_ref, pv_ref, wv_ref).astype(vc_ref.dtype)


def _compress(kr, vr, pe_k2, pe_v2, wk2, wv2, l):
    B, n, W = kr.shape
    blk = pl.BlockSpec((None, n, W), lambda b: (b, 0, 0))
    pe = _layer_spec(pe_k2, l)
    wsp = _layer_spec(wk2, l)
    osp = pl.BlockSpec((None, n, LANES), lambda b: (b, 0, 0))
    return pl.pallas_call(
        _cmp_kernel,
        grid=(B,),
        in_specs=[blk, blk, pe, pe, wsp, wsp],
        out_specs=[osp, osp],
        out_shape=[jax.ShapeDtypeStruct((B, n, LANES), BF16)] * 2,
        compiler_params=_params(("parallel",)),
        name="nsa_compress",
    )(kr, vr, pe_k2, pe_v2, wk2, wv2)


def _prep_cmp(pe_k, pe_v, w_ck, w_cv):
    L, G, d, h = w_ck.shape[0], C_KV_GROUPS, C_HEAD_DIM, C_HEAD_DIM // 2
    same = jnp.asarray(np.eye(G, dtype=np.float32))
    shape2 = (L, 2, 16 * LANES, G * d)
    wk = w_ck.reshape(L, CMP_BLOCK, 2, 1, h, 2, 1, h) * same[None, None, None, :, None, None, :, None]
    pk2 = jnp.broadcast_to(pe_k.reshape(L, CMP_BLOCK, 2, 1, h), (L, CMP_BLOCK, 2, G, h))
    wv = w_cv.reshape(L, CMP_BLOCK, 1, d, 1, d) * same[None, None, :, None, :, None]
    pv2 = jnp.broadcast_to(pe_v.reshape(L, CMP_BLOCK, 1, d), (L, CMP_BLOCK, G, d))
    flat = lambda pe: pe.reshape(L, 2, 16 * LANES)
    return flat(pk2), flat(pv2), wk.reshape(shape2).astype(BF16), wv.reshape(shape2).astype(BF16)


def _nsa_kernel(q_ref, gt_ref, kc_ref, vc_ref, ov_ref, *refs, tq, tk, T):
    wb_refs, (ks_ref, vs_ref, kw_ref, vw_ref, o_ref) = refs[:C_BLOCKS], refs[C_BLOCKS:C_BLOCKS + 5]
    scratch = refs[C_BLOCKS + 5:]
    sweep_bufs, per_block = scratch[0:4], scratch[4:]
    g = pl.program_id(1)
    step = pl.program_id(2)
    n_cmp = kc_ref.shape[0]
    n_blk = T // SEL_BLOCK
    d = C_HEAD_DIM
    nq = C_HPG * tq
    bpt = tk // SEL_BLOCK
    rowg = (lax.broadcasted_iota(jnp.int32, (LANES, tq), 0) % 64) // 32

    class Block:
        def __init__(self, blk):
            self.cols = slice(blk * tq, (blk + 1) * tq)
            self.q0 = (C_BLOCKS * step + blk) * tq
            self.tpos = self.q0 + lax.broadcasted_iota(jnp.int32, (1, tq), 1)
            self.tpos3 = self.q0 + lax.broadcasted_iota(jnp.int32, (1, nq), 1) % tq
            self.cur = (self.tpos // SEL_BLOCK).astype(F32)
            qs = []
            for hh in range(C_HPG):
                qt = q_ref[hh * LANES:(hh + 1) * LANES, self.cols]
                qs.append(jnp.where(rowg == g, qt, jnp.zeros_like(qt)))
            self.q3 = jnp.concatenate(qs, axis=1)
            self.wb_ref = wb_refs[blk]
            self.sel_ref, m_ref, acc_ref, self.raw0_ref, self.ocw_ref = per_block[5 * blk:5 * blk + 5]
            self.stats = (m_ref, acc_ref)

    blocks = [Block(blk) for blk in range(C_BLOCKS)]

    def prelude(bk, n_c, n_b):
        q3, tpos3, cur, q0 = bk.q3, bk.tpos3, bk.cur, bk.q0
        cend = lax.broadcasted_iota(jnp.int32, (n_c, nq), 0) * CMP_STRIDE + (CMP_BLOCK - 1)
        s = jnp.where(cend <= tpos3, _dot(kc_ref[0:n_c, :], q3), MASKED)
        m = jnp.maximum(jnp.max(s, axis=0, keepdims=True), NEG_BIG)
        e = jnp.exp2(s - m)
        l = jnp.sum(e, axis=0, keepdims=True)
        p = e / jnp.where(l > 0.0, l, 1.0)
        bk.ocw_ref[0:d, :] = _dot(vc_ref[:, 0:n_c], p.astype(BF16))
        psum = p[:, :tq] + p[:, tq:2 * tq] + p[:, 2 * tq:]

        p_hi = psum.astype(BF16)
        p_lo = (psum - p_hi.astype(F32)).astype(BF16)
        ov = ov_ref[0:n_b, 0:n_c]
        imp = _dot(ov, p_hi) + _dot(ov, p_lo)
        jidx = lax.broadcasted_iota(jnp.int32, (n_b, tq), 0).astype(F32)
        forced = (jidx == 0.0) | (jidx == cur) | (jidx == cur - 1.0)
        imp = jnp.where(jidx > cur, -1.0, imp)
        imp = jnp.where(forced, TAKEN, imp)
        rounds = min(N_SEL, n_blk) - N_FORCED

        band = WINDOW + tq
        w0 = pl.multiple_of(jnp.maximum(q0 - WINDOW, 0), LANES)
        s = _dot(kw_ref[pl.ds(w0, band), :], q3) + jnp.concatenate([bk.wb_ref[...]] * C_HPG, axis=1)
        e = jnp.exp2(s - jnp.max(s, axis=0, keepdims=True))
        bk.ocw_ref[d:, :] = _dot(vw_ref[:, pl.ds(w0, band)], e.astype(BF16)) / jnp.sum(e, axis=0, keepdims=True)

        bk.raw0_ref[...] = _dot(ks_ref[0:tk, :], q3)

        def store_selection(taken_imp):
            bias = jnp.where(taken_imp == TAKEN, 0.0, MASKED)
            bk.sel_ref[0:n_b, :] = jnp.concatenate([bias] * C_HPG, axis=1)

        if n_b < n_blk:
            bk.sel_ref[n_b:, :] = jnp.full((n_blk - n_b, nq), MASKED, F32)
        quick = imp
        for _ in range(rounds):
            quick = jnp.where(quick == jnp.max(quick, axis=0, keepdims=True), TAKEN, quick)
        store_selection(quick)
        free = (jidx >= 1.0) & (jidx <= cur - 2.0)
        n_taken = jnp.sum(jnp.where(free & (quick == TAKEN), 1.0, 0.0), axis=0, keepdims=True)
        n_expected = jnp.minimum(jnp.maximum(cur - 2.0, 0.0), float(rounds))
        tie = jnp.max(jnp.abs(n_taken - n_expected)) > 0.0

        def exact_rounds():
            exact = imp
            for _ in range(rounds):
                mx = jnp.max(exact, axis=0, keepdims=True)
                first = jnp.min(jnp.where(exact == mx, jidx, float(n_blk)), axis=0, keepdims=True)
                exact = jnp.where(jidx == first, TAKEN, exact)
            store_selection(exact)

        return tie, exact_rounds

    def preludes(n_c, n_b):
        pending = [prelude(bk, n_c, n_b) for bk in blocks]
        for tie, exact_rounds in pending:
            pl.when(tie)(exact_rounds)

    early = (C_BLOCKS * step + C_BLOCKS) * tq <= T // 2
    pl.when(early)(lambda: preludes(n_cmp // 2, n_blk // 2))
    pl.when(jnp.logical_not(early))(lambda: preludes(n_cmp, n_blk))

    gts = gt_ref[...]
    for bk in blocks:
        def block_bias(j, cs, bk=bk):
            rows = bk.sel_ref[pl.ds(pl.multiple_of(j * bpt, bpt), bpt), cs]
            return jnp.concatenate([jnp.broadcast_to(rows[i:i + 1, :], (SEL_BLOCK, rows.shape[1]))
                                    for i in range(bpt)], axis=0)

        def score(j, cs, bk=bk, block_bias=block_bias):
            return _dot(ks_ref[pl.ds(pl.multiple_of(j * tk, tk), tk), :], bk.q3[:, cs]) + block_bias(j, cs)

        def first_scores(bk=bk, block_bias=block_bias):
            return bk.raw0_ref[...] + block_bias(0, slice(None))

        def value(j):
            return vs_ref[:, pl.ds(pl.multiple_of(j * tk, tk), tk)]

        o_slc = _causal_sweep(bk.q0 // tk, score, bk.tpos3, value, sweep_bufs + bk.stats, first_scores)
        for hh in range(C_HPG):
            r = hh * N_BRANCH
            c = slice(hh * tq, (hh + 1) * tq)
            gc = gts[:, bk.cols]
            out = (bk.ocw_ref[0:d, c] * gc[r:r + 1, :] + o_slc[:, c] * gc[r + 1:r + 2, :]
                   + bk.ocw_ref[d:, c] * gc[r + 2:r + 3, :])
            o_ref[hh * d:(hh + 1) * d, bk.cols] = out.astype(o_ref.dtype)


def _window_bias(tq):
    band, n_edge = WINDOW + tq, WINDOW // tq
    q0 = np.arange(n_edge + 1)[:, None, None] * tq
    wpos = np.maximum(q0 - WINDOW, 0) + np.arange(band)[None, :, None]
    tpos = q0 + np.arange(tq)[None, None, :]
    ok = (wpos <= tpos) & (wpos > tpos - WINDOW)
    return jnp.asarray(np.where(ok, 0.0, MASKED).astype(np.float32))


def _nsa(qcT, gtT, kc2, vcT, ovT, ksl, vslT, kwi, vwiT, tq, tk):
    B, _, T = qcT.shape
    n_cmp = kc2.shape[1]
    n_blk = T // SEL_BLOCK
    nq = C_HPG * tq
    tqs = C_BLOCKS * tq
    assert tk % tqs == 0 or tqs % tk == 0, "a step's blocks must not straddle a partial key tile boundary"
    assert tk % tq == 0, "the causal key tile must contain the whole query block"
    assert WINDOW % tq == 0 and T >= WINDOW + tq and (T // 2) % tqs == 0
    wb = _window_bias(tq)
    n_edge = WINDOW // tq
    kv_tok = pl.BlockSpec((None, T, LANES), lambda b, g, i: (b, 0, 0))
    kv_feat = pl.BlockSpec((None, C_HEAD_DIM, T), lambda b, g, i: (b, g, 0))
    wb_spec = lambda blk: pl.BlockSpec((None, WINDOW + tq, tq),
                                       lambda b, g, i: (jnp.minimum(C_BLOCKS * i + blk, n_edge), 0, 0))
    sweep = _sweep_scratch(tk, nq, C_HEAD_DIM)
    per_block = [pltpu.VMEM((n_blk, nq), F32)] + sweep[4:6] + [pltpu.VMEM((tk, nq), F32),
                                                              pltpu.VMEM((2 * C_HEAD_DIM, nq), F32)]
    return pl.pallas_call(
        functools.partial(_nsa_kernel, tq=tq, tk=tk, T=T),
        grid=(B, C_KV_GROUPS, T // tqs),
        in_specs=[pl.BlockSpec((None, C_WIDTH, tqs), lambda b, g, i: (b, 0, i)),
                  pl.BlockSpec((None, GATE_ROWS, tqs), lambda b, g, i: (b, g, i)),
                  pl.BlockSpec((None, n_cmp, LANES), lambda b, g, i: (b, 0, 0)),
                  pl.BlockSpec((None, C_HEAD_DIM, n_cmp), lambda b, g, i: (b, g, 0)),
                  pl.BlockSpec((n_blk, n_cmp), lambda b, g, i: (0, 0))]
                 + [wb_spec(blk) for blk in range(C_BLOCKS)] + [kv_tok, kv_feat, kv_tok, kv_feat],
        out_specs=pl.BlockSpec((None, C_HPG * C_HEAD_DIM, tqs), lambda b, g, i: (b, g, i)),
        out_shape=jax.ShapeDtypeStruct((B, C_WIDTH, T), BF16),
        scratch_shapes=sweep[0:4] + per_block * C_BLOCKS,
        compiler_params=_params(("parallel", "parallel", "arbitrary")),
        name="nsa_attn",
    )(qcT, gtT, kc2, vcT, ovT, *([wb] * C_BLOCKS), ksl, vslT, kwi, vwiT)


def _overlap_T(T):
    n_cmp = T // CMP_STRIDE
    n_blk = T // SEL_BLOCK
    cs = np.arange(n_cmp)[None, :] * CMP_STRIDE
    bs = np.arange(n_blk)[:, None] * SEL_BLOCK
    ov = (cs < bs + SEL_BLOCK) & (cs + CMP_BLOCK > bs) & (np.arange(n_cmp)[None, :] < n_cmp - 1)
    return jnp.asarray(ov.astype(np.float32), dtype=BF16)


def _layer_norm(y, g, b):
    mu = jnp.mean(y, axis=-1, keepdims=True)
    yc = y - mu
    var = jnp.mean(yc * yc, axis=-1, keepdims=True)
    return yc * lax.rsqrt(var + LN_EPS) * g + b


def _outproj_kernel(x_ref, oa_ref, uh_ref, u_ref, oc_ref, w_ref, wp_ref, ps_ref, g_ref, b_ref, o_ref, *, alpha, sub):
    i, tm = pl.program_id(1), x_ref.shape[0]
    tn = lambda a, w: lax.dot_general(a, w, (((0,), (0,)), ((), ())), preferred_element_type=F32)
    r1, r2 = A_WIDTH, A_WIDTH + POOL_WIDTH
    for r0 in range(0, tm, sub):
        rows = slice(r0, r0 + sub)
        if r0 == 0:
            halo = jnp.where(i == 0, jnp.zeros(uh_ref.shape, F32), uh_ref[...])
        else:
            halo = u_ref[r0 - POOL_HALO:r0, :]
        ob = _pool_mix(halo, u_ref[rows, :], wp_ref, ps_ref, i * tm + r0)
        mix = (tn(oa_ref[:, rows], w_ref[:r1, :]) + _dot(ob, w_ref[r1:r2, :]) + tn(oc_ref[:, rows], w_ref[r2:, :]))
        o_ref[rows, :] = _layer_norm(alpha * x_ref[rows, :] + mix, g_ref[...], b_ref[...])


def _outproj(x3, oaT, u, ocT, w, w_bd, p_scale, g, b, alpha, tm, sub, l):
    B, T, D = x3.shape
    tok = lambda n: pl.BlockSpec((None, tm, n), lambda bb, i: (bb, i, 0))
    feat = lambda n: pl.BlockSpec((None, n, tm), lambda bb, i: (bb, 0, i))
    halo = pl.BlockSpec((None, POOL_HALO, POOL_WIDTH),
                        lambda bb, i: (bb, jnp.maximum(i * (tm // POOL_HALO) - 1, 0), 0))
    return pl.pallas_call(
        functools.partial(_outproj_kernel, alpha=alpha, sub=sub),
        grid=(B, T // tm),
        in_specs=[tok(D), feat(oaT.shape[1]), halo, tok(POOL_WIDTH), feat(ocT.shape[1]),
                  _layer_spec(w, l), _layer_spec(w_bd, l), _layer_spec(p_scale, l),
                  _layer_spec(g, l), _layer_spec(b, l)],
        out_specs=tok(D),
        out_shape=jax.ShapeDtypeStruct((B, T, D), F32),
        compiler_params=_params(("parallel", "parallel")),
        name="outproj_ln",
    )(x3, oaT, u, u, ocT, w, w_bd, p_scale, g, b)


def _mlp_kernel(x_ref, wu_ref, wd_ref, g_ref, b_ref, o_ref, *, alpha, sub, tf):
    tm, F = x_ref.shape[0], wu_ref.shape[1]
    for r0 in range(0, tm, sub):
        x = x_ref[r0:r0 + sub, :]
        xb = x.astype(BF16)
        y = alpha * x
        for f0 in range(0, F, tf):
            hid = jnp.square(jnp.maximum(_dot(xb, wu_ref[:, f0:f0 + tf]), 0.0))
            y = y + _dot(hid.astype(BF16), wd_ref[f0:f0 + tf, :])
        o_ref[r0:r0 + sub, :] = _layer_norm(y, g_ref[...], b_ref[...])


def _mlp(x2, wu, wd, g, b, alpha, tm, sub, tf, l):
    BT, D = x2.shape
    once = lambda a: pl.BlockSpec((None,) + a.shape[1:], lambda i: (l, 0, 0), pipeline_mode=pl.Buffered(1))
    return pl.pallas_call(
        functools.partial(_mlp_kernel, alpha=alpha, sub=sub, tf=tf),
        grid=(BT // tm,),
        in_specs=[pl.BlockSpec((tm, D), lambda i: (i, 0)), once(wu), once(wd), _layer_spec(g, l), _layer_spec(b, l)],
        out_specs=pl.BlockSpec((tm, D), lambda i: (i, 0)),
        out_shape=jax.ShapeDtypeStruct((BT, D), F32),
        compiler_params=_params(("parallel",)),
        name="mlp_ln",
    )(x2, wu, wd, g, b)


def _tiles(T):
    return dict(
        tm_proj=min(512, T), tm_out=min(1024, T), sub_out=min(512, T), tm_mlp=min(1024, T), sub_mlp=min(512, T), tf_mlp=1024,
        a_tq=min(512, T), a_tk=min(512, T), c_tq=min(256, T), c_tk=min(512, T),
    )


def kernel(x, w_in, lam_q1, lam_k1, lam_q2, lam_k2, subln_g, w_pool, pool_scale, cmp_pe_k, cmp_pe_v,
           w_cmp_k, w_cmp_v, w_out, ln1_g, ln1_b, w_up, w_down, ln2_g, ln2_b):
    B, T, D = x.shape
    depth = w_in.shape[0]
    alpha = (2 * depth) ** 0.25
    tl = _tiles(T)
    tabs = _rope_tables(T)
    ovT = _overlap_T(T)
    w_tok, w_feat = _prep_w_in(w_in)
    w_bd, p_scale = _prep_pool(w_pool, pool_scale)
    pk2, pv2, wk2, wv2 = _prep_cmp(cmp_pe_k, cmp_pe_v, w_cmp_k, w_cmp_v)
    wo, wu, wd = w_out.astype(BF16), w_up.astype(BF16), w_down.astype(BF16)
    row = lambda a: a[:, None, :]
    lams = [row(a) for a in (lam_q1, lam_k1, lam_q2, lam_k2)]
    n16 = T // CMP_STRIDE
    x3 = x
    for l in range(depth):
        lam_init = 0.8 - 0.6 * math.exp(-0.3 * l)
        (ka, u, kcm, vcm, ksl, kwi, qaT, vaT, qcT, vslT, vwiT, gtT) = _inproj(x3, w_tok, w_feat, tabs, tl["tm_proj"], l)

        oaT = _diff_attn(qaT, ka, vaT, *lams, subln_g[:, :, None], lam_init, tl["a_tq"], tl["a_tk"], l)

        kc2, vc2 = _compress(kcm.reshape(B, n16, 16 * LANES), vcm.reshape(B, n16, 16 * LANES),
                             pk2, pv2, wk2, wv2, l)
        ocT = _nsa(qcT, gtT, kc2, vc2.transpose(0, 2, 1), ovT, ksl, vslT, kwi, vwiT, tl["c_tq"], tl["c_tk"])

        x3 = _outproj(x3, oaT, u, ocT, wo, w_bd, p_scale, row(ln1_g), row(ln1_b), alpha, tl["tm_out"], tl["sub_out"], l)
        x3 = _mlp(x3.reshape(B * T, D), wu, wd, row(ln2_g), row(ln2_b), alpha, tl["tm_mlp"], tl["sub_mlp"], tl["tf_mlp"],
                  l).reshape(B, T, D)
    return x3
```
